```python
import math
import jax
import jax.numpy as jnp
from jax import lax
import numpy as np

D_MODEL = 1024
BATCH = 8
SEQ = 2048
DEPTH = 1
DEC_BATCH = 128
DEC_SEQ = 8
PAST_LEN = 16384
PAGE_SIZE = 128

RET_HEADS = 4
RET_HD = 128
RET_W = RET_HEADS * RET_HD
RET_CHUNK = 128
RET_GN_EPS = 1e-5
ROPE_BASE = 10000.0
RWKV_HEADS = 8
RWKV_HD = 64
RWKV_W = RWKV_HEADS * RWKV_HD
DECAY_LORA = 64
AAA_LORA = 64
GATE_LORA = 128
RWKV_GN_EPS = 64e-5
RWKV_IN_W = 3 * RWKV_W + DECAY_LORA + AAA_LORA + GATE_LORA
O_GATE_A = 0
O_GATE_B = D_MODEL
O_RET = 2 * D_MODEL
O_RWKV = O_RET + 4 * RET_W
IN_W = O_RWKV + RWKV_IN_W
MEM_LEN = 256
X_HEADS = 4
X_HD = D_MODEL // X_HEADS
D_FF = 4 * D_MODEL
RMS_EPS = 1e-6

kernel_name = "retention_rwkv7_gated_hybrid_step"


def _rmsnorm(x, g):
    x32 = x.astype(jnp.float32)
    y = x32 * lax.rsqrt(jnp.mean(x32 * x32, axis=-1, keepdims=True) + RMS_EPS)
    return (y * g.astype(jnp.float32)).astype(x.dtype)


def _head_norm(o, eps):
    mu = jnp.mean(o, axis=-1, keepdims=True)
    var = jnp.mean(jnp.square(o - mu), axis=-1, keepdims=True)
    return (o - mu) * lax.rsqrt(var + eps)


def _rope(t, pos):
    half = t.shape[-1] // 2
    inv = 1.0 / (ROPE_BASE ** (jnp.arange(half, dtype=jnp.float32) / half))
    ang = pos.astype(jnp.float32)[:, None] * inv[None, :]
    cos = jnp.cos(ang)[None, :, None, :]
    sin = jnp.sin(ang)[None, :, None, :]
    t1, t2 = t[..., :half], t[..., half:]
    return jnp.concatenate([t1 * cos - t2 * sin, t1 * sin + t2 * cos], axis=-1)


def _retention(q, k, v, s0):
    B, T, H, Dk = q.shape
    Dv = v.shape[-1]
    C = math.gcd(T, RET_CHUNK)
    N = T // C
    lg = jnp.log1p(-jnp.exp2(-5.0 - jnp.arange(H, dtype=jnp.float32)))
    idx = jnp.arange(C, dtype=jnp.float32)
    diff = idx[:, None] - idx[None, :]
    dmask = jnp.where(diff[None] >= 0, jnp.exp(jnp.maximum(diff, 0.0)[None] * lg[:, None, None]), 0.0)
    q_dec = jnp.exp((idx + 1.0)[:, None] * lg[None, :])
    k_dec = jnp.exp((C - 1.0 - idx)[:, None] * lg[None, :])
    c_dec = jnp.exp(C * lg)

    def to_chunks(t):
        return t.reshape(B, N, C, H, t.shape[-1]).transpose(1, 0, 2, 3, 4)

    def step(s, inp):
        qi, ki, vi = inp
        sc = jnp.einsum('bihd,bjhd->bhij', qi, ki) * dmask[None]
        o = (jnp.einsum('bhij,bjhe->bihe', sc, vi)
             + jnp.einsum('bihd,bhde->bihe', qi, s) * q_dec[None, :, :, None])
        s = s * c_dec[None, :, None, None] + jnp.einsum('bjhd,bjhe->bhde', ki * k_dec[None, :, :, None], vi)
        return s, o

    s, o = lax.scan(step, s0, (to_chunks(q), to_chunks(k), to_chunks(v)))
    o = o.transpose(1, 0, 2, 3, 4).reshape(B, T, H, Dv)
    return o, s


def _rwkv7_scan(r, w, k, v, a, b, s0):
    def tm(t):
        return jnp.swapaxes(t, 0, 1)

    def step(s, inp):
        r_t, w_t, k_t, v_t, a_t, b_t = inp
        sa = jnp.einsum('bhvk,bhk->bhv', s, a_t)
        s = s * w_t[:, :, None, :] + sa[..., None] * b_t[:, :, None, :] + v_t[..., None] * k_t[:, :, None, :]
        o = jnp.einsum('bhvk,bhk->bhv', s, r_t)
        return s, o

    s, o = lax.scan(step, s0, (tm(r), tm(w), tm(k), tm(v), tm(a), tm(b)))
    return tm(o), s


def _mem_kv(mem, g_mem, w_ck, w_cv):
    B, M, _ = mem.shape
    m = _rmsnorm(mem, g_mem)
    return (m @ w_ck).reshape(B, M, X_HEADS, X_HD), (m @ w_cv).reshape(B, M, X_HEADS, X_HD)


def _block(x, pos, s_ret, s_wkv, s_shift, mem_k, mem_v,
           g_mix, w_in, w_branch_a, w_branch_b, w_out,
           mu_shift, w0, w_decay_up, a0, w_a_up, w_g_up, k_k, k_a, r_k, lnx_g, lnx_b,
           g_cross, w_cq, w_co, g_mlp, w_up, w_down):
    B, T, _ = x.shape
    dt = x.dtype
    f32 = jnp.float32
    h = _rmsnorm(x, g_mix)
    z = h @ w_in
    gate_a = jax.nn.sigmoid(z[..., O_GATE_A:O_GATE_A + D_MODEL].astype(f32))
    gate_b = jax.nn.sigmoid(z[..., O_GATE_B:O_GATE_B + D_MODEL].astype(f32))

    zr = z[..., O_RET:O_RWKV].astype(f32).reshape(B, T, 4, RET_HEADS, RET_HD)
    q = _rope(zr[:, :, 0], pos)
    k = _rope(zr[:, :, 1], pos) * (RET_HD ** -0.5)
    v = zr[:, :, 2]
    g_ret = zr[:, :, 3].reshape(B, T, RET_W)
    o, s_ret_new = _retention(q, k, v, s_ret.astype(f32))
    o_ret = (_head_norm(o, RET_GN_EPS).reshape(B, T, RET_W) * jax.nn.silu(g_ret)).astype(dt)

    pw = z[..., O_RWKV:IN_W]
    prev = jnp.concatenate([s_shift[:, None, :].astype(dt), pw[:, :-1]], axis=1)
    xm = (pw + mu_shift * (prev - pw)).astype(f32)
    new_shift = pw[:, -1]
    hs = (B, T, RWKV_HEADS, RWKV_HD)
    r = xm[..., 0:RWKV_W].reshape(hs)
    kb = xm[..., RWKV_W:2 * RWKV_W].reshape(hs)
    vb = xm[..., 2 * RWKV_W:3 * RWKV_W].reshape(hs)
    o1 = 3 * RWKV_W
    wl = xm[..., o1:o1 + DECAY_LORA]
    al = xm[..., o1 + DECAY_LORA:o1 + DECAY_LORA + AAA_LORA]
    gl = xm[..., o1 + DECAY_LORA + AAA_LORA:]
    w_log = -jax.nn.softplus(-(w0.astype(f32) + jnp.tanh(wl) @ w_decay_up.astype(f32))) - 0.5
    decay = jnp.exp(-jnp.exp(w_log)).reshape(hs)
    a = jax.nn.sigmoid(a0.astype(f32) + al @ w_a_up.astype(f32)).reshape(hs)
    g = jax.nn.sigmoid(gl) @ w_g_up.astype(f32)
    kk = kb * k_k.astype(f32).reshape(RWKV_HEADS, RWKV_HD)
    kk = kk / jnp.maximum(jnp.sqrt(jnp.sum(kk * kk, axis=-1, keepdims=True)), 1e-12)
    kb = kb * (1.0 + (a - 1.0) * k_a.astype(f32).reshape(RWKV_HEADS, RWKV_HD))
    ob, s_wkv_new = _rwkv7_scan(r, decay, kb, vb, -kk, kk * a, s_wkv.astype(f32))
    ob = (_head_norm(ob, RWKV_GN_EPS) * lnx_g.astype(f32).reshape(RWKV_HEADS, RWKV_HD)
          + lnx_b.astype(f32).reshape(RWKV_HEADS, RWKV_HD))
    ob = ob + jnp.sum(r * kb * r_k.astype(f32), axis=-1, keepdims=True) * vb
    o_rwkv = (ob.reshape(B, T, RWKV_W) * g).astype(dt)

    merged = (gate_a * (o_ret @ w_branch_a).astype(f32)
              + gate_b * (o_rwkv @ w_branch_b).astype(f32)).astype(dt)
    x = x + merged @ w_out

    hq = _rmsnorm(x, g_cross)
    qx = (hq @ w_cq).reshape(B, T, X_HEADS, X_HD)
    sc = jnp.einsum('bthd,bmhd->bhtm', qx, mem_k).astype(f32) * (X_HD ** -0.5)
    att = jax.nn.softmax(sc, axis=-1).astype(dt)
    ox = jnp.einsum('bhtm,bmhd->bthd', att, mem_v).reshape(B, T, D_MODEL)
    x = x + ox @ w_co

    hm = _rmsnorm(x, g_mlp)
    x = x + jnp.square(jax.nn.relu(hm @ w_up)) @ w_down
    return x, s_ret_new.astype(dt), s_wkv_new.astype(dt), new_shift


def setup_inputs(seed: int = 0) -> dict:
    key = jax.random.key(seed)
    ks = iter(jax.random.split(key, 40))
    f32 = jnp.float32
    L = DEPTH

    def nrm(shape, scale):
        return jax.random.normal(next(ks), shape, f32) * scale

    def gain(shape):
        return 1.0 + nrm(shape, 0.01)

    d = {}
    d['x_prompt'] = nrm((BATCH, SEQ, D_MODEL), 1.0)
    d['x_sample'] = nrm((DEC_BATCH, DEC_SEQ, D_MODEL), 1.0)
    d['mem_prompt'] = nrm((BATCH, MEM_LEN, D_MODEL), 1.0)
    d['state_ret'] = nrm((L, DEC_BATCH, RET_HEADS, RET_HD, RET_HD), 0.1)
    d['state_wkv'] = nrm((L, DEC_BATCH, RWKV_HEADS, RWKV_HD, RWKV_HD), 0.1)
    d['state_shift'] = nrm((L, DEC_BATCH, RWKV_IN_W), 1.0)
    d['cache_mem_k'] = nrm((L, DEC_BATCH, MEM_LEN, X_HEADS, X_HD), 1.0)
    d['cache_mem_v'] = nrm((L, DEC_BATCH, MEM_LEN, X_HEADS, X_HD), 1.0)
    d['g_mix'] = gain((L, D_MODEL))
    d['w_in'] = nrm((L, D_MODEL, IN_W), D_MODEL ** -0.5)
    d['w_branch_a'] = nrm((L, RET_W, D_MODEL), RET_W ** -0.5)
    d['w_branch_b'] = nrm((L, RWKV_W, D_MODEL), RWKV_W ** -0.5)
    d['w_out'] = nrm((L, D_MODEL, D_MODEL), D_MODEL ** -0.5)
    d['mu_shift'] = jax.random.uniform(next(ks), (L, RWKV_IN_W), f32)
    d['w0'] = nrm((L, RWKV_W), 0.5) - 1.0
    d['w_decay_up'] = nrm((L, DECAY_LORA, RWKV_W), 0.1 * DECAY_LORA ** -0.5)
    d['a0'] = nrm((L, RWKV_W), 0.1)
    d['w_a_up'] = nrm((L, AAA_LORA, RWKV_W), 0.5 * AAA_LORA ** -0.5)
    d['w_g_up'] = nrm((L, GATE_LORA, RWKV_W), GATE_LORA ** -0.5)
    d['k_k'] = 0.85 + nrm((L, RWKV_W), 0.05)
    d['k_a'] = 1.0 + nrm((L, RWKV_W), 0.05)
    d['r_k'] = nrm((L, RWKV_HEADS, RWKV_HD), 0.1)
    d['lnx_g'] = gain((L, RWKV_W))
    d['lnx_b'] = nrm((L, RWKV_W), 0.01)
    d['g_cross'] = gain((L, D_MODEL))
    d['g_mem'] = gain((L, D_MODEL))
    d['w_cq'] = nrm((L, D_MODEL, D_MODEL), D_MODEL ** -0.5)
    d['w_ck'] = nrm((L, D_MODEL, D_MODEL), D_MODEL ** -0.5)
    d['w_cv'] = nrm((L, D_MODEL, D_MODEL), D_MODEL ** -0.5)
    d['w_co'] = nrm((L, D_MODEL, D_MODEL), D_MODEL ** -0.5)
    d['g_mlp'] = gain((L, D_MODEL))
    d['w_up'] = nrm((L, D_MODEL, D_FF), D_MODEL ** -0.5)
    d['w_down'] = nrm((L, D_FF, D_MODEL), D_FF ** -0.5)
    d['g_final'] = gain((D_MODEL,))
    return d


def reference(x_prompt, x_sample, mem_prompt, state_ret, state_wkv, state_shift, cache_mem_k, cache_mem_v,
              g_mix, w_in, w_branch_a, w_branch_b, w_out,
              mu_shift, w0, w_decay_up, a0, w_a_up, w_g_up, k_k, k_a, r_k, lnx_g, lnx_b,
              g_cross, g_mem, w_cq, w_ck, w_cv, w_co, g_mlp, w_up, w_down, g_final):
    dt = x_prompt.dtype
    Bp, Tp, _ = x_prompt.shape
    Ts = x_sample.shape[1]
    pos_p = jnp.arange(Tp, dtype=jnp.int32)
    pos_s = PAST_LEN + jnp.arange(Ts, dtype=jnp.int32)

    xp, xs = x_prompt, x_sample
    ret_p, wkv_p, sh_p, mk_p, mv_p = [], [], [], [], []
    ret_s, wkv_s, sh_s = [], [], []
    for l in range(DEPTH):
        shared = (g_mix[l], w_in[l], w_branch_a[l], w_branch_b[l], w_out[l],
                  mu_shift[l], w0[l], w_decay_up[l], a0[l], w_a_up[l], w_g_up[l], k_k[l], k_a[l], r_k[l],
                  lnx_g[l], lnx_b[l], g_cross[l], w_cq[l], w_co[l], g_mlp[l], w_up[l], w_down[l])
        mk, mv = _mem_kv(mem_prompt, g_mem[l], w_ck[l], w_cv[l])
        xp, sr, sw, ss = _block(
            xp, pos_p,
            jnp.zeros((Bp, RET_HEADS, RET_HD, RET_HD), dt),
            jnp.zeros((Bp, RWKV_HEADS, RWKV_HD, RWKV_HD), dt),
            jnp.zeros((Bp, RWKV_IN_W), dt),
            mk, mv, *shared)
        ret_p.append(sr); wkv_p.append(sw); sh_p.append(ss); mk_p.append(mk); mv_p.append(mv)
        xs, sr2, sw2, ss2 = _block(
            xs, pos_s, state_ret[l], state_wkv[l], state_shift[l],
            cache_mem_k[l], cache_mem_v[l], *shared)
        ret_s.append(sr2); wkv_s.append(sw2); sh_s.append(ss2)

    y_prompt = _rmsnorm(xp, g_final)
    y_sample = _rmsnorm(xs, g_final)
    return (y_prompt, y_sample,
            jnp.stack(ret_p), jnp.stack(wkv_p), jnp.stack(sh_p), jnp.stack(mk_p), jnp.stack(mv_p),
            jnp.stack(ret_s), jnp.stack(wkv_s), jnp.stack(sh_s))
```

```python
import functools
import math

import jax
import jax.numpy as jnp
from jax import lax
from jax.experimental import pallas as pl
from jax.experimental.pallas import tpu as pltpu

F32 = jnp.float32
BF16 = jnp.bfloat16

D_MODEL = 1024
PAST_LEN = 16384
RET_HEADS = 4
RET_HD = 128
RET_W = RET_HEADS * RET_HD
RET_CHUNK = 128
RET_GN_EPS = 1e-5
ROPE_BASE = 10000.0
RWKV_HEADS = 8
RWKV_HD = 64
RWKV_W = RWKV_HEADS * RWKV_HD
DECAY_LORA = 64
AAA_LORA = 64
GATE_LORA = 128
RWKV_GN_EPS = 64e-5
RWKV_IN_W = 3 * RWKV_W + DECAY_LORA + AAA_LORA + GATE_LORA
GATE_W = 2 * D_MODEL
O_RET = 2 * D_MODEL
O_RWKV = O_RET + 4 * RET_W
IN_W = O_RWKV + RWKV_IN_W
MEM_LEN = 256
X_HEADS = 4
X_HD = D_MODEL // X_HEADS
D_FF = 4 * D_MODEL
RMS_EPS = 1e-6

V7X_VMEM_BYTES = 64 * 1024 * 1024
VMEM_CAP_BYTES = V7X_VMEM_BYTES - 8 * 1024 * 1024
ROW_TILE = 512
RWKV_CHUNK = 64


def _cparams(sem, vmem_bytes):
    return pltpu.CompilerParams(dimension_semantics=sem, vmem_limit_bytes=int(min(vmem_bytes, VMEM_CAP_BYTES)))


def _nbytes(shape, dtype=F32):
    return math.prod(shape) * jnp.dtype(dtype).itemsize


def _nn(a, b):
    return jnp.dot(a.astype(BF16), b.astype(BF16), preferred_element_type=F32)


def _nt(a, b):
    return lax.dot_general(a.astype(BF16), b.astype(BF16), (((1,), (1,)), ((), ())), preferred_element_type=F32)


def _tn(a, b):
    return lax.dot_general(a.astype(BF16), b.astype(BF16), (((0,), (0,)), ((), ())), preferred_element_type=F32)


def _split(a):
    hi = a.astype(BF16)
    lo = (a - hi.astype(F32)).astype(BF16)
    return hi, lo


def _dg3(a, b, dims):
    ah, al = _split(a)
    bh, bl = _split(b)
    dn = (dims, ((), ()))
    return (lax.dot_general(ah, bh, dn, preferred_element_type=F32)
            + lax.dot_general(ah, bl, dn, preferred_element_type=F32)
            + lax.dot_general(al, bh, dn, preferred_element_type=F32))


def _nn3(a, b):
    return _dg3(a, b, ((1,), (0,)))


def _nt3(a, b):
    return _dg3(a, b, ((1,), (1,)))


def _tn3(a, b):
    return _dg3(a, b, ((0,), (0,)))


def _rms(x, g):
    return x * lax.rsqrt(jnp.mean(x * x, axis=-1, keepdims=True) + RMS_EPS) * g


def _head_norm(o, eps):
    mu = jnp.mean(o, axis=-1, keepdims=True)
    d = o - mu
    var = jnp.mean(d * d, axis=-1, keepdims=True)
    return d * lax.rsqrt(var + eps)


def _full(shape):
    zeros = (0,) * len(shape)
    return pl.BlockSpec(shape, lambda *_: zeros, pipeline_mode=pl.Buffered(1))


def _vmem(pipelined, resident, temps):
    return 2 * pipelined + resident + temps


def _in_proj_kernel(x_ref, g_ref, w_ref, zg_ref, zr_ref, zw_ref):
    h = _rms(x_ref[...], g_ref[...]).astype(BF16)
    zg_ref[...] = jnp.dot(h, w_ref[:, 0:O_RET], preferred_element_type=F32)
    zr_ref[...] = jnp.dot(h, w_ref[:, O_RET:O_RWKV], preferred_element_type=F32)
    zw_ref[...] = jnp.dot(h, w_ref[:, O_RWKV:IN_W], preferred_element_type=F32)


def _in_proj(x, g, w_in):
    n = x.shape[0]
    tm = min(ROW_TILE, n)
    vmem = _vmem(_nbytes((tm, D_MODEL + IN_W)), _nbytes((D_MODEL, IN_W), BF16), _nbytes((tm, 4 * RET_W + D_MODEL)))
    return pl.pallas_call(
        _in_proj_kernel,
        grid=(n // tm,),
        in_specs=[pl.BlockSpec((tm, D_MODEL), lambda i: (i, 0)), _full((1, D_MODEL)), _full((D_MODEL, IN_W))],
        out_specs=[pl.BlockSpec((tm, GATE_W), lambda i: (i, 0)),
                   pl.BlockSpec((tm, 4 * RET_W), lambda i: (i, 0)),
                   pl.BlockSpec((tm, RWKV_IN_W), lambda i: (i, 0))],
        out_shape=[jax.ShapeDtypeStruct((n, GATE_W), F32), jax.ShapeDtypeStruct((n, 4 * RET_W), F32),
                   jax.ShapeDtypeStruct((n, RWKV_IN_W), F32)],
        compiler_params=_cparams(("parallel",), vmem),
        name="in_proj",
    )(x, g, w_in)


def _norm_proj2_kernel(x_ref, g_ref, wa_ref, wb_ref, oa_ref, ob_ref):
    h = _rms(x_ref[...], g_ref[...]).astype(BF16)
    oa_ref[...] = jnp.dot(h, wa_ref[...], preferred_element_type=F32)
    ob_ref[...] = jnp.dot(h, wb_ref[...], preferred_element_type=F32)


def _norm_proj2(x, g, wa, wb):
    n = x.shape[0]
    tm = min(ROW_TILE, n)
    vmem = _vmem(3 * _nbytes((tm, D_MODEL)), 2 * _nbytes((D_MODEL, D_MODEL), BF16), 2 * _nbytes((tm, D_MODEL)))
    row = pl.BlockSpec((tm, D_MODEL), lambda i: (i, 0))
    return pl.pallas_call(
        _norm_proj2_kernel,
        grid=(n // tm,),
        in_specs=[row, _full((1, D_MODEL)), _full((D_MODEL, D_MODEL)), _full((D_MODEL, D_MODEL))],
        out_specs=[row, row],
        out_shape=[jax.ShapeDtypeStruct((n, D_MODEL), F32)] * 2,
        compiler_params=_cparams(("parallel",), vmem),
        name="mem_kv",
    )(x, g, wa, wb)


def _loop_bc(bb, nch, body):
    if bb == 1 and nch == 1:
        body(0, 0)
    elif nch == 1:
        lax.fori_loop(0, bb, lambda b, c: (body(b, 0), c)[1], 0)
    elif bb == 1:
        lax.fori_loop(0, nch, lambda i, c: (body(0, i), c)[1], 0)
    else:
        lax.fori_loop(0, bb * nch, lambda j, c: (body(j // nch, j % nch), c)[1], 0)


def _ret_kernel(zr_ref, s0_ref, cos_ref, sin_ref, dm_ref, qd_ref, kd_ref, cd_ref, o_ref, s_ref, *, bb, tt, chunk):
    @pl.when(pl.program_id(1) == 0)
    def _():
        s_ref[...] = s0_ref[...]

    def body(b, i):
        r0 = i * chunk if isinstance(i, int) else pl.multiple_of(i * chunk, chunk)
        rows = pl.ds(r0, chunk)
        cos = cos_ref[rows, :]
        sin = sin_ref[rows, :]
        for h in range(RET_HEADS):
            q = zr_ref[b, rows, h * RET_HD:(h + 1) * RET_HD]
            k = zr_ref[b, rows, RET_W + h * RET_HD:RET_W + (h + 1) * RET_HD]
            v = zr_ref[b, rows, 2 * RET_W + h * RET_HD:2 * RET_W + (h + 1) * RET_HD]
            g = zr_ref[b, rows, 3 * RET_W + h * RET_HD:3 * RET_W + (h + 1) * RET_HD]
            q = q * cos + pltpu.roll(q, RET_HD // 2, axis=1) * sin
            k = (k * cos + pltpu.roll(k, RET_HD // 2, axis=1) * sin) * (RET_HD ** -0.5)
            s = s_ref[b, h]
            sc = _nt(q, k) * dm_ref[h]
            o = _nn(sc, v) + _nn(q, s) * qd_ref[h]
            s_ref[b, h] = s * cd_ref[h] + _tn(k * kd_ref[h], v)
            o_ref[b, rows, h * RET_HD:(h + 1) * RET_HD] = _head_norm(o, RET_GN_EPS) * (g * jax.nn.sigmoid(g))

    _loop_bc(bb, tt // chunk, body)


def _ret_consts(chunk):
    lg = jnp.log1p(-jnp.exp2(-5.0 - jnp.arange(RET_HEADS, dtype=F32)))
    idx = jnp.arange(chunk, dtype=F32)
    diff = idx[:, None] - idx[None, :]
    dmask = jnp.where(diff[None] >= 0, jnp.exp(jnp.maximum(diff, 0.0)[None] * lg[:, None, None]), 0.0)
    q_dec = jnp.exp((idx + 1.0)[None, :] * lg[:, None])
    k_dec = jnp.exp((chunk - 1.0 - idx)[None, :] * lg[:, None])
    c_dec = jnp.exp(chunk * lg)
    bc = lambda t: jnp.broadcast_to(t[:, :, None], (RET_HEADS, t.shape[1], RET_HD))
    return dmask, bc(q_dec), bc(k_dec), bc(c_dec[:, None])


def _rope_tables(pos):
    half = RET_HD // 2
    inv = 1.0 / (ROPE_BASE ** (jnp.arange(half, dtype=F32) / half))
    ang = pos.astype(F32)[:, None] * inv[None, :]
    cos, sin = jnp.cos(ang), jnp.sin(ang)
    return jnp.concatenate([cos, cos], axis=1), jnp.concatenate([-sin, sin], axis=1)


def _retention(zr, s0, pos, bb, tt):
    nb, t, _ = zr.shape
    chunk = math.gcd(t, RET_CHUNK)
    cos, sin = _rope_tables(pos)
    dmask, q_dec, k_dec, c_dec = _ret_consts(chunk)
    st_spec = pl.BlockSpec((bb, RET_HEADS, RET_HD, RET_HD), lambda b, j: (b, 0, 0, 0))
    tab_spec = pl.BlockSpec((tt, RET_HD), lambda b, j: (j, 0))
    vmem = _vmem(_nbytes((bb, tt, 5 * RET_W)) + 2 * _nbytes((bb, RET_HEADS, RET_HD, RET_HD)) + 2 * _nbytes((tt, RET_HD)),
                 _nbytes(dmask.shape) + 3 * _nbytes(q_dec.shape), 16 * _nbytes((chunk, 4 * RET_W)))
    return pl.pallas_call(
        functools.partial(_ret_kernel, bb=bb, tt=tt, chunk=chunk),
        grid=(nb // bb, t // tt),
        in_specs=[pl.BlockSpec((bb, tt, 4 * RET_W), lambda b, j: (b, j, 0)), st_spec, tab_spec, tab_spec,
                  _full(dmask.shape), _full(q_dec.shape), _full(k_dec.shape), _full(c_dec.shape)],
        out_specs=[pl.BlockSpec((bb, tt, RET_W), lambda b, j: (b, j, 0)), st_spec],
        out_shape=[jax.ShapeDtypeStruct((nb, t, RET_W), F32), jax.ShapeDtypeStruct(s0.shape, F32)],
        compiler_params=_cparams(("parallel", "arbitrary"), vmem),
        name="retention",
    )(zr, s0, cos, sin, dmask, q_dec, k_dec, c_dec)


def _wkv_kernel(zw_ref, sh0_ref, s0_ref, mu_ref, w0_ref, a0_ref, kk_ref, ka_ref, rk_ref, lng_ref, lnb_ref,
                wd_ref, wa_ref, wg_ref, tri_ref, o_ref, s_ref, sh_ref, *, bb, tt, chunk):
    @pl.when(pl.program_id(1) == 0)
    def _():
        s_ref[...] = s0_ref[...]
        sh_ref[...] = sh0_ref[...]

    C = chunk
    n_sq = max(int(math.log2(C)) - 1, 0)
    row_id = lax.broadcasted_iota(jnp.int32, (C, C), 0)
    col_id = lax.broadcasted_iota(jnp.int32, (C, C), 1)
    strict = row_id > col_id
    incl = row_id >= col_id
    eye = jnp.where(row_id == col_id, 1.0, 0.0).astype(F32)
    first_row = lax.broadcasted_iota(jnp.int32, (C, RWKV_IN_W), 0) == 0

    def body(b, i):
        r0 = i * C if isinstance(i, int) else pl.multiple_of(i * C, C)
        rows = pl.ds(r0, C)
        pw = zw_ref[b, rows, :]
        prev = jnp.where(first_row, sh_ref[b], pltpu.roll(pw, 1, axis=0))
        sh_ref[b] = pw[C - 1:C, :]
        xm = pw + mu_ref[...] * (prev - pw)
        r = xm[:, 0:RWKV_W]
        kb = xm[:, RWKV_W:2 * RWKV_W]
        vb = xm[:, 2 * RWKV_W:3 * RWKV_W]
        o1 = 3 * RWKV_W
        wl = xm[:, o1:o1 + DECAY_LORA]
        al = xm[:, o1 + DECAY_LORA:o1 + DECAY_LORA + AAA_LORA]
        gl = xm[:, o1 + DECAY_LORA + AAA_LORA:RWKV_IN_W]
        logw = -math.exp(-0.5) * jax.nn.sigmoid(w0_ref[...] + _nn(jnp.tanh(wl), wd_ref[...]))
        a = jax.nn.sigmoid(a0_ref[...] + _nn(al, wa_ref[...]))
        g = _nn(jax.nn.sigmoid(gl), wg_ref[...])
        l1 = logw.astype(BF16)
        rem = logw - l1.astype(F32)
        l2 = rem.astype(BF16)
        l3 = (rem - l2.astype(F32)).astype(BF16)
        tri = tri_ref[...]
        cum = (jnp.dot(tri, l1, preferred_element_type=F32) + jnp.dot(tri, l2, preferred_element_type=F32)
               + jnp.dot(tri, l3, preferred_element_type=F32))
        outs = []
        for h in range(RWKV_HEADS):
            sl = slice(h * RWKV_HD, (h + 1) * RWKV_HD)
            r_h, kb_h, v_h, a_h, cum_h, logw_h = r[:, sl], kb[:, sl], vb[:, sl], a[:, sl], cum[:, sl], logw[:, sl]
            kk = kb_h * kk_ref[:, sl]
            kk = kk / jnp.maximum(jnp.sqrt(jnp.sum(kk * kk, axis=-1, keepdims=True)), 1e-12)
            k_h = kb_h * (1.0 + (a_h - 1.0) * ka_ref[:, sl])
            b_h = kk * a_h
            cum_end = cum_h[C - 1:C, :]
            e_neg = jnp.exp(-cum_h)
            e_end = jnp.exp(cum_end - cum_h)
            at = -kk * jnp.exp(cum_h - logw_h)
            rt = r_h * jnp.exp(cum_h)
            ar = jnp.concatenate([at, rt], axis=0)
            mb = _nt3(ar, b_h * e_neg)
            mk = _nt3(ar, k_h * e_neg)
            low = jnp.where(strict, mb[:C], 0.0)
            tinv = eye + low
            if n_sq:
                x = _nn3(low, low)
            for j in range(n_sq):
                if j < n_sq - 1:
                    xt = _nn3(jnp.concatenate([x, tinv], axis=0), x)
                    x = xt[:C]
                    tinv = tinv + xt[C:]
                else:
                    tinv = tinv + _nn3(tinv, x)
            s = s_ref[b, h]
            ars = _nt3(ar, s)
            u = _nn3(tinv, ars[:C] + _nn3(jnp.where(strict, mk[:C], 0.0), v_h))
            o = ars[C:] + _nn3(jnp.where(incl, mb[C:], 0.0), u) + _nn3(jnp.where(incl, mk[C:], 0.0), v_h)
            s_ref[b, h] = s * jnp.exp(cum_end) + _tn3(u, b_h * e_end) + _tn3(v_h, k_h * e_end)
            o = _head_norm(o, RWKV_GN_EPS) * lng_ref[:, sl] + lnb_ref[:, sl]
            o = o + jnp.sum(r_h * k_h * rk_ref[:, sl], axis=-1, keepdims=True) * v_h
            outs.append(o * g[:, sl])
        o_ref[b, rows, :] = jnp.concatenate(outs, axis=1)

    _loop_bc(bb, tt // C, body)


def _rwkv7(zw, sh0, s0, p, bb, tt):
    nb, t, _ = zw.shape
    chunk = math.gcd(t, RWKV_CHUNK)
    tri = jnp.tril(jnp.ones((chunk, chunk), BF16))
    st_spec = pl.BlockSpec((bb, RWKV_HEADS, RWKV_HD, RWKV_HD), lambda b, j: (b, 0, 0, 0))
    sh_spec = pl.BlockSpec((bb, 1, RWKV_IN_W), lambda b, j: (b, 0, 0))
    vec = _full((1, RWKV_W))
    vmem = _vmem(_nbytes((bb, tt, RWKV_IN_W + RWKV_W)) + 2 * _nbytes((bb, RWKV_HEADS, RWKV_HD, RWKV_HD))
                 + 2 * _nbytes((bb, 8, RWKV_IN_W)),
                 9 * _nbytes((8, RWKV_IN_W)) + _nbytes((DECAY_LORA + AAA_LORA + GATE_LORA + chunk, RWKV_W), BF16),
                 32 * _nbytes((chunk, RWKV_IN_W)))
    return pl.pallas_call(
        functools.partial(_wkv_kernel, bb=bb, tt=tt, chunk=chunk),
        grid=(nb // bb, t // tt),
        in_specs=[pl.BlockSpec((bb, tt, RWKV_IN_W), lambda b, j: (b, j, 0)), sh_spec, st_spec,
                  _full((1, RWKV_IN_W)), vec, vec, vec, vec, vec, vec, vec,
                  _full((DECAY_LORA, RWKV_W)), _full((AAA_LORA, RWKV_W)), _full((GATE_LORA, RWKV_W)),
                  _full((chunk, chunk))],
        out_specs=[pl.BlockSpec((bb, tt, RWKV_W), lambda b, j: (b, j, 0)), st_spec, sh_spec],
        out_shape=[jax.ShapeDtypeStruct((nb, t, RWKV_W), F32), jax.ShapeDtypeStruct(s0.shape, F32),
                   jax.ShapeDtypeStruct(sh0.shape, F32)],
        compiler_params=_cparams(("parallel", "arbitrary"), vmem),
        name="rwkv7",
    )(zw, sh0, s0, p["mu"], p["w0"], p["a0"], p["k_k"], p["k_a"], p["r_k"], p["lnx_g"], p["lnx_b"],
      p["w_decay_up"], p["w_a_up"], p["w_g_up"], tri)


def _merge_kernel(x_ref, zg_ref, oa_ref, ob_ref, wa_ref, wb_ref, wo_ref, o_ref):
    ga = jax.nn.sigmoid(zg_ref[:, 0:D_MODEL])
    gb = jax.nn.sigmoid(zg_ref[:, D_MODEL:GATE_W])
    merged = ga * _nn(oa_ref[...], wa_ref[...]) + gb * _nn(ob_ref[...], wb_ref[...])
    o_ref[...] = x_ref[...] + _nn(merged, wo_ref[...])


def _merge(x, zg, o_ret, o_rwkv, wa, wb, wo):
    n = x.shape[0]
    tm = min(ROW_TILE, n)
    row = lambda w: pl.BlockSpec((tm, w), lambda i: (i, 0))
    vmem = _vmem(_nbytes((tm, 2 * D_MODEL + GATE_W + RET_W + RWKV_W)), _nbytes((2 * D_MODEL, D_MODEL), BF16),
                 6 * _nbytes((tm, D_MODEL)))
    return pl.pallas_call(
        _merge_kernel,
        grid=(n // tm,),
        in_specs=[row(D_MODEL), row(GATE_W), row(RET_W), row(RWKV_W),
                  _full((RET_W, D_MODEL)), _full((RWKV_W, D_MODEL)), _full((D_MODEL, D_MODEL))],
        out_specs=row(D_MODEL),
        out_shape=jax.ShapeDtypeStruct((n, D_MODEL), F32),
        compiler_params=_cparams(("parallel",), vmem),
        name="merge",
    )(x, zg, o_ret, o_rwkv, wa, wb, wo)


def _cross_kernel(x_ref, mk_ref, mv_ref, g_ref, wq_ref, wo_ref, o_ref, q_scr, ox_scr, *, bb, tq):
    rows = bb * tq
    x = x_ref[...].reshape(rows, D_MODEL)
    q_scr[...] = _nn(_rms(x, g_ref[...]), wq_ref[...])

    def body(b, _):
        r0 = b * tq if isinstance(b, int) else pl.multiple_of(b * tq, tq)
        for h in range(X_HEADS):
            sl = slice(h * X_HD, (h + 1) * X_HD)
            sc = _nt(q_scr[pl.ds(r0, tq), sl], mk_ref[b, :, sl]) * (X_HD ** -0.5)
            e = jnp.exp(sc - jnp.max(sc, axis=-1, keepdims=True))
            att = e / jnp.sum(e, axis=-1, keepdims=True)
            ox_scr[pl.ds(r0, tq), sl] = _nn(att, mv_ref[b, :, sl])

    _loop_bc(bb, 1, body)
    o_ref[...] = (x + _nn(ox_scr[...], wo_ref[...])).reshape(bb, tq, D_MODEL)


def _cross(x, mem_k, mem_v, g, wq, wo, bb, tq):
    nb, t, _ = x.shape
    x_spec = pl.BlockSpec((bb, tq, D_MODEL), lambda b, j: (b, j, 0))
    m_spec = pl.BlockSpec((bb, MEM_LEN, D_MODEL), lambda b, j: (b, 0, 0))
    vmem = _vmem(2 * _nbytes((bb, tq, D_MODEL)) + 2 * _nbytes((bb, MEM_LEN, D_MODEL)),
                 2 * _nbytes((D_MODEL, D_MODEL), BF16) + 2 * _nbytes((bb * tq, D_MODEL)),
                 6 * _nbytes((bb * tq, D_MODEL)) + 2 * _nbytes((MEM_LEN, D_MODEL)))
    return pl.pallas_call(
        functools.partial(_cross_kernel, bb=bb, tq=tq),
        grid=(nb // bb, t // tq),
        in_specs=[x_spec, m_spec, m_spec, _full((1, D_MODEL)), _full((D_MODEL, D_MODEL)), _full((D_MODEL, D_MODEL))],
        out_specs=x_spec,
        out_shape=jax.ShapeDtypeStruct(x.shape, F32),
        scratch_shapes=[pltpu.VMEM((bb * tq, D_MODEL), F32), pltpu.VMEM((bb * tq, D_MODEL), F32)],
        compiler_params=_cparams(("parallel", "arbitrary"), vmem),
        name="cross_attn",
    )(x, mem_k, mem_v, g, wq, wo)


MLP_FF_CHUNK = 1024


def _mlp_kernel(x_ref, g_ref, wu_ref, wd_ref, gf_ref, o_ref, *, final_norm):
    x = x_ref[...]
    h = _rms(x, g_ref[...]).astype(BF16)
    acc = x
    for c in range(0, D_FF, MLP_FF_CHUNK):
        u = jnp.maximum(jnp.dot(h, wu_ref[:, c:c + MLP_FF_CHUNK], preferred_element_type=F32), 0.0)
        acc = acc + _nn(u * u, wd_ref[c:c + MLP_FF_CHUNK, :])
    o_ref[...] = _rms(acc, gf_ref[...]) if final_norm else acc


def _mlp(x, g, w_up, w_down, g_final, final_norm):
    n = x.shape[0]
    tm = min(ROW_TILE, n)
    row = pl.BlockSpec((tm, D_MODEL), lambda i: (i, 0))
    vmem = _vmem(2 * _nbytes((tm, D_MODEL)), 2 * _nbytes((D_MODEL, D_FF), BF16),
                 4 * _nbytes((tm, MLP_FF_CHUNK)) + 4 * _nbytes((tm, D_MODEL)))
    return pl.pallas_call(
        functools.partial(_mlp_kernel, final_norm=final_norm),
        grid=(n // tm,),
        in_specs=[row, _full((1, D_MODEL)), _full((D_MODEL, D_FF)), _full((D_FF, D_MODEL)), _full((1, D_MODEL))],
        out_specs=row,
        out_shape=jax.ShapeDtypeStruct((n, D_MODEL), F32),
        compiler_params=_cparams(("parallel",), vmem),
        name="mlp",
    )(x, g, w_up, w_down, g_final)


def _layer(x, pos, s_ret, s_wkv, s_shift, mem_k, mem_v, w, g_final, final_norm, *, seq_block, time_block):
    nb, t, _ = x.shape
    n = nb * t
    xf = x.reshape(n, D_MODEL)
    zg, zr, zw = _in_proj(xf, w["g_mix"], w["w_in"])
    o_ret, s_ret_new = _retention(zr.reshape(nb, t, 4 * RET_W), s_ret, pos, seq_block, time_block)
    o_wkv, s_wkv_new, shift_new = _rwkv7(zw.reshape(nb, t, RWKV_IN_W), s_shift.reshape(nb, 1, RWKV_IN_W), s_wkv,
                                         w, seq_block, time_block)
    x1 = _merge(xf, zg, o_ret.reshape(n, RET_W), o_wkv.reshape(n, RWKV_W),
                w["w_branch_a"], w["w_branch_b"], w["w_out"])
    x2 = _cross(x1.reshape(nb, t, D_MODEL), mem_k, mem_v, w["g_cross"], w["w_cq"], w["w_co"],
                seq_block, min(time_block, t))
    y = _mlp(x2.reshape(n, D_MODEL), w["g_mlp"], w["w_up"], w["w_down"], g_final, final_norm)
    return y.reshape(nb, t, D_MODEL), s_ret_new, s_wkv_new, shift_new.reshape(nb, RWKV_IN_W)


def _layer_weights(l, g_mix, w_in, w_branch_a, w_branch_b, w_out, mu_shift, w0, w_decay_up, a0, w_a_up, w_g_up,
                   k_k, k_a, r_k, lnx_g, lnx_b, g_cross, w_cq, w_co, g_mlp, w_up, w_down):
    vec = lambda v: v[l].reshape(1, -1).astype(F32)
    mat = lambda m: m[l].astype(BF16)
    return dict(g_mix=vec(g_mix), w_in=mat(w_in), w_branch_a=mat(w_branch_a), w_branch_b=mat(w_branch_b),
                w_out=mat(w_out), mu=vec(mu_shift), w0=vec(w0), w_decay_up=mat(w_decay_up), a0=vec(a0),
                w_a_up=mat(w_a_up), w_g_up=mat(w_g_up), k_k=vec(k_k), k_a=vec(k_a), r_k=vec(r_k), lnx_g=vec(lnx_g),
                lnx_b=vec(lnx_b), g_cross=vec(g_cross), w_cq=mat(w_cq), w_co=mat(w_co), g_mlp=vec(g_mlp),
                w_up=mat(w_up), w_down=mat(w_down))


def kernel(x_prompt, x_sample, mem_prompt, state_ret, state_wkv, state_shift, cache_mem_k, cache_mem_v, g_mix, w_in, w_branch_a, w_branch_b, w_out, mu_shift, w0, w_decay_up, a0, w_a_up, w_g_up, k_k, k_a, r_k, lnx_g, lnx_b, g_cross, g_mem, w_cq, w_ck, w_cv, w_co, g_mlp, w_up, w_down, g_final):
    depth = w_in.shape[0]
    bp, tp, _ = x_prompt.shape
    bs, ts, _ = x_sample.shape
    pos_p = jnp.arange(tp, dtype=jnp.int32)
    pos_s = PAST_LEN + jnp.arange(ts, dtype=jnp.int32)
    gf = g_final.reshape(1, D_MODEL)

    xp, xs = x_prompt, x_sample
    outs_p, outs_s = [], []
    for l in range(depth):
        w = _layer_weights(l, g_mix, w_in, w_branch_a, w_branch_b, w_out, mu_shift, w0, w_decay_up, a0, w_a_up,
                           w_g_up, k_k, k_a, r_k, lnx_g, lnx_b, g_cross, w_cq, w_co, g_mlp, w_up, w_down)
        last = l == depth - 1
        mk, mv = _norm_proj2(mem_prompt.reshape(bp * MEM_LEN, D_MODEL), g_mem[l].reshape(1, D_MODEL),
                             w_ck[l].astype(BF16), w_cv[l].astype(BF16))
        mk = mk.reshape(bp, MEM_LEN, D_MODEL)
        mv = mv.reshape(bp, MEM_LEN, D_MODEL)
        xp, sr, sw, ss = _layer(
            xp, pos_p,
            jnp.zeros((bp, RET_HEADS, RET_HD, RET_HD), F32), jnp.zeros((bp, RWKV_HEADS, RWKV_HD, RWKV_HD), F32),
            jnp.zeros((bp, RWKV_IN_W), F32), mk, mv, w, gf, last,
            seq_block=1, time_block=min(ROW_TILE, tp))
        outs_p.append((sr, sw, ss, mk.reshape(bp, MEM_LEN, X_HEADS, X_HD), mv.reshape(bp, MEM_LEN, X_HEADS, X_HD)))
        xs, sr2, sw2, ss2 = _layer(
            xs, pos_s, state_ret[l], state_wkv[l], state_shift[l],
            cache_mem_k[l].reshape(bs, MEM_LEN, D_MODEL), cache_mem_v[l].reshape(bs, MEM_LEN, D_MODEL), w,
            gf, last, seq_block=8, time_block=ts)
        outs_s.append((sr2, sw2, ss2))

    stack = lambda items, i: jnp.stack([it[i] for it in items])
    return (xp, xs, stack(outs_p, 0), stack(outs_p, 1), stack(outs_p, 2), stack(outs_p, 3), stack(outs_p, 4),
            stack(outs_s, 0), stack(outs_s, 1), stack(outs_s, 2))
```

```python
import functools
import math

import jax
import jax.numpy as jnp
from jax import lax
from jax.experimental import pallas as pl
from jax.experimental.pallas import tpu as pltpu

F32 = jnp.float32
BF16 = jnp.bfloat16

D_MODEL = 1024
PAST_LEN = 16384
RET_HEADS = 4
RET_HD = 128
RET_W = RET_HEADS * RET_HD
RET_CHUNK = 128
RET_GN_EPS = 1e-5
ROPE_BASE = 10000.0
RWKV_HEADS = 8
RWKV_HD = 64
RWKV_W = RWKV_HEADS * RWKV_HD
DECAY_LORA = 64
AAA_LORA = 64
GATE_LORA = 128
RWKV_GN_EPS = 64e-5
RWKV_IN_W = 3 * RWKV_W + DECAY_LORA + AAA_LORA + GATE_LORA
GATE_W = 2 * D_MODEL
O_RET = 2 * D_MODEL
O_RWKV = O_RET + 4 * RET_W
IN_W = O_RWKV + RWKV_IN_W
MEM_LEN = 256
X_HEADS = 4
X_HD = D_MODEL // X_HEADS
D_FF = 4 * D_MODEL
RMS_EPS = 1e-6

V7X_VMEM_BYTES = 64 * 1024 * 1024
VMEM_CAP_BYTES = V7X_VMEM_BYTES - 8 * 1024 * 1024
ROW_TILE = 512
RWKV_CHUNK = 64
RWKV_GROUP = 2


def _cparams(sem, vmem_bytes):
    return pltpu.CompilerParams(dimension_semantics=sem, vmem_limit_bytes=int(min(vmem_bytes, VMEM_CAP_BYTES)))


def _nbytes(shape, dtype=F32):
    return math.prod(shape) * jnp.dtype(dtype).itemsize


def _nn(a, b):
    return jnp.dot(a.astype(BF16), b.astype(BF16), preferred_element_type=F32)


def _nt(a, b):
    return lax.dot_general(a.astype(BF16), b.astype(BF16), (((1,), (1,)), ((), ())), preferred_element_type=F32)


def _tn(a, b):
    return lax.dot_general(a.astype(BF16), b.astype(BF16), (((0,), (0,)), ((), ())), preferred_element_type=F32)


def _split(a):
    hi = a.astype(BF16)
    lo = (a - hi.astype(F32)).astype(BF16)
    return hi, lo


def _dg3(a, b, dims):
    ah, al = _split(a)
    bh, bl = _split(b)
    dn = (dims, ((), ()))
    return (lax.dot_general(ah, bh, dn, preferred_element_type=F32)
            + lax.dot_general(ah, bl, dn, preferred_element_type=F32)
            + lax.dot_general(al, bh, dn, preferred_element_type=F32))


def _nn3(a, b):
    return _dg3(a, b, ((1,), (0,)))


def _nt3(a, b):
    return _dg3(a, b, ((1,), (1,)))


def _tn3(a, b):
    return _dg3(a, b, ((0,), (0,)))


def _rms(x, g):
    return x * lax.rsqrt(jnp.mean(x * x, axis=-1, keepdims=True) + RMS_EPS) * g


def _head_norm(o, eps):
    mu = jnp.mean(o, axis=-1, keepdims=True)
    d = o - mu
    var = jnp.mean(d * d, axis=-1, keepdims=True)
    return d * lax.rsqrt(var + eps)


def _full(shape):
    zeros = (0,) * len(shape)
    return pl.BlockSpec(shape, lambda *_: zeros, pipeline_mode=pl.Buffered(1))


def _vmem(pipelined, resident, temps):
    return 2 * pipelined + resident + temps


def _in_proj_kernel(x_ref, g_ref, w_ref, zg_ref, zr_ref, zw_ref):
    h = _rms(x_ref[...], g_ref[...]).astype(BF16)
    zg_ref[...] = jnp.dot(h, w_ref[:, 0:O_RET], preferred_element_type=F32)
    zr_ref[...] = jnp.dot(h, w_ref[:, O_RET:O_RWKV], preferred_element_type=F32)
    zw_ref[...] = jnp.dot(h, w_ref[:, O_RWKV:IN_W], preferred_element_type=F32)


def _in_proj(x, g, w_in):
    n = x.shape[0]
    tm = min(ROW_TILE, n)
    vmem = _vmem(_nbytes((tm, D_MODEL + IN_W)), _nbytes((D_MODEL, IN_W), BF16), _nbytes((tm, 4 * RET_W + D_MODEL)))
    return pl.pallas_call(
        _in_proj_kernel,
        grid=(n // tm,),
        in_specs=[pl.BlockSpec((tm, D_MODEL), lambda i: (i, 0)), _full((1, D_MODEL)), _full((D_MODEL, IN_W))],
        out_specs=[pl.BlockSpec((tm, GATE_W), lambda i: (i, 0)),
                   pl.BlockSpec((tm, 4 * RET_W), lambda i: (i, 0)),
                   pl.BlockSpec((tm, RWKV_IN_W), lambda i: (i, 0))],
        out_shape=[jax.ShapeDtypeStruct((n, GATE_W), F32), jax.ShapeDtypeStruct((n, 4 * RET_W), F32),
                   jax.ShapeDtypeStruct((n, RWKV_IN_W), F32)],
        compiler_params=_cparams(("parallel",), vmem),
        name="in_proj",
    )(x, g, w_in)


def _norm_proj2_kernel(x_ref, g_ref, wa_ref, wb_ref, oa_ref, ob_ref):
    h = _rms(x_ref[...], g_ref[...]).astype(BF16)
    oa_ref[...] = jnp.dot(h, wa_ref[...], preferred_element_type=F32)
    ob_ref[...] = jnp.dot(h, wb_ref[...], preferred_element_type=F32)


def _norm_proj2(x, g, wa, wb):
    n = x.shape[0]
    tm = min(ROW_TILE, n)
    vmem = _vmem(3 * _nbytes((tm, D_MODEL)), 2 * _nbytes((D_MODEL, D_MODEL), BF16), 2 * _nbytes((tm, D_MODEL)))
    row = pl.BlockSpec((tm, D_MODEL), lambda i: (i, 0))
    return pl.pallas_call(
        _norm_proj2_kernel,
        grid=(n // tm,),
        in_specs=[row, _full((1, D_MODEL)), _full((D_MODEL, D_MODEL)), _full((D_MODEL, D_MODEL))],
        out_specs=[row, row],
        out_shape=[jax.ShapeDtypeStruct((n, D_MODEL), F32)] * 2,
        compiler_params=_cparams(("parallel",), vmem),
        name="mem_kv",
    )(x, g, wa, wb)


def _loop_bc(bb, nch, body):
    if bb == 1 and nch == 1:
        body(0, 0)
    elif nch == 1:
        lax.fori_loop(0, bb, lambda b, c: (body(b, 0), c)[1], 0)
    elif bb == 1:
        lax.fori_loop(0, nch, lambda i, c: (body(0, i), c)[1], 0)
    else:
        lax.fori_loop(0, bb * nch, lambda j, c: (body(j // nch, j % nch), c)[1], 0)


def _ret_kernel(zr_ref, s0_ref, cos_ref, sin_ref, dm_ref, qd_ref, kd_ref, cd_ref, o_ref, s_ref, *, bb, tt, chunk):
    @pl.when(pl.program_id(1) == 0)
    def _():
        s_ref[...] = s0_ref[...]

    def body(b, i):
        r0 = i * chunk if isinstance(i, int) else pl.multiple_of(i * chunk, chunk)
        rows = pl.ds(r0, chunk)
        cos = cos_ref[rows, :]
        sin = sin_ref[rows, :]
        for h in range(RET_HEADS):
            q = zr_ref[b, rows, h * RET_HD:(h + 1) * RET_HD]
            k = zr_ref[b, rows, RET_W + h * RET_HD:RET_W + (h + 1) * RET_HD]
            v = zr_ref[b, rows, 2 * RET_W + h * RET_HD:2 * RET_W + (h + 1) * RET_HD]
            g = zr_ref[b, rows, 3 * RET_W + h * RET_HD:3 * RET_W + (h + 1) * RET_HD]
            q = q * cos + pltpu.roll(q, RET_HD // 2, axis=1) * sin
            k = (k * cos + pltpu.roll(k, RET_HD // 2, axis=1) * sin) * (RET_HD ** -0.5)
            s = s_ref[b, h]
            sc = _nt(q, k) * dm_ref[h]
            o = _nn(sc, v) + _nn(q, s) * qd_ref[h]
            s_ref[b, h] = s * cd_ref[h] + _tn(k * kd_ref[h], v)
            o_ref[b, rows, h * RET_HD:(h + 1) * RET_HD] = _head_norm(o, RET_GN_EPS) * (g * jax.nn.sigmoid(g))

    _loop_bc(bb, tt // chunk, body)


def _ret_consts(chunk):
    lg = jnp.log1p(-jnp.exp2(-5.0 - jnp.arange(RET_HEADS, dtype=F32)))
    idx = jnp.arange(chunk, dtype=F32)
    diff = idx[:, None] - idx[None, :]
    dmask = jnp.where(diff[None] >= 0, jnp.exp(jnp.maximum(diff, 0.0)[None] * lg[:, None, None]), 0.0)
    q_dec = jnp.exp((idx + 1.0)[None, :] * lg[:, None])
    k_dec = jnp.exp((chunk - 1.0 - idx)[None, :] * lg[:, None])
    c_dec = jnp.exp(chunk * lg)
    bc = lambda t: jnp.broadcast_to(t[:, :, None], (RET_HEADS, t.shape[1], RET_HD))
    return dmask, bc(q_dec), bc(k_dec), bc(c_dec[:, None])


def _rope_tables(pos):
    half = RET_HD // 2
    inv = 1.0 / (ROPE_BASE ** (jnp.arange(half, dtype=F32) / half))
    ang = pos.astype(F32)[:, None] * inv[None, :]
    cos, sin = jnp.cos(ang), jnp.sin(ang)
    return jnp.concatenate([cos, cos], axis=1), jnp.concatenate([-sin, sin], axis=1)


def _retention(zr, s0, pos, bb, tt):
    nb, t, _ = zr.shape
    chunk = math.gcd(t, RET_CHUNK)
    cos, sin = _rope_tables(pos)
    dmask, q_dec, k_dec, c_dec = _ret_consts(chunk)
    st_spec = pl.BlockSpec((bb, RET_HEADS, RET_HD, RET_HD), lambda b, j: (b, 0, 0, 0))
    tab_spec = pl.BlockSpec((tt, RET_HD), lambda b, j: (j, 0))
    vmem = _vmem(_nbytes((bb, tt, 5 * RET_W)) + 2 * _nbytes((bb, RET_HEADS, RET_HD, RET_HD)) + 2 * _nbytes((tt, RET_HD)),
                 _nbytes(dmask.shape) + 3 * _nbytes(q_dec.shape), 16 * _nbytes((chunk, 4 * RET_W)))
    return pl.pallas_call(
        functools.partial(_ret_kernel, bb=bb, tt=tt, chunk=chunk),
        grid=(nb // bb, t // tt),
        in_specs=[pl.BlockSpec((bb, tt, 4 * RET_W), lambda b, j: (b, j, 0)), st_spec, tab_spec, tab_spec,
                  _full(dmask.shape), _full(q_dec.shape), _full(k_dec.shape), _full(c_dec.shape)],
        out_specs=[pl.BlockSpec((bb, tt, RET_W), lambda b, j: (b, j, 0)), st_spec],
        out_shape=[jax.ShapeDtypeStruct((nb, t, RET_W), F32), jax.ShapeDtypeStruct(s0.shape, F32)],
        compiler_params=_cparams(("parallel", "arbitrary"), vmem),
        name="retention",
    )(zr, s0, cos, sin, dmask, q_dec, k_dec, c_dec)


def _wkv_kernel(zw_ref, sh0_ref, s0_ref, mu_ref, w0_ref, a0_ref, kk_ref, ka_ref, rk_ref, lng_ref, lnb_ref,
                wd_ref, wa_ref, wg_ref, tri_ref, seg_ref, o_ref, s_ref, sh_ref, *, bb, tt, chunk, group):
    @pl.when(pl.program_id(1) == 0)
    def _():
        s_ref[...] = s0_ref[...]
        sh_ref[...] = sh0_ref[...]

    C = chunk
    n_sq = max(int(math.log2(C)) - 1, 0)
    row_id = lax.broadcasted_iota(jnp.int32, (C, C), 0)
    col_id = lax.broadcasted_iota(jnp.int32, (C, C), 1)
    strict = row_id > col_id
    incl = row_id >= col_id
    eye = jnp.where(row_id == col_id, 1.0, 0.0).astype(F32)
    first_row = lax.broadcasted_iota(jnp.int32, (C, RWKV_IN_W), 0) == 0
    heads = range(RWKV_HEADS)
    hsl = [slice(h * RWKV_HD, (h + 1) * RWKV_HD) for h in heads]
    half = RWKV_W // 2

    def seg_sum(t):
        seg = seg_ref[...]
        return jnp.concatenate([_nn(t[:, :half], seg), _nn(t[:, half:], seg)], axis=1)

    def prepare(b, r0):
        pw = zw_ref[b, pl.ds(r0, C), :]
        prev = jnp.where(first_row, sh_ref[b], pltpu.roll(pw, 1, axis=0))
        sh_ref[b] = pw[C - 1:C, :]
        xm = pw + mu_ref[...] * (prev - pw)
        r = xm[:, 0:RWKV_W]
        kb = xm[:, RWKV_W:2 * RWKV_W]
        vb = xm[:, 2 * RWKV_W:3 * RWKV_W]
        o1 = 3 * RWKV_W
        wl = xm[:, o1:o1 + DECAY_LORA]
        al = xm[:, o1 + DECAY_LORA:o1 + DECAY_LORA + AAA_LORA]
        gl = xm[:, o1 + DECAY_LORA + AAA_LORA:RWKV_IN_W]
        logw = -math.exp(-0.5) * jax.nn.sigmoid(w0_ref[...] + _nn(jnp.tanh(wl), wd_ref[...]))
        a = jax.nn.sigmoid(a0_ref[...] + _nn(al, wa_ref[...]))
        g = _nn(jax.nn.sigmoid(gl), wg_ref[...])
        l1 = logw.astype(BF16)
        rem = logw - l1.astype(F32)
        l2 = rem.astype(BF16)
        l3 = (rem - l2.astype(F32)).astype(BF16)
        tri = tri_ref[...]
        cum = (jnp.dot(tri, l1, preferred_element_type=F32) + jnp.dot(tri, l2, preferred_element_type=F32)
               + jnp.dot(tri, l3, preferred_element_type=F32))
        cum_end = cum[C - 1:C, :]
        kk = kb * kk_ref[...]
        kk = kk / jnp.maximum(jnp.sqrt(seg_sum(kk * kk)), 1e-12)
        km = kb * (1.0 + (a - 1.0) * ka_ref[...])
        bv = kk * a
        e_neg = jnp.exp(-cum)
        e_end = jnp.exp(cum_end - cum)
        return dict(at=-kk * jnp.exp(cum - logw), rt=r * jnp.exp(cum), bt=bv * e_neg, kt=km * e_neg,
                    bh=bv * e_end, kh=km * e_end, v=vb, g=g, p_end=jnp.exp(cum_end),
                    bonus=seg_sum(r * km * rk_ref[...]))

    def triangular(ps):
        ph = [(p, s) for p in ps for s in hsl]
        ar = [jnp.concatenate([p["at"][:, s], p["rt"][:, s]], axis=0) for p, s in ph]
        m1 = [_nt(a, jnp.concatenate([p["bt"][:, s], p["kt"][:, s]], axis=0)) for a, (p, s) in zip(ar, ph)]
        low = [jnp.where(strict, m[:C, :C], 0.0) for m in m1]
        rab = [jnp.where(incl, m[C:, :C], 0.0) for m in m1]
        akk = [jnp.concatenate([jnp.where(strict, m[:C, C:], 0.0), jnp.where(incl, m[C:, C:], 0.0)], axis=0)
               for m in m1]
        akv = [_nn(k, p["v"][:, s]) for k, (p, s) in zip(akk, ph)]
        tinv = [eye + l for l in low]
        if n_sq:
            x = [_nn(l, l) for l in low]
        for j in range(n_sq):
            if j < n_sq - 1:
                xt = [_nn(jnp.concatenate([xi, ti], axis=0), xi) for xi, ti in zip(x, tinv)]
                x = [t[:C] for t in xt]
                tinv = [ti + t[C:] for ti, t in zip(tinv, xt)]
            else:
                tinv = [ti + _nn(ti, xi) for ti, xi in zip(tinv, x)]
        nh = RWKV_HEADS
        return [dict(ar=ar[k * nh:(k + 1) * nh], rab=rab[k * nh:(k + 1) * nh], akv=akv[k * nh:(k + 1) * nh],
                     tinv=tinv[k * nh:(k + 1) * nh]) for k in range(len(ps))]

    def advance(items, ps, qs):
        ih = [(b, p, q, h) for (b, _), p, q in zip(items, ps, qs) for h in heads]
        ars = [_nt(q["ar"][h], s_ref[b, h]) for b, p, q, h in ih]
        u = [_nn(q["tinv"][h], a[:C] + q["akv"][h][:C]) for a, (b, p, q, h) in zip(ars, ih)]
        o = [a[C:] + q["akv"][h][C:] + _nn(q["rab"][h], ui) for a, ui, (b, p, q, h) in zip(ars, u, ih)]
        upd = [_tn(jnp.concatenate([ui, p["v"][:, hsl[h]]], axis=0),
                   jnp.concatenate([p["bh"][:, hsl[h]], p["kh"][:, hsl[h]]], axis=0))
               for ui, (b, p, q, h) in zip(u, ih)]
        for up, (b, p, q, h) in zip(upd, ih):
            s_ref[b, h] = s_ref[b, h] * p["p_end"][:, hsl[h]] + up
        for k, ((b, r0), p) in enumerate(zip(items, ps)):
            ok = jnp.concatenate(o[k * RWKV_HEADS:(k + 1) * RWKV_HEADS], axis=1)
            d = ok - seg_sum(ok) * (1.0 / RWKV_HD)
            var = seg_sum(d * d) * (1.0 / RWKV_HD)
            ok = d * lax.rsqrt(var + RWKV_GN_EPS) * lng_ref[...] + lnb_ref[...]
            o_ref[b, pl.ds(r0, C), :] = (ok + p["bonus"] * p["v"]) * p["g"]

    nch = tt // C
    assert bb == 1 or nch == 1, "a block holds either one sequence or one chunk per sequence"
    assert (bb * nch) % group == 0

    def body(j, carry):
        if nch == 1:
            items = [(j * group + k, 0) for k in range(group)]
        else:
            items = [(0, pl.multiple_of((j * group + k) * C, C)) for k in range(group)]
        ps = [prepare(b, r0) for b, r0 in items]
        qs = triangular(ps)
        if nch == 1:
            advance(items, ps, qs)
        else:
            for item, p, q in zip(items, ps, qs):
                advance([item], [p], [q])
        return carry

    lax.fori_loop(0, bb * nch // group, body, 0)


def _rwkv7(zw, sh0, s0, p, bb, tt, group):
    nb, t, _ = zw.shape
    chunk = math.gcd(t, RWKV_CHUNK)
    tri = jnp.tril(jnp.ones((chunk, chunk), BF16))
    lane_head = jnp.arange(RWKV_W // 2) // RWKV_HD
    seg = (lane_head[:, None] == lane_head[None, :]).astype(BF16)
    st_spec = pl.BlockSpec((bb, RWKV_HEADS, RWKV_HD, RWKV_HD), lambda b, j: (b, 0, 0, 0))
    sh_spec = pl.BlockSpec((bb, 1, RWKV_IN_W), lambda b, j: (b, 0, 0))
    vec = _full((1, RWKV_W))
    vmem = _vmem(_nbytes((bb, tt, RWKV_IN_W + RWKV_W)) + 2 * _nbytes((bb, RWKV_HEADS, RWKV_HD, RWKV_HD))
                 + 2 * _nbytes((bb, 8, RWKV_IN_W)),
                 9 * _nbytes((8, RWKV_IN_W)) + _nbytes((DECAY_LORA + AAA_LORA + GATE_LORA + chunk, RWKV_W), BF16)
                 + _nbytes(seg.shape, BF16),
                 group * 48 * _nbytes((chunk, RWKV_IN_W)))
    return pl.pallas_call(
        functools.partial(_wkv_kernel, bb=bb, tt=tt, chunk=chunk, group=group),
        grid=(nb // bb, t // tt),
        in_specs=[pl.BlockSpec((bb, tt, RWKV_IN_W), lambda b, j: (b, j, 0)), sh_spec, st_spec,
                  _full((1, RWKV_IN_W)), vec, vec, vec, vec, vec, vec, vec,
                  _full((DECAY_LORA, RWKV_W)), _full((AAA_LORA, RWKV_W)), _full((GATE_LORA, RWKV_W)),
                  _full((chunk, chunk)), _full(seg.shape)],
        out_specs=[pl.BlockSpec((bb, tt, RWKV_W), lambda b, j: (b, j, 0)), st_spec, sh_spec],
        out_shape=[jax.ShapeDtypeStruct((nb, t, RWKV_W), F32), jax.ShapeDtypeStruct(s0.shape, F32),
                   jax.ShapeDtypeStruct(sh0.shape, F32)],
        compiler_params=_cparams(("parallel", "arbitrary"), vmem),
        name="rwkv7",
    )(zw, sh0, s0, p["mu"], p["w0"], p["a0"], p["k_k"], p["k_a"], p["r_k"], p["lnx_g"], p["lnx_b"],
      p["w_decay_up"], p["w_a_up"], p["w_g_up"], tri, seg)


def _merge_kernel(x_ref, zg_ref, oa_ref, ob_ref, wa_ref, wb_ref, wo_ref, o_ref):
    ga = jax.nn.sigmoid(zg_ref[:, 0:D_MODEL])
    gb = jax.nn.sigmoid(zg_ref[:, D_MODEL:GATE_W])
    merged = ga * _nn(oa_ref[...], wa_ref[...]) + gb * _nn(ob_ref[...], wb_ref[...])
    o_ref[...] = x_ref[...] + _nn(merged, wo_ref[...])


def _merge(x, zg, o_ret, o_rwkv, wa, wb, wo):
    n = x.shape[0]
    tm = min(ROW_TILE, n)
    row = lambda w: pl.BlockSpec((tm, w), lambda i: (i, 0))
    vmem = _vmem(_nbytes((tm, 2 * D_MODEL + GATE_W + RET_W + RWKV_W)), _nbytes((2 * D_MODEL, D_MODEL), BF16),
                 6 * _nbytes((tm, D_MODEL)))
    return pl.pallas_call(
        _merge_kernel,
        grid=(n // tm,),
        in_specs=[row(D_MODEL), row(GATE_W), row(RET_W), row(RWKV_W),
                  _full((RET_W, D_MODEL)), _full((RWKV_W, D_MODEL)), _full((D_MODEL, D_MODEL))],
        out_specs=row(D_MODEL),
        out_shape=jax.ShapeDtypeStruct((n, D_MODEL), F32),
        compiler_params=_cparams(("parallel",), vmem),
        name="merge",
    )(x, zg, o_ret, o_rwkv, wa, wb, wo)


def _cross_kernel(x_ref, mk_ref, mv_ref, g_ref, wq_ref, wo_ref, o_ref, q_scr, ox_scr, *, bb, tq):
    rows = bb * tq
    x = x_ref[...].reshape(rows, D_MODEL)
    q_scr[...] = _nn(_rms(x, g_ref[...]), wq_ref[...])

    def body(b, _):
        r0 = b * tq if isinstance(b, int) else pl.multiple_of(b * tq, tq)
        for h in range(X_HEADS):
            sl = slice(h * X_HD, (h + 1) * X_HD)
            sc = _nt(q_scr[pl.ds(r0, tq), sl], mk_ref[b, :, sl]) * (X_HD ** -0.5)
            e = jnp.exp(sc - jnp.max(sc, axis=-1, keepdims=True))
            att = e / jnp.sum(e, axis=-1, keepdims=True)
            ox_scr[pl.ds(r0, tq), sl] = _nn(att, mv_ref[b, :, sl])

    _loop_bc(bb, 1, body)
    o_ref[...] = (x + _nn(ox_scr[...], wo_ref[...])).reshape(bb, tq, D_MODEL)


def _cross(x, mem_k, mem_v, g, wq, wo, bb, tq):
    nb, t, _ = x.shape
    x_spec = pl.BlockSpec((bb, tq, D_MODEL), lambda b, j: (b, j, 0))
    m_spec = pl.BlockSpec((bb, MEM_LEN, D_MODEL), lambda b, j: (b, 0, 0))
    vmem = _vmem(2 * _nbytes((bb, tq, D_MODEL)) + 2 * _nbytes((bb, MEM_LEN, D_MODEL)),
                 2 * _nbytes((D_MODEL, D_MODEL), BF16) + 2 * _nbytes((bb * tq, D_MODEL)),
                 6 * _nbytes((bb * tq, D_MODEL)) + 2 * _nbytes((MEM_LEN, D_MODEL)))
    return pl.pallas_call(
        functools.partial(_cross_kernel, bb=bb, tq=tq),
        grid=(nb // bb, t // tq),
        in_specs=[x_spec, m_spec, m_spec, _full((1, D_MODEL)), _full((D_MODEL, D_MODEL)), _full((D_MODEL, D_MODEL))],
        out_specs=x_spec,
        out_shape=jax.ShapeDtypeStruct(x.shape, F32),
        scratch_shapes=[pltpu.VMEM((bb * tq, D_MODEL), F32), pltpu.VMEM((bb * tq, D_MODEL), F32)],
        compiler_params=_cparams(("parallel", "arbitrary"), vmem),
        name="cross_attn",
    )(x, mem_k, mem_v, g, wq, wo)


MLP_FF_CHUNK = 1024


def _mlp_kernel(x_ref, g_ref, wu_ref, wd_ref, gf_ref, o_ref, *, final_norm):
    x = x_ref[...]
    h = _rms(x, g_ref[...]).astype(BF16)
    acc = x
    for c in range(0, D_FF, MLP_FF_CHUNK):
        u = jnp.maximum(jnp.dot(h, wu_ref[:, c:c + MLP_FF_CHUNK], preferred_element_type=F32), 0.0)
        acc = acc + _nn(u * u, wd_ref[c:c + MLP_FF_CHUNK, :])
    o_ref[...] = _rms(acc, gf_ref[...]) if final_norm else acc


def _mlp(x, g, w_up, w_down, g_final, final_norm):
    n = x.shape[0]
    tm = min(ROW_TILE, n)
    row = pl.BlockSpec((tm, D_MODEL), lambda i: (i, 0))
    vmem = _vmem(2 * _nbytes((tm, D_MODEL)), 2 * _nbytes((D_MODEL, D_FF), BF16),
                 4 * _nbytes((tm, MLP_FF_CHUNK)) + 4 * _nbytes((tm, D_MODEL)))
    return pl.pallas_call(
        functools.partial(_mlp_kernel, final_norm=final_norm),
        grid=(n // tm,),
        in_specs=[row, _full((1, D_MODEL)), _full((D_MODEL, D_FF)), _full((D_FF, D_MODEL)), _full((1, D_MODEL))],
        out_specs=row,
        out_shape=jax.ShapeDtypeStruct((n, D_MODEL), F32),
        compiler_params=_cparams(("parallel",), vmem),
        name="mlp",
    )(x, g, w_up, w_down, g_final)


def _layer(x, pos, s_ret, s_wkv, s_shift, mem_k, mem_v, w, g_final, final_norm, *, seq_block, time_block):
    nb, t, _ = x.shape
    n = nb * t
    xf = x.reshape(n, D_MODEL)
    zg, zr, zw = _in_proj(xf, w["g_mix"], w["w_in"])
    o_ret, s_ret_new = _retention(zr.reshape(nb, t, 4 * RET_W), s_ret, pos, seq_block, time_block)
    o_wkv, s_wkv_new, shift_new = _rwkv7(zw.reshape(nb, t, RWKV_IN_W), s_shift.reshape(nb, 1, RWKV_IN_W), s_wkv,
                                         w, seq_block, time_block, RWKV_GROUP)
    x1 = _merge(xf, zg, o_ret.reshape(n, RET_W), o_wkv.reshape(n, RWKV_W),
                w["w_branch_a"], w["w_branch_b"], w["w_out"])
    x2 = _cross(x1.reshape(nb, t, D_MODEL), mem_k, mem_v, w["g_cross"], w["w_cq"], w["w_co"],
                seq_block, min(time_block, t))
    y = _mlp(x2.reshape(n, D_MODEL), w["g_mlp"], w["w_up"], w["w_down"], g_final, final_norm)
    return y.reshape(nb, t, D_MODEL), s_ret_new, s_wkv_new, shift_new.reshape(nb, RWKV_IN_W)


def _layer_weights(l, g_mix, w_in, w_branch_a, w_branch_b, w_out, mu_shift, w0, w_decay_up, a0, w_a_up, w_g_up,
                   k_k, k_a, r_k, lnx_g, lnx_b, g_cross, w_cq, w_co, g_mlp, w_up, w_down):
    vec = lambda v: v[l].reshape(1, -1).astype(F32)
    mat = lambda m: m[l].astype(BF16)
    return dict(g_mix=vec(g_mix), w_in=mat(w_in), w_branch_a=mat(w_branch_a), w_branch_b=mat(w_branch_b),
                w_out=mat(w_out), mu=vec(mu_shift), w0=vec(w0), w_decay_up=mat(w_decay_up), a0=vec(a0),
                w_a_up=mat(w_a_up), w_g_up=mat(w_g_up), k_k=vec(k_k), k_a=vec(k_a), r_k=vec(r_k), lnx_g=vec(lnx_g),
                lnx_b=vec(lnx_b), g_cross=vec(g_cross), w_cq=mat(w_cq), w_co=mat(w_co), g_mlp=vec(g_mlp),
                w_up=mat(w_up), w_down=mat(w_down))


def kernel(x_prompt, x_sample, mem_prompt, state_ret, state_wkv, state_shift, cache_mem_k, cache_mem_v, g_mix, w_in, w_branch_a, w_branch_b, w_out, mu_shift, w0, w_decay_up, a0, w_a_up, w_g_up, k_k, k_a, r_k, lnx_g, lnx_b, g_cross, g_mem, w_cq, w_ck, w_cv, w_co, g_mlp, w_up, w_down, g_final):
    depth = w_in.shape[0]
    bp, tp, _ = x_prompt.shape
    bs, ts, _ = x_sample.shape
    pos_p = jnp.arange(tp, dtype=jnp.int32)
    pos_s = PAST_LEN + jnp.arange(ts, dtype=jnp.int32)
    gf = g_final.reshape(1, D_MODEL)

    xp, xs = x_prompt, x_sample
    outs_p, outs_s = [], []
    for l in range(depth):
        w = _layer_weights(l, g_mix, w_in, w_branch_a, w_branch_b, w_out, mu_shift, w0, w_decay_up, a0, w_a_up,
                           w_g_up, k_k, k_a, r_k, lnx_g, lnx_b, g_cross, w_cq, w_co, g_mlp, w_up, w_down)
        last = l == depth - 1
        mk, mv = _norm_proj2(mem_prompt.reshape(bp * MEM_LEN, D_MODEL), g_mem[l].reshape(1, D_MODEL),
                             w_ck[l].astype(BF16), w_cv[l].astype(BF16))
        mk = mk.reshape(bp, MEM_LEN, D_MODEL)
        mv = mv.reshape(bp, MEM_LEN, D_MODEL)
        xp, sr, sw, ss = _layer(
            xp, pos_p,
            jnp.zeros((bp, RET_HEADS, RET_HD, RET_HD), F32), jnp.zeros((bp, RWKV_HEADS, RWKV_HD, RWKV_HD), F32),
            jnp.zeros((bp, RWKV_IN_W), F32), mk, mv, w, gf, last,
            seq_block=1, time_block=min(ROW_TILE, tp))
        outs_p.append((sr, sw, ss, mk.reshape(bp, MEM_LEN, X_HEADS, X_HD), mv.reshape(bp, MEM_LEN, X_HEADS, X_HD)))
        xs, sr2, sw2, ss2 = _layer(
            xs, pos_s, state_ret[l], state_wkv[l], state_shift[l],
            cache_mem_k[l].reshape(bs, MEM_LEN, D_MODEL), cache_mem_v[l].reshape(bs, MEM_LEN, D_MODEL), w,
            gf, last, seq_block=8, time_block=ts)
        outs_s.append((sr2, sw2, ss2))

    stack = lambda items, i: jnp.stack([it[i] for it in items])
    return (xp, xs, stack(outs_p, 0), stack(outs_p, 1), stack(outs_p, 2), stack(outs_p, 3), stack(outs_p, 4),
            stack(outs_s, 0), stack(outs_s, 1), stack(outs_s, 2))
```

```python
import functools
import math

import jax
import jax.numpy as jnp
from jax import lax
from jax.experimental import pallas as pl
from jax.experimental.pallas import tpu as pltpu

F32 = jnp.float32
BF16 = jnp.bfloat16

D_MODEL = 1024
PAST_LEN = 16384
RET_HEADS = 4
RET_HD = 128
RET_W = RET_HEADS * RET_HD
RET_CHUNK = 128
RET_GN_EPS = 1e-5
ROPE_BASE = 10000.0
RWKV_HEADS = 8
RWKV_HD = 64
RWKV_W = RWKV_HEADS * RWKV_HD
DECAY_LORA = 64
AAA_LORA = 64
GATE_LORA = 128
RWKV_GN_EPS = 64e-5
RWKV_IN_W = 3 * RWKV_W + DECAY_LORA + AAA_LORA + GATE_LORA
GATE_W = 2 * D_MODEL
O_RET = 2 * D_MODEL
O_RWKV = O_RET + 4 * RET_W
IN_W = O_RWKV + RWKV_IN_W
MEM_LEN = 256
X_HEADS = 4
X_HD = D_MODEL // X_HEADS
D_FF = 4 * D_MODEL
RMS_EPS = 1e-6

V7X_VMEM_BYTES = 64 * 1024 * 1024
VMEM_CAP_BYTES = V7X_VMEM_BYTES - 8 * 1024 * 1024
ROW_TILE = 512
RWKV_CHUNK = 64
RWKV_GROUP = 2


def _cparams(sem, vmem_bytes):
    return pltpu.CompilerParams(dimension_semantics=sem, vmem_limit_bytes=int(min(vmem_bytes, VMEM_CAP_BYTES)))


def _nbytes(shape, dtype=F32):
    return math.prod(shape) * jnp.dtype(dtype).itemsize


def _nn(a, b):
    return jnp.dot(a.astype(BF16), b.astype(BF16), preferred_element_type=F32)


def _nt(a, b):
    return lax.dot_general(a.astype(BF16), b.astype(BF16), (((1,), (1,)), ((), ())), preferred_element_type=F32)


def _tn(a, b):
    return lax.dot_general(a.astype(BF16), b.astype(BF16), (((0,), (0,)), ((), ())), preferred_element_type=F32)


def _split(a):
    hi = a.astype(BF16)
    lo = (a - hi.astype(F32)).astype(BF16)
    return hi, lo


def _dg3(a, b, dims):
    ah, al = _split(a)
    bh, bl = _split(b)
    dn = (dims, ((), ()))
    return (lax.dot_general(ah, bh, dn, preferred_element_type=F32)
            + lax.dot_general(ah, bl, dn, preferred_element_type=F32)
            + lax.dot_general(al, bh, dn, preferred_element_type=F32))


def _nn3(a, b):
    return _dg3(a, b, ((1,), (0,)))


def _nt3(a, b):
    return _dg3(a, b, ((1,), (1,)))


def _tn3(a, b):
    return _dg3(a, b, ((0,), (0,)))


def _rms(x, g):
    return x * lax.rsqrt(jnp.mean(x * x, axis=-1, keepdims=True) + RMS_EPS) * g


def _head_norm(o, eps):
    mu = jnp.mean(o, axis=-1, keepdims=True)
    d = o - mu
    var = jnp.mean(d * d, axis=-1, keepdims=True)
    return d * lax.rsqrt(var + eps)


def _full(shape):
    zeros = (0,) * len(shape)
    return pl.BlockSpec(shape, lambda *_: zeros, pipeline_mode=pl.Buffered(1))


def _vmem(pipelined, resident, temps):
    return 2 * pipelined + resident + temps


def _in_proj_kernel(x_ref, g_ref, w_ref, zg_ref, zr_ref, zw_ref):
    h = _rms(x_ref[...], g_ref[...]).astype(BF16)
    zg_ref[...] = jnp.dot(h, w_ref[:, 0:O_RET], preferred_element_type=F32)
    zr_ref[...] = jnp.dot(h, w_ref[:, O_RET:O_RWKV], preferred_element_type=F32)
    zw_ref[...] = jnp.dot(h, w_ref[:, O_RWKV:IN_W], preferred_element_type=F32)


def _in_proj(x, g, w_in):
    n = x.shape[0]
    tm = min(ROW_TILE, n)
    vmem = _vmem(_nbytes((tm, D_MODEL + IN_W)), _nbytes((D_MODEL, IN_W), BF16), _nbytes((tm, 4 * RET_W + D_MODEL)))
    return pl.pallas_call(
        _in_proj_kernel,
        grid=(n // tm,),
        in_specs=[pl.BlockSpec((tm, D_MODEL), lambda i: (i, 0)), _full((1, D_MODEL)), _full((D_MODEL, IN_W))],
        out_specs=[pl.BlockSpec((tm, GATE_W), lambda i: (i, 0)),
                   pl.BlockSpec((tm, 4 * RET_W), lambda i: (i, 0)),
                   pl.BlockSpec((tm, RWKV_IN_W), lambda i: (i, 0))],
        out_shape=[jax.ShapeDtypeStruct((n, GATE_W), F32), jax.ShapeDtypeStruct((n, 4 * RET_W), F32),
                   jax.ShapeDtypeStruct((n, RWKV_IN_W), F32)],
        compiler_params=_cparams(("parallel",), vmem),
        name="in_proj",
    )(x, g, w_in)


LANES = 128
MEM_ROWS = MEM_LEN * X_HEADS * (X_HD // LANES)
ROWS_PER_TOKEN = MEM_ROWS // MEM_LEN


def _tile_order(mem):
    nb = mem.shape[0]
    return (mem.reshape(nb, MEM_LEN, X_HEADS, X_HD // LANES, LANES).transpose(0, 1, 3, 2, 4)
            .reshape(nb, MEM_ROWS, LANES))


def _from_tile_order(raw):
    nb = raw.shape[0]
    return (raw.reshape(nb, MEM_LEN, X_HD // LANES, X_HEADS, LANES).transpose(0, 1, 3, 2, 4)
            .reshape(nb, MEM_LEN, X_HEADS, X_HD))


def _mem_kv_kernel(x_ref, g_ref, wk_ref, wv_ref, kt_ref, vt_ref, kb_ref, vb_ref, *, tm):
    h = _rms(x_ref[...], g_ref[...]).astype(BF16)
    for w_ref, t_ref, b_ref in ((wk_ref, kt_ref, kb_ref), (wv_ref, vt_ref, vb_ref)):
        y = jnp.dot(h, w_ref[...], preferred_element_type=F32)
        b_ref[...] = y.astype(BF16)
        for hd in range(X_HEADS):
            for c in range(X_HD // LANES):
                col = hd * X_HD + c * LANES
                t_ref[pl.ds(c * X_HEADS + hd, tm, stride=ROWS_PER_TOKEN), :] = y[:, col:col + LANES]


def _mem_kv(mem, g, wk, wv):
    n = mem.shape[0]
    tm = min(ROW_TILE, n)
    vmem = _vmem(4 * _nbytes((tm, D_MODEL)), 2 * _nbytes((D_MODEL, D_MODEL), BF16), 4 * _nbytes((tm, D_MODEL)))
    row = pl.BlockSpec((tm, D_MODEL), lambda i: (i, 0))
    tile = pl.BlockSpec((tm * ROWS_PER_TOKEN, LANES), lambda i: (i, 0))
    return pl.pallas_call(
        functools.partial(_mem_kv_kernel, tm=tm),
        grid=(n // tm,),
        in_specs=[row, _full((1, D_MODEL)), _full((D_MODEL, D_MODEL)), _full((D_MODEL, D_MODEL))],
        out_specs=[tile, tile, row, row],
        out_shape=[jax.ShapeDtypeStruct((n * ROWS_PER_TOKEN, LANES), F32)] * 2
        + [jax.ShapeDtypeStruct((n, D_MODEL), BF16)] * 2,
        compiler_params=_cparams(("parallel",), vmem),
        name="mem_kv",
    )(mem, g, wk, wv)


def _loop_bc(bb, nch, body):
    if bb == 1 and nch == 1:
        body(0, 0)
    elif nch == 1:
        lax.fori_loop(0, bb, lambda b, c: (body(b, 0), c)[1], 0)
    elif bb == 1:
        lax.fori_loop(0, nch, lambda i, c: (body(0, i), c)[1], 0)
    else:
        lax.fori_loop(0, bb * nch, lambda j, c: (body(j // nch, j % nch), c)[1], 0)


def _ret_kernel(zr_ref, s0_ref, cos_ref, sin_ref, dm_ref, qd_ref, kd_ref, cd_ref, o_ref, s_ref, *, bb, tt, chunk):
    @pl.when(pl.program_id(1) == 0)
    def _():
        s_ref[...] = s0_ref[...]

    def body(b, i):
        r0 = i * chunk if isinstance(i, int) else pl.multiple_of(i * chunk, chunk)
        rows = pl.ds(r0, chunk)
        cos = cos_ref[rows, :]
        sin = sin_ref[rows, :]
        for h in range(RET_HEADS):
            q = zr_ref[b, rows, h * RET_HD:(h + 1) * RET_HD]
            k = zr_ref[b, rows, RET_W + h * RET_HD:RET_W + (h + 1) * RET_HD]
            v = zr_ref[b, rows, 2 * RET_W + h * RET_HD:2 * RET_W + (h + 1) * RET_HD]
            g = zr_ref[b, rows, 3 * RET_W + h * RET_HD:3 * RET_W + (h + 1) * RET_HD]
            q = q * cos + pltpu.roll(q, RET_HD // 2, axis=1) * sin
            k = (k * cos + pltpu.roll(k, RET_HD // 2, axis=1) * sin) * (RET_HD ** -0.5)
            s = s_ref[b, h]
            sc = _nt(q, k) * dm_ref[h]
            o = _nn(sc, v) + _nn(q, s) * qd_ref[h]
            s_ref[b, h] = s * cd_ref[h] + _tn(k * kd_ref[h], v)
            o_ref[b, rows, h * RET_HD:(h + 1) * RET_HD] = _head_norm(o, RET_GN_EPS) * (g * jax.nn.sigmoid(g))

    _loop_bc(bb, tt // chunk, body)


def _ret_consts(chunk):
    lg = jnp.log1p(-jnp.exp2(-5.0 - jnp.arange(RET_HEADS, dtype=F32)))
    idx = jnp.arange(chunk, dtype=F32)
    diff = idx[:, None] - idx[None, :]
    dmask = jnp.where(diff[None] >= 0, jnp.exp(jnp.maximum(diff, 0.0)[None] * lg[:, None, None]), 0.0)
    q_dec = jnp.exp((idx + 1.0)[None, :] * lg[:, None])
    k_dec = jnp.exp((chunk - 1.0 - idx)[None, :] * lg[:, None])
    c_dec = jnp.exp(chunk * lg)
    bc = lambda t: jnp.broadcast_to(t[:, :, None], (RET_HEADS, t.shape[1], RET_HD))
    return dmask, bc(q_dec), bc(k_dec), bc(c_dec[:, None])


def _rope_tables(pos):
    half = RET_HD // 2
    inv = 1.0 / (ROPE_BASE ** (jnp.arange(half, dtype=F32) / half))
    ang = pos.astype(F32)[:, None] * inv[None, :]
    cos, sin = jnp.cos(ang), jnp.sin(ang)
    return jnp.concatenate([cos, cos], axis=1), jnp.concatenate([-sin, sin], axis=1)


def _retention(zr, s0, pos, bb, tt):
    nb, t, _ = zr.shape
    chunk = math.gcd(t, RET_CHUNK)
    cos, sin = _rope_tables(pos)
    dmask, q_dec, k_dec, c_dec = _ret_consts(chunk)
    st_spec = pl.BlockSpec((bb, RET_HEADS, RET_HD, RET_HD), lambda b, j: (b, 0, 0, 0))
    tab_spec = pl.BlockSpec((tt, RET_HD), lambda b, j: (j, 0))
    vmem = _vmem(_nbytes((bb, tt, 5 * RET_W)) + 2 * _nbytes((bb, RET_HEADS, RET_HD, RET_HD)) + 2 * _nbytes((tt, RET_HD)),
                 _nbytes(dmask.shape) + 3 * _nbytes(q_dec.shape), 16 * _nbytes((chunk, 4 * RET_W)))
    return pl.pallas_call(
        functools.partial(_ret_kernel, bb=bb, tt=tt, chunk=chunk),
        grid=(nb // bb, t // tt),
        in_specs=[pl.BlockSpec((bb, tt, 4 * RET_W), lambda b, j: (b, j, 0)), st_spec, tab_spec, tab_spec,
                  _full(dmask.shape), _full(q_dec.shape), _full(k_dec.shape), _full(c_dec.shape)],
        out_specs=[pl.BlockSpec((bb, tt, RET_W), lambda b, j: (b, j, 0)), st_spec],
        out_shape=[jax.ShapeDtypeStruct((nb, t, RET_W), F32), jax.ShapeDtypeStruct(s0.shape, F32)],
        compiler_params=_cparams(("parallel", "arbitrary"), vmem),
        name="retention",
    )(zr, s0, cos, sin, dmask, q_dec, k_dec, c_dec)


def _wkv_kernel(zw_ref, sh0_ref, s0_ref, mu_ref, w0_ref, a0_ref, kk_ref, ka_ref, rk_ref, lng_ref, lnb_ref,
                wd_ref, wa_ref, wg_ref, tri_ref, seg_ref, o_ref, s_ref, sh_ref, *, bb, tt, chunk, group):
    @pl.when(pl.program_id(1) == 0)
    def _():
        s_ref[...] = s0_ref[...]
        sh_ref[...] = sh0_ref[...]

    C = chunk
    n_sq = max(int(math.log2(C)) - 1, 0)
    row_id = lax.broadcasted_iota(jnp.int32, (C, C), 0)
    col_id = lax.broadcasted_iota(jnp.int32, (C, C), 1)
    strict = row_id > col_id
    incl = row_id >= col_id
    eye = jnp.where(row_id == col_id, 1.0, 0.0).astype(F32)
    first_row = lax.broadcasted_iota(jnp.int32, (C, RWKV_IN_W), 0) == 0
    heads = range(RWKV_HEADS)
    hsl = [slice(h * RWKV_HD, (h + 1) * RWKV_HD) for h in heads]
    half = RWKV_W // 2

    def seg_sum(t):
        seg = seg_ref[...]
        return jnp.concatenate([_nn(t[:, :half], seg), _nn(t[:, half:], seg)], axis=1)

    def prepare(b, r0):
        pw = zw_ref[b, pl.ds(r0, C), :]
        prev = jnp.where(first_row, sh_ref[b], pltpu.roll(pw, 1, axis=0))
        sh_ref[b] = pw[C - 1:C, :]
        xm = pw + mu_ref[...] * (prev - pw)
        r = xm[:, 0:RWKV_W]
        kb = xm[:, RWKV_W:2 * RWKV_W]
        vb = xm[:, 2 * RWKV_W:3 * RWKV_W]
        o1 = 3 * RWKV_W
        wl = xm[:, o1:o1 + DECAY_LORA]
        al = xm[:, o1 + DECAY_LORA:o1 + DECAY_LORA + AAA_LORA]
        gl = xm[:, o1 + DECAY_LORA + AAA_LORA:RWKV_IN_W]
        logw = -math.exp(-0.5) * jax.nn.sigmoid(w0_ref[...] + _nn(jnp.tanh(wl), wd_ref[...]))
        a = jax.nn.sigmoid(a0_ref[...] + _nn(al, wa_ref[...]))
        g = _nn(jax.nn.sigmoid(gl), wg_ref[...])
        l1 = logw.astype(BF16)
        rem = logw - l1.astype(F32)
        l2 = rem.astype(BF16)
        l3 = (rem - l2.astype(F32)).astype(BF16)
        tri = tri_ref[...]
        cum = (jnp.dot(tri, l1, preferred_element_type=F32) + jnp.dot(tri, l2, preferred_element_type=F32)
               + jnp.dot(tri, l3, preferred_element_type=F32))
        cum_end = cum[C - 1:C, :]
        kk = kb * kk_ref[...]
        kk = kk / jnp.maximum(jnp.sqrt(seg_sum(kk * kk)), 1e-12)
        km = kb * (1.0 + (a - 1.0) * ka_ref[...])
        bv = kk * a
        e_neg = jnp.exp(-cum)
        e_end = jnp.exp(cum_end - cum)
        return dict(at=-kk * jnp.exp(cum - logw), rt=r * jnp.exp(cum), bt=bv * e_neg, kt=km * e_neg,
                    bh=bv * e_end, kh=km * e_end, v=vb, g=g, p_end=jnp.exp(cum_end),
                    bonus=seg_sum(r * km * rk_ref[...]))

    def triangular(ps):
        ph = [(p, s) for p in ps for s in hsl]
        ar = [jnp.concatenate([p["at"][:, s], p["rt"][:, s]], axis=0) for p, s in ph]
        m1 = [_nt(a, jnp.concatenate([p["bt"][:, s], p["kt"][:, s]], axis=0)) for a, (p, s) in zip(ar, ph)]
        low = [jnp.where(strict, m[:C, :C], 0.0) for m in m1]
        rab = [jnp.where(incl, m[C:, :C], 0.0) for m in m1]
        akk = [jnp.concatenate([jnp.where(strict, m[:C, C:], 0.0), jnp.where(incl, m[C:, C:], 0.0)], axis=0)
               for m in m1]
        akv = [_nn(k, p["v"][:, s]) for k, (p, s) in zip(akk, ph)]
        tinv = [eye + l for l in low]
        if n_sq:
            x = [_nn(l, l) for l in low]
        for j in range(n_sq):
            if j < n_sq - 1:
                xt = [_nn(jnp.concatenate([xi, ti], axis=0), xi) for xi, ti in zip(x, tinv)]
                x = [t[:C] for t in xt]
                tinv = [ti + t[C:] for ti, t in zip(tinv, xt)]
            else:
                tinv = [ti + _nn(ti, xi) for ti, xi in zip(tinv, x)]
        nh = RWKV_HEADS
        return [dict(ar=ar[k * nh:(k + 1) * nh], rab=rab[k * nh:(k + 1) * nh], akv=akv[k * nh:(k + 1) * nh],
                     tinv=tinv[k * nh:(k + 1) * nh]) for k in range(len(ps))]

    def advance(items, ps, qs):
        ih = [(b, p, q, h) for (b, _), p, q in zip(items, ps, qs) for h in heads]
        ars = [_nt(q["ar"][h], s_ref[b, h]) for b, p, q, h in ih]
        u = [_nn(q["tinv"][h], a[:C] + q["akv"][h][:C]) for a, (b, p, q, h) in zip(ars, ih)]
        o = [a[C:] + q["akv"][h][C:] + _nn(q["rab"][h], ui) for a, ui, (b, p, q, h) in zip(ars, u, ih)]
        upd = [_tn(jnp.concatenate([ui, p["v"][:, hsl[h]]], axis=0),
                   jnp.concatenate([p["bh"][:, hsl[h]], p["kh"][:, hsl[h]]], axis=0))
               for ui, (b, p, q, h) in zip(u, ih)]
        for up, (b, p, q, h) in zip(upd, ih):
            s_ref[b, h] = s_ref[b, h] * p["p_end"][:, hsl[h]] + up
        for k, ((b, r0), p) in enumerate(zip(items, ps)):
            ok = jnp.concatenate(o[k * RWKV_HEADS:(k + 1) * RWKV_HEADS], axis=1)
            d = ok - seg_sum(ok) * (1.0 / RWKV_HD)
            var = seg_sum(d * d) * (1.0 / RWKV_HD)
            ok = d * lax.rsqrt(var + RWKV_GN_EPS) * lng_ref[...] + lnb_ref[...]
            o_ref[b, pl.ds(r0, C), :] = (ok + p["bonus"] * p["v"]) * p["g"]

    nch = tt // C
    assert bb == 1 or nch == 1, "a block holds either one sequence or one chunk per sequence"
    assert (bb * nch) % group == 0

    def body(j, carry):
        if nch == 1:
            items = [(j * group + k, 0) for k in range(group)]
        else:
            items = [(0, pl.multiple_of((j * group + k) * C, C)) for k in range(group)]
        ps = [prepare(b, r0) for b, r0 in items]
        qs = triangular(ps)
        if nch == 1:
            advance(items, ps, qs)
        else:
            for item, p, q in zip(items, ps, qs):
                advance([item], [p], [q])
        return carry

    lax.fori_loop(0, bb * nch // group, body, 0)


def _rwkv7(zw, sh0, s0, p, bb, tt, group):
    nb, t, _ = zw.shape
    chunk = math.gcd(t, RWKV_CHUNK)
    tri = jnp.tril(jnp.ones((chunk, chunk), BF16))
    lane_head = jnp.arange(RWKV_W // 2) // RWKV_HD
    seg = (lane_head[:, None] == lane_head[None, :]).astype(BF16)
    st_spec = pl.BlockSpec((bb, RWKV_HEADS, RWKV_HD, RWKV_HD), lambda b, j: (b, 0, 0, 0))
    sh_spec = pl.BlockSpec((bb, 1, RWKV_IN_W), lambda b, j: (b, 0, 0))
    vec = _full((1, RWKV_W))
    vmem = _vmem(_nbytes((bb, tt, RWKV_IN_W + RWKV_W)) + 2 * _nbytes((bb, RWKV_HEADS, RWKV_HD, RWKV_HD))
                 + 2 * _nbytes((bb, 8, RWKV_IN_W)),
                 9 * _nbytes((8, RWKV_IN_W)) + _nbytes((DECAY_LORA + AAA_LORA + GATE_LORA + chunk, RWKV_W), BF16)
                 + _nbytes(seg.shape, BF16),
                 group * 48 * _nbytes((chunk, RWKV_IN_W)))
    return pl.pallas_call(
        functools.partial(_wkv_kernel, bb=bb, tt=tt, chunk=chunk, group=group),
        grid=(nb // bb, t // tt),
        in_specs=[pl.BlockSpec((bb, tt, RWKV_IN_W), lambda b, j: (b, j, 0)), sh_spec, st_spec,
                  _full((1, RWKV_IN_W)), vec, vec, vec, vec, vec, vec, vec,
                  _full((DECAY_LORA, RWKV_W)), _full((AAA_LORA, RWKV_W)), _full((GATE_LORA, RWKV_W)),
                  _full((chunk, chunk)), _full(seg.shape)],
        out_specs=[pl.BlockSpec((bb, tt, RWKV_W), lambda b, j: (b, j, 0)), st_spec, sh_spec],
        out_shape=[jax.ShapeDtypeStruct((nb, t, RWKV_W), F32), jax.ShapeDtypeStruct(s0.shape, F32),
                   jax.ShapeDtypeStruct(sh0.shape, F32)],
        compiler_params=_cparams(("parallel", "arbitrary"), vmem),
        name="rwkv7",
    )(zw, sh0, s0, p["mu"], p["w0"], p["a0"], p["k_k"], p["k_a"], p["r_k"], p["lnx_g"], p["lnx_b"],
      p["w_decay_up"], p["w_a_up"], p["w_g_up"], tri, seg)


def _merge_kernel(x_ref, zg_ref, oa_ref, ob_ref, wa_ref, wb_ref, wo_ref, o_ref):
    ga = jax.nn.sigmoid(zg_ref[:, 0:D_MODEL])
    gb = jax.nn.sigmoid(zg_ref[:, D_MODEL:GATE_W])
    merged = ga * _nn(oa_ref[...], wa_ref[...]) + gb * _nn(ob_ref[...], wb_ref[...])
    o_ref[...] = x_ref[...] + _nn(merged, wo_ref[...])


def _merge(x, zg, o_ret, o_rwkv, wa, wb, wo):
    n = x.shape[0]
    tm = min(ROW_TILE, n)
    row = lambda w: pl.BlockSpec((tm, w), lambda i: (i, 0))
    vmem = _vmem(_nbytes((tm, 2 * D_MODEL + GATE_W + RET_W + RWKV_W)), _nbytes((2 * D_MODEL, D_MODEL), BF16),
                 6 * _nbytes((tm, D_MODEL)))
    return pl.pallas_call(
        _merge_kernel,
        grid=(n // tm,),
        in_specs=[row(D_MODEL), row(GATE_W), row(RET_W), row(RWKV_W),
                  _full((RET_W, D_MODEL)), _full((RWKV_W, D_MODEL)), _full((D_MODEL, D_MODEL))],
        out_specs=row(D_MODEL),
        out_shape=jax.ShapeDtypeStruct((n, D_MODEL), F32),
        compiler_params=_cparams(("parallel",), vmem),
        name="merge",
    )(x, zg, o_ret, o_rwkv, wa, wb, wo)


def _softmax(sc):
    e = jnp.exp(sc - jnp.max(sc, axis=-1, keepdims=True))
    return e / jnp.sum(e, axis=-1, keepdims=True)


def _attend_heads(q_scr, mk_ref, mv_ref, ox_scr, b, r0, tq):
    for h in range(X_HEADS):
        sl = slice(h * X_HD, (h + 1) * X_HD)
        att = _softmax(_nt(q_scr[pl.ds(r0, tq), sl], mk_ref[b, :, sl]) * (X_HD ** -0.5))
        ox_scr[pl.ds(r0, tq), sl] = _nn(att, mv_ref[b, :, sl])


def _attend_tiles(q_scr, mk_ref, mv_ref, ox_scr, b, r0, tq):
    halves = X_HD // LANES
    q = q_scr[pl.ds(r0, tq), :]
    qs = jnp.concatenate([q[:, h * X_HD + c * LANES:h * X_HD + (c + 1) * LANES]
                          for c in range(halves) for h in range(X_HEADS)], axis=0)
    z = _nt(qs, mk_ref[b])
    n = X_HEADS * tq
    slot = lax.broadcasted_iota(jnp.int32, (n, MEM_ROWS), 1) % ROWS_PER_TOKEN
    head = lax.broadcasted_iota(jnp.int32, (n, MEM_ROWS), 0) // tq
    own = slot == head
    part = jnp.where(own, z[:n], 0.0) + jnp.where(slot == head + X_HEADS, z[n:], 0.0)
    sc = part + pltpu.roll(part, MEM_ROWS - X_HEADS, axis=1)
    att = _softmax(jnp.where(own, sc * (X_HD ** -0.5), -jnp.inf))
    att2 = jnp.concatenate([att, pltpu.roll(att, X_HEADS, axis=1)], axis=0)
    o = _nn(att2, mv_ref[b])
    ox_scr[pl.ds(r0, tq), :] = jnp.concatenate(
        [o[(c * X_HEADS + h) * tq:(c * X_HEADS + h + 1) * tq] for h in range(X_HEADS) for c in range(halves)], axis=1)


def _cross_kernel(x_ref, mk_ref, mv_ref, g_ref, wq_ref, wo_ref, o_ref, q_scr, ox_scr, *, bb, tq, attend):
    rows = bb * tq
    x = x_ref[...].reshape(rows, D_MODEL)
    q_scr[...] = _nn(_rms(x, g_ref[...]), wq_ref[...])

    def body(b, _):
        r0 = b * tq if isinstance(b, int) else pl.multiple_of(b * tq, tq)
        attend(q_scr, mk_ref, mv_ref, ox_scr, b, r0, tq)

    _loop_bc(bb, 1, body)
    o_ref[...] = (x + _nn(ox_scr[...], wo_ref[...])).reshape(bb, tq, D_MODEL)


def _cross(x, mem_k, mem_v, g, wq, wo, bb, tq):
    nb, t, _ = x.shape
    tiled = mem_k.shape[1:] == (MEM_ROWS, LANES)
    x_spec = pl.BlockSpec((bb, tq, D_MODEL), lambda b, j: (b, j, 0))
    m_spec = pl.BlockSpec((bb,) + mem_k.shape[1:], lambda b, j: (b, 0, 0))
    vmem = _vmem(2 * _nbytes((bb, tq, D_MODEL)) + 2 * _nbytes((bb,) + mem_k.shape[1:], mem_k.dtype),
                 2 * _nbytes((D_MODEL, D_MODEL), BF16) + 2 * _nbytes((bb * tq, D_MODEL)),
                 6 * _nbytes((bb * tq, D_MODEL)) + 2 * _nbytes((MEM_LEN, D_MODEL))
                 + (8 * _nbytes((2 * X_HEADS * tq, MEM_ROWS)) if tiled else 0))
    return pl.pallas_call(
        functools.partial(_cross_kernel, bb=bb, tq=tq, attend=_attend_tiles if tiled else _attend_heads),
        grid=(nb // bb, t // tq),
        in_specs=[x_spec, m_spec, m_spec, _full((1, D_MODEL)), _full((D_MODEL, D_MODEL)), _full((D_MODEL, D_MODEL))],
        out_specs=x_spec,
        out_shape=jax.ShapeDtypeStruct(x.shape, F32),
        scratch_shapes=[pltpu.VMEM((bb * tq, D_MODEL), F32), pltpu.VMEM((bb * tq, D_MODEL), F32)],
        compiler_params=_cparams(("parallel", "arbitrary"), vmem),
        name="cross_attn",
    )(x, mem_k, mem_v, g, wq, wo)


MLP_FF_CHUNK = 1024


def _mlp_kernel(x_ref, g_ref, wu_ref, wd_ref, gf_ref, o_ref, *, final_norm):
    x = x_ref[...]
    h = _rms(x, g_ref[...]).astype(BF16)
    acc = x
    for c in range(0, D_FF, MLP_FF_CHUNK):
        u = jnp.maximum(jnp.dot(h, wu_ref[:, c:c + MLP_FF_CHUNK], preferred_element_type=F32), 0.0)
        acc = acc + _nn(u * u, wd_ref[c:c + MLP_FF_CHUNK, :])
    o_ref[...] = _rms(acc, gf_ref[...]) if final_norm else acc


def _mlp(x, g, w_up, w_down, g_final, final_norm):
    n = x.shape[0]
    tm = min(ROW_TILE, n)
    row = pl.BlockSpec((tm, D_MODEL), lambda i: (i, 0))
    vmem = _vmem(2 * _nbytes((tm, D_MODEL)), 2 * _nbytes((D_MODEL, D_FF), BF16),
                 4 * _nbytes((tm, MLP_FF_CHUNK)) + 4 * _nbytes((tm, D_MODEL)))
    return pl.pallas_call(
        functools.partial(_mlp_kernel, final_norm=final_norm),
        grid=(n // tm,),
        in_specs=[row, _full((1, D_MODEL)), _full((D_MODEL, D_FF)), _full((D_FF, D_MODEL)), _full((1, D_MODEL))],
        out_specs=row,
        out_shape=jax.ShapeDtypeStruct((n, D_MODEL), F32),
        compiler_params=_cparams(("parallel",), vmem),
        name="mlp",
    )(x, g, w_up, w_down, g_final)


def _layer(x, pos, s_ret, s_wkv, s_shift, mem_k, mem_v, w, g_final, final_norm, *, seq_block, cross_block,
           time_block):
    nb, t, _ = x.shape
    n = nb * t
    xf = x.reshape(n, D_MODEL)
    zg, zr, zw = _in_proj(xf, w["g_mix"], w["w_in"])
    o_ret, s_ret_new = _retention(zr.reshape(nb, t, 4 * RET_W), s_ret, pos, seq_block, time_block)
    o_wkv, s_wkv_new, shift_new = _rwkv7(zw.reshape(nb, t, RWKV_IN_W), s_shift.reshape(nb, 1, RWKV_IN_W), s_wkv,
                                         w, seq_block, time_block, RWKV_GROUP)
    x1 = _merge(xf, zg, o_ret.reshape(n, RET_W), o_wkv.reshape(n, RWKV_W),
                w["w_branch_a"], w["w_branch_b"], w["w_out"])
    x2 = _cross(x1.reshape(nb, t, D_MODEL), mem_k, mem_v, w["g_cross"], w["w_cq"], w["w_co"],
                cross_block, min(time_block, t))
    y = _mlp(x2.reshape(n, D_MODEL), w["g_mlp"], w["w_up"], w["w_down"], g_final, final_norm)
    return y.reshape(nb, t, D_MODEL), s_ret_new, s_wkv_new, shift_new.reshape(nb, RWKV_IN_W)


def _layer_weights(l, g_mix, w_in, w_branch_a, w_branch_b, w_out, mu_shift, w0, w_decay_up, a0, w_a_up, w_g_up,
                   k_k, k_a, r_k, lnx_g, lnx_b, g_cross, w_cq, w_co, g_mlp, w_up, w_down):
    vec = lambda v: v[l].reshape(1, -1).astype(F32)
    mat = lambda m: m[l].astype(BF16)
    return dict(g_mix=vec(g_mix), w_in=mat(w_in), w_branch_a=mat(w_branch_a), w_branch_b=mat(w_branch_b),
                w_out=mat(w_out), mu=vec(mu_shift), w0=vec(w0), w_decay_up=mat(w_decay_up), a0=vec(a0),
                w_a_up=mat(w_a_up), w_g_up=mat(w_g_up), k_k=vec(k_k), k_a=vec(k_a), r_k=vec(r_k), lnx_g=vec(lnx_g),
                lnx_b=vec(lnx_b), g_cross=vec(g_cross), w_cq=mat(w_cq), w_co=mat(w_co), g_mlp=vec(g_mlp),
                w_up=mat(w_up), w_down=mat(w_down))


def kernel(x_prompt, x_sample, mem_prompt, state_ret, state_wkv, state_shift, cache_mem_k, cache_mem_v, g_mix, w_in, w_branch_a, w_branch_b, w_out, mu_shift, w0, w_decay_up, a0, w_a_up, w_g_up, k_k, k_a, r_k, lnx_g, lnx_b, g_cross, g_mem, w_cq, w_ck, w_cv, w_co, g_mlp, w_up, w_down, g_final):
    depth = w_in.shape[0]
    bp, tp, _ = x_prompt.shape
    bs, ts, _ = x_sample.shape
    pos_p = jnp.arange(tp, dtype=jnp.int32)
    pos_s = PAST_LEN + jnp.arange(ts, dtype=jnp.int32)
    gf = g_final.reshape(1, D_MODEL)

    xp, xs = x_prompt, x_sample
    outs_p, outs_s = [], []
    for l in range(depth):
        w = _layer_weights(l, g_mix, w_in, w_branch_a, w_branch_b, w_out, mu_shift, w0, w_decay_up, a0, w_a_up,
                           w_g_up, k_k, k_a, r_k, lnx_g, lnx_b, g_cross, w_cq, w_co, g_mlp, w_up, w_down)
        last = l == depth - 1
        k_tiles, v_tiles, k_att, v_att = _mem_kv(mem_prompt.reshape(bp * MEM_LEN, D_MODEL),
                                                 g_mem[l].reshape(1, D_MODEL), w_ck[l].astype(BF16), w_cv[l].astype(BF16))
        xp, sr, sw, ss = _layer(
            xp, pos_p,
            jnp.zeros((bp, RET_HEADS, RET_HD, RET_HD), F32), jnp.zeros((bp, RWKV_HEADS, RWKV_HD, RWKV_HD), F32),
            jnp.zeros((bp, RWKV_IN_W), F32), k_att.reshape(bp, MEM_LEN, D_MODEL), v_att.reshape(bp, MEM_LEN, D_MODEL),
            w, gf, last, seq_block=1, cross_block=1, time_block=min(ROW_TILE, tp))
        outs_p.append((sr, sw, ss, _from_tile_order(k_tiles.reshape(bp, MEM_ROWS, LANES)),
                       _from_tile_order(v_tiles.reshape(bp, MEM_ROWS, LANES))))
        xs, sr2, sw2, ss2 = _layer(
            xs, pos_s, state_ret[l], state_wkv[l], state_shift[l],
            _tile_order(cache_mem_k[l]), _tile_order(cache_mem_v[l]), w,
            gf, last, seq_block=8, cross_block=4, time_block=ts)
        outs_s.append((sr2, sw2, ss2))

    stack = lambda items, i: jnp.stack([it[i] for it in items])
    return (xp, xs, stack(outs_p, 0), stack(outs_p, 1), stack(outs_p, 2), stack(outs_p, 3), stack(outs_p, 4),
            stack(outs_s, 0), stack(outs_s, 1), stack(outs_s, 2))
```

```python
import functools
import math
from typing import NamedTuple

import jax
import jax.numpy as jnp
from jax import lax
from jax.experimental import pallas as pl
from jax.experimental.pallas import tpu as pltpu

F32 = jnp.float32
BF16 = jnp.bfloat16

D_MODEL = 1024
PAST_LEN = 16384
RET_HEADS = 4
RET_HD = 128
RET_W = RET_HEADS * RET_HD
RET_CHUNK = 128
RET_GN_EPS = 1e-5
ROPE_BASE = 10000.0
RWKV_HEADS = 8
RWKV_HD = 64
RWKV_W = RWKV_HEADS * RWKV_HD
DECAY_LORA = 64
AAA_LORA = 64
GATE_LORA = 128
RWKV_GN_EPS = 64e-5
RWKV_IN_W = 3 * RWKV_W + DECAY_LORA + AAA_LORA + GATE_LORA
GATE_W = 2 * D_MODEL
O_RET = 2 * D_MODEL
O_RWKV = O_RET + 4 * RET_W
IN_W = O_RWKV + RWKV_IN_W
MEM_LEN = 256
X_HEADS = 4
X_HD = D_MODEL // X_HEADS
D_FF = 4 * D_MODEL
RMS_EPS = 1e-6

V7X_VMEM_BYTES = 64 * 1024 * 1024
VMEM_CAP_BYTES = V7X_VMEM_BYTES - 8 * 1024 * 1024
ROW_TILE = 512
RWKV_CHUNK = 64


def _cparams(sem, vmem_bytes):
    return pltpu.CompilerParams(dimension_semantics=sem, vmem_limit_bytes=int(min(vmem_bytes, VMEM_CAP_BYTES)))


def _nbytes(shape, dtype=F32):
    return math.prod(shape) * jnp.dtype(dtype).itemsize


def _nn(a, b):
    return jnp.dot(a.astype(BF16), b.astype(BF16), preferred_element_type=F32)


def _nt(a, b):
    return lax.dot_general(a.astype(BF16), b.astype(BF16), (((1,), (1,)), ((), ())), preferred_element_type=F32)


def _tn(a, b):
    return lax.dot_general(a.astype(BF16), b.astype(BF16), (((0,), (0,)), ((), ())), preferred_element_type=F32)


def _split(a):
    hi = a.astype(BF16)
    lo = (a - hi.astype(F32)).astype(BF16)
    return hi, lo


def _dg3(a, b, dims):
    ah, al = _split(a)
    bh, bl = _split(b)
    dn = (dims, ((), ()))
    return (lax.dot_general(ah, bh, dn, preferred_element_type=F32)
            + lax.dot_general(ah, bl, dn, preferred_element_type=F32)
            + lax.dot_general(al, bh, dn, preferred_element_type=F32))


def _nn3(a, b):
    return _dg3(a, b, ((1,), (0,)))


def _nt3(a, b):
    return _dg3(a, b, ((1,), (1,)))


def _tn3(a, b):
    return _dg3(a, b, ((0,), (0,)))


def _rms(x, g):
    return x * lax.rsqrt(jnp.mean(x * x, axis=-1, keepdims=True) + RMS_EPS) * g


def _head_norm(o, eps):
    mu = jnp.mean(o, axis=-1, keepdims=True)
    d = o - mu
    var = jnp.mean(d * d, axis=-1, keepdims=True)
    return d * lax.rsqrt(var + eps)


def _full(shape):
    zeros = (0,) * len(shape)
    return pl.BlockSpec(shape, lambda *_: zeros, pipeline_mode=pl.Buffered(1))


def _vmem(pipelined, resident, temps):
    return 2 * pipelined + resident + temps


def _in_proj_kernel(x_ref, g_ref, w_ref, zg_ref, zr_ref, zw_ref):
    h = _rms(x_ref[...], g_ref[...]).astype(BF16)
    zg_ref[...] = jnp.dot(h, w_ref[:, 0:O_RET], preferred_element_type=F32)
    zr_ref[...] = jnp.dot(h, w_ref[:, O_RET:O_RWKV], preferred_element_type=F32)
    zw_ref[...] = jnp.dot(h, w_ref[:, O_RWKV:IN_W], preferred_element_type=F32)


def _in_proj(x, g, w_in):
    n = x.shape[0]
    tm = min(ROW_TILE, n)
    vmem = _vmem(_nbytes((tm, D_MODEL + IN_W)), _nbytes((D_MODEL, IN_W), BF16), _nbytes((tm, 4 * RET_W + D_MODEL)))
    return pl.pallas_call(
        _in_proj_kernel,
        grid=(n // tm,),
        in_specs=[pl.BlockSpec((tm, D_MODEL), lambda i: (i, 0)), _full((1, D_MODEL)), _full((D_MODEL, IN_W))],
        out_specs=[pl.BlockSpec((tm, GATE_W), lambda i: (i, 0)),
                   pl.BlockSpec((tm, 4 * RET_W), lambda i: (i, 0)),
                   pl.BlockSpec((tm, RWKV_IN_W), lambda i: (i, 0))],
        out_shape=[jax.ShapeDtypeStruct((n, GATE_W), F32), jax.ShapeDtypeStruct((n, 4 * RET_W), F32),
                   jax.ShapeDtypeStruct((n, RWKV_IN_W), F32)],
        compiler_params=_cparams(("parallel",), vmem),
        name="in_proj",
    )(x, g, w_in)


LANES = 128
MEM_ROWS = MEM_LEN * X_HEADS * (X_HD // LANES)
ROWS_PER_TOKEN = MEM_ROWS // MEM_LEN


def _tile_order(mem):
    nb = mem.shape[0]
    return (mem.reshape(nb, MEM_LEN, X_HEADS, X_HD // LANES, LANES).transpose(0, 1, 3, 2, 4)
            .reshape(nb, MEM_ROWS, LANES))


def _from_tile_order(raw):
    nb = raw.shape[0]
    return (raw.reshape(nb, MEM_LEN, X_HD // LANES, X_HEADS, LANES).transpose(0, 1, 3, 2, 4)
            .reshape(nb, MEM_LEN, X_HEADS, X_HD))


def _mem_kv_kernel(x_ref, g_ref, wk_ref, wv_ref, kt_ref, vt_ref, kb_ref, vb_ref, *, tm):
    h = _rms(x_ref[...], g_ref[...]).astype(BF16)
    for w_ref, t_ref, b_ref in ((wk_ref, kt_ref, kb_ref), (wv_ref, vt_ref, vb_ref)):
        y = jnp.dot(h, w_ref[...], preferred_element_type=F32)
        b_ref[...] = y.astype(BF16)
        for hd in range(X_HEADS):
            for c in range(X_HD // LANES):
                col = hd * X_HD + c * LANES
                t_ref[pl.ds(c * X_HEADS + hd, tm, stride=ROWS_PER_TOKEN), :] = y[:, col:col + LANES]


def _mem_kv(mem, g, wk, wv):
    n = mem.shape[0]
    tm = min(ROW_TILE, n)
    vmem = _vmem(4 * _nbytes((tm, D_MODEL)), 2 * _nbytes((D_MODEL, D_MODEL), BF16), 4 * _nbytes((tm, D_MODEL)))
    row = pl.BlockSpec((tm, D_MODEL), lambda i: (i, 0))
    tile = pl.BlockSpec((tm * ROWS_PER_TOKEN, LANES), lambda i: (i, 0))
    return pl.pallas_call(
        functools.partial(_mem_kv_kernel, tm=tm),
        grid=(n // tm,),
        in_specs=[row, _full((1, D_MODEL)), _full((D_MODEL, D_MODEL)), _full((D_MODEL, D_MODEL))],
        out_specs=[tile, tile, row, row],
        out_shape=[jax.ShapeDtypeStruct((n * ROWS_PER_TOKEN, LANES), F32)] * 2
        + [jax.ShapeDtypeStruct((n, D_MODEL), BF16)] * 2,
        compiler_params=_cparams(("parallel",), vmem),
        name="mem_kv",
    )(mem, g, wk, wv)


def _loop_bc(bb, nch, body):
    if bb == 1 and nch == 1:
        body(0, 0)
    elif nch == 1:
        lax.fori_loop(0, bb, lambda b, c: (body(b, 0), c)[1], 0)
    elif bb == 1:
        lax.fori_loop(0, nch, lambda i, c: (body(0, i), c)[1], 0)
    else:
        lax.fori_loop(0, bb * nch, lambda j, c: (body(j // nch, j % nch), c)[1], 0)


def _ret_kernel(zr_ref, s0_ref, cos_ref, sin_ref, dm_ref, qd_ref, kd_ref, cd_ref, o_ref, s_ref, *,
                bb, tt, chunk, group):
    @pl.when(pl.program_id(1) == 0)
    def _():
        s_ref[...] = s0_ref[...]

    nch = tt // chunk
    assert bb == 1 or nch == 1, "a block holds either one sequence or one chunk per sequence"
    assert (bb * nch) % group == 0
    heads = range(RET_HEADS)

    def body(j, carry):
        if nch == 1:
            items = [(j * group + n, 0) for n in range(group)]
        else:
            items = [(0, pl.multiple_of((j * group + n) * chunk, chunk)) for n in range(group)]
        q, k, v = [], [], []
        for b, r0 in items:
            rows = pl.ds(r0, chunk)
            cos = cos_ref[rows, :]
            sin = sin_ref[rows, :]
            for h in heads:
                qh = zr_ref[b, rows, h * RET_HD:(h + 1) * RET_HD]
                kh = zr_ref[b, rows, RET_W + h * RET_HD:RET_W + (h + 1) * RET_HD]
                q.append(qh * cos + pltpu.roll(qh, RET_HD // 2, axis=1) * sin)
                k.append((kh * cos + pltpu.roll(kh, RET_HD // 2, axis=1) * sin) * (RET_HD ** -0.5))
                v.append(zr_ref[b, rows, 2 * RET_W + h * RET_HD:2 * RET_W + (h + 1) * RET_HD])
        hd = [h for _ in items for h in heads]
        sc = [_nt(qi, ki) * dm_ref[h] for qi, ki, h in zip(q, k, hd)]
        kv = [_tn(ki * kd_ref[h], vi) for ki, vi, h in zip(k, v, hd)]
        inner = [_nn(si, vi) for si, vi in zip(sc, v)]
        states = []
        for n, (b, _) in enumerate(items):
            for h in heads:
                s = s_ref[b, h] if (nch == 1 or n == 0) else states[-RET_HEADS] * cd_ref[h] + kv[(n - 1) * RET_HEADS + h]
                states.append(s)
        last = len(items) - 1
        for n, (b, _) in enumerate(items):
            if nch == 1 or n == last:
                for h in heads:
                    i = n * RET_HEADS + h
                    s_ref[b, h] = states[i] * cd_ref[h] + kv[i]
        cross = [_nn(qi, si) * qd_ref[h] for qi, si, h in zip(q, states, hd)]
        for n, (b, r0) in enumerate(items):
            rows = pl.ds(r0, chunk)
            for h in heads:
                i = n * RET_HEADS + h
                g = zr_ref[b, rows, 3 * RET_W + h * RET_HD:3 * RET_W + (h + 1) * RET_HD]
                o_ref[b, rows, h * RET_HD:(h + 1) * RET_HD] = (_head_norm(inner[i] + cross[i], RET_GN_EPS)
                                                              * (g * jax.nn.sigmoid(g)))
        return carry

    lax.fori_loop(0, bb * nch // group, body, 0)


def _ret_consts(chunk):
    lg = jnp.log1p(-jnp.exp2(-5.0 - jnp.arange(RET_HEADS, dtype=F32)))
    idx = jnp.arange(chunk, dtype=F32)
    diff = idx[:, None] - idx[None, :]
    dmask = jnp.where(diff[None] >= 0, jnp.exp(jnp.maximum(diff, 0.0)[None] * lg[:, None, None]), 0.0)
    q_dec = jnp.exp((idx + 1.0)[None, :] * lg[:, None])
    k_dec = jnp.exp((chunk - 1.0 - idx)[None, :] * lg[:, None])
    c_dec = jnp.exp(chunk * lg)
    bc = lambda t: jnp.broadcast_to(t[:, :, None], (RET_HEADS, t.shape[1], RET_HD))
    return dmask, bc(q_dec), bc(k_dec), bc(c_dec[:, None])


def _rope_tables(pos):
    half = RET_HD // 2
    inv = 1.0 / (ROPE_BASE ** (jnp.arange(half, dtype=F32) / half))
    ang = pos.astype(F32)[:, None] * inv[None, :]
    cos, sin = jnp.cos(ang), jnp.sin(ang)
    return jnp.concatenate([cos, cos], axis=1), jnp.concatenate([-sin, sin], axis=1)


def _retention(zr, s0, pos, bb, tt, group):
    nb, t, _ = zr.shape
    chunk = math.gcd(t, RET_CHUNK)
    cos, sin = _rope_tables(pos)
    dmask, q_dec, k_dec, c_dec = _ret_consts(chunk)
    st_spec = pl.BlockSpec((bb, RET_HEADS, RET_HD, RET_HD), lambda b, j: (b, 0, 0, 0))
    tab_spec = pl.BlockSpec((tt, RET_HD), lambda b, j: (j, 0))
    vmem = _vmem(_nbytes((bb, tt, 5 * RET_W)) + 2 * _nbytes((bb, RET_HEADS, RET_HD, RET_HD)) + 2 * _nbytes((tt, RET_HD)),
                 _nbytes(dmask.shape) + 3 * _nbytes(q_dec.shape), group * 8 * _nbytes((chunk, 4 * RET_W)))
    return pl.pallas_call(
        functools.partial(_ret_kernel, bb=bb, tt=tt, chunk=chunk, group=group),
        grid=(nb // bb, t // tt),
        in_specs=[pl.BlockSpec((bb, tt, 4 * RET_W), lambda b, j: (b, j, 0)), st_spec, tab_spec, tab_spec,
                  _full(dmask.shape), _full(q_dec.shape), _full(k_dec.shape), _full(c_dec.shape)],
        out_specs=[pl.BlockSpec((bb, tt, RET_W), lambda b, j: (b, j, 0)), st_spec],
        out_shape=[jax.ShapeDtypeStruct((nb, t, RET_W), F32), jax.ShapeDtypeStruct(s0.shape, F32)],
        compiler_params=_cparams(("parallel", "arbitrary"), vmem),
        name="retention",
    )(zr, s0, cos, sin, dmask, q_dec, k_dec, c_dec)


def _wkv_kernel(zw_ref, sh0_ref, s0_ref, mu_ref, w0_ref, a0_ref, kk_ref, ka_ref, rk_ref, lng_ref, lnb_ref,
                wd_ref, wa_ref, wg_ref, tri_ref, seg_ref, o_ref, s_ref, sh_ref, *, bb, tt, chunk, group):
    @pl.when(pl.program_id(1) == 0)
    def _():
        s_ref[...] = s0_ref[...]
        sh_ref[...] = sh0_ref[...]

    C = chunk
    n_sq = max(int(math.log2(C)) - 1, 0)
    row_id = lax.broadcasted_iota(jnp.int32, (C, C), 0)
    col_id = lax.broadcasted_iota(jnp.int32, (C, C), 1)
    strict = row_id > col_id
    incl = row_id >= col_id
    eye = jnp.where(row_id == col_id, 1.0, 0.0).astype(F32)
    first_row = lax.broadcasted_iota(jnp.int32, (C, RWKV_IN_W), 0) == 0
    heads = range(RWKV_HEADS)
    hsl = [slice(h * RWKV_HD, (h + 1) * RWKV_HD) for h in heads]
    half = RWKV_W // 2

    def seg_sum(t):
        seg = seg_ref[...]
        return jnp.concatenate([_nn(t[:, :half], seg), _nn(t[:, half:], seg)], axis=1)

    def prepare(b, r0):
        pw = zw_ref[b, pl.ds(r0, C), :]
        prev = jnp.where(first_row, sh_ref[b], pltpu.roll(pw, 1, axis=0))
        sh_ref[b] = pw[C - 1:C, :]
        xm = pw + mu_ref[...] * (prev - pw)
        r = xm[:, 0:RWKV_W]
        kb = xm[:, RWKV_W:2 * RWKV_W]
        vb = xm[:, 2 * RWKV_W:3 * RWKV_W]
        o1 = 3 * RWKV_W
        wl = xm[:, o1:o1 + DECAY_LORA]
        al = xm[:, o1 + DECAY_LORA:o1 + DECAY_LORA + AAA_LORA]
        gl = xm[:, o1 + DECAY_LORA + AAA_LORA:RWKV_IN_W]
        logw = -math.exp(-0.5) * jax.nn.sigmoid(w0_ref[...] + _nn(jnp.tanh(wl), wd_ref[...]))
        a = jax.nn.sigmoid(a0_ref[...] + _nn(al, wa_ref[...]))
        g = _nn(jax.nn.sigmoid(gl), wg_ref[...])
        l1 = logw.astype(BF16)
        rem = logw - l1.astype(F32)
        l2 = rem.astype(BF16)
        l3 = (rem - l2.astype(F32)).astype(BF16)
        tri = tri_ref[...]
        cum = (jnp.dot(tri, l1, preferred_element_type=F32) + jnp.dot(tri, l2, preferred_element_type=F32)
               + jnp.dot(tri, l3, preferred_element_type=F32))
        cum_end = cum[C - 1:C, :]
        kk = kb * kk_ref[...]
        kk = kk / jnp.maximum(jnp.sqrt(seg_sum(kk * kk)), 1e-12)
        km = kb * (1.0 + (a - 1.0) * ka_ref[...])
        bv = kk * a
        e_neg = jnp.exp(-cum)
        e_end = jnp.exp(cum_end - cum)
        return dict(at=-kk * jnp.exp(cum - logw), rt=r * jnp.exp(cum), bt=bv * e_neg, kt=km * e_neg,
                    bh=bv * e_end, kh=km * e_end, v=vb, g=g, p_end=jnp.exp(cum_end),
                    bonus=seg_sum(r * km * rk_ref[...]))

    def triangular(ps):
        ph = [(p, s) for p in ps for s in hsl]
        ar = [jnp.concatenate([p["at"][:, s], p["rt"][:, s]], axis=0) for p, s in ph]
        m1 = [_nt(a, jnp.concatenate([p["bt"][:, s], p["kt"][:, s]], axis=0)) for a, (p, s) in zip(ar, ph)]
        low = [jnp.where(strict, m[:C, :C], 0.0) for m in m1]
        rab = [jnp.where(incl, m[C:, :C], 0.0) for m in m1]
        akk = [jnp.concatenate([jnp.where(strict, m[:C, C:], 0.0), jnp.where(incl, m[C:, C:], 0.0)], axis=0)
               for m in m1]
        akv = [_nn(k, p["v"][:, s]) for k, (p, s) in zip(akk, ph)]
        tinv = [eye + l for l in low]
        if n_sq:
            x = [_nn(l, l) for l in low]
        for j in range(n_sq):
            if j < n_sq - 1:
                xt = [_nn(jnp.concatenate([xi, ti], axis=0), xi) for xi, ti in zip(x, tinv)]
                x = [t[:C] for t in xt]
                tinv = [ti + t[C:] for ti, t in zip(tinv, xt)]
            else:
                tinv = [ti + _nn(ti, xi) for ti, xi in zip(tinv, x)]
        nh = RWKV_HEADS
        return [dict(ar=ar[k * nh:(k + 1) * nh], rab=rab[k * nh:(k + 1) * nh], akv=akv[k * nh:(k + 1) * nh],
                     tinv=tinv[k * nh:(k + 1) * nh]) for k in range(len(ps))]

    def advance(items, ps, qs):
        ih = [(b, p, q, h) for (b, _), p, q in zip(items, ps, qs) for h in heads]
        ars = [_nt(q["ar"][h], s_ref[b, h]) for b, p, q, h in ih]
        u = [_nn(q["tinv"][h], a[:C] + q["akv"][h][:C]) for a, (b, p, q, h) in zip(ars, ih)]
        o = [a[C:] + q["akv"][h][C:] + _nn(q["rab"][h], ui) for a, ui, (b, p, q, h) in zip(ars, u, ih)]
        upd = [_tn(jnp.concatenate([ui, p["v"][:, hsl[h]]], axis=0),
                   jnp.concatenate([p["bh"][:, hsl[h]], p["kh"][:, hsl[h]]], axis=0))
               for ui, (b, p, q, h) in zip(u, ih)]
        for up, (b, p, q, h) in zip(upd, ih):
            s_ref[b, h] = s_ref[b, h] * p["p_end"][:, hsl[h]] + up
        for k, ((b, r0), p) in enumerate(zip(items, ps)):
            ok = jnp.concatenate(o[k * RWKV_HEADS:(k + 1) * RWKV_HEADS], axis=1)
            d = ok - seg_sum(ok) * (1.0 / RWKV_HD)
            var = seg_sum(d * d) * (1.0 / RWKV_HD)
            ok = d * lax.rsqrt(var + RWKV_GN_EPS) * lng_ref[...] + lnb_ref[...]
            o_ref[b, pl.ds(r0, C), :] = (ok + p["bonus"] * p["v"]) * p["g"]

    nch = tt // C
    assert bb == 1 or nch == 1, "a block holds either one sequence or one chunk per sequence"
    assert (bb * nch) % group == 0

    def body(j, carry):
        if nch == 1:
            items = [(j * group + k, 0) for k in range(group)]
        else:
            items = [(0, pl.multiple_of((j * group + k) * C, C)) for k in range(group)]
        ps = [prepare(b, r0) for b, r0 in items]
        qs = triangular(ps)
        if nch == 1:
            advance(items, ps, qs)
        else:
            for item, p, q in zip(items, ps, qs):
                advance([item], [p], [q])
        return carry

    lax.fori_loop(0, bb * nch // group, body, 0)


def _rwkv7(zw, sh0, s0, p, bb, tt, group):
    nb, t, _ = zw.shape
    chunk = math.gcd(t, RWKV_CHUNK)
    tri = jnp.tril(jnp.ones((chunk, chunk), BF16))
    lane_head = jnp.arange(RWKV_W // 2) // RWKV_HD
    seg = (lane_head[:, None] == lane_head[None, :]).astype(BF16)
    st_spec = pl.BlockSpec((bb, RWKV_HEADS, RWKV_HD, RWKV_HD), lambda b, j: (b, 0, 0, 0))
    sh_spec = pl.BlockSpec((bb, 1, RWKV_IN_W), lambda b, j: (b, 0, 0))
    vec = _full((1, RWKV_W))
    vmem = _vmem(_nbytes((bb, tt, RWKV_IN_W + RWKV_W)) + 2 * _nbytes((bb, RWKV_HEADS, RWKV_HD, RWKV_HD))
                 + 2 * _nbytes((bb, 8, RWKV_IN_W)),
                 9 * _nbytes((8, RWKV_IN_W)) + _nbytes((DECAY_LORA + AAA_LORA + GATE_LORA + chunk, RWKV_W), BF16)
                 + _nbytes(seg.shape, BF16),
                 group * 48 * _nbytes((chunk, RWKV_IN_W)))
    return pl.pallas_call(
        functools.partial(_wkv_kernel, bb=bb, tt=tt, chunk=chunk, group=group),
        grid=(nb // bb, t // tt),
        in_specs=[pl.BlockSpec((bb, tt, RWKV_IN_W), lambda b, j: (b, j, 0)), sh_spec, st_spec,
                  _full((1, RWKV_IN_W)), vec, vec, vec, vec, vec, vec, vec,
                  _full((DECAY_LORA, RWKV_W)), _full((AAA_LORA, RWKV_W)), _full((GATE_LORA, RWKV_W)),
                  _full((chunk, chunk)), _full(seg.shape)],
        out_specs=[pl.BlockSpec((bb, tt, RWKV_W), lambda b, j: (b, j, 0)), st_spec, sh_spec],
        out_shape=[jax.ShapeDtypeStruct((nb, t, RWKV_W), F32), jax.ShapeDtypeStruct(s0.shape, F32),
                   jax.ShapeDtypeStruct(sh0.shape, F32)],
        compiler_params=_cparams(("parallel", "arbitrary"), vmem),
        name="rwkv7",
    )(zw, sh0, s0, p["mu"], p["w0"], p["a0"], p["k_k"], p["k_a"], p["r_k"], p["lnx_g"], p["lnx_b"],
      p["w_decay_up"], p["w_a_up"], p["w_g_up"], tri, seg)


def _merge_kernel(x_ref, zg_ref, oa_ref, ob_ref, wa_ref, wb_ref, wo_ref, o_ref):
    ga = jax.nn.sigmoid(zg_ref[:, 0:D_MODEL])
    gb = jax.nn.sigmoid(zg_ref[:, D_MODEL:GATE_W])
    merged = ga * _nn(oa_ref[...], wa_ref[...]) + gb * _nn(ob_ref[...], wb_ref[...])
    o_ref[...] = x_ref[...] + _nn(merged, wo_ref[...])


def _merge(x, zg, o_ret, o_rwkv, wa, wb, wo):
    n = x.shape[0]
    tm = min(ROW_TILE, n)
    row = lambda w: pl.BlockSpec((tm, w), lambda i: (i, 0))
    vmem = _vmem(_nbytes((tm, 2 * D_MODEL + GATE_W + RET_W + RWKV_W)), _nbytes((2 * D_MODEL, D_MODEL), BF16),
                 6 * _nbytes((tm, D_MODEL)))
    return pl.pallas_call(
        _merge_kernel,
        grid=(n // tm,),
        in_specs=[row(D_MODEL), row(GATE_W), row(RET_W), row(RWKV_W),
                  _full((RET_W, D_MODEL)), _full((RWKV_W, D_MODEL)), _full((D_MODEL, D_MODEL))],
        out_specs=row(D_MODEL),
        out_shape=jax.ShapeDtypeStruct((n, D_MODEL), F32),
        compiler_params=_cparams(("parallel",), vmem),
        name="merge",
    )(x, zg, o_ret, o_rwkv, wa, wb, wo)


def _softmax(sc):
    e = jnp.exp(sc - jnp.max(sc, axis=-1, keepdims=True))
    return e / jnp.sum(e, axis=-1, keepdims=True)


def _attend_heads(q_scr, mk_ref, mv_ref, ox_scr, b, r0, tq):
    for h in range(X_HEADS):
        sl = slice(h * X_HD, (h + 1) * X_HD)
        att = _softmax(_nt(q_scr[pl.ds(r0, tq), sl], mk_ref[b, :, sl]) * (X_HD ** -0.5))
        ox_scr[pl.ds(r0, tq), sl] = _nn(att, mv_ref[b, :, sl])


def _attend_tiles(q_scr, mk_ref, mv_ref, ox_scr, b, r0, tq):
    halves = X_HD // LANES
    q = q_scr[pl.ds(r0, tq), :]
    qs = jnp.concatenate([q[:, h * X_HD + c * LANES:h * X_HD + (c + 1) * LANES]
                          for c in range(halves) for h in range(X_HEADS)], axis=0)
    z = _nt(qs, mk_ref[b])
    n = X_HEADS * tq
    slot = lax.broadcasted_iota(jnp.int32, (n, MEM_ROWS), 1) % ROWS_PER_TOKEN
    head = lax.broadcasted_iota(jnp.int32, (n, MEM_ROWS), 0) // tq
    own = slot == head
    part = jnp.where(own, z[:n], 0.0) + jnp.where(slot == head + X_HEADS, z[n:], 0.0)
    sc = part + pltpu.roll(part, MEM_ROWS - X_HEADS, axis=1)
    att = _softmax(jnp.where(own, sc * (X_HD ** -0.5), -jnp.inf))
    att2 = jnp.concatenate([att, pltpu.roll(att, X_HEADS, axis=1)], axis=0)
    o = _nn(att2, mv_ref[b])
    ox_scr[pl.ds(r0, tq), :] = jnp.concatenate(
        [o[(c * X_HEADS + h) * tq:(c * X_HEADS + h + 1) * tq] for h in range(X_HEADS) for c in range(halves)], axis=1)


def _cross_kernel(x_ref, mk_ref, mv_ref, g_ref, wq_ref, wo_ref, o_ref, q_scr, ox_scr, *, bb, tq, attend):
    rows = bb * tq
    x = x_ref[...].reshape(rows, D_MODEL)
    q_scr[...] = _nn(_rms(x, g_ref[...]), wq_ref[...])

    def body(b, _):
        r0 = b * tq if isinstance(b, int) else pl.multiple_of(b * tq, tq)
        attend(q_scr, mk_ref, mv_ref, ox_scr, b, r0, tq)

    _loop_bc(bb, 1, body)
    o_ref[...] = (x + _nn(ox_scr[...], wo_ref[...])).reshape(bb, tq, D_MODEL)


def _cross(x, mem_k, mem_v, g, wq, wo, bb, tq):
    nb, t, _ = x.shape
    tiled = mem_k.shape[1:] == (MEM_ROWS, LANES)
    x_spec = pl.BlockSpec((bb, tq, D_MODEL), lambda b, j: (b, j, 0))
    m_spec = pl.BlockSpec((bb,) + mem_k.shape[1:], lambda b, j: (b, 0, 0))
    vmem = _vmem(2 * _nbytes((bb, tq, D_MODEL)) + 2 * _nbytes((bb,) + mem_k.shape[1:], mem_k.dtype),
                 2 * _nbytes((D_MODEL, D_MODEL), BF16) + 2 * _nbytes((bb * tq, D_MODEL)),
                 6 * _nbytes((bb * tq, D_MODEL)) + 2 * _nbytes((MEM_LEN, D_MODEL))
                 + (8 * _nbytes((2 * X_HEADS * tq, MEM_ROWS)) if tiled else 0))
    return pl.pallas_call(
        functools.partial(_cross_kernel, bb=bb, tq=tq, attend=_attend_tiles if tiled else _attend_heads),
        grid=(nb // bb, t // tq),
        in_specs=[x_spec, m_spec, m_spec, _full((1, D_MODEL)), _full((D_MODEL, D_MODEL)), _full((D_MODEL, D_MODEL))],
        out_specs=x_spec,
        out_shape=jax.ShapeDtypeStruct(x.shape, F32),
        scratch_shapes=[pltpu.VMEM((bb * tq, D_MODEL), F32), pltpu.VMEM((bb * tq, D_MODEL), F32)],
        compiler_params=_cparams(("parallel", "arbitrary"), vmem),
        name="cross_attn",
    )(x, mem_k, mem_v, g, wq, wo)


MLP_FF_CHUNK = 1024


def _mlp_kernel(x_ref, g_ref, wu_ref, wd_ref, gf_ref, o_ref, *, final_norm):
    x = x_ref[...]
    h = _rms(x, g_ref[...]).astype(BF16)
    acc = x
    for c in range(0, D_FF, MLP_FF_CHUNK):
        u = jnp.maximum(jnp.dot(h, wu_ref[:, c:c + MLP_FF_CHUNK], preferred_element_type=F32), 0.0)
        acc = acc + _nn(u * u, wd_ref[c:c + MLP_FF_CHUNK, :])
    o_ref[...] = _rms(acc, gf_ref[...]) if final_norm else acc


def _mlp(x, g, w_up, w_down, g_final, final_norm):
    n = x.shape[0]
    tm = min(ROW_TILE, n)
    row = pl.BlockSpec((tm, D_MODEL), lambda i: (i, 0))
    vmem = _vmem(2 * _nbytes((tm, D_MODEL)), 2 * _nbytes((D_MODEL, D_FF), BF16),
                 4 * _nbytes((tm, MLP_FF_CHUNK)) + 4 * _nbytes((tm, D_MODEL)))
    return pl.pallas_call(
        functools.partial(_mlp_kernel, final_norm=final_norm),
        grid=(n // tm,),
        in_specs=[row, _full((1, D_MODEL)), _full((D_MODEL, D_FF)), _full((D_FF, D_MODEL)), _full((1, D_MODEL))],
        out_specs=row,
        out_shape=jax.ShapeDtypeStruct((n, D_MODEL), F32),
        compiler_params=_cparams(("parallel",), vmem),
        name="mlp",
    )(x, g, w_up, w_down, g_final)


class _Tiling(NamedTuple):
    seq_block: int
    time_block: int
    cross_block: int
    ret_group: int
    wkv_group: int


def _tiling(nb, t):
    if t > RET_CHUNK:
        tt = min(ROW_TILE, t)
        return _Tiling(seq_block=1, time_block=tt, cross_block=1,
                       ret_group=math.gcd(4, tt // RET_CHUNK), wkv_group=math.gcd(2, tt // RWKV_CHUNK))
    seqs = math.gcd(8, nb)
    return _Tiling(seq_block=seqs, time_block=t, cross_block=math.gcd(4, nb), ret_group=seqs, wkv_group=seqs)


def _layer(x, pos, s_ret, s_wkv, s_shift, mem_k, mem_v, w, g_final, final_norm):
    nb, t, _ = x.shape
    n = nb * t
    tl = _tiling(nb, t)
    xf = x.reshape(n, D_MODEL)
    zg, zr, zw = _in_proj(xf, w["g_mix"], w["w_in"])
    o_ret, s_ret_new = _retention(zr.reshape(nb, t, 4 * RET_W), s_ret, pos, tl.seq_block, tl.time_block, tl.ret_group)
    o_wkv, s_wkv_new, shift_new = _rwkv7(zw.reshape(nb, t, RWKV_IN_W), s_shift.reshape(nb, 1, RWKV_IN_W), s_wkv,
                                         w, tl.seq_block, tl.time_block, tl.wkv_group)
    x1 = _merge(xf, zg, o_ret.reshape(n, RET_W), o_wkv.reshape(n, RWKV_W),
                w["w_branch_a"], w["w_branch_b"], w["w_out"])
    x2 = _cross(x1.reshape(nb, t, D_MODEL), mem_k, mem_v, w["g_cross"], w["w_cq"], w["w_co"],
                tl.cross_block, tl.time_block)
    y = _mlp(x2.reshape(n, D_MODEL), w["g_mlp"], w["w_up"], w["w_down"], g_final, final_norm)
    return y.reshape(nb, t, D_MODEL), s_ret_new, s_wkv_new, shift_new.reshape(nb, RWKV_IN_W)


def _layer_weights(l, g_mix, w_in, w_branch_a, w_branch_b, w_out, mu_shift, w0, w_decay_up, a0, w_a_up, w_g_up,
                   k_k, k_a, r_k, lnx_g, lnx_b, g_cross, w_cq, w_co, g_mlp, w_up, w_down):
    vec = lambda v: v[l].reshape(1, -1).astype(F32)
    mat = lambda m: m[l].astype(BF16)
    return dict(g_mix=vec(g_mix), w_in=mat(w_in), w_branch_a=mat(w_branch_a), w_branch_b=mat(w_branch_b),
                w_out=mat(w_out), mu=vec(mu_shift), w0=vec(w0), w_decay_up=mat(w_decay_up), a0=vec(a0),
                w_a_up=mat(w_a_up), w_g_up=mat(w_g_up), k_k=vec(k_k), k_a=vec(k_a), r_k=vec(r_k), lnx_g=vec(lnx_g),
                lnx_b=vec(lnx_b), g_cross=vec(g_cross), w_cq=mat(w_cq), w_co=mat(w_co), g_mlp=vec(g_mlp),
                w_up=mat(w_up), w_down=mat(w_down))


def kernel(x_prompt, x_sample, mem_prompt, state_ret, state_wkv, state_shift, cache_mem_k, cache_mem_v, g_mix, w_in, w_branch_a, w_branch_b, w_out, mu_shift, w0, w_decay_up, a0, w_a_up, w_g_up, k_k, k_a, r_k, lnx_g, lnx_b, g_cross, g_mem, w_cq, w_ck, w_cv, w_co, g_mlp, w_up, w_down, g_final):
    depth = w_in.shape[0]
    bp, tp, _ = x_prompt.shape
    bs, ts, _ = x_sample.shape
    pos_p = jnp.arange(tp, dtype=jnp.int32)
    pos_s = PAST_LEN + jnp.arange(ts, dtype=jnp.int32)
    gf = g_final.reshape(1, D_MODEL)

    xp, xs = x_prompt, x_sample
    outs_p, outs_s = [], []
    for l in range(depth):
        w = _layer_weights(l, g_mix, w_in, w_branch_a, w_branch_b, w_out, mu_shift, w0, w_decay_up, a0, w_a_up,
                           w_g_up, k_k, k_a, r_k, lnx_g, lnx_b, g_cross, w_cq, w_co, g_mlp, w_up, w_down)
        last = l == depth - 1
        k_tiles, v_tiles, k_att, v_att = _mem_kv(mem_prompt.reshape(bp * MEM_LEN, D_MODEL),
                                                 g_mem[l].reshape(1, D_MODEL), w_ck[l].astype(BF16), w_cv[l].astype(BF16))
        xp, sr, sw, ss = _layer(
            xp, pos_p,
            jnp.zeros((bp, RET_HEADS, RET_HD, RET_HD), F32), jnp.zeros((bp, RWKV_HEADS, RWKV_HD, RWKV_HD), F32),
            jnp.zeros((bp, RWKV_IN_W), F32), k_att.reshape(bp, MEM_LEN, D_MODEL), v_att.reshape(bp, MEM_LEN, D_MODEL),
            w, gf, last)
        outs_p.append((sr, sw, ss, _from_tile_order(k_tiles.reshape(bp, MEM_ROWS, LANES)),
                       _from_tile_order(v_tiles.reshape(bp, MEM_ROWS, LANES))))
        xs, sr2, sw2, ss2 = _layer(
            xs, pos_s, state_ret[l], state_wkv[l], state_shift[l],
            _tile_order(cache_mem_k[l]), _tile_order(cache_mem_v[l]), w, gf, last)
        outs_s.append((sr2, sw2, ss2))

    stack = lambda items, i: jnp.stack([it[i] for it in items])
    return (xp, xs, stack(outs_p, 0), stack(outs_p, 1), stack(outs_p, 2), stack(outs_p, 3), stack(outs_p, 4),
            stack(outs_s, 0), stack(outs_s, 1), stack(outs_s, 2))
```

```python
import functools
import math
from typing import NamedTuple

import jax
import jax.numpy as jnp
from jax import lax
from jax.experimental import pallas as pl
from jax.experimental.pallas import tpu as pltpu

F32 = jnp.float32
BF16 = jnp.bfloat16

D_MODEL = 1024
PAST_LEN = 16384
RET_HEADS = 4
RET_HD = 128
RET_W = RET_HEADS * RET_HD
RET_CHUNK = 128
RET_GN_EPS = 1e-5
ROPE_BASE = 10000.0
RWKV_HEADS = 8
RWKV_HD = 64
RWKV_W = RWKV_HEADS * RWKV_HD
DECAY_LORA = 64
AAA_LORA = 64
GATE_LORA = 128
RWKV_GN_EPS = 64e-5
RWKV_IN_W = 3 * RWKV_W + DECAY_LORA + AAA_LORA + GATE_LORA
GATE_W = 2 * D_MODEL
O_RET = 2 * D_MODEL
O_RWKV = O_RET + 4 * RET_W
IN_W = O_RWKV + RWKV_IN_W
MEM_LEN = 256
X_HEADS = 4
X_HD = D_MODEL // X_HEADS
D_FF = 4 * D_MODEL
RMS_EPS = 1e-6

V7X_VMEM_BYTES = 64 * 1024 * 1024
VMEM_CAP_BYTES = V7X_VMEM_BYTES - 8 * 1024 * 1024
ROW_TILE = 512
RWKV_CHUNK = 64


def _cparams(sem, vmem_bytes):
    return pltpu.CompilerParams(dimension_semantics=sem, vmem_limit_bytes=int(min(vmem_bytes, VMEM_CAP_BYTES)))


def _nbytes(shape, dtype=F32):
    return math.prod(shape) * jnp.dtype(dtype).itemsize


def _nn(a, b):
    return jnp.dot(a.astype(BF16), b.astype(BF16), preferred_element_type=F32)


def _nt(a, b):
    return lax.dot_general(a.astype(BF16), b.astype(BF16), (((1,), (1,)), ((), ())), preferred_element_type=F32)


def _tn(a, b):
    return lax.dot_general(a.astype(BF16), b.astype(BF16), (((0,), (0,)), ((), ())), preferred_element_type=F32)


def _split(a):
    hi = a.astype(BF16)
    lo = (a - hi.astype(F32)).astype(BF16)
    return hi, lo


def _dg3(a, b, dims):
    ah, al = _split(a)
    bh, bl = _split(b)
    dn = (dims, ((), ()))
    return (lax.dot_general(ah, bh, dn, preferred_element_type=F32)
            + lax.dot_general(ah, bl, dn, preferred_element_type=F32)
            + lax.dot_general(al, bh, dn, preferred_element_type=F32))


def _nn3(a, b):
    return _dg3(a, b, ((1,), (0,)))


def _nt3(a, b):
    return _dg3(a, b, ((1,), (1,)))


def _tn3(a, b):
    return _dg3(a, b, ((0,), (0,)))


def _rms(x, g):
    return x * lax.rsqrt(jnp.mean(x * x, axis=-1, keepdims=True) + RMS_EPS) * g


def _head_norm(o, eps):
    mu = jnp.mean(o, axis=-1, keepdims=True)
    d = o - mu
    var = jnp.mean(d * d, axis=-1, keepdims=True)
    return d * lax.rsqrt(var + eps)


def _full(shape):
    zeros = (0,) * len(shape)
    return pl.BlockSpec(shape, lambda *_: zeros, pipeline_mode=pl.Buffered(1))


def _vmem(pipelined, resident, temps):
    return 2 * pipelined + resident + temps


def _in_proj_kernel(x_ref, g_ref, w_ref, zg_ref, zr_ref, zw_ref):
    h = _rms(x_ref[...], g_ref[...]).astype(BF16)
    zg_ref[...] = jnp.dot(h, w_ref[:, 0:O_RET], preferred_element_type=F32)
    zr_ref[...] = jnp.dot(h, w_ref[:, O_RET:O_RWKV], preferred_element_type=F32)
    zw_ref[...] = jnp.dot(h, w_ref[:, O_RWKV:IN_W], preferred_element_type=F32)


def _in_proj(x, g, w_in):
    n = x.shape[0]
    tm = min(ROW_TILE, n)
    vmem = _vmem(_nbytes((tm, D_MODEL + IN_W)), _nbytes((D_MODEL, IN_W), BF16), _nbytes((tm, 4 * RET_W + D_MODEL)))
    return pl.pallas_call(
        _in_proj_kernel,
        grid=(n // tm,),
        in_specs=[pl.BlockSpec((tm, D_MODEL), lambda i: (i, 0)), _full((1, D_MODEL)), _full((D_MODEL, IN_W))],
        out_specs=[pl.BlockSpec((tm, GATE_W), lambda i: (i, 0)),
                   pl.BlockSpec((tm, 4 * RET_W), lambda i: (i, 0)),
                   pl.BlockSpec((tm, RWKV_IN_W), lambda i: (i, 0))],
        out_shape=[jax.ShapeDtypeStruct((n, GATE_W), F32), jax.ShapeDtypeStruct((n, 4 * RET_W), F32),
                   jax.ShapeDtypeStruct((n, RWKV_IN_W), F32)],
        compiler_params=_cparams(("parallel",), vmem),
        name="in_proj",
    )(x, g, w_in)


LANES = 128
MEM_ROWS = MEM_LEN * X_HEADS * (X_HD // LANES)
ROWS_PER_TOKEN = MEM_ROWS // MEM_LEN


def _tile_order(mem):
    nb = mem.shape[0]
    return (mem.reshape(nb, MEM_LEN, X_HEADS, X_HD // LANES, LANES).transpose(0, 1, 3, 2, 4)
            .reshape(nb, MEM_ROWS, LANES))


def _from_tile_order(raw):
    nb = raw.shape[0]
    return (raw.reshape(nb, MEM_LEN, X_HD // LANES, X_HEADS, LANES).transpose(0, 1, 3, 2, 4)
            .reshape(nb, MEM_LEN, X_HEADS, X_HD))


def _mem_kv_kernel(x_ref, g_ref, wk_ref, wv_ref, kt_ref, vt_ref, kb_ref, vb_ref, *, tm):
    h = _rms(x_ref[...], g_ref[...]).astype(BF16)
    for w_ref, t_ref, b_ref in ((wk_ref, kt_ref, kb_ref), (wv_ref, vt_ref, vb_ref)):
        y = jnp.dot(h, w_ref[...], preferred_element_type=F32)
        b_ref[...] = y.astype(BF16)
        for hd in range(X_HEADS):
            for c in range(X_HD // LANES):
                col = hd * X_HD + c * LANES
                t_ref[pl.ds(c * X_HEADS + hd, tm, stride=ROWS_PER_TOKEN), :] = y[:, col:col + LANES]


def _mem_kv(mem, g, wk, wv):
    n = mem.shape[0]
    tm = min(ROW_TILE, n)
    vmem = _vmem(4 * _nbytes((tm, D_MODEL)), 2 * _nbytes((D_MODEL, D_MODEL), BF16), 4 * _nbytes((tm, D_MODEL)))
    row = pl.BlockSpec((tm, D_MODEL), lambda i: (i, 0))
    tile = pl.BlockSpec((tm * ROWS_PER_TOKEN, LANES), lambda i: (i, 0))
    return pl.pallas_call(
        functools.partial(_mem_kv_kernel, tm=tm),
        grid=(n // tm,),
        in_specs=[row, _full((1, D_MODEL)), _full((D_MODEL, D_MODEL)), _full((D_MODEL, D_MODEL))],
        out_specs=[tile, tile, row, row],
        out_shape=[jax.ShapeDtypeStruct((n * ROWS_PER_TOKEN, LANES), F32)] * 2
        + [jax.ShapeDtypeStruct((n, D_MODEL), BF16)] * 2,
        compiler_params=_cparams(("parallel",), vmem),
        name="mem_kv",
    )(mem, g, wk, wv)


def _loop_bc(bb, nch, body):
    if bb == 1 and nch == 1:
        body(0, 0)
    elif nch == 1:
        lax.fori_loop(0, bb, lambda b, c: (body(b, 0), c)[1], 0)
    elif bb == 1:
        lax.fori_loop(0, nch, lambda i, c: (body(0, i), c)[1], 0)
    else:
        lax.fori_loop(0, bb * nch, lambda j, c: (body(j // nch, j % nch), c)[1], 0)


def _ret_kernel(zr_ref, s0_ref, cos_ref, sin_ref, dm_ref, qd_ref, kd_ref, cd_ref, o_ref, s_ref, *,
                bb, tt, chunk, group):
    @pl.when(pl.program_id(1) == 0)
    def _():
        s_ref[...] = s0_ref[...]

    nch = tt // chunk
    assert bb == 1 or nch == 1, "a block holds either one sequence or one chunk per sequence"
    assert (bb * nch) % group == 0
    heads = range(RET_HEADS)

    def body(j, carry):
        if nch == 1:
            items = [(j * group + n, 0) for n in range(group)]
        else:
            items = [(0, pl.multiple_of((j * group + n) * chunk, chunk)) for n in range(group)]
        q, k, v = [], [], []
        for b, r0 in items:
            rows = pl.ds(r0, chunk)
            cos = cos_ref[rows, :]
            sin = sin_ref[rows, :]
            for h in heads:
                qh = zr_ref[b, rows, h * RET_HD:(h + 1) * RET_HD]
                kh = zr_ref[b, rows, RET_W + h * RET_HD:RET_W + (h + 1) * RET_HD]
                q.append(qh * cos + pltpu.roll(qh, RET_HD // 2, axis=1) * sin)
                k.append((kh * cos + pltpu.roll(kh, RET_HD // 2, axis=1) * sin) * (RET_HD ** -0.5))
                v.append(zr_ref[b, rows, 2 * RET_W + h * RET_HD:2 * RET_W + (h + 1) * RET_HD])
        hd = [h for _ in items for h in heads]
        sc = [_nt(qi, ki) * dm_ref[h] for qi, ki, h in zip(q, k, hd)]
        kv = [_tn(ki * kd_ref[h], vi) for ki, vi, h in zip(k, v, hd)]
        inner = [_nn(si, vi) for si, vi in zip(sc, v)]
        states = []
        for n, (b, _) in enumerate(items):
            for h in heads:
                s = s_ref[b, h] if (nch == 1 or n == 0) else states[-RET_HEADS] * cd_ref[h] + kv[(n - 1) * RET_HEADS + h]
                states.append(s)
        last = len(items) - 1
        for n, (b, _) in enumerate(items):
            if nch == 1 or n == last:
                for h in heads:
                    i = n * RET_HEADS + h
                    s_ref[b, h] = states[i] * cd_ref[h] + kv[i]
        cross = [_nn(qi, si) * qd_ref[h] for qi, si, h in zip(q, states, hd)]
        for n, (b, r0) in enumerate(items):
            rows = pl.ds(r0, chunk)
            for h in heads:
                i = n * RET_HEADS + h
                g = zr_ref[b, rows, 3 * RET_W + h * RET_HD:3 * RET_W + (h + 1) * RET_HD]
                o_ref[b, rows, h * RET_HD:(h + 1) * RET_HD] = (_head_norm(inner[i] + cross[i], RET_GN_EPS)
                                                              * (g * jax.nn.sigmoid(g)))
        return carry

    lax.fori_loop(0, bb * nch // group, body, 0)


def _ret_consts(chunk):
    lg = jnp.log1p(-jnp.exp2(-5.0 - jnp.arange(RET_HEADS, dtype=F32)))
    idx = jnp.arange(chunk, dtype=F32)
    diff = idx[:, None] - idx[None, :]
    dmask = jnp.where(diff[None] >= 0, jnp.exp(jnp.maximum(diff, 0.0)[None] * lg[:, None, None]), 0.0)
    q_dec = jnp.exp((idx + 1.0)[None, :] * lg[:, None])
    k_dec = jnp.exp((chunk - 1.0 - idx)[None, :] * lg[:, None])
    c_dec = jnp.exp(chunk * lg)
    bc = lambda t: jnp.broadcast_to(t[:, :, None], (RET_HEADS, t.shape[1], RET_HD))
    return dmask, bc(q_dec), bc(k_dec), bc(c_dec[:, None])


def _rope_tables(pos):
    half = RET_HD // 2
    inv = 1.0 / (ROPE_BASE ** (jnp.arange(half, dtype=F32) / half))
    ang = pos.astype(F32)[:, None] * inv[None, :]
    cos, sin = jnp.cos(ang), jnp.sin(ang)
    return jnp.concatenate([cos, cos], axis=1), jnp.concatenate([-sin, sin], axis=1)


def _retention(zr, s0, pos, bb, tt, group):
    nb, t, _ = zr.shape
    chunk = math.gcd(t, RET_CHUNK)
    cos, sin = _rope_tables(pos)
    dmask, q_dec, k_dec, c_dec = _ret_consts(chunk)
    st_spec = pl.BlockSpec((bb, RET_HEADS, RET_HD, RET_HD), lambda b, j: (b, 0, 0, 0))
    tab_spec = pl.BlockSpec((tt, RET_HD), lambda b, j: (j, 0))
    vmem = _vmem(_nbytes((bb, tt, 5 * RET_W)) + 2 * _nbytes((bb, RET_HEADS, RET_HD, RET_HD)) + 2 * _nbytes((tt, RET_HD)),
                 _nbytes(dmask.shape) + 3 * _nbytes(q_dec.shape), group * 8 * _nbytes((chunk, 4 * RET_W)))
    return pl.pallas_call(
        functools.partial(_ret_kernel, bb=bb, tt=tt, chunk=chunk, group=group),
        grid=(nb // bb, t // tt),
        in_specs=[pl.BlockSpec((bb, tt, 4 * RET_W), lambda b, j: (b, j, 0)), st_spec, tab_spec, tab_spec,
                  _full(dmask.shape), _full(q_dec.shape), _full(k_dec.shape), _full(c_dec.shape)],
        out_specs=[pl.BlockSpec((bb, tt, RET_W), lambda b, j: (b, j, 0)), st_spec],
        out_shape=[jax.ShapeDtypeStruct((nb, t, RET_W), F32), jax.ShapeDtypeStruct(s0.shape, F32)],
        compiler_params=_cparams(("parallel", "arbitrary"), vmem),
        name="retention",
    )(zr, s0, cos, sin, dmask, q_dec, k_dec, c_dec)


def _wkv_kernel(zw_ref, sh0_ref, s0_ref, mu_ref, w0_ref, a0_ref, kk_ref, ka_ref, rk_ref, lng_ref, lnb_ref,
                wd_ref, wa_ref, wg_ref, tri_ref, seg_ref, o_ref, s_ref, sh_ref, *, bb, tt, chunk, group):
    @pl.when(pl.program_id(1) == 0)
    def _():
        s_ref[...] = s0_ref[...]
        sh_ref[...] = sh0_ref[...]

    C = chunk
    n_sq = max(int(math.log2(C)) - 1, 0)
    row_id = lax.broadcasted_iota(jnp.int32, (C, C), 0)
    col_id = lax.broadcasted_iota(jnp.int32, (C, C), 1)
    strict = row_id > col_id
    incl = row_id >= col_id
    eye = jnp.where(row_id == col_id, 1.0, 0.0).astype(F32)
    first_row = lax.broadcasted_iota(jnp.int32, (C, RWKV_IN_W), 0) == 0
    heads = range(RWKV_HEADS)
    hsl = [slice(h * RWKV_HD, (h + 1) * RWKV_HD) for h in heads]
    half = RWKV_W // 2

    def seg_sum(t):
        seg = seg_ref[...]
        return jnp.concatenate([_nn(t[:, :half], seg), _nn(t[:, half:], seg)], axis=1)

    def prepare(b, r0):
        pw = zw_ref[b, pl.ds(r0, C), :]
        prev = jnp.where(first_row, sh_ref[b], pltpu.roll(pw, 1, axis=0))
        sh_ref[b] = pw[C - 1:C, :]
        xm = pw + mu_ref[...] * (prev - pw)
        r = xm[:, 0:RWKV_W]
        kb = xm[:, RWKV_W:2 * RWKV_W]
        vb = xm[:, 2 * RWKV_W:3 * RWKV_W]
        o1 = 3 * RWKV_W
        wl = xm[:, o1:o1 + DECAY_LORA]
        al = xm[:, o1 + DECAY_LORA:o1 + DECAY_LORA + AAA_LORA]
        gl = xm[:, o1 + DECAY_LORA + AAA_LORA:RWKV_IN_W]
        logw = -math.exp(-0.5) * jax.nn.sigmoid(w0_ref[...] + _nn(jnp.tanh(wl), wd_ref[...]))
        a = jax.nn.sigmoid(a0_ref[...] + _nn(al, wa_ref[...]))
        g = _nn(jax.nn.sigmoid(gl), wg_ref[...])
        l1 = logw.astype(BF16)
        rem = logw - l1.astype(F32)
        l2 = rem.astype(BF16)
        l3 = (rem - l2.astype(F32)).astype(BF16)
        tri = tri_ref[...]
        cum = (jnp.dot(tri, l1, preferred_element_type=F32) + jnp.dot(tri, l2, preferred_element_type=F32)
               + jnp.dot(tri, l3, preferred_element_type=F32))
        cum_end = cum[C - 1:C, :]
        kk = kb * kk_ref[...]
        kk = kk / jnp.maximum(jnp.sqrt(seg_sum(kk * kk)), 1e-12)
        km = kb * (1.0 + (a - 1.0) * ka_ref[...])
        bv = kk * a
        e_neg = jnp.exp(-cum)
        e_end = jnp.exp(cum_end - cum)
        return dict(at=-kk * jnp.exp(cum - logw), rt=r * jnp.exp(cum), bt=bv * e_neg, kt=km * e_neg,
                    bh=bv * e_end, kh=km * e_end, v=vb, g=g, p_end=jnp.exp(cum_end),
                    bonus=seg_sum(r * km * rk_ref[...]))

    def triangular(ps):
        ph = [(p, s) for p in ps for s in hsl]
        ar = [jnp.concatenate([p["at"][:, s], p["rt"][:, s]], axis=0) for p, s in ph]
        m1 = [_nt(a, jnp.concatenate([p["bt"][:, s], p["kt"][:, s]], axis=0)) for a, (p, s) in zip(ar, ph)]
        low = [jnp.where(strict, m[:C, :C], 0.0) for m in m1]
        rab = [jnp.where(incl, m[C:, :C], 0.0) for m in m1]
        akk = [jnp.concatenate([jnp.where(strict, m[:C, C:], 0.0), jnp.where(incl, m[C:, C:], 0.0)], axis=0)
               for m in m1]
        akv = [_nn(k, p["v"][:, s]) for k, (p, s) in zip(akk, ph)]
        tinv = [eye + l for l in low]
        if n_sq:
            x = [_nn(l, l) for l in low]
        for j in range(n_sq):
            if j < n_sq - 1:
                xt = [_nn(jnp.concatenate([xi, ti], axis=0), xi) for xi, ti in zip(x, tinv)]
                x = [t[:C] for t in xt]
                tinv = [ti + t[C:] for ti, t in zip(tinv, xt)]
            else:
                tinv = [ti + _nn(ti, xi) for ti, xi in zip(tinv, x)]
        nh = RWKV_HEADS
        return [dict(ar=ar[k * nh:(k + 1) * nh], rab=rab[k * nh:(k + 1) * nh], akv=akv[k * nh:(k + 1) * nh],
                     tinv=tinv[k * nh:(k + 1) * nh]) for k in range(len(ps))]

    def advance(items, ps, qs):
        ih = [(b, p, q, h) for (b, _), p, q in zip(items, ps, qs) for h in heads]
        ars = [_nt(q["ar"][h], s_ref[b, h]) for b, p, q, h in ih]
        u = [_nn(q["tinv"][h], a[:C] + q["akv"][h][:C]) for a, (b, p, q, h) in zip(ars, ih)]
        o = [a[C:] + q["akv"][h][C:] + _nn(q["rab"][h], ui) for a, ui, (b, p, q, h) in zip(ars, u, ih)]
        upd = [_tn(jnp.concatenate([ui, p["v"][:, hsl[h]]], axis=0),
                   jnp.concatenate([p["bh"][:, hsl[h]], p["kh"][:, hsl[h]]], axis=0))
               for ui, (b, p, q, h) in zip(u, ih)]
        for up, (b, p, q, h) in zip(upd, ih):
            s_ref[b, h] = s_ref[b, h] * p["p_end"][:, hsl[h]] + up
        for k, ((b, r0), p) in enumerate(zip(items, ps)):
            ok = jnp.concatenate(o[k * RWKV_HEADS:(k + 1) * RWKV_HEADS], axis=1)
            d = ok - seg_sum(ok) * (1.0 / RWKV_HD)
            var = seg_sum(d * d) * (1.0 / RWKV_HD)
            ok = d * lax.rsqrt(var + RWKV_GN_EPS) * lng_ref[...] + lnb_ref[...]
            o_ref[b, pl.ds(r0, C), :] = (ok + p["bonus"] * p["v"]) * p["g"]

    nch = tt // C
    assert bb == 1 or nch == 1, "a block holds either one sequence or one chunk per sequence"
    assert (bb * nch) % group == 0

    def body(j, carry):
        if nch == 1:
            items = [(j * group + k, 0) for k in range(group)]
        else:
            items = [(0, pl.multiple_of((j * group + k) * C, C)) for k in range(group)]
        ps = [prepare(b, r0) for b, r0 in items]
        qs = triangular(ps)
        if nch == 1:
            advance(items, ps, qs)
        else:
            for item, p, q in zip(items, ps, qs):
                advance([item], [p], [q])
        return carry

    lax.fori_loop(0, bb * nch // group, body, 0)


def _rwkv7(zw, sh0, s0, p, bb, tt, group):
    nb, t, _ = zw.shape
    chunk = math.gcd(t, RWKV_CHUNK)
    tri = jnp.tril(jnp.ones((chunk, chunk), BF16))
    lane_head = jnp.arange(RWKV_W // 2) // RWKV_HD
    seg = (lane_head[:, None] == lane_head[None, :]).astype(BF16)
    st_spec = pl.BlockSpec((bb, RWKV_HEADS, RWKV_HD, RWKV_HD), lambda b, j: (b, 0, 0, 0))
    sh_spec = pl.BlockSpec((bb, 1, RWKV_IN_W), lambda b, j: (b, 0, 0))
    vec = _full((1, RWKV_W))
    vmem = _vmem(_nbytes((bb, tt, RWKV_IN_W + RWKV_W)) + 2 * _nbytes((bb, RWKV_HEADS, RWKV_HD, RWKV_HD))
                 + 2 * _nbytes((bb, 8, RWKV_IN_W)),
                 9 * _nbytes((8, RWKV_IN_W)) + _nbytes((DECAY_LORA + AAA_LORA + GATE_LORA + chunk, RWKV_W), BF16)
                 + _nbytes(seg.shape, BF16),
                 group * 48 * _nbytes((chunk, RWKV_IN_W)))
    return pl.pallas_call(
        functools.partial(_wkv_kernel, bb=bb, tt=tt, chunk=chunk, group=group),
        grid=(nb // bb, t // tt),
        in_specs=[pl.BlockSpec((bb, tt, RWKV_IN_W), lambda b, j: (b, j, 0)), sh_spec, st_spec,
                  _full((1, RWKV_IN_W)), vec, vec, vec, vec, vec, vec, vec,
                  _full((DECAY_LORA, RWKV_W)), _full((AAA_LORA, RWKV_W)), _full((GATE_LORA, RWKV_W)),
                  _full((chunk, chunk)), _full(seg.shape)],
        out_specs=[pl.BlockSpec((bb, tt, RWKV_W), lambda b, j: (b, j, 0)), st_spec, sh_spec],
        out_shape=[jax.ShapeDtypeStruct((nb, t, RWKV_W), F32), jax.ShapeDtypeStruct(s0.shape, F32),
                   jax.ShapeDtypeStruct(sh0.shape, F32)],
        compiler_params=_cparams(("parallel", "arbitrary"), vmem),
        name="rwkv7",
    )(zw, sh0, s0, p["mu"], p["w0"], p["a0"], p["k_k"], p["k_a"], p["r_k"], p["lnx_g"], p["lnx_b"],
      p["w_decay_up"], p["w_a_up"], p["w_g_up"], tri, seg)


def _merge_rows(x, zg, oa, ob, wa_ref, wb_ref, wo_ref):
    ga = jax.nn.sigmoid(zg[:, 0:D_MODEL])
    gb = jax.nn.sigmoid(zg[:, D_MODEL:GATE_W])
    merged = ga * _nn(oa, wa_ref[...]) + gb * _nn(ob, wb_ref[...])
    return x + _nn(merged, wo_ref[...])


def _merge_kernel(x_ref, zg_ref, oa_ref, ob_ref, wa_ref, wb_ref, wo_ref, o_ref):
    o_ref[...] = _merge_rows(x_ref[...], zg_ref[...], oa_ref[...], ob_ref[...], wa_ref, wb_ref, wo_ref)


def _merge(x, zg, o_ret, o_rwkv, wa, wb, wo):
    n = x.shape[0]
    tm = min(ROW_TILE, n)
    row = lambda w: pl.BlockSpec((tm, w), lambda i: (i, 0))
    vmem = _vmem(_nbytes((tm, 2 * D_MODEL + GATE_W + RET_W + RWKV_W)), _nbytes((2 * D_MODEL, D_MODEL), BF16),
                 6 * _nbytes((tm, D_MODEL)))
    return pl.pallas_call(
        _merge_kernel,
        grid=(n // tm,),
        in_specs=[row(D_MODEL), row(GATE_W), row(RET_W), row(RWKV_W),
                  _full((RET_W, D_MODEL)), _full((RWKV_W, D_MODEL)), _full((D_MODEL, D_MODEL))],
        out_specs=row(D_MODEL),
        out_shape=jax.ShapeDtypeStruct((n, D_MODEL), F32),
        compiler_params=_cparams(("parallel",), vmem),
        name="merge",
    )(x, zg, o_ret, o_rwkv, wa, wb, wo)


def _softmax(sc):
    e = jnp.exp(sc - jnp.max(sc, axis=-1, keepdims=True))
    return e / jnp.sum(e, axis=-1, keepdims=True)


def _attend_heads(q_scr, mk_ref, mv_ref, ox_scr, b, r0, tq):
    for h in range(X_HEADS):
        sl = slice(h * X_HD, (h + 1) * X_HD)
        att = _softmax(_nt(q_scr[pl.ds(r0, tq), sl], mk_ref[b, :, sl]) * (X_HD ** -0.5))
        ox_scr[pl.ds(r0, tq), sl] = _nn(att, mv_ref[b, :, sl])


def _attend_tiles(q_scr, mk_ref, mv_ref, ox_scr, b, r0, tq):
    halves = X_HD // LANES
    q = q_scr[pl.ds(r0, tq), :]
    qs = jnp.concatenate([q[:, h * X_HD + c * LANES:h * X_HD + (c + 1) * LANES]
                          for c in range(halves) for h in range(X_HEADS)], axis=0)
    z = _nt(qs, mk_ref[b])
    n = X_HEADS * tq
    slot = lax.broadcasted_iota(jnp.int32, (n, MEM_ROWS), 1) % ROWS_PER_TOKEN
    head = lax.broadcasted_iota(jnp.int32, (n, MEM_ROWS), 0) // tq
    own = slot == head
    part = jnp.where(own, z[:n], 0.0) + jnp.where(slot == head + X_HEADS, z[n:], 0.0)
    sc = part + pltpu.roll(part, MEM_ROWS - X_HEADS, axis=1)
    att = _softmax(jnp.where(own, sc * (X_HD ** -0.5), -jnp.inf))
    att2 = jnp.concatenate([att, pltpu.roll(att, X_HEADS, axis=1)], axis=0)
    o = _nn(att2, mv_ref[b])
    ox_scr[pl.ds(r0, tq), :] = jnp.concatenate(
        [o[(c * X_HEADS + h) * tq:(c * X_HEADS + h + 1) * tq] for h in range(X_HEADS) for c in range(halves)], axis=1)


def _cross_kernel(x_ref, mk_ref, mv_ref, g_ref, wq_ref, wo_ref, o_ref, q_scr, ox_scr, *, bb, tq, attend):
    rows = bb * tq
    x = x_ref[...].reshape(rows, D_MODEL)
    q_scr[...] = _nn(_rms(x, g_ref[...]), wq_ref[...])

    def body(b, _):
        r0 = b * tq if isinstance(b, int) else pl.multiple_of(b * tq, tq)
        attend(q_scr, mk_ref, mv_ref, ox_scr, b, r0, tq)

    _loop_bc(bb, 1, body)
    o_ref[...] = (x + _nn(ox_scr[...], wo_ref[...])).reshape(bb, tq, D_MODEL)


def _cross(x, mem_k, mem_v, g, wq, wo, bb, tq):
    nb, t, _ = x.shape
    tiled = mem_k.shape[1:] == (MEM_ROWS, LANES)
    x_spec = pl.BlockSpec((bb, tq, D_MODEL), lambda b, j: (b, j, 0))
    m_spec = pl.BlockSpec((bb,) + mem_k.shape[1:], lambda b, j: (b, 0, 0))
    vmem = _vmem(2 * _nbytes((bb, tq, D_MODEL)) + 2 * _nbytes((bb,) + mem_k.shape[1:], mem_k.dtype),
                 2 * _nbytes((D_MODEL, D_MODEL), BF16) + 2 * _nbytes((bb * tq, D_MODEL)),
                 6 * _nbytes((bb * tq, D_MODEL)) + 2 * _nbytes((MEM_LEN, D_MODEL))
                 + (8 * _nbytes((2 * X_HEADS * tq, MEM_ROWS)) if tiled else 0))
    return pl.pallas_call(
        functools.partial(_cross_kernel, bb=bb, tq=tq, attend=_attend_tiles if tiled else _attend_heads),
        grid=(nb // bb, t // tq),
        in_specs=[x_spec, m_spec, m_spec, _full((1, D_MODEL)), _full((D_MODEL, D_MODEL)), _full((D_MODEL, D_MODEL))],
        out_specs=x_spec,
        out_shape=jax.ShapeDtypeStruct(x.shape, F32),
        scratch_shapes=[pltpu.VMEM((bb * tq, D_MODEL), F32), pltpu.VMEM((bb * tq, D_MODEL), F32)],
        compiler_params=_cparams(("parallel", "arbitrary"), vmem),
        name="cross_attn",
    )(x, mem_k, mem_v, g, wq, wo)


MLP_FF_CHUNK = 1024


def _mlp_rows(x, g_ref, wu_ref, wd_ref, gf_ref, final_norm):
    h = _rms(x, g_ref[...]).astype(BF16)
    acc = x
    for c in range(0, D_FF, MLP_FF_CHUNK):
        u = jnp.maximum(jnp.dot(h, wu_ref[:, c:c + MLP_FF_CHUNK], preferred_element_type=F32), 0.0)
        acc = acc + _nn(u * u, wd_ref[c:c + MLP_FF_CHUNK, :])
    return _rms(acc, gf_ref[...]) if final_norm else acc


def _mlp_kernel(x_ref, g_ref, wu_ref, wd_ref, gf_ref, o_ref, *, final_norm):
    o_ref[...] = _mlp_rows(x_ref[...], g_ref, wu_ref, wd_ref, gf_ref, final_norm)


def _mlp(x, g, w_up, w_down, g_final, final_norm):
    n = x.shape[0]
    tm = min(ROW_TILE, n)
    row = pl.BlockSpec((tm, D_MODEL), lambda i: (i, 0))
    vmem = _vmem(2 * _nbytes((tm, D_MODEL)), 2 * _nbytes((D_MODEL, D_FF), BF16),
                 4 * _nbytes((tm, MLP_FF_CHUNK)) + 4 * _nbytes((tm, D_MODEL)))
    return pl.pallas_call(
        functools.partial(_mlp_kernel, final_norm=final_norm),
        grid=(n // tm,),
        in_specs=[row, _full((1, D_MODEL)), _full((D_MODEL, D_FF)), _full((D_FF, D_MODEL)), _full((1, D_MODEL))],
        out_specs=row,
        out_shape=jax.ShapeDtypeStruct((n, D_MODEL), F32),
        compiler_params=_cparams(("parallel",), vmem),
        name="mlp",
    )(x, g, w_up, w_down, g_final)


POST_TILE = 512


def _post_kernel(x_ref, zg_ref, oa_ref, ob_ref, mk_ref, mv_ref, wa_ref, wb_ref, wo_ref, gc_ref, wq_ref, wco_ref,
                 gm_ref, wu_ref, wd_ref, gf_ref, o_ref, q_scr, ox_scr, *, tq, final_norm):
    x1 = _merge_rows(x_ref[0], zg_ref[0], oa_ref[0], ob_ref[0], wa_ref, wb_ref, wo_ref)
    q_scr[...] = _nn(_rms(x1, gc_ref[...]), wq_ref[...])
    _attend_heads(q_scr, mk_ref, mv_ref, ox_scr, 0, 0, tq)
    x2 = x1 + _nn(ox_scr[...], wco_ref[...])
    o_ref[0] = _mlp_rows(x2, gm_ref, wu_ref, wd_ref, gf_ref, final_norm)


def _post(x, zg, o_ret, o_rwkv, mem_k, mem_v, w, g_final, final_norm):
    nb, t, _ = x.shape
    tq = min(POST_TILE, t)
    row = lambda width: pl.BlockSpec((1, tq, width), lambda b, j: (b, j, 0))
    mem = pl.BlockSpec((1, MEM_LEN, D_MODEL), lambda b, j: (b, 0, 0))
    sq = _full((D_MODEL, D_MODEL))
    vec = _full((1, D_MODEL))
    vmem = _vmem(_nbytes((tq, 2 * D_MODEL + GATE_W + RET_W + RWKV_W)) + 2 * _nbytes((MEM_LEN, D_MODEL), BF16),
                 _nbytes((4 * D_MODEL + 2 * D_FF, D_MODEL), BF16) + 2 * _nbytes((tq, D_MODEL)),
                 8 * _nbytes((tq, D_MODEL)) + 4 * _nbytes((tq, MLP_FF_CHUNK)))
    return pl.pallas_call(
        functools.partial(_post_kernel, tq=tq, final_norm=final_norm),
        grid=(nb, t // tq),
        in_specs=[row(D_MODEL), row(GATE_W), row(RET_W), row(RWKV_W), mem, mem,
                  _full((RET_W, D_MODEL)), _full((RWKV_W, D_MODEL)), sq, vec, sq, sq,
                  vec, _full((D_MODEL, D_FF)), _full((D_FF, D_MODEL)), vec],
        out_specs=row(D_MODEL),
        out_shape=jax.ShapeDtypeStruct(x.shape, F32),
        scratch_shapes=[pltpu.VMEM((tq, D_MODEL), F32), pltpu.VMEM((tq, D_MODEL), F32)],
        compiler_params=_cparams(("parallel", "arbitrary"), vmem),
        name="post",
    )(x, zg, o_ret, o_rwkv, mem_k, mem_v, w["w_branch_a"], w["w_branch_b"], w["w_out"], w["g_cross"], w["w_cq"],
      w["w_co"], w["g_mlp"], w["w_up"], w["w_down"], g_final)


class _Tiling(NamedTuple):
    seq_block: int
    time_block: int
    cross_block: int
    ret_group: int
    wkv_group: int


def _tiling(nb, t):
    if t > RET_CHUNK:
        tt = min(ROW_TILE, t)
        return _Tiling(seq_block=1, time_block=tt, cross_block=1,
                       ret_group=math.gcd(4, tt // RET_CHUNK), wkv_group=math.gcd(2, tt // RWKV_CHUNK))
    seqs = math.gcd(8, nb)
    return _Tiling(seq_block=seqs, time_block=t, cross_block=math.gcd(4, nb), ret_group=seqs, wkv_group=seqs)


def _layer(x, pos, s_ret, s_wkv, s_shift, mem_k, mem_v, w, g_final, final_norm):
    nb, t, _ = x.shape
    n = nb * t
    tl = _tiling(nb, t)
    xf = x.reshape(n, D_MODEL)
    zg, zr, zw = _in_proj(xf, w["g_mix"], w["w_in"])
    o_ret, s_ret_new = _retention(zr.reshape(nb, t, 4 * RET_W), s_ret, pos, tl.seq_block, tl.time_block, tl.ret_group)
    o_wkv, s_wkv_new, shift_new = _rwkv7(zw.reshape(nb, t, RWKV_IN_W), s_shift.reshape(nb, 1, RWKV_IN_W), s_wkv,
                                         w, tl.seq_block, tl.time_block, tl.wkv_group)
    if mem_k.shape[1:] == (MEM_LEN, D_MODEL):
        y = _post(x, zg.reshape(nb, t, GATE_W), o_ret, o_wkv, mem_k, mem_v, w, g_final, final_norm)
    else:
        x1 = _merge(xf, zg, o_ret.reshape(n, RET_W), o_wkv.reshape(n, RWKV_W),
                    w["w_branch_a"], w["w_branch_b"], w["w_out"])
        x2 = _cross(x1.reshape(nb, t, D_MODEL), mem_k, mem_v, w["g_cross"], w["w_cq"], w["w_co"],
                    tl.cross_block, tl.time_block)
        y = _mlp(x2.reshape(n, D_MODEL), w["g_mlp"], w["w_up"], w["w_down"], g_final, final_norm).reshape(x.shape)
    return y, s_ret_new, s_wkv_new, shift_new.reshape(nb, RWKV_IN_W)


def _layer_weights(l, g_mix, w_in, w_branch_a, w_branch_b, w_out, mu_shift, w0, w_decay_up, a0, w_a_up, w_g_up,
                   k_k, k_a, r_k, lnx_g, lnx_b, g_cross, w_cq, w_co, g_mlp, w_up, w_down):
    vec = lambda v: v[l].reshape(1, -1).astype(F32)
    mat = lambda m: m[l].astype(BF16)
    return dict(g_mix=vec(g_mix), w_in=mat(w_in), w_branch_a=mat(w_branch_a), w_branch_b=mat(w_branch_b),
                w_out=mat(w_out), mu=vec(mu_shift), w0=vec(w0), w_decay_up=mat(w_decay_up), a0=vec(a0),
                w_a_up=mat(w_a_up), w_g_up=mat(w_g_up), k_k=vec(k_k), k_a=vec(k_a), r_k=vec(r_k), lnx_g=vec(lnx_g),
                lnx_b=vec(lnx_b), g_cross=vec(g_cross), w_cq=mat(w_cq), w_co=mat(w_co), g_mlp=vec(g_mlp),
                w_up=mat(w_up), w_down=mat(w_down))


def kernel(x_prompt, x_sample, mem_prompt, state_ret, state_wkv, state_shift, cache_mem_k, cache_mem_v, g_mix, w_in, w_branch_a, w_branch_b, w_out, mu_shift, w0, w_decay_up, a0, w_a_up, w_g_up, k_k, k_a, r_k, lnx_g, lnx_b, g_cross, g_mem, w_cq, w_ck, w_cv, w_co, g_mlp, w_up, w_down, g_final):
    depth = w_in.shape[0]
    bp, tp, _ = x_prompt.shape
    bs, ts, _ = x_sample.shape
    pos_p = jnp.arange(tp, dtype=jnp.int32)
    pos_s = PAST_LEN + jnp.arange(ts, dtype=jnp.int32)
    gf = g_final.reshape(1, D_MODEL)

    xp, xs = x_prompt, x_sample
    outs_p, outs_s = [], []
    for l in range(depth):
        w = _layer_weights(l, g_mix, w_in, w_branch_a, w_branch_b, w_out, mu_shift, w0, w_decay_up, a0, w_a_up,
                           w_g_up, k_k, k_a, r_k, lnx_g, lnx_b, g_cross, w_cq, w_co, g_mlp, w_up, w_down)
        last = l == depth - 1
        k_tiles, v_tiles, k_att, v_att = _mem_kv(mem_prompt.reshape(bp * MEM_LEN, D_MODEL),
                                                 g_mem[l].reshape(1, D_MODEL), w_ck[l].astype(BF16), w_cv[l].astype(BF16))
        xp, sr, sw, ss = _layer(
            xp, pos_p,
            jnp.zeros((bp, RET_HEADS, RET_HD, RET_HD), F32), jnp.zeros((bp, RWKV_HEADS, RWKV_HD, RWKV_HD), F32),
            jnp.zeros((bp, RWKV_IN_W), F32), k_att.reshape(bp, MEM_LEN, D_MODEL), v_att.reshape(bp, MEM_LEN, D_MODEL),
            w, gf, last)
        outs_p.append((sr, sw, ss, _from_tile_order(k_tiles.reshape(bp, MEM_ROWS, LANES)),
                       _from_tile_order(v_tiles.reshape(bp, MEM_ROWS, LANES))))
        xs, sr2, sw2, ss2 = _layer(
            xs, pos_s, state_ret[l], state_wkv[l], state_shift[l],
            _tile_order(cache_mem_k[l]), _tile_order(cache_mem_v[l]), w, gf, last)
        outs_s.append((sr2, sw2, ss2))

    stack = lambda items, i: jnp.stack([it[i] for it in items])
    return (xp, xs, stack(outs_p, 0), stack(outs_p, 1), stack(outs_p, 2), stack(outs_p, 3), stack(outs_p, 4),
            stack(outs_s, 0), stack(outs_s, 1), stack(outs_s, 2))
```

```python
import functools
import math
from typing import NamedTuple

import jax
import jax.numpy as jnp
from jax import lax
from jax.experimental import pallas as pl
from jax.experimental.pallas import tpu as pltpu

F32 = jnp.float32
BF16 = jnp.bfloat16

D_MODEL = 1024
PAST_LEN = 16384
RET_HEADS = 4
RET_HD = 128
RET_W = RET_HEADS * RET_HD
RET_CHUNK = 128
RET_GN_EPS = 1e-5
ROPE_BASE = 10000.0
RWKV_HEADS = 8
RWKV_HD = 64
RWKV_W = RWKV_HEADS * RWKV_HD
DECAY_LORA = 64
AAA_LORA = 64
GATE_LORA = 128
RWKV_GN_EPS = 64e-5
RWKV_IN_W = 3 * RWKV_W + DECAY_LORA + AAA_LORA + GATE_LORA
GATE_W = 2 * D_MODEL
O_RET = 2 * D_MODEL
O_RWKV = O_RET + 4 * RET_W
IN_W = O_RWKV + RWKV_IN_W
MEM_LEN = 256
X_HEADS = 4
X_HD = D_MODEL // X_HEADS
D_FF = 4 * D_MODEL
RMS_EPS = 1e-6

V7X_VMEM_BYTES = 64 * 1024 * 1024
VMEM_CAP_BYTES = V7X_VMEM_BYTES - 8 * 1024 * 1024
ROW_TILE = 512


def _cparams(sem, vmem_bytes):
    return pltpu.CompilerParams(dimension_semantics=sem, vmem_limit_bytes=int(min(vmem_bytes, VMEM_CAP_BYTES)))


def _nbytes(shape, dtype=F32):
    return math.prod(shape) * jnp.dtype(dtype).itemsize


def _nn(a, b):
    return jnp.dot(a.astype(BF16), b.astype(BF16), preferred_element_type=F32)


def _nt(a, b):
    return lax.dot_general(a.astype(BF16), b.astype(BF16), (((1,), (1,)), ((), ())), preferred_element_type=F32)


def _tn(a, b):
    return lax.dot_general(a.astype(BF16), b.astype(BF16), (((0,), (0,)), ((), ())), preferred_element_type=F32)


def _rms(x, g):
    return x * lax.rsqrt(jnp.mean(x * x, axis=-1, keepdims=True) + RMS_EPS) * g


def _head_norm(o, eps):
    mu = jnp.mean(o, axis=-1, keepdims=True)
    d = o - mu
    var = jnp.mean(d * d, axis=-1, keepdims=True)
    return d * lax.rsqrt(var + eps)


def _full(shape):
    zeros = (0,) * len(shape)
    return pl.BlockSpec(shape, lambda *_: zeros, pipeline_mode=pl.Buffered(1))


def _vmem(pipelined, resident, temps):
    return 2 * pipelined + resident + temps


def _in_proj_kernel(x_ref, g_ref, w_ref, zg_ref, zr_ref, zw_ref):
    h = _rms(x_ref[...], g_ref[...]).astype(BF16)
    zg_ref[...] = jnp.dot(h, w_ref[:, 0:O_RET], preferred_element_type=F32)
    zr_ref[...] = jnp.dot(h, w_ref[:, O_RET:O_RWKV], preferred_element_type=F32)
    zw_ref[...] = jnp.dot(h, w_ref[:, O_RWKV:IN_W], preferred_element_type=F32)


def _in_proj(x, g, w_in):
    n = x.shape[0]
    tm = min(ROW_TILE, n)
    vmem = _vmem(_nbytes((tm, D_MODEL + IN_W)), _nbytes((D_MODEL, IN_W), BF16), _nbytes((tm, 4 * RET_W + D_MODEL)))
    return pl.pallas_call(
        _in_proj_kernel,
        grid=(n // tm,),
        in_specs=[pl.BlockSpec((tm, D_MODEL), lambda i: (i, 0)), _full((1, D_MODEL)), _full((D_MODEL, IN_W))],
        out_specs=[pl.BlockSpec((tm, GATE_W), lambda i: (i, 0)),
                   pl.BlockSpec((tm, 4 * RET_W), lambda i: (i, 0)),
                   pl.BlockSpec((tm, RWKV_IN_W), lambda i: (i, 0))],
        out_shape=[jax.ShapeDtypeStruct((n, GATE_W), F32), jax.ShapeDtypeStruct((n, 4 * RET_W), F32),
                   jax.ShapeDtypeStruct((n, RWKV_IN_W), F32)],
        compiler_params=_cparams(("parallel",), vmem),
        name="in_proj",
    )(x, g, w_in)


LANES = 128
MEM_ROWS = MEM_LEN * X_HEADS * (X_HD // LANES)
ROWS_PER_TOKEN = MEM_ROWS // MEM_LEN


def _tile_order(mem):
    nb = mem.shape[0]
    return (mem.reshape(nb, MEM_LEN, X_HEADS, X_HD // LANES, LANES).transpose(0, 1, 3, 2, 4)
            .reshape(nb, MEM_ROWS, LANES))


def _from_tile_order(raw):
    nb = raw.shape[0]
    return (raw.reshape(nb, MEM_LEN, X_HD // LANES, X_HEADS, LANES).transpose(0, 1, 3, 2, 4)
            .reshape(nb, MEM_LEN, X_HEADS, X_HD))


def _mem_kv_kernel(x_ref, g_ref, wk_ref, wv_ref, kt_ref, vt_ref, kb_ref, vb_ref, *, tm):
    h = _rms(x_ref[...], g_ref[...]).astype(BF16)
    for w_ref, t_ref, b_ref in ((wk_ref, kt_ref, kb_ref), (wv_ref, vt_ref, vb_ref)):
        y = jnp.dot(h, w_ref[...], preferred_element_type=F32)
        b_ref[...] = y.astype(BF16)
        for hd in range(X_HEADS):
            for c in range(X_HD // LANES):
                col = hd * X_HD + c * LANES
                t_ref[pl.ds(c * X_HEADS + hd, tm, stride=ROWS_PER_TOKEN), :] = y[:, col:col + LANES]


def _mem_kv(mem, g, wk, wv):
    n = mem.shape[0]
    tm = min(ROW_TILE, n)
    vmem = _vmem(4 * _nbytes((tm, D_MODEL)), 2 * _nbytes((D_MODEL, D_MODEL), BF16), 4 * _nbytes((tm, D_MODEL)))
    row = pl.BlockSpec((tm, D_MODEL), lambda i: (i, 0))
    tile = pl.BlockSpec((tm * ROWS_PER_TOKEN, LANES), lambda i: (i, 0))
    return pl.pallas_call(
        functools.partial(_mem_kv_kernel, tm=tm),
        grid=(n // tm,),
        in_specs=[row, _full((1, D_MODEL)), _full((D_MODEL, D_MODEL)), _full((D_MODEL, D_MODEL))],
        out_specs=[tile, tile, row, row],
        out_shape=[jax.ShapeDtypeStruct((n * ROWS_PER_TOKEN, LANES), F32)] * 2
        + [jax.ShapeDtypeStruct((n, D_MODEL), BF16)] * 2,
        compiler_params=_cparams(("parallel",), vmem),
        name="mem_kv",
    )(mem, g, wk, wv)


def _loop_bc(bb, nch, body):
    if bb == 1 and nch == 1:
        body(0, 0)
    elif nch == 1:
        lax.fori_loop(0, bb, lambda b, c: (body(b, 0), c)[1], 0)
    elif bb == 1:
        lax.fori_loop(0, nch, lambda i, c: (body(0, i), c)[1], 0)
    else:
        lax.fori_loop(0, bb * nch, lambda j, c: (body(j // nch, j % nch), c)[1], 0)


def _ret_kernel(zr_ref, s0_ref, cos_ref, sin_ref, dm_ref, qd_ref, kd_ref, cd_ref, o_ref, s_ref, *,
                bb, tt, chunk, group):
    @pl.when(pl.program_id(1) == 0)
    def _():
        s_ref[...] = s0_ref[...]

    nch = tt // chunk
    assert bb == 1 or nch == 1, "a block holds either one sequence or one chunk per sequence"
    assert (bb * nch) % group == 0
    heads = range(RET_HEADS)

    def body(j, carry):
        if nch == 1:
            items = [(j * group + n, 0) for n in range(group)]
        else:
            items = [(0, pl.multiple_of((j * group + n) * chunk, chunk)) for n in range(group)]
        q, k, v = [], [], []
        for b, r0 in items:
            rows = pl.ds(r0, chunk)
            cos = cos_ref[rows, :]
            sin = sin_ref[rows, :]
            for h in heads:
                qh = zr_ref[b, rows, h * RET_HD:(h + 1) * RET_HD]
                kh = zr_ref[b, rows, RET_W + h * RET_HD:RET_W + (h + 1) * RET_HD]
                q.append(qh * cos + pltpu.roll(qh, RET_HD // 2, axis=1) * sin)
                k.append((kh * cos + pltpu.roll(kh, RET_HD // 2, axis=1) * sin) * (RET_HD ** -0.5))
                v.append(zr_ref[b, rows, 2 * RET_W + h * RET_HD:2 * RET_W + (h + 1) * RET_HD])
        hd = [h for _ in items for h in heads]
        sc = [_nt(qi, ki) * dm_ref[h] for qi, ki, h in zip(q, k, hd)]
        kv = [_tn(ki * kd_ref[h], vi) for ki, vi, h in zip(k, v, hd)]
        inner = [_nn(si, vi) for si, vi in zip(sc, v)]
        states = []
        for n, (b, _) in enumerate(items):
            for h in heads:
                s = s_ref[b, h] if (nch == 1 or n == 0) else states[-RET_HEADS] * cd_ref[h] + kv[(n - 1) * RET_HEADS + h]
                states.append(s)
        last = len(items) - 1
        for n, (b, _) in enumerate(items):
            if nch == 1 or n == last:
                for h in heads:
                    i = n * RET_HEADS + h
                    s_ref[b, h] = states[i] * cd_ref[h] + kv[i]
        cross = [_nn(qi, si) * qd_ref[h] for qi, si, h in zip(q, states, hd)]
        for n, (b, r0) in enumerate(items):
            rows = pl.ds(r0, chunk)
            for h in heads:
                i = n * RET_HEADS + h
                g = zr_ref[b, rows, 3 * RET_W + h * RET_HD:3 * RET_W + (h + 1) * RET_HD]
                o_ref[b, rows, h * RET_HD:(h + 1) * RET_HD] = (_head_norm(inner[i] + cross[i], RET_GN_EPS)
                                                              * (g * jax.nn.sigmoid(g)))
        return carry

    lax.fori_loop(0, bb * nch // group, body, 0)


def _ret_consts(chunk):
    lg = jnp.log1p(-jnp.exp2(-5.0 - jnp.arange(RET_HEADS, dtype=F32)))
    idx = jnp.arange(chunk, dtype=F32)
    diff = idx[:, None] - idx[None, :]
    dmask = jnp.where(diff[None] >= 0, jnp.exp(jnp.maximum(diff, 0.0)[None] * lg[:, None, None]), 0.0)
    q_dec = jnp.exp((idx + 1.0)[None, :] * lg[:, None])
    k_dec = jnp.exp((chunk - 1.0 - idx)[None, :] * lg[:, None])
    c_dec = jnp.exp(chunk * lg)
    bc = lambda t: jnp.broadcast_to(t[:, :, None], (RET_HEADS, t.shape[1], RET_HD))
    return dmask, bc(q_dec), bc(k_dec), bc(c_dec[:, None])


def _rope_tables(pos):
    half = RET_HD // 2
    inv = 1.0 / (ROPE_BASE ** (jnp.arange(half, dtype=F32) / half))
    ang = pos.astype(F32)[:, None] * inv[None, :]
    cos, sin = jnp.cos(ang), jnp.sin(ang)
    return jnp.concatenate([cos, cos], axis=1), jnp.concatenate([-sin, sin], axis=1)


def _retention(zr, s0, pos, bb, tt, group):
    nb, t, _ = zr.shape
    chunk = math.gcd(t, RET_CHUNK)
    cos, sin = _rope_tables(pos)
    dmask, q_dec, k_dec, c_dec = _ret_consts(chunk)
    st_spec = pl.BlockSpec((bb, RET_HEADS, RET_HD, RET_HD), lambda b, j: (b, 0, 0, 0))
    tab_spec = pl.BlockSpec((tt, RET_HD), lambda b, j: (j, 0))
    vmem = _vmem(_nbytes((bb, tt, 5 * RET_W)) + 2 * _nbytes((bb, RET_HEADS, RET_HD, RET_HD)) + 2 * _nbytes((tt, RET_HD)),
                 _nbytes(dmask.shape) + 3 * _nbytes(q_dec.shape), group * 8 * _nbytes((chunk, 4 * RET_W)))
    return pl.pallas_call(
        functools.partial(_ret_kernel, bb=bb, tt=tt, chunk=chunk, group=group),
        grid=(nb // bb, t // tt),
        in_specs=[pl.BlockSpec((bb, tt, 4 * RET_W), lambda b, j: (b, j, 0)), st_spec, tab_spec, tab_spec,
                  _full(dmask.shape), _full(q_dec.shape), _full(k_dec.shape), _full(c_dec.shape)],
        out_specs=[pl.BlockSpec((bb, tt, RET_W), lambda b, j: (b, j, 0)), st_spec],
        out_shape=[jax.ShapeDtypeStruct((nb, t, RET_W), F32), jax.ShapeDtypeStruct(s0.shape, F32)],
        compiler_params=_cparams(("parallel", "arbitrary"), vmem),
        name="retention",
    )(zr, s0, cos, sin, dmask, q_dec, k_dec, c_dec)


WKV_BLOCK = 64
WKV_SCRATCH = 10


def _pow2(n):
    return n & (n - 1) == 0


def _imod(x, n):
    return jnp.bitwise_and(x, n - 1) if _pow2(n) else lax.rem(x, n)


def _idiv(x, n):
    return jnp.right_shift(x, n.bit_length() - 1) if _pow2(n) else lax.div(x, n)


def _wkv_kernel(zw_ref, sh0_ref, s0_ref, mu_ref, w0_ref, a0_ref, kk_ref, ka_ref, rk_ref, lng_ref, lnb_ref,
                wd_ref, wa_ref, wg_ref, tri_ref, seg_ref, o_ref, s_ref, sh_ref,
                at_s, rt_s, bt_s, kt_s, bh_s, kh_s, v_s, g_s, bonus_s, pend_s, *, bb, tt, sub, group):
    @pl.when(pl.program_id(1) == 0)
    def _():
        s_ref[...] = s0_ref[...]
        sh_ref[...] = sh0_ref[...]

    C = WKV_BLOCK
    R = bb * tt
    nblk = R // C
    nseq = C // sub
    assert (bb == 1 and sub == C) or (nblk == 1 and sub == tt), "one sequence per tile, or whole sequences in one block"
    assert nblk % group == 0
    n_sq = max(int(math.log2(sub)) - 1, 0)
    heads = range(RWKV_HEADS)
    hsl = [slice(h * RWKV_HD, (h + 1) * RWKV_HD) for h in heads]
    half = RWKV_W // 2

    def seg_sum(t):
        seg = seg_ref[...]
        return jnp.concatenate([_nn(t[:, :half], seg), _nn(t[:, half:], seg)], axis=1)

    pw = zw_ref[...].reshape(R, RWKV_IN_W)
    row = lax.broadcasted_iota(jnp.int32, (R, RWKV_IN_W), 0)
    rolled = pltpu.roll(pw, 1, axis=0)
    if bb == 1:
        prev = jnp.where(row == 0, sh_ref[0], rolled)
        sh_ref[0] = pw[R - 1:R, :]
    else:
        carry = jnp.broadcast_to(sh_ref[...], (bb, tt, RWKV_IN_W)).reshape(R, RWKV_IN_W)
        prev = jnp.where(_imod(row, tt) == 0, carry, rolled)
        sh_ref[...] = pw.reshape(bb, tt, RWKV_IN_W)[:, tt - 1:tt, :]
    xm = pw + mu_ref[...] * (prev - pw)
    r = xm[:, 0:RWKV_W]
    kb = xm[:, RWKV_W:2 * RWKV_W]
    o1 = 3 * RWKV_W
    wl = xm[:, o1:o1 + DECAY_LORA]
    al = xm[:, o1 + DECAY_LORA:o1 + DECAY_LORA + AAA_LORA]
    gl = xm[:, o1 + DECAY_LORA + AAA_LORA:RWKV_IN_W]
    logw = -math.exp(-0.5) * jax.nn.sigmoid(w0_ref[...] + _nn(jnp.tanh(wl), wd_ref[...]))
    a = jax.nn.sigmoid(a0_ref[...] + _nn(al, wa_ref[...]))
    l1 = logw.astype(BF16)
    rem = logw - l1.astype(F32)
    l2 = rem.astype(BF16)
    l3 = (rem - l2.astype(F32)).astype(BF16)
    tri = tri_ref[...]
    cum = jnp.concatenate(
        [sum(jnp.dot(tri, part[k * C:(k + 1) * C], preferred_element_type=F32) for part in (l1, l2, l3))
         for k in range(nblk)], axis=0)
    cum_end = jnp.broadcast_to(cum.reshape(R // sub, sub, RWKV_W)[:, sub - 1:sub, :],
                               (R // sub, sub, RWKV_W)).reshape(R, RWKV_W)
    kk = kb * kk_ref[...]
    kk = kk * jnp.minimum(lax.rsqrt(seg_sum(kk * kk)), 1e12)
    km = kb * (1.0 + (a - 1.0) * ka_ref[...])
    bv = kk * a
    p_end = jnp.exp(cum_end)
    e_neg = jnp.exp(-cum)
    e_end = p_end * e_neg
    at_s[...] = -kk * jnp.exp(cum - logw)
    rt_s[...] = r * jnp.exp(cum)
    bt_s[...] = bv * e_neg
    kt_s[...] = km * e_neg
    bh_s[...] = bv * e_end
    kh_s[...] = km * e_end
    v_s[...] = xm[:, 2 * RWKV_W:3 * RWKV_W]
    g_s[...] = _nn(jax.nn.sigmoid(gl), wg_ref[...])
    bonus_s[...] = seg_sum(r * km * rk_ref[...])
    pend_s[...] = p_end

    row_id = lax.broadcasted_iota(jnp.int32, (C, C), 0)
    col_id = lax.broadcasted_iota(jnp.int32, (C, C), 1)
    strict = row_id > col_id
    incl = row_id >= col_id
    if nseq > 1:
        same = _idiv(row_id, sub) == _idiv(col_id, sub)
        strict = jnp.logical_and(strict, same)
        incl = jnp.logical_and(incl, same)
    eye = jnp.where(row_id == col_id, 1.0, 0.0).astype(F32)

    def triangular(r0s):
        rows = [pl.ds(r0, C) for r0 in r0s for _ in heads]
        sl = [s for _ in r0s for s in hsl]
        at = [at_s[rw, :][:, s] for rw, s in zip(rows, sl)]
        rt = [rt_s[rw, :][:, s] for rw, s in zip(rows, sl)]
        v = [v_s[rw, :][:, s] for rw, s in zip(rows, sl)]
        m1 = [_nt(jnp.concatenate([a_, r_], axis=0), jnp.concatenate([bt_s[rw, :][:, s], kt_s[rw, :][:, s]], axis=0))
              for a_, r_, rw, s in zip(at, rt, rows, sl)]
        low = [jnp.where(strict, m[:C, :C], 0.0) for m in m1]
        rab = [jnp.where(incl, m[C:, :C], 0.0) for m in m1]
        akk = [jnp.concatenate([jnp.where(strict, m[:C, C:], 0.0), jnp.where(incl, m[C:, C:], 0.0)], axis=0)
               for m in m1]
        akv = [_nn(k_, v_) for k_, v_ in zip(akk, v)]
        tinv = [eye + l for l in low]
        if n_sq:
            x = [_nn(l, l) for l in low]
        for j in range(n_sq):
            if j < n_sq - 1:
                xt = [_nn(jnp.concatenate([xi, ti], axis=0), xi) for xi, ti in zip(x, tinv)]
                x = [t[:C] for t in xt]
                tinv = [ti + t[C:] for ti, t in zip(tinv, xt)]
            else:
                tinv = [ti + _nn(ti, xi) for ti, xi in zip(tinv, x)]
        ua = [_nn(t, a_) for t, a_ in zip(tinv, at)]
        uv = [_nn(t, k_[:C]) for t, k_ in zip(tinv, akv)]
        oa = [r_ + _nn(rb, u_) for r_, rb, u_ in zip(rt, rab, ua)]
        ov = [k_[C:] + _nn(rb, u_) for k_, rb, u_ in zip(akv, rab, uv)]
        nh = RWKV_HEADS
        return [dict(ua=ua[k * nh:(k + 1) * nh], uv=uv[k * nh:(k + 1) * nh], oa=oa[k * nh:(k + 1) * nh],
                     ov=ov[k * nh:(k + 1) * nh], v=v[k * nh:(k + 1) * nh]) for k in range(len(r0s))]

    def advance(r0, q):
        rw = pl.ds(r0, C)
        bh, kh, pend = bh_s[rw, :], kh_s[rw, :], pend_s[rw, :]
        items = [(n, h) for n in range(nseq) for h in heads]
        rs = [slice(n * sub, (n + 1) * sub) for n, _ in items]
        uo = [_nt(jnp.concatenate([q["ua"][h][s], q["oa"][h][s]], axis=0), s_ref[n, h])
              for (n, h), s in zip(items, rs)]
        u = [m[:sub] + q["uv"][h][s] for m, (n, h), s in zip(uo, items, rs)]
        upd = [_tn(jnp.concatenate([u_, q["v"][h][s]], axis=0),
                   jnp.concatenate([bh[s, hsl[h]], kh[s, hsl[h]]], axis=0)) for u_, (n, h), s in zip(u, items, rs)]
        for up, (n, h), s in zip(upd, items, rs):
            s_ref[n, h] = s_ref[n, h] * pend[s, hsl[h]][0:1] + up
        o = [m[sub:] + q["ov"][h][s] for m, (n, h), s in zip(uo, items, rs)]
        o = jnp.concatenate([jnp.concatenate([o[n * RWKV_HEADS + h] for n in range(nseq)], axis=0) for h in heads],
                            axis=1)
        d = o - seg_sum(o) * (1.0 / RWKV_HD)
        var = seg_sum(d * d) * (1.0 / RWKV_HD)
        o = d * lax.rsqrt(var + RWKV_GN_EPS) * lng_ref[...] + lnb_ref[...]
        o = (o + bonus_s[rw, :] * v_s[rw, :]) * g_s[rw, :]
        if bb == 1:
            o_ref[0, rw, :] = o
        else:
            o_ref[...] = o.reshape(bb, tt, RWKV_W)

    def body(j, carry):
        r0s = [j * group * C + k * C for k in range(group)]
        r0s = [r if isinstance(r, int) else pl.multiple_of(r, C) for r in r0s]
        for r0, q in zip(r0s, triangular(r0s)):
            advance(r0, q)
        return carry

    if nblk == group:
        body(0, 0)
    else:
        lax.fori_loop(0, nblk // group, body, 0)


def _rwkv7(zw, sh0, s0, p, bb, tt, group):
    nb, t, _ = zw.shape
    sub = min(t, WKV_BLOCK)
    step = jnp.arange(WKV_BLOCK)
    tri = ((step[:, None] >= step[None, :]) & (step[:, None] // sub == step[None, :] // sub)).astype(BF16)
    lane_head = jnp.arange(RWKV_W // 2) // RWKV_HD
    seg = (lane_head[:, None] == lane_head[None, :]).astype(BF16)
    rows = bb * tt
    st_spec = pl.BlockSpec((bb, RWKV_HEADS, RWKV_HD, RWKV_HD), lambda b, j: (b, 0, 0, 0))
    sh_spec = pl.BlockSpec((bb, 1, RWKV_IN_W), lambda b, j: (b, 0, 0))
    vec = _full((1, RWKV_W))
    vmem = _vmem(_nbytes((rows, RWKV_IN_W + RWKV_W)) + 2 * _nbytes((bb, RWKV_HEADS, RWKV_HD, RWKV_HD))
                 + 2 * _nbytes((bb, 8, RWKV_IN_W)),
                 9 * _nbytes((8, RWKV_IN_W)) + _nbytes((DECAY_LORA + AAA_LORA + GATE_LORA + WKV_BLOCK, RWKV_W), BF16)
                 + _nbytes(seg.shape, BF16) + WKV_SCRATCH * _nbytes((rows, RWKV_W)),
                 4 * _nbytes((rows, RWKV_IN_W)) + 8 * _nbytes((rows, RWKV_W))
                 + group * 32 * _nbytes((WKV_BLOCK, RWKV_IN_W)))
    return pl.pallas_call(
        functools.partial(_wkv_kernel, bb=bb, tt=tt, sub=sub, group=group),
        grid=(nb // bb, t // tt),
        in_specs=[pl.BlockSpec((bb, tt, RWKV_IN_W), lambda b, j: (b, j, 0)), sh_spec, st_spec,
                  _full((1, RWKV_IN_W)), vec, vec, vec, vec, vec, vec, vec,
                  _full((DECAY_LORA, RWKV_W)), _full((AAA_LORA, RWKV_W)), _full((GATE_LORA, RWKV_W)),
                  _full(tri.shape), _full(seg.shape)],
        out_specs=[pl.BlockSpec((bb, tt, RWKV_W), lambda b, j: (b, j, 0)), st_spec, sh_spec],
        out_shape=[jax.ShapeDtypeStruct((nb, t, RWKV_W), F32), jax.ShapeDtypeStruct(s0.shape, F32),
                   jax.ShapeDtypeStruct(sh0.shape, F32)],
        scratch_shapes=[pltpu.VMEM((rows, RWKV_W), F32)] * WKV_SCRATCH,
        compiler_params=_cparams(("parallel", "arbitrary"), vmem),
        name="rwkv7",
    )(zw, sh0, s0, p["mu"], p["w0"], p["a0"], p["k_k"], p["k_a"], p["r_k"], p["lnx_g"], p["lnx_b"],
      p["w_decay_up"], p["w_a_up"], p["w_g_up"], tri, seg)


def _merge_rows(x, zg, oa, ob, wa_ref, wb_ref, wo_ref):
    ga = jax.nn.sigmoid(zg[:, 0:D_MODEL])
    gb = jax.nn.sigmoid(zg[:, D_MODEL:GATE_W])
    merged = ga * _nn(oa, wa_ref[...]) + gb * _nn(ob, wb_ref[...])
    return x + _nn(merged, wo_ref[...])


def _merge_kernel(x_ref, zg_ref, oa_ref, ob_ref, wa_ref, wb_ref, wo_ref, o_ref):
    o_ref[...] = _merge_rows(x_ref[...], zg_ref[...], oa_ref[...], ob_ref[...], wa_ref, wb_ref, wo_ref)


def _merge(x, zg, o_ret, o_rwkv, wa, wb, wo):
    n = x.shape[0]
    tm = min(ROW_TILE, n)
    row = lambda w: pl.BlockSpec((tm, w), lambda i: (i, 0))
    vmem = _vmem(_nbytes((tm, 2 * D_MODEL + GATE_W + RET_W + RWKV_W)), _nbytes((2 * D_MODEL, D_MODEL), BF16),
                 6 * _nbytes((tm, D_MODEL)))
    return pl.pallas_call(
        _merge_kernel,
        grid=(n // tm,),
        in_specs=[row(D_MODEL), row(GATE_W), row(RET_W), row(RWKV_W),
                  _full((RET_W, D_MODEL)), _full((RWKV_W, D_MODEL)), _full((D_MODEL, D_MODEL))],
        out_specs=row(D_MODEL),
        out_shape=jax.ShapeDtypeStruct((n, D_MODEL), F32),
        compiler_params=_cparams(("parallel",), vmem),
        name="merge",
    )(x, zg, o_ret, o_rwkv, wa, wb, wo)


def _softmax(sc):
    e = jnp.exp(sc - jnp.max(sc, axis=-1, keepdims=True))
    return e / jnp.sum(e, axis=-1, keepdims=True)


def _attend_heads(q_scr, mk_ref, mv_ref, ox_scr, b, r0, tq):
    for h in range(X_HEADS):
        sl = slice(h * X_HD, (h + 1) * X_HD)
        att = _softmax(_nt(q_scr[pl.ds(r0, tq), sl], mk_ref[b, :, sl]) * (X_HD ** -0.5))
        ox_scr[pl.ds(r0, tq), sl] = _nn(att, mv_ref[b, :, sl])


def _attend_tiles(q_scr, mk_ref, mv_ref, ox_scr, b, r0, tq):
    halves = X_HD // LANES
    q = q_scr[pl.ds(r0, tq), :]
    qs = jnp.concatenate([q[:, h * X_HD + c * LANES:h * X_HD + (c + 1) * LANES]
                          for c in range(halves) for h in range(X_HEADS)], axis=0)
    z = _nt(qs, mk_ref[b])
    n = X_HEADS * tq
    slot = lax.broadcasted_iota(jnp.int32, (n, MEM_ROWS), 1) % ROWS_PER_TOKEN
    head = lax.broadcasted_iota(jnp.int32, (n, MEM_ROWS), 0) // tq
    own = slot == head
    part = jnp.where(own, z[:n], 0.0) + jnp.where(slot == head + X_HEADS, z[n:], 0.0)
    sc = part + pltpu.roll(part, MEM_ROWS - X_HEADS, axis=1)
    att = _softmax(jnp.where(own, sc * (X_HD ** -0.5), -jnp.inf))
    att2 = jnp.concatenate([att, pltpu.roll(att, X_HEADS, axis=1)], axis=0)
    o = _nn(att2, mv_ref[b])
    ox_scr[pl.ds(r0, tq), :] = jnp.concatenate(
        [o[(c * X_HEADS + h) * tq:(c * X_HEADS + h + 1) * tq] for h in range(X_HEADS) for c in range(halves)], axis=1)


def _cross_kernel(x_ref, mk_ref, mv_ref, g_ref, wq_ref, wo_ref, o_ref, q_scr, ox_scr, *, bb, tq, attend):
    rows = bb * tq
    x = x_ref[...].reshape(rows, D_MODEL)
    q_scr[...] = _nn(_rms(x, g_ref[...]), wq_ref[...])

    def body(b, _):
        r0 = b * tq if isinstance(b, int) else pl.multiple_of(b * tq, tq)
        attend(q_scr, mk_ref, mv_ref, ox_scr, b, r0, tq)

    _loop_bc(bb, 1, body)
    o_ref[...] = (x + _nn(ox_scr[...], wo_ref[...])).reshape(bb, tq, D_MODEL)


def _cross(x, mem_k, mem_v, g, wq, wo, bb, tq):
    nb, t, _ = x.shape
    tiled = mem_k.shape[1:] == (MEM_ROWS, LANES)
    x_spec = pl.BlockSpec((bb, tq, D_MODEL), lambda b, j: (b, j, 0))
    m_spec = pl.BlockSpec((bb,) + mem_k.shape[1:], lambda b, j: (b, 0, 0))
    vmem = _vmem(2 * _nbytes((bb, tq, D_MODEL)) + 2 * _nbytes((bb,) + mem_k.shape[1:], mem_k.dtype),
                 2 * _nbytes((D_MODEL, D_MODEL), BF16) + 2 * _nbytes((bb * tq, D_MODEL)),
                 6 * _nbytes((bb * tq, D_MODEL)) + 2 * _nbytes((MEM_LEN, D_MODEL))
                 + (8 * _nbytes((2 * X_HEADS * tq, MEM_ROWS)) if tiled else 0))
    return pl.pallas_call(
        functools.partial(_cross_kernel, bb=bb, tq=tq, attend=_attend_tiles if tiled else _attend_heads),
        grid=(nb // bb, t // tq),
        in_specs=[x_spec, m_spec, m_spec, _full((1, D_MODEL)), _full((D_MODEL, D_MODEL)), _full((D_MODEL, D_MODEL))],
        out_specs=x_spec,
        out_shape=jax.ShapeDtypeStruct(x.shape, F32),
        scratch_shapes=[pltpu.VMEM((bb * tq, D_MODEL), F32), pltpu.VMEM((bb * tq, D_MODEL), F32)],
        compiler_params=_cparams(("parallel", "arbitrary"), vmem),
        name="cross_attn",
    )(x, mem_k, mem_v, g, wq, wo)


MLP_FF_CHUNK = 1024


def _mlp_rows(x, g_ref, wu_ref, wd_ref, gf_ref, final_norm):
    h = _rms(x, g_ref[...]).astype(BF16)
    acc = x
    for c in range(0, D_FF, MLP_FF_CHUNK):
        u = jnp.maximum(jnp.dot(h, wu_ref[:, c:c + MLP_FF_CHUNK], preferred_element_type=F32), 0.0)
        acc = acc + _nn(u * u, wd_ref[c:c + MLP_FF_CHUNK, :])
    return _rms(acc, gf_ref[...]) if final_norm else acc


def _mlp_kernel(x_ref, g_ref, wu_ref, wd_ref, gf_ref, o_ref, *, final_norm):
    o_ref[...] = _mlp_rows(x_ref[...], g_ref, wu_ref, wd_ref, gf_ref, final_norm)


def _mlp(x, g, w_up, w_down, g_final, final_norm):
    n = x.shape[0]
    tm = min(ROW_TILE, n)
    row = pl.BlockSpec((tm, D_MODEL), lambda i: (i, 0))
    vmem = _vmem(2 * _nbytes((tm, D_MODEL)), 2 * _nbytes((D_MODEL, D_FF), BF16),
                 4 * _nbytes((tm, MLP_FF_CHUNK)) + 4 * _nbytes((tm, D_MODEL)))
    return pl.pallas_call(
        functools.partial(_mlp_kernel, final_norm=final_norm),
        grid=(n // tm,),
        in_specs=[row, _full((1, D_MODEL)), _full((D_MODEL, D_FF)), _full((D_FF, D_MODEL)), _full((1, D_MODEL))],
        out_specs=row,
        out_shape=jax.ShapeDtypeStruct((n, D_MODEL), F32),
        compiler_params=_cparams(("parallel",), vmem),
        name="mlp",
    )(x, g, w_up, w_down, g_final)


POST_TILE = 512


def _post_kernel(x_ref, zg_ref, oa_ref, ob_ref, mk_ref, mv_ref, wa_ref, wb_ref, wo_ref, gc_ref, wq_ref, wco_ref,
                 gm_ref, wu_ref, wd_ref, gf_ref, o_ref, q_scr, ox_scr, *, tq, final_norm):
    x1 = _merge_rows(x_ref[0], zg_ref[0], oa_ref[0], ob_ref[0], wa_ref, wb_ref, wo_ref)
    q_scr[...] = _nn(_rms(x1, gc_ref[...]), wq_ref[...])
    _attend_heads(q_scr, mk_ref, mv_ref, ox_scr, 0, 0, tq)
    x2 = x1 + _nn(ox_scr[...], wco_ref[...])
    o_ref[0] = _mlp_rows(x2, gm_ref, wu_ref, wd_ref, gf_ref, final_norm)


def _post(x, zg, o_ret, o_rwkv, mem_k, mem_v, w, g_final, final_norm):
    nb, t, _ = x.shape
    tq = min(POST_TILE, t)
    row = lambda width: pl.BlockSpec((1, tq, width), lambda b, j: (b, j, 0))
    mem = pl.BlockSpec((1, MEM_LEN, D_MODEL), lambda b, j: (b, 0, 0))
    sq = _full((D_MODEL, D_MODEL))
    vec = _full((1, D_MODEL))
    vmem = _vmem(_nbytes((tq, 2 * D_MODEL + GATE_W + RET_W + RWKV_W)) + 2 * _nbytes((MEM_LEN, D_MODEL), BF16),
                 _nbytes((4 * D_MODEL + 2 * D_FF, D_MODEL), BF16) + 2 * _nbytes((tq, D_MODEL)),
                 8 * _nbytes((tq, D_MODEL)) + 4 * _nbytes((tq, MLP_FF_CHUNK)))
    return pl.pallas_call(
        functools.partial(_post_kernel, tq=tq, final_norm=final_norm),
        grid=(nb, t // tq),
        in_specs=[row(D_MODEL), row(GATE_W), row(RET_W), row(RWKV_W), mem, mem,
                  _full((RET_W, D_MODEL)), _full((RWKV_W, D_MODEL)), sq, vec, sq, sq,
                  vec, _full((D_MODEL, D_FF)), _full((D_FF, D_MODEL)), vec],
        out_specs=row(D_MODEL),
        out_shape=jax.ShapeDtypeStruct(x.shape, F32),
        scratch_shapes=[pltpu.VMEM((tq, D_MODEL), F32), pltpu.VMEM((tq, D_MODEL), F32)],
        compiler_params=_cparams(("parallel", "arbitrary"), vmem),
        name="post",
    )(x, zg, o_ret, o_rwkv, mem_k, mem_v, w["w_branch_a"], w["w_branch_b"], w["w_out"], w["g_cross"], w["w_cq"],
      w["w_co"], w["g_mlp"], w["w_up"], w["w_down"], g_final)


class _Tiling(NamedTuple):
    seq_block: int
    time_block: int
    cross_block: int
    ret_group: int
    wkv_group: int


def _tiling(nb, t):
    if t > RET_CHUNK:
        tt = min(ROW_TILE, t)
        return _Tiling(seq_block=1, time_block=tt, cross_block=1,
                       ret_group=math.gcd(4, tt // RET_CHUNK), wkv_group=math.gcd(4, tt // WKV_BLOCK))
    seqs = WKV_BLOCK // t
    return _Tiling(seq_block=seqs, time_block=t, cross_block=math.gcd(4, nb), ret_group=seqs, wkv_group=1)


def _layer(x, pos, s_ret, s_wkv, s_shift, mem_k, mem_v, w, g_final, final_norm):
    nb, t, _ = x.shape
    n = nb * t
    tl = _tiling(nb, t)
    xf = x.reshape(n, D_MODEL)
    zg, zr, zw = _in_proj(xf, w["g_mix"], w["w_in"])
    o_ret, s_ret_new = _retention(zr.reshape(nb, t, 4 * RET_W), s_ret, pos, tl.seq_block, tl.time_block, tl.ret_group)
    o_wkv, s_wkv_new, shift_new = _rwkv7(zw.reshape(nb, t, RWKV_IN_W), s_shift.reshape(nb, 1, RWKV_IN_W), s_wkv,
                                         w, tl.seq_block, tl.time_block, tl.wkv_group)
    if mem_k.shape[1:] == (MEM_LEN, D_MODEL):
        y = _post(x, zg.reshape(nb, t, GATE_W), o_ret, o_wkv, mem_k, mem_v, w, g_final, final_norm)
    else:
        x1 = _merge(xf, zg, o_ret.reshape(n, RET_W), o_wkv.reshape(n, RWKV_W),
                    w["w_branch_a"], w["w_branch_b"], w["w_out"])
        x2 = _cross(x1.reshape(nb, t, D_MODEL), mem_k, mem_v, w["g_cross"], w["w_cq"], w["w_co"],
                    tl.cross_block, tl.time_block)
        y = _mlp(x2.reshape(n, D_MODEL), w["g_mlp"], w["w_up"], w["w_down"], g_final, final_norm).reshape(x.shape)
    return y, s_ret_new, s_wkv_new, shift_new.reshape(nb, RWKV_IN_W)


def _layer_weights(l, g_mix, w_in, w_branch_a, w_branch_b, w_out, mu_shift, w0, w_decay_up, a0, w_a_up, w_g_up,
                   k_k, k_a, r_k, lnx_g, lnx_b, g_cross, w_cq, w_co, g_mlp, w_up, w_down):
    vec = lambda v: v[l].reshape(1, -1).astype(F32)
    mat = lambda m: m[l].astype(BF16)
    return dict(g_mix=vec(g_mix), w_in=mat(w_in), w_branch_a=mat(w_branch_a), w_branch_b=mat(w_branch_b),
                w_out=mat(w_out), mu=vec(mu_shift), w0=vec(w0), w_decay_up=mat(w_decay_up), a0=vec(a0),
                w_a_up=mat(w_a_up), w_g_up=mat(w_g_up), k_k=vec(k_k), k_a=vec(k_a), r_k=vec(r_k), lnx_g=vec(lnx_g),
                lnx_b=vec(lnx_b), g_cross=vec(g_cross), w_cq=mat(w_cq), w_co=mat(w_co), g_mlp=vec(g_mlp),
                w_up=mat(w_up), w_down=mat(w_down))


def kernel(x_prompt, x_sample, mem_prompt, state_ret, state_wkv, state_shift, cache_mem_k, cache_mem_v, g_mix, w_in, w_branch_a, w_branch_b, w_out, mu_shift, w0, w_decay_up, a0, w_a_up, w_g_up, k_k, k_a, r_k, lnx_g, lnx_b, g_cross, g_mem, w_cq, w_ck, w_cv, w_co, g_mlp, w_up, w_down, g_final):
    depth = w_in.shape[0]
    bp, tp, _ = x_prompt.shape
    bs, ts, _ = x_sample.shape
    pos_p = jnp.arange(tp, dtype=jnp.int32)
    pos_s = PAST_LEN + jnp.arange(ts, dtype=jnp.int32)
    gf = g_final.reshape(1, D_MODEL)

    xp, xs = x_prompt, x_sample
    outs_p, outs_s = [], []
    for l in range(depth):
        w = _layer_weights(l, g_mix, w_in, w_branch_a, w_branch_b, w_out, mu_shift, w0, w_decay_up, a0, w_a_up,
                           w_g_up, k_k, k_a, r_k, lnx_g, lnx_b, g_cross, w_cq, w_co, g_mlp, w_up, w_down)
        last = l == depth - 1
        k_tiles, v_tiles, k_att, v_att = _mem_kv(mem_prompt.reshape(bp * MEM_LEN, D_MODEL),
                                                 g_mem[l].reshape(1, D_MODEL), w_ck[l].astype(BF16), w_cv[l].astype(BF16))
        xp, sr, sw, ss = _layer(
            xp, pos_p,
            jnp.zeros((bp, RET_HEADS, RET_HD, RET_HD), F32), jnp.zeros((bp, RWKV_HEADS, RWKV_HD, RWKV_HD), F32),
            jnp.zeros((bp, RWKV_IN_W), F32), k_att.reshape(bp, MEM_LEN, D_MODEL), v_att.reshape(bp, MEM_LEN, D_MODEL),
            w, gf, last)
        outs_p.append((sr, sw, ss, _from_tile_order(k_tiles.reshape(bp, MEM_ROWS, LANES)),
                       _from_tile_order(v_tiles.reshape(bp, MEM_ROWS, LANES))))
        xs, sr2, sw2, ss2 = _layer(
            xs, pos_s, state_ret[l], state_wkv[l], state_shift[l],
            _tile_order(cache_mem_k[l]), _tile_order(cache_mem_v[l]), w, gf, last)
        outs_s.append((sr2, sw2, ss2))

    stack = lambda items, i: jnp.stack([it[i] for it in items])
    return (xp, xs, stack(outs_p, 0), stack(outs_p, 1), stack(outs_p, 2), stack(outs_p, 3), stack(outs_p, 4),
            stack(outs_s, 0), stack(outs_s, 1), stack(outs_s, 2))
```

```python
import functools
import math
from typing import NamedTuple

import jax
import jax.numpy as jnp
from jax import lax
from jax.experimental import pallas as pl
from jax.experimental.pallas import tpu as pltpu

F32 = jnp.float32
BF16 = jnp.bfloat16

D_MODEL = 1024
PAST_LEN = 16384
RET_HEADS = 4
RET_HD = 128
RET_W = RET_HEADS * RET_HD
RET_CHUNK = 128
RET_GN_EPS = 1e-5
ROPE_BASE = 10000.0
RWKV_HEADS = 8
RWKV_HD = 64
RWKV_W = RWKV_HEADS * RWKV_HD
DECAY_LORA = 64
AAA_LORA = 64
GATE_LORA = 128
RWKV_GN_EPS = 64e-5
RWKV_IN_W = 3 * RWKV_W + DECAY_LORA + AAA_LORA + GATE_LORA
GATE_W = 2 * D_MODEL
O_RET = 2 * D_MODEL
O_RWKV = O_RET + 4 * RET_W
IN_W = O_RWKV + RWKV_IN_W
MEM_LEN = 256
X_HEADS = 4
X_HD = D_MODEL // X_HEADS
D_FF = 4 * D_MODEL
RMS_EPS = 1e-6

V7X_VMEM_BYTES = 64 * 1024 * 1024
VMEM_CAP_BYTES = V7X_VMEM_BYTES - 8 * 1024 * 1024
ROW_TILE = 512


def _cparams(sem, vmem_bytes):
    return pltpu.CompilerParams(dimension_semantics=sem, vmem_limit_bytes=int(min(vmem_bytes, VMEM_CAP_BYTES)))


def _nbytes(shape, dtype=F32):
    return math.prod(shape) * jnp.dtype(dtype).itemsize


def _nn(a, b):
    return jnp.dot(a.astype(BF16), b.astype(BF16), preferred_element_type=F32)


def _nt(a, b):
    return lax.dot_general(a.astype(BF16), b.astype(BF16), (((1,), (1,)), ((), ())), preferred_element_type=F32)


def _tn(a, b):
    return lax.dot_general(a.astype(BF16), b.astype(BF16), (((0,), (0,)), ((), ())), preferred_element_type=F32)


def _rms(x, g):
    return x * lax.rsqrt(jnp.mean(x * x, axis=-1, keepdims=True) + RMS_EPS) * g


def _head_norm(o, eps):
    mu = jnp.mean(o, axis=-1, keepdims=True)
    d = o - mu
    var = jnp.mean(d * d, axis=-1, keepdims=True)
    return d * lax.rsqrt(var + eps)


def _full(shape):
    zeros = (0,) * len(shape)
    return pl.BlockSpec(shape, lambda *_: zeros, pipeline_mode=pl.Buffered(1))


def _vmem(pipelined, resident, temps):
    return 2 * pipelined + resident + temps


def _in_proj_kernel(x_ref, g_ref, w_ref, zg_ref, zr_ref, zw_ref):
    h = _rms(x_ref[...], g_ref[...]).astype(BF16)
    zg_ref[...] = jnp.dot(h, w_ref[:, 0:O_RET], preferred_element_type=F32)
    zr_ref[...] = jnp.dot(h, w_ref[:, O_RET:O_RWKV], preferred_element_type=F32)
    zw_ref[...] = jnp.dot(h, w_ref[:, O_RWKV:IN_W], preferred_element_type=F32)


def _in_proj(x, g, w_in):
    n = x.shape[0]
    tm = min(ROW_TILE, n)
    vmem = _vmem(_nbytes((tm, D_MODEL + IN_W)), _nbytes((D_MODEL, IN_W), BF16), _nbytes((tm, 4 * RET_W + D_MODEL)))
    return pl.pallas_call(
        _in_proj_kernel,
        grid=(n // tm,),
        in_specs=[pl.BlockSpec((tm, D_MODEL), lambda i: (i, 0)), _full((1, D_MODEL)), _full((D_MODEL, IN_W))],
        out_specs=[pl.BlockSpec((tm, GATE_W), lambda i: (i, 0)),
                   pl.BlockSpec((tm, 4 * RET_W), lambda i: (i, 0)),
                   pl.BlockSpec((tm, RWKV_IN_W), lambda i: (i, 0))],
        out_shape=[jax.ShapeDtypeStruct((n, GATE_W), F32), jax.ShapeDtypeStruct((n, 4 * RET_W), F32),
                   jax.ShapeDtypeStruct((n, RWKV_IN_W), F32)],
        compiler_params=_cparams(("parallel",), vmem),
        name="in_proj",
    )(x, g, w_in)


LANES = 128
MEM_ROWS = MEM_LEN * X_HEADS * (X_HD // LANES)
ROWS_PER_TOKEN = MEM_ROWS // MEM_LEN


def _tile_order(mem):
    nb = mem.shape[0]
    return (mem.reshape(nb, MEM_LEN, X_HEADS, X_HD // LANES, LANES).transpose(0, 1, 3, 2, 4)
            .reshape(nb, MEM_ROWS, LANES))


def _from_tile_order(raw):
    nb = raw.shape[0]
    return (raw.reshape(nb, MEM_LEN, X_HD // LANES, X_HEADS, LANES).transpose(0, 1, 3, 2, 4)
            .reshape(nb, MEM_LEN, X_HEADS, X_HD))


def _mem_kv_kernel(x_ref, g_ref, wk_ref, wv_ref, kt_ref, vt_ref, kb_ref, vb_ref, *, tm):
    h = _rms(x_ref[...], g_ref[...]).astype(BF16)
    for w_ref, t_ref, b_ref in ((wk_ref, kt_ref, kb_ref), (wv_ref, vt_ref, vb_ref)):
        y = jnp.dot(h, w_ref[...], preferred_element_type=F32)
        b_ref[...] = y.astype(BF16)
        for hd in range(X_HEADS):
            for c in range(X_HD // LANES):
                col = hd * X_HD + c * LANES
                t_ref[pl.ds(c * X_HEADS + hd, tm, stride=ROWS_PER_TOKEN), :] = y[:, col:col + LANES]


def _mem_kv(mem, g, wk, wv):
    n = mem.shape[0]
    tm = min(ROW_TILE, n)
    vmem = _vmem(4 * _nbytes((tm, D_MODEL)), 2 * _nbytes((D_MODEL, D_MODEL), BF16), 4 * _nbytes((tm, D_MODEL)))
    row = pl.BlockSpec((tm, D_MODEL), lambda i: (i, 0))
    tile = pl.BlockSpec((tm * ROWS_PER_TOKEN, LANES), lambda i: (i, 0))
    return pl.pallas_call(
        functools.partial(_mem_kv_kernel, tm=tm),
        grid=(n // tm,),
        in_specs=[row, _full((1, D_MODEL)), _full((D_MODEL, D_MODEL)), _full((D_MODEL, D_MODEL))],
        out_specs=[tile, tile, row, row],
        out_shape=[jax.ShapeDtypeStruct((n * ROWS_PER_TOKEN, LANES), F32)] * 2
        + [jax.ShapeDtypeStruct((n, D_MODEL), BF16)] * 2,
        compiler_params=_cparams(("parallel",), vmem),
        name="mem_kv",
    )(mem, g, wk, wv)


def _loop_bc(bb, nch, body):
    if bb == 1 and nch == 1:
        body(0, 0)
    elif nch == 1:
        lax.fori_loop(0, bb, lambda b, c: (body(b, 0), c)[1], 0)
    elif bb == 1:
        lax.fori_loop(0, nch, lambda i, c: (body(0, i), c)[1], 0)
    else:
        lax.fori_loop(0, bb * nch, lambda j, c: (body(j // nch, j % nch), c)[1], 0)


def _ret_kernel(zr_ref, s0_ref, cos_ref, sin_ref, dm_ref, qd_ref, kd_ref, cd_ref, o_ref, s_ref, *,
                bb, tt, chunk, group):
    @pl.when(pl.program_id(1) == 0)
    def _():
        s_ref[...] = s0_ref[...]

    nch = tt // chunk
    assert bb == 1 or nch == 1, "a block holds either one sequence or one chunk per sequence"
    assert (bb * nch) % group == 0
    heads = range(RET_HEADS)

    def body(j, carry):
        if nch == 1:
            items = [(j * group + n, 0) for n in range(group)]
        else:
            items = [(0, pl.multiple_of((j * group + n) * chunk, chunk)) for n in range(group)]
        q, k, v = [], [], []
        for b, r0 in items:
            rows = pl.ds(r0, chunk)
            cos = cos_ref[rows, :]
            sin = sin_ref[rows, :]
            for h in heads:
                qh = zr_ref[b, rows, h * RET_HD:(h + 1) * RET_HD]
                kh = zr_ref[b, rows, RET_W + h * RET_HD:RET_W + (h + 1) * RET_HD]
                q.append(qh * cos + pltpu.roll(qh, RET_HD // 2, axis=1) * sin)
                k.append((kh * cos + pltpu.roll(kh, RET_HD // 2, axis=1) * sin) * (RET_HD ** -0.5))
                v.append(zr_ref[b, rows, 2 * RET_W + h * RET_HD:2 * RET_W + (h + 1) * RET_HD])
        hd = [h for _ in items for h in heads]
        sc = [_nt(qi, ki) * dm_ref[h] for qi, ki, h in zip(q, k, hd)]
        kv = [_tn(ki * kd_ref[h], vi) for ki, vi, h in zip(k, v, hd)]
        inner = [_nn(si, vi) for si, vi in zip(sc, v)]
        states = []
        for n, (b, _) in enumerate(items):
            for h in heads:
                s = s_ref[b, h] if (nch == 1 or n == 0) else states[-RET_HEADS] * cd_ref[h] + kv[(n - 1) * RET_HEADS + h]
                states.append(s)
        last = len(items) - 1
        for n, (b, _) in enumerate(items):
            if nch == 1 or n == last:
                for h in heads:
                    i = n * RET_HEADS + h
                    s_ref[b, h] = states[i] * cd_ref[h] + kv[i]
        cross = [_nn(qi, si) * qd_ref[h] for qi, si, h in zip(q, states, hd)]
        for n, (b, r0) in enumerate(items):
            rows = pl.ds(r0, chunk)
            for h in heads:
                i = n * RET_HEADS + h
                g = zr_ref[b, rows, 3 * RET_W + h * RET_HD:3 * RET_W + (h + 1) * RET_HD]
                o_ref[b, rows, h * RET_HD:(h + 1) * RET_HD] = (_head_norm(inner[i] + cross[i], RET_GN_EPS)
                                                              * (g * jax.nn.sigmoid(g)))
        return carry

    lax.fori_loop(0, bb * nch // group, body, 0)


def _ret_consts(chunk):
    lg = jnp.log1p(-jnp.exp2(-5.0 - jnp.arange(RET_HEADS, dtype=F32)))
    idx = jnp.arange(chunk, dtype=F32)
    diff = idx[:, None] - idx[None, :]
    dmask = jnp.where(diff[None] >= 0, jnp.exp(jnp.maximum(diff, 0.0)[None] * lg[:, None, None]), 0.0)
    q_dec = jnp.exp((idx + 1.0)[None, :] * lg[:, None])
    k_dec = jnp.exp((chunk - 1.0 - idx)[None, :] * lg[:, None])
    c_dec = jnp.exp(chunk * lg)
    bc = lambda t: jnp.broadcast_to(t[:, :, None], (RET_HEADS, t.shape[1], RET_HD))
    return dmask, bc(q_dec), bc(k_dec), bc(c_dec[:, None])


def _rope_tables(pos):
    half = RET_HD // 2
    inv = 1.0 / (ROPE_BASE ** (jnp.arange(half, dtype=F32) / half))
    ang = pos.astype(F32)[:, None] * inv[None, :]
    cos, sin = jnp.cos(ang), jnp.sin(ang)
    return jnp.concatenate([cos, cos], axis=1), jnp.concatenate([-sin, sin], axis=1)


def _retention(zr, s0, pos, bb, tt, group):
    nb, t, _ = zr.shape
    chunk = math.gcd(t, RET_CHUNK)
    cos, sin = _rope_tables(pos)
    dmask, q_dec, k_dec, c_dec = _ret_consts(chunk)
    st_spec = pl.BlockSpec((bb, RET_HEADS, RET_HD, RET_HD), lambda b, j: (b, 0, 0, 0))
    tab_spec = pl.BlockSpec((tt, RET_HD), lambda b, j: (j, 0))
    vmem = _vmem(_nbytes((bb, tt, 5 * RET_W)) + 2 * _nbytes((bb, RET_HEADS, RET_HD, RET_HD)) + 2 * _nbytes((tt, RET_HD)),
                 _nbytes(dmask.shape) + 3 * _nbytes(q_dec.shape), group * 8 * _nbytes((chunk, 4 * RET_W)))
    return pl.pallas_call(
        functools.partial(_ret_kernel, bb=bb, tt=tt, chunk=chunk, group=group),
        grid=(nb // bb, t // tt),
        in_specs=[pl.BlockSpec((bb, tt, 4 * RET_W), lambda b, j: (b, j, 0)), st_spec, tab_spec, tab_spec,
                  _full(dmask.shape), _full(q_dec.shape), _full(k_dec.shape), _full(c_dec.shape)],
        out_specs=[pl.BlockSpec((bb, tt, RET_W), lambda b, j: (b, j, 0)), st_spec],
        out_shape=[jax.ShapeDtypeStruct((nb, t, RET_W), F32), jax.ShapeDtypeStruct(s0.shape, F32)],
        compiler_params=_cparams(("parallel", "arbitrary"), vmem),
        name="retention",
    )(zr, s0, cos, sin, dmask, q_dec, k_dec, c_dec)


WKV_BLOCK = 64
WKV_SCRATCH = 10


def _pow2(n):
    return n & (n - 1) == 0


def _imod(x, n):
    return jnp.bitwise_and(x, n - 1) if _pow2(n) else lax.rem(x, n)


def _idiv(x, n):
    return jnp.right_shift(x, n.bit_length() - 1) if _pow2(n) else lax.div(x, n)


def _wkv_kernel(zw_ref, sh0_ref, s0_ref, mu_ref, w0_ref, a0_ref, kk_ref, ka_ref, rk_ref, lng_ref, lnb_ref,
                wd_ref, wa_ref, wg_ref, tri_ref, seg_ref, o_ref, s_ref, sh_ref,
                at_s, rt_s, bt_s, kt_s, bh_s, kh_s, v_s, g_s, bonus_s, pend_s, *, bb, tt, sub, group):
    @pl.when(pl.program_id(1) == 0)
    def _():
        s_ref[...] = s0_ref[...]
        sh_ref[...] = sh0_ref[...]

    C = WKV_BLOCK
    R = bb * tt
    nblk = R // C
    nseq = C // sub
    assert (bb == 1 and sub == C) or (nblk == 1 and sub == tt), "one sequence per tile, or whole sequences in one block"
    assert nblk % group == 0
    n_sq = max(int(math.log2(sub)) - 1, 0)
    heads = range(RWKV_HEADS)
    hsl = [slice(h * RWKV_HD, (h + 1) * RWKV_HD) for h in heads]
    half = RWKV_W // 2

    def seg_sum(t):
        seg = seg_ref[...]
        return jnp.concatenate([_nn(t[:, :half], seg), _nn(t[:, half:], seg)], axis=1)

    pw = zw_ref[...].reshape(R, RWKV_IN_W)
    row = lax.broadcasted_iota(jnp.int32, (R, RWKV_IN_W), 0)
    rolled = pltpu.roll(pw, 1, axis=0)
    if bb == 1:
        prev = jnp.where(row == 0, sh_ref[0], rolled)
        sh_ref[0] = pw[R - 1:R, :]
    else:
        carry = jnp.broadcast_to(sh_ref[...], (bb, tt, RWKV_IN_W)).reshape(R, RWKV_IN_W)
        prev = jnp.where(_imod(row, tt) == 0, carry, rolled)
        sh_ref[...] = pw.reshape(bb, tt, RWKV_IN_W)[:, tt - 1:tt, :]
    xm = pw + mu_ref[...] * (prev - pw)
    r = xm[:, 0:RWKV_W]
    kb = xm[:, RWKV_W:2 * RWKV_W]
    o1 = 3 * RWKV_W
    wl = xm[:, o1:o1 + DECAY_LORA]
    al = xm[:, o1 + DECAY_LORA:o1 + DECAY_LORA + AAA_LORA]
    gl = xm[:, o1 + DECAY_LORA + AAA_LORA:RWKV_IN_W]
    logw = -math.exp(-0.5) * jax.nn.sigmoid(w0_ref[...] + _nn(jnp.tanh(wl), wd_ref[...]))
    a = jax.nn.sigmoid(a0_ref[...] + _nn(al, wa_ref[...]))
    l1 = logw.astype(BF16)
    rem = logw - l1.astype(F32)
    l2 = rem.astype(BF16)
    l3 = (rem - l2.astype(F32)).astype(BF16)
    tri = tri_ref[...]
    cum = jnp.concatenate(
        [sum(jnp.dot(tri, part[k * C:(k + 1) * C], preferred_element_type=F32) for part in (l1, l2, l3))
         for k in range(nblk)], axis=0)
    cum_end = jnp.broadcast_to(cum.reshape(R // sub, sub, RWKV_W)[:, sub - 1:sub, :],
                               (R // sub, sub, RWKV_W)).reshape(R, RWKV_W)
    kk = kb * kk_ref[...]
    kk = kk * jnp.minimum(lax.rsqrt(seg_sum(kk * kk)), 1e12)
    km = kb * (1.0 + (a - 1.0) * ka_ref[...])
    bv = kk * a
    p_end = jnp.exp(cum_end)
    e_neg = jnp.exp(-cum)
    e_end = p_end * e_neg
    at_s[...] = -kk * jnp.exp(cum - logw)
    rt_s[...] = r * jnp.exp(cum)
    bt_s[...] = bv * e_neg
    kt_s[...] = km * e_neg
    bh_s[...] = bv * e_end
    kh_s[...] = km * e_end
    v_s[...] = xm[:, 2 * RWKV_W:3 * RWKV_W]
    g_s[...] = _nn(jax.nn.sigmoid(gl), wg_ref[...])
    bonus_s[...] = seg_sum(r * km * rk_ref[...])
    pend_s[...] = p_end

    row_id = lax.broadcasted_iota(jnp.int32, (C, C), 0)
    col_id = lax.broadcasted_iota(jnp.int32, (C, C), 1)
    strict = row_id > col_id
    incl = row_id >= col_id
    if nseq > 1:
        same = _idiv(row_id, sub) == _idiv(col_id, sub)
        strict = jnp.logical_and(strict, same)
        incl = jnp.logical_and(incl, same)
    eye = jnp.where(row_id == col_id, 1.0, 0.0).astype(F32)

    def triangular(r0s):
        rows = [pl.ds(r0, C) for r0 in r0s for _ in heads]
        sl = [s for _ in r0s for s in hsl]
        at = [at_s[rw, :][:, s] for rw, s in zip(rows, sl)]
        rt = [rt_s[rw, :][:, s] for rw, s in zip(rows, sl)]
        v = [v_s[rw, :][:, s] for rw, s in zip(rows, sl)]
        m1 = [_nt(jnp.concatenate([a_, r_], axis=0), jnp.concatenate([bt_s[rw, :][:, s], kt_s[rw, :][:, s]], axis=0))
              for a_, r_, rw, s in zip(at, rt, rows, sl)]
        low = [jnp.where(strict, m[:C, :C], 0.0) for m in m1]
        rab = [jnp.where(incl, m[C:, :C], 0.0) for m in m1]
        akk = [jnp.concatenate([jnp.where(strict, m[:C, C:], 0.0), jnp.where(incl, m[C:, C:], 0.0)], axis=0)
               for m in m1]
        akv = [_nn(k_, v_) for k_, v_ in zip(akk, v)]
        tinv = [eye + l for l in low]
        if n_sq:
            x = [_nn(l, l) for l in low]
        for j in range(n_sq):
            if j < n_sq - 1:
                xt = [_nn(jnp.concatenate([xi, ti], axis=0), xi) for xi, ti in zip(x, tinv)]
                x = [t[:C] for t in xt]
                tinv = [ti + t[C:] for ti, t in zip(tinv, xt)]
            else:
                tinv = [ti + _nn(ti, xi) for ti, xi in zip(tinv, x)]
        ua = [_nn(t, a_) for t, a_ in zip(tinv, at)]
        uv = [_nn(t, k_[:C]) for t, k_ in zip(tinv, akv)]
        oa = [r_ + _nn(rb, u_) for r_, rb, u_ in zip(rt, rab, ua)]
        ov = [k_[C:] + _nn(rb, u_) for k_, rb, u_ in zip(akv, rab, uv)]
        nh = RWKV_HEADS
        return [dict(ua=ua[k * nh:(k + 1) * nh], uv=uv[k * nh:(k + 1) * nh], oa=oa[k * nh:(k + 1) * nh],
                     ov=ov[k * nh:(k + 1) * nh], v=v[k * nh:(k + 1) * nh]) for k in range(len(r0s))]

    def advance(r0, q):
        rw = pl.ds(r0, C)
        bh, kh, pend = bh_s[rw, :], kh_s[rw, :], pend_s[rw, :]
        items = [(n, h) for n in range(nseq) for h in heads]
        rs = [slice(n * sub, (n + 1) * sub) for n, _ in items]
        uo = [_nt(jnp.concatenate([q["ua"][h][s], q["oa"][h][s]], axis=0), s_ref[n, h])
              for (n, h), s in zip(items, rs)]
        u = [m[:sub] + q["uv"][h][s] for m, (n, h), s in zip(uo, items, rs)]
        upd = [_tn(jnp.concatenate([u_, q["v"][h][s]], axis=0),
                   jnp.concatenate([bh[s, hsl[h]], kh[s, hsl[h]]], axis=0)) for u_, (n, h), s in zip(u, items, rs)]
        for up, (n, h), s in zip(upd, items, rs):
            s_ref[n, h] = s_ref[n, h] * pend[s, hsl[h]][0:1] + up
        o = [m[sub:] + q["ov"][h][s] for m, (n, h), s in zip(uo, items, rs)]
        o = jnp.concatenate([jnp.concatenate([o[n * RWKV_HEADS + h] for n in range(nseq)], axis=0) for h in heads],
                            axis=1)
        d = o - seg_sum(o) * (1.0 / RWKV_HD)
        var = seg_sum(d * d) * (1.0 / RWKV_HD)
        o = d * lax.rsqrt(var + RWKV_GN_EPS) * lng_ref[...] + lnb_ref[...]
        o = (o + bonus_s[rw, :] * v_s[rw, :]) * g_s[rw, :]
        if bb == 1:
            o_ref[0, rw, :] = o
        else:
            o_ref[...] = o.reshape(bb, tt, RWKV_W)

    def body(j, carry):
        r0s = [j * group * C + k * C for k in range(group)]
        r0s = [r if isinstance(r, int) else pl.multiple_of(r, C) for r in r0s]
        for r0, q in zip(r0s, triangular(r0s)):
            advance(r0, q)
        return carry

    if nblk == group:
        body(0, 0)
    else:
        lax.fori_loop(0, nblk // group, body, 0)


def _rwkv7(zw, sh0, s0, p, bb, tt, group):
    nb, t, _ = zw.shape
    sub = min(t, WKV_BLOCK)
    step = jnp.arange(WKV_BLOCK)
    tri = ((step[:, None] >= step[None, :]) & (step[:, None] // sub == step[None, :] // sub)).astype(BF16)
    lane_head = jnp.arange(RWKV_W // 2) // RWKV_HD
    seg = (lane_head[:, None] == lane_head[None, :]).astype(BF16)
    rows = bb * tt
    st_spec = pl.BlockSpec((bb, RWKV_HEADS, RWKV_HD, RWKV_HD), lambda b, j: (b, 0, 0, 0))
    sh_spec = pl.BlockSpec((bb, 1, RWKV_IN_W), lambda b, j: (b, 0, 0))
    vec = _full((1, RWKV_W))
    vmem = _vmem(_nbytes((rows, RWKV_IN_W + RWKV_W)) + 2 * _nbytes((bb, RWKV_HEADS, RWKV_HD, RWKV_HD))
                 + 2 * _nbytes((bb, 8, RWKV_IN_W)),
                 9 * _nbytes((8, RWKV_IN_W)) + _nbytes((DECAY_LORA + AAA_LORA + GATE_LORA + WKV_BLOCK, RWKV_W), BF16)
                 + _nbytes(seg.shape, BF16) + WKV_SCRATCH * _nbytes((rows, RWKV_W)),
                 4 * _nbytes((rows, RWKV_IN_W)) + 8 * _nbytes((rows, RWKV_W))
                 + group * 32 * _nbytes((WKV_BLOCK, RWKV_IN_W)))
    return pl.pallas_call(
        functools.partial(_wkv_kernel, bb=bb, tt=tt, sub=sub, group=group),
        grid=(nb // bb, t // tt),
        in_specs=[pl.BlockSpec((bb, tt, RWKV_IN_W), lambda b, j: (b, j, 0)), sh_spec, st_spec,
                  _full((1, RWKV_IN_W)), vec, vec, vec, vec, vec, vec, vec,
                  _full((DECAY_LORA, RWKV_W)), _full((AAA_LORA, RWKV_W)), _full((GATE_LORA, RWKV_W)),
                  _full(tri.shape), _full(seg.shape)],
        out_specs=[pl.BlockSpec((bb, tt, RWKV_W), lambda b, j: (b, j, 0)), st_spec, sh_spec],
        out_shape=[jax.ShapeDtypeStruct((nb, t, RWKV_W), F32), jax.ShapeDtypeStruct(s0.shape, F32),
                   jax.ShapeDtypeStruct(sh0.shape, F32)],
        scratch_shapes=[pltpu.VMEM((rows, RWKV_W), F32)] * WKV_SCRATCH,
        compiler_params=_cparams(("parallel", "arbitrary"), vmem),
        name="rwkv7",
    )(zw, sh0, s0, p["mu"], p["w0"], p["a0"], p["k_k"], p["k_a"], p["r_k"], p["lnx_g"], p["lnx_b"],
      p["w_decay_up"], p["w_a_up"], p["w_g_up"], tri, seg)


def _merge_rows(x, zg, oa, ob, wa_ref, wb_ref, wo_ref):
    ga = jax.nn.sigmoid(zg[:, 0:D_MODEL])
    gb = jax.nn.sigmoid(zg[:, D_MODEL:GATE_W])
    merged = ga * _nn(oa, wa_ref[...]) + gb * _nn(ob, wb_ref[...])
    return x + _nn(merged, wo_ref[...])


def _merge_kernel(x_ref, zg_ref, oa_ref, ob_ref, wa_ref, wb_ref, wo_ref, o_ref):
    o_ref[...] = _merge_rows(x_ref[...], zg_ref[...], oa_ref[...], ob_ref[...], wa_ref, wb_ref, wo_ref)


def _merge(x, zg, o_ret, o_rwkv, wa, wb, wo):
    n = x.shape[0]
    tm = min(ROW_TILE, n)
    row = lambda w: pl.BlockSpec((tm, w), lambda i: (i, 0))
    vmem = _vmem(_nbytes((tm, 2 * D_MODEL + GATE_W + RET_W + RWKV_W)), _nbytes((2 * D_MODEL, D_MODEL), BF16),
                 6 * _nbytes((tm, D_MODEL)))
    return pl.pallas_call(
        _merge_kernel,
        grid=(n // tm,),
        in_specs=[row(D_MODEL), row(GATE_W), row(RET_W), row(RWKV_W),
                  _full((RET_W, D_MODEL)), _full((RWKV_W, D_MODEL)), _full((D_MODEL, D_MODEL))],
        out_specs=row(D_MODEL),
        out_shape=jax.ShapeDtypeStruct((n, D_MODEL), F32),
        compiler_params=_cparams(("parallel",), vmem),
        name="merge",
    )(x, zg, o_ret, o_rwkv, wa, wb, wo)


def _softmax(sc):
    e = jnp.exp(sc - jnp.max(sc, axis=-1, keepdims=True))
    return e / jnp.sum(e, axis=-1, keepdims=True)


def _attend_heads(q_scr, mk_ref, mv_ref, ox_scr, b, r0, tq):
    for h in range(X_HEADS):
        sl = slice(h * X_HD, (h + 1) * X_HD)
        att = _softmax(_nt(q_scr[pl.ds(r0, tq), sl], mk_ref[b, :, sl]) * (X_HD ** -0.5))
        ox_scr[pl.ds(r0, tq), sl] = _nn(att, mv_ref[b, :, sl])


def _attend_tiles(q_scr, mk_refs, mv_refs, ox_scr, b, r0, tq):
    halves = X_HD // LANES
    q = q_scr[pl.ds(r0, tq), :]
    qs = jnp.concatenate([q[:, h * X_HD + c * LANES:h * X_HD + (c + 1) * LANES]
                          for c in range(halves) for h in range(X_HEADS)], axis=0)
    z = jnp.concatenate([_nt(qs, k_ref[b]) for k_ref in mk_refs], axis=1)
    n = X_HEADS * tq
    slot = lax.broadcasted_iota(jnp.int32, (n, MEM_ROWS), 1) % ROWS_PER_TOKEN
    head = lax.broadcasted_iota(jnp.int32, (n, MEM_ROWS), 0) // tq
    own = slot == head
    part = jnp.where(own, z[:n], 0.0) + jnp.where(slot == head + X_HEADS, z[n:], 0.0)
    sc = part + pltpu.roll(part, MEM_ROWS - X_HEADS, axis=1)
    att = _softmax(jnp.where(own, sc * (X_HD ** -0.5), -jnp.inf))
    att2 = jnp.concatenate([att, pltpu.roll(att, X_HEADS, axis=1)], axis=0)
    band = MEM_ROWS // len(mv_refs)
    o = sum(_nn(att2[:, n * band:(n + 1) * band], v_ref[b]) for n, v_ref in enumerate(mv_refs))
    ox_scr[pl.ds(r0, tq), :] = jnp.concatenate(
        [o[(c * X_HEADS + h) * tq:(c * X_HEADS + h + 1) * tq] for h in range(X_HEADS) for c in range(halves)], axis=1)


MEM_STREAMS = 4


def _cross_kernel(x_ref, *refs, bb, tq):
    mk_refs, mv_refs = refs[:MEM_STREAMS], refs[MEM_STREAMS:2 * MEM_STREAMS]
    g_ref, wq_ref, wo_ref, o_ref, q_scr, ox_scr = refs[2 * MEM_STREAMS:]
    rows = bb * tq
    x = x_ref[...].reshape(rows, D_MODEL)
    q_scr[...] = _nn(_rms(x, g_ref[...]), wq_ref[...])

    def body(b, _):
        r0 = b * tq if isinstance(b, int) else pl.multiple_of(b * tq, tq)
        _attend_tiles(q_scr, mk_refs, mv_refs, ox_scr, b, r0, tq)

    _loop_bc(bb, 1, body)
    o_ref[...] = (x + _nn(ox_scr[...], wo_ref[...])).reshape(bb, tq, D_MODEL)


def _cross(x, mem_k, mem_v, g, wq, wo, bb, tq):
    nb, t, _ = x.shape
    band = MEM_ROWS // MEM_STREAMS
    x_spec = pl.BlockSpec((bb, tq, D_MODEL), lambda b, j: (b, j, 0))
    m_specs = [pl.BlockSpec((bb, band, LANES), functools.partial(lambda b, j, n: (b, n, 0), n=n))
               for n in range(MEM_STREAMS)]
    vmem = _vmem(2 * _nbytes((bb, tq, D_MODEL)) + 2 * _nbytes((bb, MEM_ROWS, LANES)),
                 2 * _nbytes((D_MODEL, D_MODEL), BF16) + 2 * _nbytes((bb * tq, D_MODEL)),
                 6 * _nbytes((bb * tq, D_MODEL)) + 2 * _nbytes((MEM_LEN, D_MODEL))
                 + 8 * _nbytes((2 * X_HEADS * tq, MEM_ROWS)))
    return pl.pallas_call(
        functools.partial(_cross_kernel, bb=bb, tq=tq),
        grid=(nb // bb, t // tq),
        in_specs=[x_spec] + m_specs + m_specs
        + [_full((1, D_MODEL)), _full((D_MODEL, D_MODEL)), _full((D_MODEL, D_MODEL))],
        out_specs=x_spec,
        out_shape=jax.ShapeDtypeStruct(x.shape, F32),
        scratch_shapes=[pltpu.VMEM((bb * tq, D_MODEL), F32), pltpu.VMEM((bb * tq, D_MODEL), F32)],
        compiler_params=_cparams(("parallel", "arbitrary"), vmem),
        name="cross_attn",
    )(x, *([mem_k] * MEM_STREAMS), *([mem_v] * MEM_STREAMS), g, wq, wo)


MLP_FF_CHUNK = 1024


def _mlp_rows(x, g_ref, wu_ref, wd_ref, gf_ref, final_norm):
    h = _rms(x, g_ref[...]).astype(BF16)
    acc = x
    for c in range(0, D_FF, MLP_FF_CHUNK):
        u = jnp.maximum(jnp.dot(h, wu_ref[:, c:c + MLP_FF_CHUNK], preferred_element_type=F32), 0.0)
        acc = acc + _nn(u * u, wd_ref[c:c + MLP_FF_CHUNK, :])
    return _rms(acc, gf_ref[...]) if final_norm else acc


def _mlp_kernel(x_ref, g_ref, wu_ref, wd_ref, gf_ref, o_ref, *, final_norm):
    o_ref[...] = _mlp_rows(x_ref[...], g_ref, wu_ref, wd_ref, gf_ref, final_norm)


def _mlp(x, g, w_up, w_down, g_final, final_norm):
    n = x.shape[0]
    tm = min(ROW_TILE, n)
    row = pl.BlockSpec((tm, D_MODEL), lambda i: (i, 0))
    vmem = _vmem(2 * _nbytes((tm, D_MODEL)), 2 * _nbytes((D_MODEL, D_FF), BF16),
                 4 * _nbytes((tm, MLP_FF_CHUNK)) + 4 * _nbytes((tm, D_MODEL)))
    return pl.pallas_call(
        functools.partial(_mlp_kernel, final_norm=final_norm),
        grid=(n // tm,),
        in_specs=[row, _full((1, D_MODEL)), _full((D_MODEL, D_FF)), _full((D_FF, D_MODEL)), _full((1, D_MODEL))],
        out_specs=row,
        out_shape=jax.ShapeDtypeStruct((n, D_MODEL), F32),
        compiler_params=_cparams(("parallel",), vmem),
        name="mlp",
    )(x, g, w_up, w_down, g_final)


POST_TILE = 512


def _post_kernel(x_ref, zg_ref, oa_ref, ob_ref, mk_ref, mv_ref, wa_ref, wb_ref, wo_ref, gc_ref, wq_ref, wco_ref,
                 gm_ref, wu_ref, wd_ref, gf_ref, o_ref, q_scr, ox_scr, *, tq, final_norm):
    x1 = _merge_rows(x_ref[0], zg_ref[0], oa_ref[0], ob_ref[0], wa_ref, wb_ref, wo_ref)
    q_scr[...] = _nn(_rms(x1, gc_ref[...]), wq_ref[...])
    _attend_heads(q_scr, mk_ref, mv_ref, ox_scr, 0, 0, tq)
    x2 = x1 + _nn(ox_scr[...], wco_ref[...])
    o_ref[0] = _mlp_rows(x2, gm_ref, wu_ref, wd_ref, gf_ref, final_norm)


def _post(x, zg, o_ret, o_rwkv, mem_k, mem_v, w, g_final, final_norm):
    nb, t, _ = x.shape
    tq = min(POST_TILE, t)
    row = lambda width: pl.BlockSpec((1, tq, width), lambda b, j: (b, j, 0))
    mem = pl.BlockSpec((1, MEM_LEN, D_MODEL), lambda b, j: (b, 0, 0))
    sq = _full((D_MODEL, D_MODEL))
    vec = _full((1, D_MODEL))
    vmem = _vmem(_nbytes((tq, 2 * D_MODEL + GATE_W + RET_W + RWKV_W)) + 2 * _nbytes((MEM_LEN, D_MODEL), BF16),
                 _nbytes((4 * D_MODEL + 2 * D_FF, D_MODEL), BF16) + 2 * _nbytes((tq, D_MODEL)),
                 8 * _nbytes((tq, D_MODEL)) + 4 * _nbytes((tq, MLP_FF_CHUNK)))
    return pl.pallas_call(
        functools.partial(_post_kernel, tq=tq, final_norm=final_norm),
        grid=(nb, t // tq),
        in_specs=[row(D_MODEL), row(GATE_W), row(RET_W), row(RWKV_W), mem, mem,
                  _full((RET_W, D_MODEL)), _full((RWKV_W, D_MODEL)), sq, vec, sq, sq,
                  vec, _full((D_MODEL, D_FF)), _full((D_FF, D_MODEL)), vec],
        out_specs=row(D_MODEL),
        out_shape=jax.ShapeDtypeStruct(x.shape, F32),
        scratch_shapes=[pltpu.VMEM((tq, D_MODEL), F32), pltpu.VMEM((tq, D_MODEL), F32)],
        compiler_params=_cparams(("parallel", "arbitrary"), vmem),
        name="post",
    )(x, zg, o_ret, o_rwkv, mem_k, mem_v, w["w_branch_a"], w["w_branch_b"], w["w_out"], w["g_cross"], w["w_cq"],
      w["w_co"], w["g_mlp"], w["w_up"], w["w_down"], g_final)


class _Tiling(NamedTuple):
    seq_block: int
    time_block: int
    cross_block: int
    ret_group: int
    wkv_group: int


def _tiling(nb, t):
    if t > RET_CHUNK:
        tt = min(ROW_TILE, t)
        return _Tiling(seq_block=1, time_block=tt, cross_block=1,
                       ret_group=math.gcd(4, tt // RET_CHUNK), wkv_group=math.gcd(4, tt // WKV_BLOCK))
    seqs = WKV_BLOCK // t
    return _Tiling(seq_block=seqs, time_block=t, cross_block=math.gcd(4, nb), ret_group=seqs, wkv_group=1)


def _layer(x, pos, s_ret, s_wkv, s_shift, mem_k, mem_v, w, g_final, final_norm):
    nb, t, _ = x.shape
    n = nb * t
    tl = _tiling(nb, t)
    xf = x.reshape(n, D_MODEL)
    zg, zr, zw = _in_proj(xf, w["g_mix"], w["w_in"])
    o_ret, s_ret_new = _retention(zr.reshape(nb, t, 4 * RET_W), s_ret, pos, tl.seq_block, tl.time_block, tl.ret_group)
    o_wkv, s_wkv_new, shift_new = _rwkv7(zw.reshape(nb, t, RWKV_IN_W), s_shift.reshape(nb, 1, RWKV_IN_W), s_wkv,
                                         w, tl.seq_block, tl.time_block, tl.wkv_group)
    if mem_k.shape[1:] == (MEM_LEN, D_MODEL):
        y = _post(x, zg.reshape(nb, t, GATE_W), o_ret, o_wkv, mem_k, mem_v, w, g_final, final_norm)
    else:
        x1 = _merge(xf, zg, o_ret.reshape(n, RET_W), o_wkv.reshape(n, RWKV_W),
                    w["w_branch_a"], w["w_branch_b"], w["w_out"])
        x2 = _cross(x1.reshape(nb, t, D_MODEL), mem_k, mem_v, w["g_cross"], w["w_cq"], w["w_co"],
                    tl.cross_block, tl.time_block)
        y = _mlp(x2.reshape(n, D_MODEL), w["g_mlp"], w["w_up"], w["w_down"], g_final, final_norm).reshape(x.shape)
    return y, s_ret_new, s_wkv_new, shift_new.reshape(nb, RWKV_IN_W)


def _layer_weights(l, g_mix, w_in, w_branch_a, w_branch_b, w_out, mu_shift, w0, w_decay_up, a0, w_a_up, w_g_up,
                   k_k, k_a, r_k, lnx_g, lnx_b, g_cross, w_cq, w_co, g_mlp, w_up, w_down):
    vec = lambda v: v[l].reshape(1, -1).astype(F32)
    mat = lambda m: m[l].astype(BF16)
    return dict(g_mix=vec(g_mix), w_in=mat(w_in), w_branch_a=mat(w_branch_a), w_branch_b=mat(w_branch_b),
                w_out=mat(w_out), mu=vec(mu_shift), w0=vec(w0), w_decay_up=mat(w_decay_up), a0=vec(a0),
                w_a_up=mat(w_a_up), w_g_up=mat(w_g_up), k_k=vec(k_k), k_a=vec(k_a), r_k=vec(r_k), lnx_g=vec(lnx_g),
                lnx_b=vec(lnx_b), g_cross=vec(g_cross), w_cq=mat(w_cq), w_co=mat(w_co), g_mlp=vec(g_mlp),
                w_up=mat(w_up), w_down=mat(w_down))


def kernel(x_prompt, x_sample, mem_prompt, state_ret, state_wkv, state_shift, cache_mem_k, cache_mem_v, g_mix, w_in, w_branch_a, w_branch_b, w_out, mu_shift, w0, w_decay_up, a0, w_a_up, w_g_up, k_k, k_a, r_k, lnx_g, lnx_b, g_cross, g_mem, w_cq, w_ck, w_cv, w_co, g_mlp, w_up, w_down, g_final):
    depth = w_in.shape[0]
    bp, tp, _ = x_prompt.shape
    bs, ts, _ = x_sample.shape
    pos_p = jnp.arange(tp, dtype=jnp.int32)
    pos_s = PAST_LEN + jnp.arange(ts, dtype=jnp.int32)
    gf = g_final.reshape(1, D_MODEL)

    xp, xs = x_prompt, x_sample
    outs_p, outs_s = [], []
    for l in range(depth):
        w = _layer_weights(l, g_mix, w_in, w_branch_a, w_branch_b, w_out, mu_shift, w0, w_decay_up, a0, w_a_up,
                           w_g_up, k_k, k_a, r_k, lnx_g, lnx_b, g_cross, w_cq, w_co, g_mlp, w_up, w_down)
        last = l == depth - 1
        k_tiles, v_tiles, k_att, v_att = _mem_kv(mem_prompt.reshape(bp * MEM_LEN, D_MODEL),
                                                 g_mem[l].reshape(1, D_MODEL), w_ck[l].astype(BF16), w_cv[l].astype(BF16))
        xp, sr, sw, ss = _layer(
            xp, pos_p,
            jnp.zeros((bp, RET_HEADS, RET_HD, RET_HD), F32), jnp.zeros((bp, RWKV_HEADS, RWKV_HD, RWKV_HD), F32),
            jnp.zeros((bp, RWKV_IN_W), F32), k_att.reshape(bp, MEM_LEN, D_MODEL), v_att.reshape(bp, MEM_LEN, D_MODEL),
            w, gf, last)
        outs_p.append((sr, sw, ss, _from_tile_order(k_tiles.reshape(bp, MEM_ROWS, LANES)),
                       _from_tile_order(v_tiles.reshape(bp, MEM_ROWS, LANES))))
        xs, sr2, sw2, ss2 = _layer(
            xs, pos_s, state_ret[l], state_wkv[l], state_shift[l],
            _tile_order(cache_mem_k[l]), _tile_order(cache_mem_v[l]), w, gf, last)
        outs_s.append((sr2, sw2, ss2))

    stack = lambda items, i: jnp.stack([it[i] for it in items])
    return (xp, xs, stack(outs_p, 0), stack(outs_p, 1), stack(outs_p, 2), stack(outs_p, 3), stack(outs_p, 4),
            stack(outs_s, 0), stack(outs_s, 1), stack(outs_s, 2))
```

```python
import functools
import math
from typing import NamedTuple

import jax
import jax.numpy as jnp
from jax import lax
from jax.experimental import pallas as pl
from jax.experimental.pallas import tpu as pltpu

F32 = jnp.float32
BF16 = jnp.bfloat16

D_MODEL = 1024
PAST_LEN = 16384
RET_HEADS = 4
RET_HD = 128
RET_W = RET_HEADS * RET_HD
RET_CHUNK = 128
RET_GN_EPS = 1e-5
ROPE_BASE = 10000.0
RWKV_HEADS = 8
RWKV_HD = 64
RWKV_W = RWKV_HEADS * RWKV_HD
DECAY_LORA = 64
AAA_LORA = 64
GATE_LORA = 128
RWKV_GN_EPS = 64e-5
RWKV_IN_W = 3 * RWKV_W + DECAY_LORA + AAA_LORA + GATE_LORA
GATE_W = 2 * D_MODEL
O_RET = 2 * D_MODEL
O_RWKV = O_RET + 4 * RET_W
IN_W = O_RWKV + RWKV_IN_W
MEM_LEN = 256
X_HEADS = 4
X_HD = D_MODEL // X_HEADS
D_FF = 4 * D_MODEL
RMS_EPS = 1e-6

V7X_VMEM_BYTES = 64 * 1024 * 1024
VMEM_CAP_BYTES = V7X_VMEM_BYTES - 8 * 1024 * 1024
ROW_TILE = 512


def _cparams(sem, vmem_bytes):
    return pltpu.CompilerParams(dimension_semantics=sem, vmem_limit_bytes=int(min(vmem_bytes, VMEM_CAP_BYTES)))


def _nbytes(shape, dtype=F32):
    return math.prod(shape) * jnp.dtype(dtype).itemsize


def _nn(a, b):
    return jnp.dot(a.astype(BF16), b.astype(BF16), preferred_element_type=F32)


def _nt(a, b):
    return lax.dot_general(a.astype(BF16), b.astype(BF16), (((1,), (1,)), ((), ())), preferred_element_type=F32)


def _tn(a, b):
    return lax.dot_general(a.astype(BF16), b.astype(BF16), (((0,), (0,)), ((), ())), preferred_element_type=F32)


def _rms(x, g):
    return x * lax.rsqrt(jnp.mean(x * x, axis=-1, keepdims=True) + RMS_EPS) * g


def _head_norm(o, eps):
    mu = jnp.mean(o, axis=-1, keepdims=True)
    d = o - mu
    var = jnp.mean(d * d, axis=-1, keepdims=True)
    return d * lax.rsqrt(var + eps)


def _full(shape):
    zeros = (0,) * len(shape)
    return pl.BlockSpec(shape, lambda *_: zeros, pipeline_mode=pl.Buffered(1))


def _vmem(pipelined, resident, temps):
    return 2 * pipelined + resident + temps


def _in_proj_kernel(x_ref, g_ref, w_ref, zg_ref, zr_ref, zw_ref):
    h = _rms(x_ref[...], g_ref[...]).astype(BF16)
    zg_ref[...] = jnp.dot(h, w_ref[:, 0:O_RET], preferred_element_type=F32)
    zr_ref[...] = jnp.dot(h, w_ref[:, O_RET:O_RWKV], preferred_element_type=F32)
    zw_ref[...] = jnp.dot(h, w_ref[:, O_RWKV:IN_W], preferred_element_type=F32)


def _in_proj(x, g, w_in):
    n = x.shape[0]
    tm = min(ROW_TILE, n)
    vmem = _vmem(_nbytes((tm, D_MODEL + IN_W)), _nbytes((D_MODEL, IN_W), BF16), _nbytes((tm, 4 * RET_W + D_MODEL)))
    return pl.pallas_call(
        _in_proj_kernel,
        grid=(n // tm,),
        in_specs=[pl.BlockSpec((tm, D_MODEL), lambda i: (i, 0)), _full((1, D_MODEL)), _full((D_MODEL, IN_W))],
        out_specs=[pl.BlockSpec((tm, GATE_W), lambda i: (i, 0)),
                   pl.BlockSpec((tm, 4 * RET_W), lambda i: (i, 0)),
                   pl.BlockSpec((tm, RWKV_IN_W), lambda i: (i, 0))],
        out_shape=[jax.ShapeDtypeStruct((n, GATE_W), F32), jax.ShapeDtypeStruct((n, 4 * RET_W), F32),
                   jax.ShapeDtypeStruct((n, RWKV_IN_W), F32)],
        compiler_params=_cparams(("parallel",), vmem),
        name="in_proj",
    )(x, g, w_in)


LANES = 128
MEM_ROWS = MEM_LEN * X_HEADS * (X_HD // LANES)
ROWS_PER_TOKEN = MEM_ROWS // MEM_LEN


def _tile_order(mem):
    nb = mem.shape[0]
    return (mem.reshape(nb, MEM_LEN, X_HEADS, X_HD // LANES, LANES).transpose(0, 1, 3, 2, 4)
            .reshape(nb, MEM_ROWS, LANES))


def _from_tile_order(raw):
    nb = raw.shape[0]
    return (raw.reshape(nb, MEM_LEN, X_HD // LANES, X_HEADS, LANES).transpose(0, 1, 3, 2, 4)
            .reshape(nb, MEM_LEN, X_HEADS, X_HD))


def _mem_kv_kernel(x_ref, g_ref, wk_ref, wv_ref, kt_ref, vt_ref, kb_ref, vb_ref, *, tm):
    h = _rms(x_ref[...], g_ref[...]).astype(BF16)
    for w_ref, t_ref, b_ref in ((wk_ref, kt_ref, kb_ref), (wv_ref, vt_ref, vb_ref)):
        y = jnp.dot(h, w_ref[...], preferred_element_type=F32)
        b_ref[...] = y.astype(BF16)
        for hd in range(X_HEADS):
            for c in range(X_HD // LANES):
                col = hd * X_HD + c * LANES
                t_ref[pl.ds(c * X_HEADS + hd, tm, stride=ROWS_PER_TOKEN), :] = y[:, col:col + LANES]


def _mem_kv(mem, g, wk, wv):
    n = mem.shape[0]
    tm = min(ROW_TILE, n)
    vmem = _vmem(4 * _nbytes((tm, D_MODEL)), 2 * _nbytes((D_MODEL, D_MODEL), BF16), 4 * _nbytes((tm, D_MODEL)))
    row = pl.BlockSpec((tm, D_MODEL), lambda i: (i, 0))
    tile = pl.BlockSpec((tm * ROWS_PER_TOKEN, LANES), lambda i: (i, 0))
    return pl.pallas_call(
        functools.partial(_mem_kv_kernel, tm=tm),
        grid=(n // tm,),
        in_specs=[row, _full((1, D_MODEL)), _full((D_MODEL, D_MODEL)), _full((D_MODEL, D_MODEL))],
        out_specs=[tile, tile, row, row],
        out_shape=[jax.ShapeDtypeStruct((n * ROWS_PER_TOKEN, LANES), F32)] * 2
        + [jax.ShapeDtypeStruct((n, D_MODEL), BF16)] * 2,
        compiler_params=_cparams(("parallel",), vmem),
        name="mem_kv",
    )(mem, g, wk, wv)


def _ret_kernel(zr_ref, s0_ref, cos_ref, sin_ref, dm_ref, qd_ref, kd_ref, cd_ref, o_ref, s_ref, *,
                bb, tt, chunk, group):
    @pl.when(pl.program_id(1) == 0)
    def _():
        s_ref[...] = s0_ref[...]

    nch = tt // chunk
    assert bb == 1 or nch == 1, "a block holds either one sequence or one chunk per sequence"
    assert (bb * nch) % group == 0
    heads = range(RET_HEADS)

    def body(j, carry):
        if nch == 1:
            items = [(j * group + n, 0) for n in range(group)]
        else:
            items = [(0, pl.multiple_of((j * group + n) * chunk, chunk)) for n in range(group)]
        q, k, v = [], [], []
        for b, r0 in items:
            rows = pl.ds(r0, chunk)
            cos = cos_ref[rows, :]
            sin = sin_ref[rows, :]
            for h in heads:
                qh = zr_ref[b, rows, h * RET_HD:(h + 1) * RET_HD]
                kh = zr_ref[b, rows, RET_W + h * RET_HD:RET_W + (h + 1) * RET_HD]
                q.append(qh * cos + pltpu.roll(qh, RET_HD // 2, axis=1) * sin)
                k.append((kh * cos + pltpu.roll(kh, RET_HD // 2, axis=1) * sin) * (RET_HD ** -0.5))
                v.append(zr_ref[b, rows, 2 * RET_W + h * RET_HD:2 * RET_W + (h + 1) * RET_HD])
        hd = [h for _ in items for h in heads]
        sc = [_nt(qi, ki) * dm_ref[h] for qi, ki, h in zip(q, k, hd)]
        kv = [_tn(ki * kd_ref[h], vi) for ki, vi, h in zip(k, v, hd)]
        inner = [_nn(si, vi) for si, vi in zip(sc, v)]
        states = []
        for n, (b, _) in enumerate(items):
            for h in heads:
                s = s_ref[b, h] if (nch == 1 or n == 0) else states[-RET_HEADS] * cd_ref[h] + kv[(n - 1) * RET_HEADS + h]
                states.append(s)
        last = len(items) - 1
        for n, (b, _) in enumerate(items):
            if nch == 1 or n == last:
                for h in heads:
                    i = n * RET_HEADS + h
                    s_ref[b, h] = states[i] * cd_ref[h] + kv[i]
        cross = [_nn(qi, si) * qd_ref[h] for qi, si, h in zip(q, states, hd)]
        for n, (b, r0) in enumerate(items):
            rows = pl.ds(r0, chunk)
            for h in heads:
                i = n * RET_HEADS + h
                g = zr_ref[b, rows, 3 * RET_W + h * RET_HD:3 * RET_W + (h + 1) * RET_HD]
                o_ref[b, rows, h * RET_HD:(h + 1) * RET_HD] = (_head_norm(inner[i] + cross[i], RET_GN_EPS)
                                                              * (g * jax.nn.sigmoid(g)))
        return carry

    lax.fori_loop(0, bb * nch // group, body, 0)


def _ret_consts(chunk):
    lg = jnp.log1p(-jnp.exp2(-5.0 - jnp.arange(RET_HEADS, dtype=F32)))
    idx = jnp.arange(chunk, dtype=F32)
    diff = idx[:, None] - idx[None, :]
    dmask = jnp.where(diff[None] >= 0, jnp.exp(jnp.maximum(diff, 0.0)[None] * lg[:, None, None]), 0.0)
    q_dec = jnp.exp((idx + 1.0)[None, :] * lg[:, None])
    k_dec = jnp.exp((chunk - 1.0 - idx)[None, :] * lg[:, None])
    c_dec = jnp.exp(chunk * lg)
    bc = lambda t: jnp.broadcast_to(t[:, :, None], (RET_HEADS, t.shape[1], RET_HD))
    return dmask, bc(q_dec), bc(k_dec), bc(c_dec[:, None])


def _rope_tables(pos):
    half = RET_HD // 2
    inv = 1.0 / (ROPE_BASE ** (jnp.arange(half, dtype=F32) / half))
    ang = pos.astype(F32)[:, None] * inv[None, :]
    cos, sin = jnp.cos(ang), jnp.sin(ang)
    return jnp.concatenate([cos, cos], axis=1), jnp.concatenate([-sin, sin], axis=1)


def _retention(zr, s0, pos, bb, tt, group):
    nb, t, _ = zr.shape
    chunk = math.gcd(t, RET_CHUNK)
    cos, sin = _rope_tables(pos)
    dmask, q_dec, k_dec, c_dec = _ret_consts(chunk)
    st_spec = pl.BlockSpec((bb, RET_HEADS, RET_HD, RET_HD), lambda b, j: (b, 0, 0, 0))
    tab_spec = pl.BlockSpec((tt, RET_HD), lambda b, j: (j, 0))
    vmem = _vmem(_nbytes((bb, tt, 5 * RET_W)) + 2 * _nbytes((bb, RET_HEADS, RET_HD, RET_HD)) + 2 * _nbytes((tt, RET_HD)),
                 _nbytes(dmask.shape) + 3 * _nbytes(q_dec.shape), group * 8 * _nbytes((chunk, 4 * RET_W)))
    return pl.pallas_call(
        functools.partial(_ret_kernel, bb=bb, tt=tt, chunk=chunk, group=group),
        grid=(nb // bb, t // tt),
        in_specs=[pl.BlockSpec((bb, tt, 4 * RET_W), lambda b, j: (b, j, 0)), st_spec, tab_spec, tab_spec,
                  _full(dmask.shape), _full(q_dec.shape), _full(k_dec.shape), _full(c_dec.shape)],
        out_specs=[pl.BlockSpec((bb, tt, RET_W), lambda b, j: (b, j, 0)), st_spec],
        out_shape=[jax.ShapeDtypeStruct((nb, t, RET_W), F32), jax.ShapeDtypeStruct(s0.shape, F32)],
        compiler_params=_cparams(("parallel", "arbitrary"), vmem),
        name="retention",
    )(zr, s0, cos, sin, dmask, q_dec, k_dec, c_dec)


WKV_BLOCK = 64
WKV_SCRATCH = 10


def _pow2(n):
    return n & (n - 1) == 0


def _imod(x, n):
    return jnp.bitwise_and(x, n - 1) if _pow2(n) else lax.rem(x, n)


def _idiv(x, n):
    return jnp.right_shift(x, n.bit_length() - 1) if _pow2(n) else lax.div(x, n)


def _wkv_kernel(zw_ref, sh0_ref, s0_ref, mu_ref, w0_ref, a0_ref, kk_ref, ka_ref, rk_ref, lng_ref, lnb_ref,
                wd_ref, wa_ref, wg_ref, tri_ref, seg_ref, o_ref, s_ref, sh_ref,
                at_s, rt_s, bt_s, kt_s, bh_s, kh_s, v_s, g_s, bonus_s, pend_s, *, bb, tt, sub, group):
    @pl.when(pl.program_id(1) == 0)
    def _():
        s_ref[...] = s0_ref[...]
        sh_ref[...] = sh0_ref[...]

    C = WKV_BLOCK
    R = bb * tt
    nblk = R // C
    nseq = C // sub
    assert (bb == 1 and sub == C) or (nblk == 1 and sub == tt), "one sequence per tile, or whole sequences in one block"
    assert nblk % group == 0
    n_sq = max(int(math.log2(sub)) - 1, 0)
    heads = range(RWKV_HEADS)
    hsl = [slice(h * RWKV_HD, (h + 1) * RWKV_HD) for h in heads]
    half = RWKV_W // 2

    def seg_sum(t):
        seg = seg_ref[...]
        return jnp.concatenate([_nn(t[:, :half], seg), _nn(t[:, half:], seg)], axis=1)

    pw = zw_ref[...].reshape(R, RWKV_IN_W)
    row = lax.broadcasted_iota(jnp.int32, (R, RWKV_IN_W), 0)
    rolled = pltpu.roll(pw, 1, axis=0)
    if bb == 1:
        prev = jnp.where(row == 0, sh_ref[0], rolled)
        sh_ref[0] = pw[R - 1:R, :]
    else:
        carry = jnp.broadcast_to(sh_ref[...], (bb, tt, RWKV_IN_W)).reshape(R, RWKV_IN_W)
        prev = jnp.where(_imod(row, tt) == 0, carry, rolled)
        sh_ref[...] = pw.reshape(bb, tt, RWKV_IN_W)[:, tt - 1:tt, :]
    xm = pw + mu_ref[...] * (prev - pw)
    r = xm[:, 0:RWKV_W]
    kb = xm[:, RWKV_W:2 * RWKV_W]
    o1 = 3 * RWKV_W
    wl = xm[:, o1:o1 + DECAY_LORA]
    al = xm[:, o1 + DECAY_LORA:o1 + DECAY_LORA + AAA_LORA]
    gl = xm[:, o1 + DECAY_LORA + AAA_LORA:RWKV_IN_W]
    logw = -math.exp(-0.5) * jax.nn.sigmoid(w0_ref[...] + _nn(jnp.tanh(wl), wd_ref[...]))
    a = jax.nn.sigmoid(a0_ref[...] + _nn(al, wa_ref[...]))
    l1 = logw.astype(BF16)
    rem = logw - l1.astype(F32)
    l2 = rem.astype(BF16)
    l3 = (rem - l2.astype(F32)).astype(BF16)
    tri = tri_ref[...]
    cum = jnp.concatenate(
        [sum(jnp.dot(tri, part[k * C:(k + 1) * C], preferred_element_type=F32) for part in (l1, l2, l3))
         for k in range(nblk)], axis=0)
    cum_end = jnp.broadcast_to(cum.reshape(R // sub, sub, RWKV_W)[:, sub - 1:sub, :],
                               (R // sub, sub, RWKV_W)).reshape(R, RWKV_W)
    kk = kb * kk_ref[...]
    kk = kk * jnp.minimum(lax.rsqrt(seg_sum(kk * kk)), 1e12)
    km = kb * (1.0 + (a - 1.0) * ka_ref[...])
    bv = kk * a
    p_end = jnp.exp(cum_end)
    e_neg = jnp.exp(-cum)
    e_end = p_end * e_neg
    at_s[...] = -kk * jnp.exp(cum - logw)
    rt_s[...] = r * jnp.exp(cum)
    bt_s[...] = bv * e_neg
    kt_s[...] = km * e_neg
    bh_s[...] = bv * e_end
    kh_s[...] = km * e_end
    v_s[...] = xm[:, 2 * RWKV_W:3 * RWKV_W]
    g_s[...] = _nn(jax.nn.sigmoid(gl), wg_ref[...])
    bonus_s[...] = seg_sum(r * km * rk_ref[...])
    pend_s[...] = p_end

    row_id = lax.broadcasted_iota(jnp.int32, (C, C), 0)
    col_id = lax.broadcasted_iota(jnp.int32, (C, C), 1)
    strict = row_id > col_id
    incl = row_id >= col_id
    if nseq > 1:
        same = _idiv(row_id, sub) == _idiv(col_id, sub)
        strict = jnp.logical_and(strict, same)
        incl = jnp.logical_and(incl, same)
    eye = jnp.where(row_id == col_id, 1.0, 0.0).astype(F32)

    def triangular(r0s):
        rows = [pl.ds(r0, C) for r0 in r0s for _ in heads]
        sl = [s for _ in r0s for s in hsl]
        at = [at_s[rw, :][:, s] for rw, s in zip(rows, sl)]
        rt = [rt_s[rw, :][:, s] for rw, s in zip(rows, sl)]
        v = [v_s[rw, :][:, s] for rw, s in zip(rows, sl)]
        m1 = [_nt(jnp.concatenate([a_, r_], axis=0), jnp.concatenate([bt_s[rw, :][:, s], kt_s[rw, :][:, s]], axis=0))
              for a_, r_, rw, s in zip(at, rt, rows, sl)]
        low = [jnp.where(strict, m[:C, :C], 0.0) for m in m1]
        rab = [jnp.where(incl, m[C:, :C], 0.0) for m in m1]
        akk = [jnp.concatenate([jnp.where(strict, m[:C, C:], 0.0), jnp.where(incl, m[C:, C:], 0.0)], axis=0)
               for m in m1]
        akv = [_nn(k_, v_) for k_, v_ in zip(akk, v)]
        tinv = [eye + l for l in low]
        if n_sq:
            x = [_nn(l, l) for l in low]
        for j in range(n_sq):
            if j < n_sq - 1:
                xt = [_nn(jnp.concatenate([xi, ti], axis=0), xi) for xi, ti in zip(x, tinv)]
                x = [t[:C] for t in xt]
                tinv = [ti + t[C:] for ti, t in zip(tinv, xt)]
            else:
                tinv = [ti + _nn(ti, xi) for ti, xi in zip(tinv, x)]
        ua = [_nn(t, a_) for t, a_ in zip(tinv, at)]
        uv = [_nn(t, k_[:C]) for t, k_ in zip(tinv, akv)]
        oa = [r_ + _nn(rb, u_) for r_, rb, u_ in zip(rt, rab, ua)]
        ov = [k_[C:] + _nn(rb, u_) for k_, rb, u_ in zip(akv, rab, uv)]
        nh = RWKV_HEADS
        return [dict(ua=ua[k * nh:(k + 1) * nh], uv=uv[k * nh:(k + 1) * nh], oa=oa[k * nh:(k + 1) * nh],
                     ov=ov[k * nh:(k + 1) * nh], v=v[k * nh:(k + 1) * nh]) for k in range(len(r0s))]

    def advance(r0, q):
        rw = pl.ds(r0, C)
        bh, kh, pend = bh_s[rw, :], kh_s[rw, :], pend_s[rw, :]
        items = [(n, h) for n in range(nseq) for h in heads]
        rs = [slice(n * sub, (n + 1) * sub) for n, _ in items]
        uo = [_nt(jnp.concatenate([q["ua"][h][s], q["oa"][h][s]], axis=0), s_ref[n, h])
              for (n, h), s in zip(items, rs)]
        u = [m[:sub] + q["uv"][h][s] for m, (n, h), s in zip(uo, items, rs)]
        upd = [_tn(jnp.concatenate([u_, q["v"][h][s]], axis=0),
                   jnp.concatenate([bh[s, hsl[h]], kh[s, hsl[h]]], axis=0)) for u_, (n, h), s in zip(u, items, rs)]
        for up, (n, h), s in zip(upd, items, rs):
            s_ref[n, h] = s_ref[n, h] * pend[s, hsl[h]][0:1] + up
        o = [m[sub:] + q["ov"][h][s] for m, (n, h), s in zip(uo, items, rs)]
        o = jnp.concatenate([jnp.concatenate([o[n * RWKV_HEADS + h] for n in range(nseq)], axis=0) for h in heads],
                            axis=1)
        d = o - seg_sum(o) * (1.0 / RWKV_HD)
        var = seg_sum(d * d) * (1.0 / RWKV_HD)
        o = d * lax.rsqrt(var + RWKV_GN_EPS) * lng_ref[...] + lnb_ref[...]
        o = (o + bonus_s[rw, :] * v_s[rw, :]) * g_s[rw, :]
        if bb == 1:
            o_ref[0, rw, :] = o
        else:
            o_ref[...] = o.reshape(bb, tt, RWKV_W)

    def body(j, carry):
        r0s = [j * group * C + k * C for k in range(group)]
        r0s = [r if isinstance(r, int) else pl.multiple_of(r, C) for r in r0s]
        for r0, q in zip(r0s, triangular(r0s)):
            advance(r0, q)
        return carry

    if nblk == group:
        body(0, 0)
    else:
        lax.fori_loop(0, nblk // group, body, 0)


def _rwkv7(zw, sh0, s0, p, bb, tt, group):
    nb, t, _ = zw.shape
    sub = min(t, WKV_BLOCK)
    step = jnp.arange(WKV_BLOCK)
    tri = ((step[:, None] >= step[None, :]) & (step[:, None] // sub == step[None, :] // sub)).astype(BF16)
    lane_head = jnp.arange(RWKV_W // 2) // RWKV_HD
    seg = (lane_head[:, None] == lane_head[None, :]).astype(BF16)
    rows = bb * tt
    st_spec = pl.BlockSpec((bb, RWKV_HEADS, RWKV_HD, RWKV_HD), lambda b, j: (b, 0, 0, 0))
    sh_spec = pl.BlockSpec((bb, 1, RWKV_IN_W), lambda b, j: (b, 0, 0))
    vec = _full((1, RWKV_W))
    vmem = _vmem(_nbytes((rows, RWKV_IN_W + RWKV_W)) + 2 * _nbytes((bb, RWKV_HEADS, RWKV_HD, RWKV_HD))
                 + 2 * _nbytes((bb, 8, RWKV_IN_W)),
                 9 * _nbytes((8, RWKV_IN_W)) + _nbytes((DECAY_LORA + AAA_LORA + GATE_LORA + WKV_BLOCK, RWKV_W), BF16)
                 + _nbytes(seg.shape, BF16) + WKV_SCRATCH * _nbytes((rows, RWKV_W)),
                 4 * _nbytes((rows, RWKV_IN_W)) + 8 * _nbytes((rows, RWKV_W))
                 + group * 32 * _nbytes((WKV_BLOCK, RWKV_IN_W)))
    return pl.pallas_call(
        functools.partial(_wkv_kernel, bb=bb, tt=tt, sub=sub, group=group),
        grid=(nb // bb, t // tt),
        in_specs=[pl.BlockSpec((bb, tt, RWKV_IN_W), lambda b, j: (b, j, 0)), sh_spec, st_spec,
                  _full((1, RWKV_IN_W)), vec, vec, vec, vec, vec, vec, vec,
                  _full((DECAY_LORA, RWKV_W)), _full((AAA_LORA, RWKV_W)), _full((GATE_LORA, RWKV_W)),
                  _full(tri.shape), _full(seg.shape)],
        out_specs=[pl.BlockSpec((bb, tt, RWKV_W), lambda b, j: (b, j, 0)), st_spec, sh_spec],
        out_shape=[jax.ShapeDtypeStruct((nb, t, RWKV_W), F32), jax.ShapeDtypeStruct(s0.shape, F32),
                   jax.ShapeDtypeStruct(sh0.shape, F32)],
        scratch_shapes=[pltpu.VMEM((rows, RWKV_W), F32)] * WKV_SCRATCH,
        compiler_params=_cparams(("parallel", "arbitrary"), vmem),
        name="rwkv7",
    )(zw, sh0, s0, p["mu"], p["w0"], p["a0"], p["k_k"], p["k_a"], p["r_k"], p["lnx_g"], p["lnx_b"],
      p["w_decay_up"], p["w_a_up"], p["w_g_up"], tri, seg)


def _merge_rows(x, zg, oa, ob, wa_ref, wb_ref, wo_ref):
    ga = jax.nn.sigmoid(zg[:, 0:D_MODEL])
    gb = jax.nn.sigmoid(zg[:, D_MODEL:GATE_W])
    merged = ga * _nn(oa, wa_ref[...]) + gb * _nn(ob, wb_ref[...])
    return x + _nn(merged, wo_ref[...])


def _merge_kernel(x_ref, zg_ref, oa_ref, ob_ref, wa_ref, wb_ref, wo_ref, o_ref):
    o_ref[...] = _merge_rows(x_ref[...], zg_ref[...], oa_ref[...], ob_ref[...], wa_ref, wb_ref, wo_ref)


def _merge(x, zg, o_ret, o_rwkv, wa, wb, wo):
    n = x.shape[0]
    tm = min(ROW_TILE, n)
    row = lambda w: pl.BlockSpec((tm, w), lambda i: (i, 0))
    vmem = _vmem(_nbytes((tm, 2 * D_MODEL + GATE_W + RET_W + RWKV_W)), _nbytes((2 * D_MODEL, D_MODEL), BF16),
                 6 * _nbytes((tm, D_MODEL)))
    return pl.pallas_call(
        _merge_kernel,
        grid=(n // tm,),
        in_specs=[row(D_MODEL), row(GATE_W), row(RET_W), row(RWKV_W),
                  _full((RET_W, D_MODEL)), _full((RWKV_W, D_MODEL)), _full((D_MODEL, D_MODEL))],
        out_specs=row(D_MODEL),
        out_shape=jax.ShapeDtypeStruct((n, D_MODEL), F32),
        compiler_params=_cparams(("parallel",), vmem),
        name="merge",
    )(x, zg, o_ret, o_rwkv, wa, wb, wo)


def _softmax(sc):
    e = jnp.exp(sc - jnp.max(sc, axis=-1, keepdims=True))
    return e / jnp.sum(e, axis=-1, keepdims=True)


def _attend_heads(q_scr, mk_ref, mv_ref, ox_scr, b, r0, tq):
    for h in range(X_HEADS):
        sl = slice(h * X_HD, (h + 1) * X_HD)
        att = _softmax(_nt(q_scr[pl.ds(r0, tq), sl], mk_ref[b, :, sl]) * (X_HD ** -0.5))
        ox_scr[pl.ds(r0, tq), sl] = _nn(att, mv_ref[b, :, sl])


def _attend_tiles(q_scr, mk_ref, mv_ref, ox_scr, seqs, tq):
    halves = X_HD // LANES
    n = X_HEADS * tq
    slot = lax.broadcasted_iota(jnp.int32, (n, MEM_ROWS), 1) % ROWS_PER_TOKEN
    head = lax.broadcasted_iota(jnp.int32, (n, MEM_ROWS), 0) // tq
    own = slot == head
    other = slot == head + X_HEADS
    qs = []
    for b in seqs:
        q = q_scr[b * tq:(b + 1) * tq, :]
        qs.append(jnp.concatenate([q[:, h * X_HD + c * LANES:h * X_HD + (c + 1) * LANES]
                                   for c in range(halves) for h in range(X_HEADS)], axis=0))
    z = [_nt(q, mk_ref[b]) for q, b in zip(qs, seqs)]
    part = [jnp.where(own, zi[:n], 0.0) + jnp.where(other, zi[n:], 0.0) for zi in z]
    sc = [p + pltpu.roll(p, MEM_ROWS - X_HEADS, axis=1) for p in part]
    att = [_softmax(jnp.where(own, s * (X_HD ** -0.5), -jnp.inf)) for s in sc]
    att2 = [jnp.concatenate([a, pltpu.roll(a, X_HEADS, axis=1)], axis=0) for a in att]
    o = [_nn(a, mv_ref[b]) for a, b in zip(att2, seqs)]
    for oi, b in zip(o, seqs):
        ox_scr[b * tq:(b + 1) * tq, :] = jnp.concatenate(
            [oi[(c * X_HEADS + h) * tq:(c * X_HEADS + h + 1) * tq] for h in range(X_HEADS) for c in range(halves)],
            axis=1)


def _cross_kernel(x_ref, mk_ref, mv_ref, g_ref, wq_ref, wo_ref, o_ref, q_scr, ox_scr, *, bb, tq):
    rows = bb * tq
    x = x_ref[...].reshape(rows, D_MODEL)
    q_scr[...] = _nn(_rms(x, g_ref[...]), wq_ref[...])
    _attend_tiles(q_scr, mk_ref, mv_ref, ox_scr, range(bb), tq)
    o_ref[...] = (x + _nn(ox_scr[...], wo_ref[...])).reshape(bb, tq, D_MODEL)


def _cross(x, mem_k, mem_v, g, wq, wo, bb, tq):
    nb, t, _ = x.shape
    x_spec = pl.BlockSpec((bb, tq, D_MODEL), lambda b, j: (b, j, 0))
    m_spec = pl.BlockSpec((bb, MEM_ROWS, LANES), lambda b, j: (b, 0, 0))
    vmem = _vmem(2 * _nbytes((bb, tq, D_MODEL)) + 2 * _nbytes((bb, MEM_ROWS, LANES)),
                 2 * _nbytes((D_MODEL, D_MODEL), BF16) + 2 * _nbytes((bb * tq, D_MODEL)),
                 6 * _nbytes((bb * tq, D_MODEL)) + bb * 8 * _nbytes((2 * X_HEADS * tq, MEM_ROWS)))
    return pl.pallas_call(
        functools.partial(_cross_kernel, bb=bb, tq=tq),
        grid=(nb // bb, t // tq),
        in_specs=[x_spec, m_spec, m_spec, _full((1, D_MODEL)), _full((D_MODEL, D_MODEL)), _full((D_MODEL, D_MODEL))],
        out_specs=x_spec,
        out_shape=jax.ShapeDtypeStruct(x.shape, F32),
        scratch_shapes=[pltpu.VMEM((bb * tq, D_MODEL), F32), pltpu.VMEM((bb * tq, D_MODEL), F32)],
        compiler_params=_cparams(("parallel", "arbitrary"), vmem),
        name="cross_attn",
    )(x, mem_k, mem_v, g, wq, wo)


MLP_FF_CHUNK = 1024


def _mlp_rows(x, g_ref, wu_ref, wd_ref, gf_ref, final_norm):
    h = _rms(x, g_ref[...]).astype(BF16)
    acc = x
    for c in range(0, D_FF, MLP_FF_CHUNK):
        u = jnp.maximum(jnp.dot(h, wu_ref[:, c:c + MLP_FF_CHUNK], preferred_element_type=F32), 0.0)
        acc = acc + _nn(u * u, wd_ref[c:c + MLP_FF_CHUNK, :])
    return _rms(acc, gf_ref[...]) if final_norm else acc


def _mlp_kernel(x_ref, g_ref, wu_ref, wd_ref, gf_ref, o_ref, *, final_norm):
    o_ref[...] = _mlp_rows(x_ref[...], g_ref, wu_ref, wd_ref, gf_ref, final_norm)


def _mlp(x, g, w_up, w_down, g_final, final_norm):
    n = x.shape[0]
    tm = min(ROW_TILE, n)
    row = pl.BlockSpec((tm, D_MODEL), lambda i: (i, 0))
    vmem = _vmem(2 * _nbytes((tm, D_MODEL)), 2 * _nbytes((D_MODEL, D_FF), BF16),
                 4 * _nbytes((tm, MLP_FF_CHUNK)) + 4 * _nbytes((tm, D_MODEL)))
    return pl.pallas_call(
        functools.partial(_mlp_kernel, final_norm=final_norm),
        grid=(n // tm,),
        in_specs=[row, _full((1, D_MODEL)), _full((D_MODEL, D_FF)), _full((D_FF, D_MODEL)), _full((1, D_MODEL))],
        out_specs=row,
        out_shape=jax.ShapeDtypeStruct((n, D_MODEL), F32),
        compiler_params=_cparams(("parallel",), vmem),
        name="mlp",
    )(x, g, w_up, w_down, g_final)


POST_TILE = 512


def _post_kernel(x_ref, zg_ref, oa_ref, ob_ref, mk_ref, mv_ref, wa_ref, wb_ref, wo_ref, gc_ref, wq_ref, wco_ref,
                 gm_ref, wu_ref, wd_ref, gf_ref, o_ref, q_scr, ox_scr, *, tq, final_norm):
    x1 = _merge_rows(x_ref[0], zg_ref[0], oa_ref[0], ob_ref[0], wa_ref, wb_ref, wo_ref)
    q_scr[...] = _nn(_rms(x1, gc_ref[...]), wq_ref[...])
    _attend_heads(q_scr, mk_ref, mv_ref, ox_scr, 0, 0, tq)
    x2 = x1 + _nn(ox_scr[...], wco_ref[...])
    o_ref[0] = _mlp_rows(x2, gm_ref, wu_ref, wd_ref, gf_ref, final_norm)


def _post(x, zg, o_ret, o_rwkv, mem_k, mem_v, w, g_final, final_norm):
    nb, t, _ = x.shape
    tq = min(POST_TILE, t)
    row = lambda width: pl.BlockSpec((1, tq, width), lambda b, j: (b, j, 0))
    mem = pl.BlockSpec((1, MEM_LEN, D_MODEL), lambda b, j: (b, 0, 0))
    sq = _full((D_MODEL, D_MODEL))
    vec = _full((1, D_MODEL))
    vmem = _vmem(_nbytes((tq, 2 * D_MODEL + GATE_W + RET_W + RWKV_W)) + 2 * _nbytes((MEM_LEN, D_MODEL), BF16),
                 _nbytes((4 * D_MODEL + 2 * D_FF, D_MODEL), BF16) + 2 * _nbytes((tq, D_MODEL)),
                 8 * _nbytes((tq, D_MODEL)) + 4 * _nbytes((tq, MLP_FF_CHUNK)))
    return pl.pallas_call(
        functools.partial(_post_kernel, tq=tq, final_norm=final_norm),
        grid=(nb, t // tq),
        in_specs=[row(D_MODEL), row(GATE_W), row(RET_W), row(RWKV_W), mem, mem,
                  _full((RET_W, D_MODEL)), _full((RWKV_W, D_MODEL)), sq, vec, sq, sq,
                  vec, _full((D_MODEL, D_FF)), _full((D_FF, D_MODEL)), vec],
        out_specs=row(D_MODEL),
        out_shape=jax.ShapeDtypeStruct(x.shape, F32),
        scratch_shapes=[pltpu.VMEM((tq, D_MODEL), F32), pltpu.VMEM((tq, D_MODEL), F32)],
        compiler_params=_cparams(("parallel", "arbitrary"), vmem),
        name="post",
    )(x, zg, o_ret, o_rwkv, mem_k, mem_v, w["w_branch_a"], w["w_branch_b"], w["w_out"], w["g_cross"], w["w_cq"],
      w["w_co"], w["g_mlp"], w["w_up"], w["w_down"], g_final)


class _Tiling(NamedTuple):
    seq_block: int
    time_block: int
    cross_block: int
    ret_group: int
    wkv_group: int


def _tiling(nb, t):
    if t > RET_CHUNK:
        tt = min(ROW_TILE, t)
        return _Tiling(seq_block=1, time_block=tt, cross_block=1,
                       ret_group=math.gcd(4, tt // RET_CHUNK), wkv_group=math.gcd(4, tt // WKV_BLOCK))
    seqs = WKV_BLOCK // t
    return _Tiling(seq_block=seqs, time_block=t, cross_block=math.gcd(4, nb), ret_group=seqs, wkv_group=1)


def _layer(x, pos, s_ret, s_wkv, s_shift, mem_k, mem_v, w, g_final, final_norm):
    nb, t, _ = x.shape
    n = nb * t
    tl = _tiling(nb, t)
    xf = x.reshape(n, D_MODEL)
    zg, zr, zw = _in_proj(xf, w["g_mix"], w["w_in"])
    o_ret, s_ret_new = _retention(zr.reshape(nb, t, 4 * RET_W), s_ret, pos, tl.seq_block, tl.time_block, tl.ret_group)
    o_wkv, s_wkv_new, shift_new = _rwkv7(zw.reshape(nb, t, RWKV_IN_W), s_shift.reshape(nb, 1, RWKV_IN_W), s_wkv,
                                         w, tl.seq_block, tl.time_block, tl.wkv_group)
    if mem_k.shape[1:] == (MEM_LEN, D_MODEL):
        y = _post(x, zg.reshape(nb, t, GATE_W), o_ret, o_wkv, mem_k, mem_v, w, g_final, final_norm)
    else:
        x1 = _merge(xf, zg, o_ret.reshape(n, RET_W), o_wkv.reshape(n, RWKV_W),
                    w["w_branch_a"], w["w_branch_b"], w["w_out"])
        x2 = _cross(x1.reshape(nb, t, D_MODEL), mem_k, mem_v, w["g_cross"], w["w_cq"], w["w_co"],
                    tl.cross_block, tl.time_block)
        y = _mlp(x2.reshape(n, D_MODEL), w["g_mlp"], w["w_up"], w["w_down"], g_final, final_norm).reshape(x.shape)
    return y, s_ret_new, s_wkv_new, shift_new.reshape(nb, RWKV_IN_W)


def _layer_weights(l, g_mix, w_in, w_branch_a, w_branch_b, w_out, mu_shift, w0, w_decay_up, a0, w_a_up, w_g_up,
                   k_k, k_a, r_k, lnx_g, lnx_b, g_cross, w_cq, w_co, g_mlp, w_up, w_down):
    vec = lambda v: v[l].reshape(1, -1).astype(F32)
    mat = lambda m: m[l].astype(BF16)
    return dict(g_mix=vec(g_mix), w_in=mat(w_in), w_branch_a=mat(w_branch_a), w_branch_b=mat(w_branch_b),
                w_out=mat(w_out), mu=vec(mu_shift), w0=vec(w0), w_decay_up=mat(w_decay_up), a0=vec(a0),
                w_a_up=mat(w_a_up), w_g_up=mat(w_g_up), k_k=vec(k_k), k_a=vec(k_a), r_k=vec(r_k), lnx_g=vec(lnx_g),
                lnx_b=vec(lnx_b), g_cross=vec(g_cross), w_cq=mat(w_cq), w_co=mat(w_co), g_mlp=vec(g_mlp),
                w_up=mat(w_up), w_down=mat(w_down))


def kernel(x_prompt, x_sample, mem_prompt, state_ret, state_wkv, state_shift, cache_mem_k, cache_mem_v, g_mix, w_in, w_branch_a, w_branch_b, w_out, mu_shift, w0, w_decay_up, a0, w_a_up, w_g_up, k_k, k_a, r_k, lnx_g, lnx_b, g_cross, g_mem, w_cq, w_ck, w_cv, w_co, g_mlp, w_up, w_down, g_final):
    depth = w_in.shape[0]
    bp, tp, _ = x_prompt.shape
    bs, ts, _ = x_sample.shape
    pos_p = jnp.arange(tp, dtype=jnp.int32)
    pos_s = PAST_LEN + jnp.arange(ts, dtype=jnp.int32)
    gf = g_final.reshape(1, D_MODEL)

    xp, xs = x_prompt, x_sample
    outs_p, outs_s = [], []
    for l in range(depth):
        w = _layer_weights(l, g_mix, w_in, w_branch_a, w_branch_b, w_out, mu_shift, w0, w_decay_up, a0, w_a_up,
                           w_g_up, k_k, k_a, r_k, lnx_g, lnx_b, g_cross, w_cq, w_co, g_mlp, w_up, w_down)
        last = l == depth - 1
        k_tiles, v_tiles, k_att, v_att = _mem_kv(mem_prompt.reshape(bp * MEM_LEN, D_MODEL),
                                                 g_mem[l].reshape(1, D_MODEL), w_ck[l].astype(BF16), w_cv[l].astype(BF16))
        xp, sr, sw, ss = _layer(
            xp, pos_p,
            jnp.zeros((bp, RET_HEADS, RET_HD, RET_HD), F32), jnp.zeros((bp, RWKV_HEADS, RWKV_HD, RWKV_HD), F32),
            jnp.zeros((bp, RWKV_IN_W), F32), k_att.reshape(bp, MEM_LEN, D_MODEL), v_att.reshape(bp, MEM_LEN, D_MODEL),
            w, gf, last)
        outs_p.append((sr, sw, ss, _from_tile_order(k_tiles.reshape(bp, MEM_ROWS, LANES)),
                       _from_tile_order(v_tiles.reshape(bp, MEM_ROWS, LANES))))
        xs, sr2, sw2, ss2 = _layer(
            xs, pos_s, state_ret[l], state_wkv[l], state_shift[l],
            _tile_order(cache_mem_k[l]), _tile_order(cache_mem_v[l]), w, gf, last)
        outs_s.append((sr2, sw2, ss2))

    stack = lambda items, i: jnp.stack([it[i] for it in items])
    return (xp, xs, stack(outs_p, 0), stack(outs_p, 1), stack(outs_p, 2), stack(outs_p, 3), stack(outs_p, 4),
            stack(outs_s, 0), stack(outs_s, 1), stack(outs_s, 2))
```

```python
import functools
import math
from typing import NamedTuple

import jax
import jax.numpy as jnp
from jax import lax
from jax.experimental import pallas as pl
from jax.experimental.pallas import tpu as pltpu

F32 = jnp.float32
BF16 = jnp.bfloat16

D_MODEL = 1024
PAST_LEN = 16384
RET_HEADS = 4
RET_HD = 128
RET_W = RET_HEADS * RET_HD
RET_CHUNK = 128
RET_GN_EPS = 1e-5
ROPE_BASE = 10000.0
RWKV_HEADS = 8
RWKV_HD = 64
RWKV_W = RWKV_HEADS * RWKV_HD
DECAY_LORA = 64
AAA_LORA = 64
GATE_LORA = 128
RWKV_GN_EPS = 64e-5
RWKV_IN_W = 3 * RWKV_W + DECAY_LORA + AAA_LORA + GATE_LORA
GATE_W = 2 * D_MODEL
O_RET = 2 * D_MODEL
O_RWKV = O_RET + 4 * RET_W
IN_W = O_RWKV + RWKV_IN_W
MEM_LEN = 256
X_HEADS = 4
X_HD = D_MODEL // X_HEADS
D_FF = 4 * D_MODEL
RMS_EPS = 1e-6

V7X_VMEM_BYTES = 64 * 1024 * 1024
VMEM_CAP_BYTES = V7X_VMEM_BYTES - 8 * 1024 * 1024
ROW_TILE = 512


def _cparams(sem, vmem_bytes):
    return pltpu.CompilerParams(dimension_semantics=sem, vmem_limit_bytes=int(min(vmem_bytes, VMEM_CAP_BYTES)))


def _nbytes(shape, dtype=F32):
    return math.prod(shape) * jnp.dtype(dtype).itemsize


def _nn(a, b):
    return jnp.dot(a.astype(BF16), b.astype(BF16), preferred_element_type=F32)


def _nt(a, b):
    return lax.dot_general(a.astype(BF16), b.astype(BF16), (((1,), (1,)), ((), ())), preferred_element_type=F32)


def _tn(a, b):
    return lax.dot_general(a.astype(BF16), b.astype(BF16), (((0,), (0,)), ((), ())), preferred_element_type=F32)


def _rms(x, g):
    return x * lax.rsqrt(jnp.mean(x * x, axis=-1, keepdims=True) + RMS_EPS) * g


def _head_norm(o, eps):
    mu = jnp.mean(o, axis=-1, keepdims=True)
    d = o - mu
    var = jnp.mean(d * d, axis=-1, keepdims=True)
    return d * lax.rsqrt(var + eps)


def _full(shape):
    zeros = (0,) * len(shape)
    return pl.BlockSpec(shape, lambda *_: zeros, pipeline_mode=pl.Buffered(1))


def _vmem(pipelined, resident, temps):
    return 2 * pipelined + resident + temps


def _in_proj_kernel(x_ref, g_ref, w_ref, zg_ref, zr_ref, zw_ref):
    h = _rms(x_ref[...], g_ref[...]).astype(BF16)
    zg_ref[...] = jnp.dot(h, w_ref[:, 0:O_RET], preferred_element_type=F32)
    zr_ref[...] = jnp.dot(h, w_ref[:, O_RET:O_RWKV], preferred_element_type=F32)
    zw_ref[...] = jnp.dot(h, w_ref[:, O_RWKV:IN_W], preferred_element_type=F32)


def _in_proj(x, g, w_in):
    n = x.shape[0]
    tm = min(ROW_TILE, n)
    vmem = _vmem(_nbytes((tm, D_MODEL + IN_W)), _nbytes((D_MODEL, IN_W), BF16), _nbytes((tm, 4 * RET_W + D_MODEL)))
    return pl.pallas_call(
        _in_proj_kernel,
        grid=(n // tm,),
        in_specs=[pl.BlockSpec((tm, D_MODEL), lambda i: (i, 0)), _full((1, D_MODEL)), _full((D_MODEL, IN_W))],
        out_specs=[pl.BlockSpec((tm, GATE_W), lambda i: (i, 0)),
                   pl.BlockSpec((tm, 4 * RET_W), lambda i: (i, 0)),
                   pl.BlockSpec((tm, RWKV_IN_W), lambda i: (i, 0))],
        out_shape=[jax.ShapeDtypeStruct((n, GATE_W), F32), jax.ShapeDtypeStruct((n, 4 * RET_W), F32),
                   jax.ShapeDtypeStruct((n, RWKV_IN_W), F32)],
        compiler_params=_cparams(("parallel",), vmem),
        name="in_proj",
    )(x, g, w_in)


LANES = 128
MEM_ROWS = MEM_LEN * X_HEADS * (X_HD // LANES)
ROWS_PER_TOKEN = MEM_ROWS // MEM_LEN


def _tile_order(mem):
    nb = mem.shape[0]
    return (mem.reshape(nb, MEM_LEN, X_HEADS, X_HD // LANES, LANES).transpose(0, 1, 3, 2, 4)
            .reshape(nb, MEM_ROWS, LANES))


def _from_tile_order(raw):
    nb = raw.shape[0]
    return (raw.reshape(nb, MEM_LEN, X_HD // LANES, X_HEADS, LANES).transpose(0, 1, 3, 2, 4)
            .reshape(nb, MEM_LEN, X_HEADS, X_HD))


def _mem_kv_kernel(x_ref, g_ref, wk_ref, wv_ref, kt_ref, vt_ref, kb_ref, vb_ref, *, tm):
    h = _rms(x_ref[...], g_ref[...]).astype(BF16)
    for w_ref, t_ref, b_ref in ((wk_ref, kt_ref, kb_ref), (wv_ref, vt_ref, vb_ref)):
        y = jnp.dot(h, w_ref[...], preferred_element_type=F32)
        b_ref[...] = y.astype(BF16)
        for hd in range(X_HEADS):
            for c in range(X_HD // LANES):
                col = hd * X_HD + c * LANES
                t_ref[pl.ds(c * X_HEADS + hd, tm, stride=ROWS_PER_TOKEN), :] = y[:, col:col + LANES]


def _mem_kv(mem, g, wk, wv):
    n = mem.shape[0]
    tm = min(ROW_TILE, n)
    vmem = _vmem(4 * _nbytes((tm, D_MODEL)), 2 * _nbytes((D_MODEL, D_MODEL), BF16), 4 * _nbytes((tm, D_MODEL)))
    row = pl.BlockSpec((tm, D_MODEL), lambda i: (i, 0))
    tile = pl.BlockSpec((tm * ROWS_PER_TOKEN, LANES), lambda i: (i, 0))
    return pl.pallas_call(
        functools.partial(_mem_kv_kernel, tm=tm),
        grid=(n // tm,),
        in_specs=[row, _full((1, D_MODEL)), _full((D_MODEL, D_MODEL)), _full((D_MODEL, D_MODEL))],
        out_specs=[tile, tile, row, row],
        out_shape=[jax.ShapeDtypeStruct((n * ROWS_PER_TOKEN, LANES), F32)] * 2
        + [jax.ShapeDtypeStruct((n, D_MODEL), BF16)] * 2,
        compiler_params=_cparams(("parallel",), vmem),
        name="mem_kv",
    )(mem, g, wk, wv)


def _ret_kernel(zr_ref, s0_ref, cos_ref, sin_ref, dm_ref, qd_ref, kd_ref, cd_ref, o_ref, s_ref, *,
                bb, tt, chunk, group):
    @pl.when(pl.program_id(1) == 0)
    def _():
        s_ref[...] = s0_ref[...]

    nch = tt // chunk
    assert bb == 1 or nch == 1, "a block holds either one sequence or one chunk per sequence"
    assert (bb * nch) % group == 0
    heads = range(RET_HEADS)

    def body(j, carry):
        if nch == 1:
            items = [(j * group + n, 0) for n in range(group)]
        else:
            items = [(0, pl.multiple_of((j * group + n) * chunk, chunk)) for n in range(group)]
        q, k, v = [], [], []
        for b, r0 in items:
            rows = pl.ds(r0, chunk)
            cos = cos_ref[rows, :]
            sin = sin_ref[rows, :]
            for h in heads:
                qh = zr_ref[b, rows, h * RET_HD:(h + 1) * RET_HD]
                kh = zr_ref[b, rows, RET_W + h * RET_HD:RET_W + (h + 1) * RET_HD]
                q.append(qh * cos + pltpu.roll(qh, RET_HD // 2, axis=1) * sin)
                k.append((kh * cos + pltpu.roll(kh, RET_HD // 2, axis=1) * sin) * (RET_HD ** -0.5))
                v.append(zr_ref[b, rows, 2 * RET_W + h * RET_HD:2 * RET_W + (h + 1) * RET_HD])
        hd = [h for _ in items for h in heads]
        sc = [_nt(qi, ki) * dm_ref[h] for qi, ki, h in zip(q, k, hd)]
        kv = [_tn(ki * kd_ref[h], vi) for ki, vi, h in zip(k, v, hd)]
        inner = [_nn(si, vi) for si, vi in zip(sc, v)]
        states = []
        for n, (b, _) in enumerate(items):
            for h in heads:
                s = s_ref[b, h] if (nch == 1 or n == 0) else states[-RET_HEADS] * cd_ref[h] + kv[(n - 1) * RET_HEADS + h]
                states.append(s)
        last = len(items) - 1
        for n, (b, _) in enumerate(items):
            if nch == 1 or n == last:
                for h in heads:
                    i = n * RET_HEADS + h
                    s_ref[b, h] = states[i] * cd_ref[h] + kv[i]
        cross = [_nn(qi, si) * qd_ref[h] for qi, si, h in zip(q, states, hd)]
        for n, (b, r0) in enumerate(items):
            rows = pl.ds(r0, chunk)
            for h in heads:
                i = n * RET_HEADS + h
                g = zr_ref[b, rows, 3 * RET_W + h * RET_HD:3 * RET_W + (h + 1) * RET_HD]
                o_ref[b, rows, h * RET_HD:(h + 1) * RET_HD] = (_head_norm(inner[i] + cross[i], RET_GN_EPS)
                                                              * (g * jax.nn.sigmoid(g)))
        return carry

    lax.fori_loop(0, bb * nch // group, body, 0)


def _ret_consts(chunk):
    lg = jnp.log1p(-jnp.exp2(-5.0 - jnp.arange(RET_HEADS, dtype=F32)))
    idx = jnp.arange(chunk, dtype=F32)
    diff = idx[:, None] - idx[None, :]
    dmask = jnp.where(diff[None] >= 0, jnp.exp(jnp.maximum(diff, 0.0)[None] * lg[:, None, None]), 0.0)
    q_dec = jnp.exp((idx + 1.0)[None, :] * lg[:, None])
    k_dec = jnp.exp((chunk - 1.0 - idx)[None, :] * lg[:, None])
    c_dec = jnp.exp(chunk * lg)
    bc = lambda t: jnp.broadcast_to(t[:, :, None], (RET_HEADS, t.shape[1], RET_HD))
    return dmask, bc(q_dec), bc(k_dec), bc(c_dec[:, None])


def _rope_tables(pos):
    half = RET_HD // 2
    inv = 1.0 / (ROPE_BASE ** (jnp.arange(half, dtype=F32) / half))
    ang = pos.astype(F32)[:, None] * inv[None, :]
    cos, sin = jnp.cos(ang), jnp.sin(ang)
    return jnp.concatenate([cos, cos], axis=1), jnp.concatenate([-sin, sin], axis=1)


def _retention(zr, s0, pos, bb, tt, group):
    nb, t, _ = zr.shape
    chunk = math.gcd(t, RET_CHUNK)
    cos, sin = _rope_tables(pos)
    dmask, q_dec, k_dec, c_dec = _ret_consts(chunk)
    st_spec = pl.BlockSpec((bb, RET_HEADS, RET_HD, RET_HD), lambda b, j: (b, 0, 0, 0))
    tab_spec = pl.BlockSpec((tt, RET_HD), lambda b, j: (j, 0))
    vmem = _vmem(_nbytes((bb, tt, 5 * RET_W)) + 2 * _nbytes((bb, RET_HEADS, RET_HD, RET_HD)) + 2 * _nbytes((tt, RET_HD)),
                 _nbytes(dmask.shape) + 3 * _nbytes(q_dec.shape), group * 8 * _nbytes((chunk, 4 * RET_W)))
    return pl.pallas_call(
        functools.partial(_ret_kernel, bb=bb, tt=tt, chunk=chunk, group=group),
        grid=(nb // bb, t // tt),
        in_specs=[pl.BlockSpec((bb, tt, 4 * RET_W), lambda b, j: (b, j, 0)), st_spec, tab_spec, tab_spec,
                  _full(dmask.shape), _full(q_dec.shape), _full(k_dec.shape), _full(c_dec.shape)],
        out_specs=[pl.BlockSpec((bb, tt, RET_W), lambda b, j: (b, j, 0)), st_spec],
        out_shape=[jax.ShapeDtypeStruct((nb, t, RET_W), F32), jax.ShapeDtypeStruct(s0.shape, F32)],
        compiler_params=_cparams(("parallel", "arbitrary"), vmem),
        name="retention",
    )(zr, s0, cos, sin, dmask, q_dec, k_dec, c_dec)


WKV_BLOCK = 64
WKV_SCRATCH = 10


def _pow2(n):
    return n & (n - 1) == 0


def _imod(x, n):
    return jnp.bitwise_and(x, n - 1) if _pow2(n) else lax.rem(x, n)


def _idiv(x, n):
    return jnp.right_shift(x, n.bit_length() - 1) if _pow2(n) else lax.div(x, n)


def _wkv_kernel(zw_ref, sh0_ref, s0_ref, mu_ref, w0_ref, a0_ref, kk_ref, ka_ref, rk_ref, lng_ref, lnb_ref,
                wd_ref, wa_ref, wg_ref, tri_ref, seg_ref, o_ref, s_ref, sh_ref,
                at_s, rt_s, bt_s, kt_s, bh_s, kh_s, v_s, g_s, bonus_s, pend_s, *, bb, tt, sub, group):
    @pl.when(pl.program_id(1) == 0)
    def _():
        s_ref[...] = s0_ref[...]
        sh_ref[...] = sh0_ref[...]

    C = WKV_BLOCK
    R = bb * tt
    nblk = R // C
    nseq = C // sub
    assert (bb == 1 and sub == C) or (nblk == 1 and sub == tt), "one sequence per tile, or whole sequences in one block"
    assert nblk % group == 0
    n_sq = max(int(math.log2(sub)) - 1, 0)
    heads = range(RWKV_HEADS)
    hsl = [slice(h * RWKV_HD, (h + 1) * RWKV_HD) for h in heads]
    half = RWKV_W // 2

    def seg_sum(t):
        seg = seg_ref[...]
        return jnp.concatenate([_nn(t[:, :half], seg), _nn(t[:, half:], seg)], axis=1)

    pw = zw_ref[...].reshape(R, RWKV_IN_W)
    row = lax.broadcasted_iota(jnp.int32, (R, RWKV_IN_W), 0)
    rolled = pltpu.roll(pw, 1, axis=0)
    if bb == 1:
        prev = jnp.where(row == 0, sh_ref[0], rolled)
        sh_ref[0] = pw[R - 1:R, :]
    else:
        carry = jnp.broadcast_to(sh_ref[...], (bb, tt, RWKV_IN_W)).reshape(R, RWKV_IN_W)
        prev = jnp.where(_imod(row, tt) == 0, carry, rolled)
        sh_ref[...] = pw.reshape(bb, tt, RWKV_IN_W)[:, tt - 1:tt, :]
    xm = pw + mu_ref[...] * (prev - pw)
    r = xm[:, 0:RWKV_W]
    kb = xm[:, RWKV_W:2 * RWKV_W]
    o1 = 3 * RWKV_W
    wl = xm[:, o1:o1 + DECAY_LORA]
    al = xm[:, o1 + DECAY_LORA:o1 + DECAY_LORA + AAA_LORA]
    gl = xm[:, o1 + DECAY_LORA + AAA_LORA:RWKV_IN_W]
    logw = -math.exp(-0.5) * jax.nn.sigmoid(w0_ref[...] + _nn(jnp.tanh(wl), wd_ref[...]))
    a = jax.nn.sigmoid(a0_ref[...] + _nn(al, wa_ref[...]))
    l1 = logw.astype(BF16)
    rem = logw - l1.astype(F32)
    l2 = rem.astype(BF16)
    l3 = (rem - l2.astype(F32)).astype(BF16)
    tri = tri_ref[...]
    cum = jnp.concatenate(
        [sum(jnp.dot(tri, part[k * C:(k + 1) * C], preferred_element_type=F32) for part in (l1, l2, l3))
         for k in range(nblk)], axis=0)
    cum_end = jnp.broadcast_to(cum.reshape(R // sub, sub, RWKV_W)[:, sub - 1:sub, :],
                               (R // sub, sub, RWKV_W)).reshape(R, RWKV_W)
    kk = kb * kk_ref[...]
    kk = kk * jnp.minimum(lax.rsqrt(seg_sum(kk * kk)), 1e12)
    km = kb * (1.0 + (a - 1.0) * ka_ref[...])
    bv = kk * a
    p_end = jnp.exp(cum_end)
    e_neg = jnp.exp(-cum)
    e_end = p_end * e_neg
    at_s[...] = -kk * jnp.exp(cum - logw)
    rt_s[...] = r * jnp.exp(cum)
    bt_s[...] = bv * e_neg
    kt_s[...] = km * e_neg
    bh_s[...] = bv * e_end
    kh_s[...] = km * e_end
    v_s[...] = xm[:, 2 * RWKV_W:3 * RWKV_W]
    g_s[...] = _nn(jax.nn.sigmoid(gl), wg_ref[...])
    bonus_s[...] = seg_sum(r * km * rk_ref[...])
    pend_s[...] = p_end

    row_id = lax.broadcasted_iota(jnp.int32, (C, C), 0)
    col_id = lax.broadcasted_iota(jnp.int32, (C, C), 1)
    strict = row_id > col_id
    incl = row_id >= col_id
    if nseq > 1:
        same = _idiv(row_id, sub) == _idiv(col_id, sub)
        strict = jnp.logical_and(strict, same)
        incl = jnp.logical_and(incl, same)
    eye = jnp.where(row_id == col_id, 1.0, 0.0).astype(F32)

    def triangular(r0s):
        rows = [pl.ds(r0, C) for r0 in r0s for _ in heads]
        sl = [s for _ in r0s for s in hsl]
        at = [at_s[rw, :][:, s] for rw, s in zip(rows, sl)]
        rt = [rt_s[rw, :][:, s] for rw, s in zip(rows, sl)]
        v = [v_s[rw, :][:, s] for rw, s in zip(rows, sl)]
        m1 = [_nt(jnp.concatenate([a_, r_], axis=0), jnp.concatenate([bt_s[rw, :][:, s], kt_s[rw, :][:, s]], axis=0))
              for a_, r_, rw, s in zip(at, rt, rows, sl)]
        low = [jnp.where(strict, m[:C, :C], 0.0) for m in m1]
        rab = [jnp.where(incl, m[C:, :C], 0.0) for m in m1]
        akk = [jnp.concatenate([jnp.where(strict, m[:C, C:], 0.0), jnp.where(incl, m[C:, C:], 0.0)], axis=0)
               for m in m1]
        akv = [_nn(k_, v_) for k_, v_ in zip(akk, v)]
        tinv = [eye + l for l in low]
        if n_sq:
            x = [_nn(l, l) for l in low]
        for j in range(n_sq):
            if j < n_sq - 1:
                xt = [_nn(jnp.concatenate([xi, ti], axis=0), xi) for xi, ti in zip(x, tinv)]
                x = [t[:C] for t in xt]
                tinv = [ti + t[C:] for ti, t in zip(tinv, xt)]
            else:
                tinv = [ti + _nn(ti, xi) for ti, xi in zip(tinv, x)]
        ua = [_nn(t, a_) for t, a_ in zip(tinv, at)]
        uv = [_nn(t, k_[:C]) for t, k_ in zip(tinv, akv)]
        oa = [r_ + _nn(rb, u_) for r_, rb, u_ in zip(rt, rab, ua)]
        ov = [k_[C:] + _nn(rb, u_) for k_, rb, u_ in zip(akv, rab, uv)]
        nh = RWKV_HEADS
        return [dict(ua=ua[k * nh:(k + 1) * nh], uv=uv[k * nh:(k + 1) * nh], oa=oa[k * nh:(k + 1) * nh],
                     ov=ov[k * nh:(k + 1) * nh], v=v[k * nh:(k + 1) * nh]) for k in range(len(r0s))]

    def advance(r0, q):
        rw = pl.ds(r0, C)
        bh, kh, pend = bh_s[rw, :], kh_s[rw, :], pend_s[rw, :]
        items = [(n, h) for n in range(nseq) for h in heads]
        rs = [slice(n * sub, (n + 1) * sub) for n, _ in items]
        uo = [_nt(jnp.concatenate([q["ua"][h][s], q["oa"][h][s]], axis=0), s_ref[n, h])
              for (n, h), s in zip(items, rs)]
        u = [m[:sub] + q["uv"][h][s] for m, (n, h), s in zip(uo, items, rs)]
        upd = [_tn(jnp.concatenate([u_, q["v"][h][s]], axis=0),
                   jnp.concatenate([bh[s, hsl[h]], kh[s, hsl[h]]], axis=0)) for u_, (n, h), s in zip(u, items, rs)]
        for up, (n, h), s in zip(upd, items, rs):
            s_ref[n, h] = s_ref[n, h] * pend[s, hsl[h]][0:1] + up
        o = [m[sub:] + q["ov"][h][s] for m, (n, h), s in zip(uo, items, rs)]
        o = jnp.concatenate([jnp.concatenate([o[n * RWKV_HEADS + h] for n in range(nseq)], axis=0) for h in heads],
                            axis=1)
        d = o - seg_sum(o) * (1.0 / RWKV_HD)
        var = seg_sum(d * d) * (1.0 / RWKV_HD)
        o = d * lax.rsqrt(var + RWKV_GN_EPS) * lng_ref[...] + lnb_ref[...]
        o = (o + bonus_s[rw, :] * v_s[rw, :]) * g_s[rw, :]
        if bb == 1:
            o_ref[0, rw, :] = o
        else:
            o_ref[...] = o.reshape(bb, tt, RWKV_W)

    def body(j, carry):
        r0s = [j * group * C + k * C for k in range(group)]
        r0s = [r if isinstance(r, int) else pl.multiple_of(r, C) for r in r0s]
        for r0, q in zip(r0s, triangular(r0s)):
            advance(r0, q)
        return carry

    if nblk == group:
        body(0, 0)
    else:
        lax.fori_loop(0, nblk // group, body, 0)


def _rwkv7(zw, sh0, s0, p, bb, tt, group):
    nb, t, _ = zw.shape
    sub = min(t, WKV_BLOCK)
    step = jnp.arange(WKV_BLOCK)
    tri = ((step[:, None] >= step[None, :]) & (step[:, None] // sub == step[None, :] // sub)).astype(BF16)
    lane_head = jnp.arange(RWKV_W // 2) // RWKV_HD
    seg = (lane_head[:, None] == lane_head[None, :]).astype(BF16)
    rows = bb * tt
    st_spec = pl.BlockSpec((bb, RWKV_HEADS, RWKV_HD, RWKV_HD), lambda b, j: (b, 0, 0, 0))
    sh_spec = pl.BlockSpec((bb, 1, RWKV_IN_W), lambda b, j: (b, 0, 0))
    vec = _full((1, RWKV_W))
    vmem = _vmem(_nbytes((rows, RWKV_IN_W + RWKV_W)) + 2 * _nbytes((bb, RWKV_HEADS, RWKV_HD, RWKV_HD))
                 + 2 * _nbytes((bb, 8, RWKV_IN_W)),
                 9 * _nbytes((8, RWKV_IN_W)) + _nbytes((DECAY_LORA + AAA_LORA + GATE_LORA + WKV_BLOCK, RWKV_W), BF16)
                 + _nbytes(seg.shape, BF16) + WKV_SCRATCH * _nbytes((rows, RWKV_W)),
                 4 * _nbytes((rows, RWKV_IN_W)) + 8 * _nbytes((rows, RWKV_W))
                 + group * 32 * _nbytes((WKV_BLOCK, RWKV_IN_W)))
    return pl.pallas_call(
        functools.partial(_wkv_kernel, bb=bb, tt=tt, sub=sub, group=group),
        grid=(nb // bb, t // tt),
        in_specs=[pl.BlockSpec((bb, tt, RWKV_IN_W), lambda b, j: (b, j, 0)), sh_spec, st_spec,
                  _full((1, RWKV_IN_W)), vec, vec, vec, vec, vec, vec, vec,
                  _full((DECAY_LORA, RWKV_W)), _full((AAA_LORA, RWKV_W)), _full((GATE_LORA, RWKV_W)),
                  _full(tri.shape), _full(seg.shape)],
        out_specs=[pl.BlockSpec((bb, tt, RWKV_W), lambda b, j: (b, j, 0)), st_spec, sh_spec],
        out_shape=[jax.ShapeDtypeStruct((nb, t, RWKV_W), F32), jax.ShapeDtypeStruct(s0.shape, F32),
                   jax.ShapeDtypeStruct(sh0.shape, F32)],
        scratch_shapes=[pltpu.VMEM((rows, RWKV_W), F32)] * WKV_SCRATCH,
        compiler_params=_cparams(("parallel", "arbitrary"), vmem),
        name="rwkv7",
    )(zw, sh0, s0, p["mu"], p["w0"], p["a0"], p["k_k"], p["k_a"], p["r_k"], p["lnx_g"], p["lnx_b"],
      p["w_decay_up"], p["w_a_up"], p["w_g_up"], tri, seg)


def _merge_rows(x, zg, oa, ob, wa_ref, wb_ref, wo_ref):
    ga = jax.nn.sigmoid(zg[:, 0:D_MODEL])
    gb = jax.nn.sigmoid(zg[:, D_MODEL:GATE_W])
    merged = ga * _nn(oa, wa_ref[...]) + gb * _nn(ob, wb_ref[...])
    return x + _nn(merged, wo_ref[...])


def _merge_kernel(x_ref, zg_ref, oa_ref, ob_ref, wa_ref, wb_ref, wo_ref, o_ref):
    o_ref[...] = _merge_rows(x_ref[...], zg_ref[...], oa_ref[...], ob_ref[...], wa_ref, wb_ref, wo_ref)


def _merge(x, zg, o_ret, o_rwkv, wa, wb, wo):
    n = x.shape[0]
    tm = min(ROW_TILE, n)
    row = lambda w: pl.BlockSpec((tm, w), lambda i: (i, 0))
    vmem = _vmem(_nbytes((tm, 2 * D_MODEL + GATE_W + RET_W + RWKV_W)), _nbytes((2 * D_MODEL, D_MODEL), BF16),
                 6 * _nbytes((tm, D_MODEL)))
    return pl.pallas_call(
        _merge_kernel,
        grid=(n // tm,),
        in_specs=[row(D_MODEL), row(GATE_W), row(RET_W), row(RWKV_W),
                  _full((RET_W, D_MODEL)), _full((RWKV_W, D_MODEL)), _full((D_MODEL, D_MODEL))],
        out_specs=row(D_MODEL),
        out_shape=jax.ShapeDtypeStruct((n, D_MODEL), F32),
        compiler_params=_cparams(("parallel",), vmem),
        name="merge",
    )(x, zg, o_ret, o_rwkv, wa, wb, wo)


def _softmax(sc):
    e = jnp.exp(sc - jnp.max(sc, axis=-1, keepdims=True))
    return e / jnp.sum(e, axis=-1, keepdims=True)


def _attend_heads(q_scr, mk_ref, mv_ref, ox_scr, b, r0, tq):
    for h in range(X_HEADS):
        sl = slice(h * X_HD, (h + 1) * X_HD)
        att = _softmax(_nt(q_scr[pl.ds(r0, tq), sl], mk_ref[b, :, sl]) * (X_HD ** -0.5))
        ox_scr[pl.ds(r0, tq), sl] = _nn(att, mv_ref[b, :, sl])


def _attend_tiles(q_scr, mk_ref, mv_ref, ox_scr, seqs, tq):
    halves = X_HD // LANES
    n = X_HEADS * tq
    slot = lax.broadcasted_iota(jnp.int32, (n, MEM_ROWS), 1) % ROWS_PER_TOKEN
    head = lax.broadcasted_iota(jnp.int32, (n, MEM_ROWS), 0) // tq
    own = slot == head
    other = slot == head + X_HEADS
    qs = []
    for b in seqs:
        q = q_scr[b * tq:(b + 1) * tq, :]
        qs.append(jnp.concatenate([q[:, h * X_HD + c * LANES:h * X_HD + (c + 1) * LANES]
                                   for c in range(halves) for h in range(X_HEADS)], axis=0))
    z = [_nt(q, mk_ref[b]) for q, b in zip(qs, seqs)]
    part = [jnp.where(own, zi[:n], 0.0) + jnp.where(other, zi[n:], 0.0) for zi in z]
    sc = [p + pltpu.roll(p, MEM_ROWS - X_HEADS, axis=1) for p in part]
    att = [_softmax(jnp.where(own, s * (X_HD ** -0.5), -jnp.inf)) for s in sc]
    att2 = [jnp.concatenate([a, pltpu.roll(a, X_HEADS, axis=1)], axis=0) for a in att]
    o = [_nn(a, mv_ref[b]) for a, b in zip(att2, seqs)]
    for oi, b in zip(o, seqs):
        ox_scr[b * tq:(b + 1) * tq, :] = jnp.concatenate(
            [oi[(c * X_HEADS + h) * tq:(c * X_HEADS + h + 1) * tq] for h in range(X_HEADS) for c in range(halves)],
            axis=1)


def _cross_kernel(x_ref, mk_ref, mv_ref, g_ref, wq_ref, wo_ref, o_ref, q_scr, ox_scr, *, bb, tq):
    rows = bb * tq
    x = x_ref[...].reshape(rows, D_MODEL)
    q_scr[...] = _nn(_rms(x, g_ref[...]), wq_ref[...])
    _attend_tiles(q_scr, mk_ref, mv_ref, ox_scr, range(bb), tq)
    o_ref[...] = (x + _nn(ox_scr[...], wo_ref[...])).reshape(bb, tq, D_MODEL)


def _cross(x, mem_k, mem_v, g, wq, wo, bb, tq):
    nb, t, _ = x.shape
    x_spec = pl.BlockSpec((bb, tq, D_MODEL), lambda b, j: (b, j, 0))
    m_spec = pl.BlockSpec((bb, MEM_ROWS, LANES), lambda b, j: (b, 0, 0))
    vmem = _vmem(2 * _nbytes((bb, tq, D_MODEL)) + 2 * _nbytes((bb, MEM_ROWS, LANES)),
                 2 * _nbytes((D_MODEL, D_MODEL), BF16) + 2 * _nbytes((bb * tq, D_MODEL)),
                 6 * _nbytes((bb * tq, D_MODEL)) + bb * 8 * _nbytes((2 * X_HEADS * tq, MEM_ROWS)))
    return pl.pallas_call(
        functools.partial(_cross_kernel, bb=bb, tq=tq),
        grid=(nb // bb, t // tq),
        in_specs=[x_spec, m_spec, m_spec, _full((1, D_MODEL)), _full((D_MODEL, D_MODEL)), _full((D_MODEL, D_MODEL))],
        out_specs=x_spec,
        out_shape=jax.ShapeDtypeStruct(x.shape, F32),
        scratch_shapes=[pltpu.VMEM((bb * tq, D_MODEL), F32), pltpu.VMEM((bb * tq, D_MODEL), F32)],
        compiler_params=_cparams(("parallel", "arbitrary"), vmem),
        name="cross_attn",
    )(x, mem_k, mem_v, g, wq, wo)


MLP_FF_CHUNK = 1024


def _mlp_rows(x, g_ref, wu_ref, wd_ref, gf_ref, final_norm):
    h = _rms(x, g_ref[...]).astype(BF16)
    acc = x
    for c in range(0, D_FF, MLP_FF_CHUNK):
        u = jnp.maximum(jnp.dot(h, wu_ref[:, c:c + MLP_FF_CHUNK], preferred_element_type=F32), 0.0)
        acc = acc + _nn(u * u, wd_ref[c:c + MLP_FF_CHUNK, :])
    return _rms(acc, gf_ref[...]) if final_norm else acc


def _mlp_kernel(x_ref, g_ref, wu_ref, wd_ref, gf_ref, o_ref, *, final_norm):
    o_ref[...] = _mlp_rows(x_ref[...], g_ref, wu_ref, wd_ref, gf_ref, final_norm)


def _mlp(x, g, w_up, w_down, g_final, final_norm):
    n = x.shape[0]
    tm = min(ROW_TILE, n)
    row = pl.BlockSpec((tm, D_MODEL), lambda i: (i, 0))
    vmem = _vmem(2 * _nbytes((tm, D_MODEL)), 2 * _nbytes((D_MODEL, D_FF), BF16),
                 4 * _nbytes((tm, MLP_FF_CHUNK)) + 4 * _nbytes((tm, D_MODEL)))
    return pl.pallas_call(
        functools.partial(_mlp_kernel, final_norm=final_norm),
        grid=(n // tm,),
        in_specs=[row, _full((1, D_MODEL)), _full((D_MODEL, D_FF)), _full((D_FF, D_MODEL)), _full((1, D_MODEL))],
        out_specs=row,
        out_shape=jax.ShapeDtypeStruct((n, D_MODEL), F32),
        compiler_params=_cparams(("parallel",), vmem),
        name="mlp",
    )(x, g, w_up, w_down, g_final)


POST_TILE = 512


def _post_kernel(x_ref, zg_ref, oa_ref, ob_ref, mk_ref, mv_ref, wa_ref, wb_ref, wo_ref, gc_ref, wq_ref, wco_ref,
                 gm_ref, wu_ref, wd_ref, gf_ref, o_ref, q_scr, ox_scr, *, tq, final_norm):
    x1 = _merge_rows(x_ref[0], zg_ref[0], oa_ref[0], ob_ref[0], wa_ref, wb_ref, wo_ref)
    q_scr[...] = _nn(_rms(x1, gc_ref[...]), wq_ref[...])
    _attend_heads(q_scr, mk_ref, mv_ref, ox_scr, 0, 0, tq)
    x2 = x1 + _nn(ox_scr[...], wco_ref[...])
    o_ref[0] = _mlp_rows(x2, gm_ref, wu_ref, wd_ref, gf_ref, final_norm)


def _post(x, zg, o_ret, o_rwkv, mem_k, mem_v, w, g_final, final_norm):
    nb, t, _ = x.shape
    tq = min(POST_TILE, t)
    row = lambda width: pl.BlockSpec((1, tq, width), lambda b, j: (b, j, 0))
    mem = pl.BlockSpec((1, MEM_LEN, D_MODEL), lambda b, j: (b, 0, 0))
    sq = _full((D_MODEL, D_MODEL))
    vec = _full((1, D_MODEL))
    vmem = _vmem(_nbytes((tq, 2 * D_MODEL + GATE_W + RET_W + RWKV_W)) + 2 * _nbytes((MEM_LEN, D_MODEL), BF16),
                 _nbytes((4 * D_MODEL + 2 * D_FF, D_MODEL), BF16) + 2 * _nbytes((tq, D_MODEL)),
                 8 * _nbytes((tq, D_MODEL)) + 4 * _nbytes((tq, MLP_FF_CHUNK)))
    return pl.pallas_call(
        functools.partial(_post_kernel, tq=tq, final_norm=final_norm),
        grid=(nb, t // tq),
        in_specs=[row(D_MODEL), row(GATE_W), row(RET_W), row(RWKV_W), mem, mem,
                  _full((RET_W, D_MODEL)), _full((RWKV_W, D_MODEL)), sq, vec, sq, sq,
                  vec, _full((D_MODEL, D_FF)), _full((D_FF, D_MODEL)), vec],
        out_specs=row(D_MODEL),
        out_shape=jax.ShapeDtypeStruct(x.shape, F32),
        scratch_shapes=[pltpu.VMEM((tq, D_MODEL), F32), pltpu.VMEM((tq, D_MODEL), F32)],
        compiler_params=_cparams(("parallel", "arbitrary"), vmem),
        name="post",
    )(x, zg, o_ret, o_rwkv, mem_k, mem_v, w["w_branch_a"], w["w_branch_b"], w["w_out"], w["g_cross"], w["w_cq"],
      w["w_co"], w["g_mlp"], w["w_up"], w["w_down"], g_final)


class _Tiling(NamedTuple):
    seq_block: int
    time_block: int
    cross_block: int
    ret_group: int
    wkv_group: int


def _tiling(nb, t):
    if t > RET_CHUNK:
        tt = min(ROW_TILE, t)
        return _Tiling(seq_block=1, time_block=tt, cross_block=1,
                       ret_group=math.gcd(4, tt // RET_CHUNK), wkv_group=math.gcd(8, tt // WKV_BLOCK))
    seqs = WKV_BLOCK // t
    return _Tiling(seq_block=seqs, time_block=t, cross_block=math.gcd(4, nb), ret_group=seqs, wkv_group=1)


def _layer(x, pos, s_ret, s_wkv, s_shift, mem_k, mem_v, w, g_final, final_norm):
    nb, t, _ = x.shape
    n = nb * t
    tl = _tiling(nb, t)
    xf = x.reshape(n, D_MODEL)
    zg, zr, zw = _in_proj(xf, w["g_mix"], w["w_in"])
    o_ret, s_ret_new = _retention(zr.reshape(nb, t, 4 * RET_W), s_ret, pos, tl.seq_block, tl.time_block, tl.ret_group)
    o_wkv, s_wkv_new, shift_new = _rwkv7(zw.reshape(nb, t, RWKV_IN_W), s_shift.reshape(nb, 1, RWKV_IN_W), s_wkv,
                                         w, tl.seq_block, tl.time_block, tl.wkv_group)
    if mem_k.shape[1:] == (MEM_LEN, D_MODEL):
        y = _post(x, zg.reshape(nb, t, GATE_W), o_ret, o_wkv, mem_k, mem_v, w, g_final, final_norm)
    else:
        x1 = _merge(xf, zg, o_ret.reshape(n, RET_W), o_wkv.reshape(n, RWKV_W),
                    w["w_branch_a"], w["w_branch_b"], w["w_out"])
        x2 = _cross(x1.reshape(nb, t, D_MODEL), mem_k, mem_v, w["g_cross"], w["w_cq"], w["w_co"],
                    tl.cross_block, tl.time_block)
        y = _mlp(x2.reshape(n, D_MODEL), w["g_mlp"], w["w_up"], w["w_down"], g_final, final_norm).reshape(x.shape)
    return y, s_ret_new, s_wkv_new, shift_new.reshape(nb, RWKV_IN_W)


def _layer_weights(l, g_mix, w_in, w_branch_a, w_branch_b, w_out, mu_shift, w0, w_decay_up, a0, w_a_up, w_g_up,
                   k_k, k_a, r_k, lnx_g, lnx_b, g_cross, w_cq, w_co, g_mlp, w_up, w_down):
    vec = lambda v: v[l].reshape(1, -1).astype(F32)
    mat = lambda m: m[l].astype(BF16)
    return dict(g_mix=vec(g_mix), w_in=mat(w_in), w_branch_a=mat(w_branch_a), w_branch_b=mat(w_branch_b),
                w_out=mat(w_out), mu=vec(mu_shift), w0=vec(w0), w_decay_up=mat(w_decay_up), a0=vec(a0),
                w_a_up=mat(w_a_up), w_g_up=mat(w_g_up), k_k=vec(k_k), k_a=vec(k_a), r_k=vec(r_k), lnx_g=vec(lnx_g),
                lnx_b=vec(lnx_b), g_cross=vec(g_cross), w_cq=mat(w_cq), w_co=mat(w_co), g_mlp=vec(g_mlp),
                w_up=mat(w_up), w_down=mat(w_down))


def kernel(x_prompt, x_sample, mem_prompt, state_ret, state_wkv, state_shift, cache_mem_k, cache_mem_v, g_mix, w_in, w_branch_a, w_branch_b, w_out, mu_shift, w0, w_decay_up, a0, w_a_up, w_g_up, k_k, k_a, r_k, lnx_g, lnx_b, g_cross, g_mem, w_cq, w_ck, w_cv, w_co, g_mlp, w_up, w_down, g_final):
    depth = w_in.shape[0]
    bp, tp, _ = x_prompt.shape
    bs, ts, _ = x_sample.shape
    pos_p = jnp.arange(tp, dtype=jnp.int32)
    pos_s = PAST_LEN + jnp.arange(ts, dtype=jnp.int32)
    gf = g_final.reshape(1, D_MODEL)

    xp, xs = x_prompt, x_sample
    outs_p, outs_s = [], []
    for l in range(depth):
        w = _layer_weights(l, g_mix, w_in, w_branch_a, w_branch_b, w_out, mu_shift, w0, w_decay_up, a0, w_a_up,
                           w_g_up, k_k, k_a, r_k, lnx_g, lnx_b, g_cross, w_cq, w_co, g_mlp, w_up, w_down)
        last = l == depth - 1
        k_tiles, v_tiles, k_att, v_att = _mem_kv(mem_prompt.reshape(bp * MEM_LEN, D_MODEL),
                                                 g_mem[l].reshape(1, D_MODEL), w_ck[l].astype(BF16), w_cv[l].astype(BF16))
        xp, sr, sw, ss = _layer(
            xp, pos_p,
            jnp.zeros((bp, RET_HEADS, RET_HD, RET_HD), F32), jnp.zeros((bp, RWKV_HEADS, RWKV_HD, RWKV_HD), F32),
            jnp.zeros((bp, RWKV_IN_W), F32), k_att.reshape(bp, MEM_LEN, D_MODEL), v_att.reshape(bp, MEM_LEN, D_MODEL),
            w, gf, last)
        outs_p.append((sr, sw, ss, _from_tile_order(k_tiles.reshape(bp, MEM_ROWS, LANES)),
                       _from_tile_order(v_tiles.reshape(bp, MEM_ROWS, LANES))))
        xs, sr2, sw2, ss2 = _layer(
            xs, pos_s, state_ret[l], state_wkv[l], state_shift[l],
            _tile_order(cache_mem_k[l]), _tile_order(cache_mem_v[l]), w, gf, last)
        outs_s.append((sr2, sw2, ss2))

    stack = lambda items, i: jnp.stack([it[i] for it in items])
    return (xp, xs, stack(outs_p, 0), stack(outs_p, 1), stack(outs_p, 2), stack(outs_p, 3), stack(outs_p, 4),
            stack(outs_s, 0), stack(outs_s, 1), stack(outs_s, 2))
```

```python
import functools
import math
from typing import NamedTuple

import jax
import jax.numpy as jnp
from jax import lax
from jax.experimental import pallas as pl
from jax.experimental.pallas import tpu as pltpu

F32 = jnp.float32
BF16 = jnp.bfloat16

D_MODEL = 1024
PAST_LEN = 16384
RET_HEADS = 4
RET_HD = 128
RET_W = RET_HEADS * RET_HD
RET_CHUNK = 128
RET_GN_EPS = 1e-5
ROPE_BASE = 10000.0
RWKV_HEADS = 8
RWKV_HD = 64
RWKV_W = RWKV_HEADS * RWKV_HD
DECAY_LORA = 64
AAA_LORA = 64
GATE_LORA = 128
RWKV_GN_EPS = 64e-5
RWKV_IN_W = 3 * RWKV_W + DECAY_LORA + AAA_LORA + GATE_LORA
GATE_W = 2 * D_MODEL
O_RET = 2 * D_MODEL
O_RWKV = O_RET + 4 * RET_W
IN_W = O_RWKV + RWKV_IN_W
MEM_LEN = 256
X_HEADS = 4
X_HD = D_MODEL // X_HEADS
D_FF = 4 * D_MODEL
RMS_EPS = 1e-6

V7X_VMEM_BYTES = 64 * 1024 * 1024
VMEM_CAP_BYTES = V7X_VMEM_BYTES - 8 * 1024 * 1024
ROW_TILE = 512


def _cparams(sem, vmem_bytes):
    return pltpu.CompilerParams(dimension_semantics=sem, vmem_limit_bytes=int(min(vmem_bytes, VMEM_CAP_BYTES)))


def _nbytes(shape, dtype=F32):
    return math.prod(shape) * jnp.dtype(dtype).itemsize


def _nn(a, b):
    return jnp.dot(a.astype(BF16), b.astype(BF16), preferred_element_type=F32)


def _nt(a, b):
    return lax.dot_general(a.astype(BF16), b.astype(BF16), (((1,), (1,)), ((), ())), preferred_element_type=F32)


def _tn(a, b):
    return lax.dot_general(a.astype(BF16), b.astype(BF16), (((0,), (0,)), ((), ())), preferred_element_type=F32)


def _rms(x, g):
    return x * lax.rsqrt(jnp.mean(x * x, axis=-1, keepdims=True) + RMS_EPS) * g


def _head_norms(blocks, eps):
    rows, width = blocks[0].shape
    ones = jnp.ones((width, width), BF16)
    o = jnp.concatenate(blocks, axis=0)
    d = o - _nn(o, ones) * (1.0 / width)
    out = d * lax.rsqrt(_nn(d * d, ones) * (1.0 / width) + eps)
    return [out[i * rows:(i + 1) * rows] for i in range(len(blocks))]


def _full(shape):
    zeros = (0,) * len(shape)
    return pl.BlockSpec(shape, lambda *_: zeros, pipeline_mode=pl.Buffered(1))


def _vmem(pipelined, resident, temps):
    return 2 * pipelined + resident + temps


def _in_proj_kernel(x_ref, g_ref, w_ref, zg_ref, zr_ref, zw_ref):
    h = _rms(x_ref[...], g_ref[...]).astype(BF16)
    zg_ref[...] = jnp.dot(h, w_ref[:, 0:O_RET], preferred_element_type=F32)
    zr_ref[...] = jnp.dot(h, w_ref[:, O_RET:O_RWKV], preferred_element_type=F32)
    zw_ref[...] = jnp.dot(h, w_ref[:, O_RWKV:IN_W], preferred_element_type=F32)


def _in_proj(x, g, w_in):
    n = x.shape[0]
    tm = min(ROW_TILE, n)
    vmem = _vmem(_nbytes((tm, D_MODEL + IN_W)), _nbytes((D_MODEL, IN_W), BF16), _nbytes((tm, 4 * RET_W + D_MODEL)))
    return pl.pallas_call(
        _in_proj_kernel,
        grid=(n // tm,),
        in_specs=[pl.BlockSpec((tm, D_MODEL), lambda i: (i, 0)), _full((1, D_MODEL)), _full((D_MODEL, IN_W))],
        out_specs=[pl.BlockSpec((tm, GATE_W), lambda i: (i, 0)),
                   pl.BlockSpec((tm, 4 * RET_W), lambda i: (i, 0)),
                   pl.BlockSpec((tm, RWKV_IN_W), lambda i: (i, 0))],
        out_shape=[jax.ShapeDtypeStruct((n, GATE_W), F32), jax.ShapeDtypeStruct((n, 4 * RET_W), F32),
                   jax.ShapeDtypeStruct((n, RWKV_IN_W), F32)],
        compiler_params=_cparams(("parallel",), vmem),
        name="in_proj",
    )(x, g, w_in)


LANES = 128
MEM_ROWS = MEM_LEN * X_HEADS * (X_HD // LANES)
ROWS_PER_TOKEN = MEM_ROWS // MEM_LEN


def _tile_order(mem):
    nb = mem.shape[0]
    return (mem.reshape(nb, MEM_LEN, X_HEADS, X_HD // LANES, LANES).transpose(0, 1, 3, 2, 4)
            .reshape(nb, MEM_ROWS, LANES))


def _from_tile_order(raw):
    nb = raw.shape[0]
    return (raw.reshape(nb, MEM_LEN, X_HD // LANES, X_HEADS, LANES).transpose(0, 1, 3, 2, 4)
            .reshape(nb, MEM_LEN, X_HEADS, X_HD))


def _mem_kv_kernel(x_ref, g_ref, wk_ref, wv_ref, kt_ref, vt_ref, kb_ref, vb_ref, *, tm):
    h = _rms(x_ref[...], g_ref[...]).astype(BF16)
    for w_ref, t_ref, b_ref in ((wk_ref, kt_ref, kb_ref), (wv_ref, vt_ref, vb_ref)):
        y = jnp.dot(h, w_ref[...], preferred_element_type=F32)
        b_ref[...] = y.astype(BF16)
        for hd in range(X_HEADS):
            for c in range(X_HD // LANES):
                col = hd * X_HD + c * LANES
                t_ref[pl.ds(c * X_HEADS + hd, tm, stride=ROWS_PER_TOKEN), :] = y[:, col:col + LANES]


def _mem_kv(mem, g, wk, wv):
    n = mem.shape[0]
    tm = min(ROW_TILE, n)
    vmem = _vmem(4 * _nbytes((tm, D_MODEL)), 2 * _nbytes((D_MODEL, D_MODEL), BF16), 4 * _nbytes((tm, D_MODEL)))
    row = pl.BlockSpec((tm, D_MODEL), lambda i: (i, 0))
    tile = pl.BlockSpec((tm * ROWS_PER_TOKEN, LANES), lambda i: (i, 0))
    return pl.pallas_call(
        functools.partial(_mem_kv_kernel, tm=tm),
        grid=(n // tm,),
        in_specs=[row, _full((1, D_MODEL)), _full((D_MODEL, D_MODEL)), _full((D_MODEL, D_MODEL))],
        out_specs=[tile, tile, row, row],
        out_shape=[jax.ShapeDtypeStruct((n * ROWS_PER_TOKEN, LANES), F32)] * 2
        + [jax.ShapeDtypeStruct((n, D_MODEL), BF16)] * 2,
        compiler_params=_cparams(("parallel",), vmem),
        name="mem_kv",
    )(mem, g, wk, wv)


def _ret_kernel(zr_ref, s0_ref, cos_ref, sin_ref, dm_ref, qd_ref, kd_ref, cd_ref, o_ref, s_ref, *,
                bb, tt, chunk, group):
    @pl.when(pl.program_id(1) == 0)
    def _():
        s_ref[...] = s0_ref[...]

    nch = tt // chunk
    assert bb == 1 or nch == 1, "a block holds either one sequence or one chunk per sequence"
    assert (bb * nch) % group == 0
    heads = range(RET_HEADS)

    def body(j, carry):
        if nch == 1:
            items = [(j * group + n, 0) for n in range(group)]
        else:
            items = [(0, pl.multiple_of((j * group + n) * chunk, chunk)) for n in range(group)]
        q, k, v = [], [], []
        for b, r0 in items:
            rows = pl.ds(r0, chunk)
            cos = cos_ref[rows, :]
            sin = sin_ref[rows, :]
            for h in heads:
                qh = zr_ref[b, rows, h * RET_HD:(h + 1) * RET_HD]
                kh = zr_ref[b, rows, RET_W + h * RET_HD:RET_W + (h + 1) * RET_HD]
                q.append(qh * cos + pltpu.roll(qh, RET_HD // 2, axis=1) * sin)
                k.append((kh * cos + pltpu.roll(kh, RET_HD // 2, axis=1) * sin) * (RET_HD ** -0.5))
                v.append(zr_ref[b, rows, 2 * RET_W + h * RET_HD:2 * RET_W + (h + 1) * RET_HD])
        hd = [h for _ in items for h in heads]
        sc = [_nt(qi, ki) * dm_ref[h] for qi, ki, h in zip(q, k, hd)]
        kv = [_tn(ki * kd_ref[h], vi) for ki, vi, h in zip(k, v, hd)]
        inner = [_nn(si, vi) for si, vi in zip(sc, v)]
        states = []
        for n, (b, _) in enumerate(items):
            for h in heads:
                s = s_ref[b, h] if (nch == 1 or n == 0) else states[-RET_HEADS] * cd_ref[h] + kv[(n - 1) * RET_HEADS + h]
                states.append(s)
        last = len(items) - 1
        for n, (b, _) in enumerate(items):
            if nch == 1 or n == last:
                for h in heads:
                    i = n * RET_HEADS + h
                    s_ref[b, h] = states[i] * cd_ref[h] + kv[i]
        cross = [_nn(qi, si) * qd_ref[h] for qi, si, h in zip(q, states, hd)]
        normed = _head_norms([a + c for a, c in zip(inner, cross)], RET_GN_EPS)
        for n, (b, r0) in enumerate(items):
            rows = pl.ds(r0, chunk)
            for h in heads:
                g = zr_ref[b, rows, 3 * RET_W + h * RET_HD:3 * RET_W + (h + 1) * RET_HD]
                o_ref[b, rows, h * RET_HD:(h + 1) * RET_HD] = normed[n * RET_HEADS + h] * (g * jax.nn.sigmoid(g))
        return carry

    lax.fori_loop(0, bb * nch // group, body, 0)


def _ret_consts(chunk):
    lg = jnp.log1p(-jnp.exp2(-5.0 - jnp.arange(RET_HEADS, dtype=F32)))
    idx = jnp.arange(chunk, dtype=F32)
    diff = idx[:, None] - idx[None, :]
    dmask = jnp.where(diff[None] >= 0, jnp.exp(jnp.maximum(diff, 0.0)[None] * lg[:, None, None]), 0.0)
    q_dec = jnp.exp((idx + 1.0)[None, :] * lg[:, None])
    k_dec = jnp.exp((chunk - 1.0 - idx)[None, :] * lg[:, None])
    c_dec = jnp.exp(chunk * lg)
    bc = lambda t: jnp.broadcast_to(t[:, :, None], (RET_HEADS, t.shape[1], RET_HD))
    return dmask, bc(q_dec), bc(k_dec), bc(c_dec[:, None])


def _rope_tables(pos):
    half = RET_HD // 2
    inv = 1.0 / (ROPE_BASE ** (jnp.arange(half, dtype=F32) / half))
    ang = pos.astype(F32)[:, None] * inv[None, :]
    cos, sin = jnp.cos(ang), jnp.sin(ang)
    return jnp.concatenate([cos, cos], axis=1), jnp.concatenate([-sin, sin], axis=1)


def _retention(zr, s0, pos, bb, tt, group):
    nb, t, _ = zr.shape
    chunk = math.gcd(t, RET_CHUNK)
    cos, sin = _rope_tables(pos)
    dmask, q_dec, k_dec, c_dec = _ret_consts(chunk)
    st_spec = pl.BlockSpec((bb, RET_HEADS, RET_HD, RET_HD), lambda b, j: (b, 0, 0, 0))
    tab_spec = pl.BlockSpec((tt, RET_HD), lambda b, j: (j, 0))
    vmem = _vmem(_nbytes((bb, tt, 5 * RET_W)) + 2 * _nbytes((bb, RET_HEADS, RET_HD, RET_HD)) + 2 * _nbytes((tt, RET_HD)),
                 _nbytes(dmask.shape) + 3 * _nbytes(q_dec.shape), group * 8 * _nbytes((chunk, 4 * RET_W)))
    return pl.pallas_call(
        functools.partial(_ret_kernel, bb=bb, tt=tt, chunk=chunk, group=group),
        grid=(nb // bb, t // tt),
        in_specs=[pl.BlockSpec((bb, tt, 4 * RET_W), lambda b, j: (b, j, 0)), st_spec, tab_spec, tab_spec,
                  _full(dmask.shape), _full(q_dec.shape), _full(k_dec.shape), _full(c_dec.shape)],
        out_specs=[pl.BlockSpec((bb, tt, RET_W), lambda b, j: (b, j, 0)), st_spec],
        out_shape=[jax.ShapeDtypeStruct((nb, t, RET_W), F32), jax.ShapeDtypeStruct(s0.shape, F32)],
        compiler_params=_cparams(("parallel", "arbitrary"), vmem),
        name="retention",
    )(zr, s0, cos, sin, dmask, q_dec, k_dec, c_dec)


WKV_BLOCK = 64
WKV_SCRATCH = 10


def _pow2(n):
    return n & (n - 1) == 0


def _imod(x, n):
    return jnp.bitwise_and(x, n - 1) if _pow2(n) else lax.rem(x, n)


def _idiv(x, n):
    return jnp.right_shift(x, n.bit_length() - 1) if _pow2(n) else lax.div(x, n)


def _wkv_kernel(zw_ref, sh0_ref, s0_ref, mu_ref, w0_ref, a0_ref, kk_ref, ka_ref, rk_ref, lng_ref, lnb_ref,
                wd_ref, wa_ref, wg_ref, tri_ref, seg_ref, o_ref, s_ref, sh_ref,
                at_s, rt_s, bt_s, kt_s, bh_s, kh_s, v_s, g_s, bonus_s, pend_s, *, bb, tt, sub, group):
    @pl.when(pl.program_id(1) == 0)
    def _():
        s_ref[...] = s0_ref[...]
        sh_ref[...] = sh0_ref[...]

    C = WKV_BLOCK
    R = bb * tt
    nblk = R // C
    nseq = C // sub
    assert (bb == 1 and sub == C) or (nblk == 1 and sub == tt), "one sequence per tile, or whole sequences in one block"
    assert nblk % group == 0
    n_sq = max(int(math.log2(sub)) - 1, 0)
    heads = range(RWKV_HEADS)
    hsl = [slice(h * RWKV_HD, (h + 1) * RWKV_HD) for h in heads]
    half = RWKV_W // 2

    def seg_sum(t):
        seg = seg_ref[...]
        return jnp.concatenate([_nn(t[:, :half], seg), _nn(t[:, half:], seg)], axis=1)

    pw = zw_ref[...].reshape(R, RWKV_IN_W)
    row = lax.broadcasted_iota(jnp.int32, (R, RWKV_IN_W), 0)
    rolled = pltpu.roll(pw, 1, axis=0)
    if bb == 1:
        prev = jnp.where(row == 0, sh_ref[0], rolled)
        sh_ref[0] = pw[R - 1:R, :]
    else:
        carry = jnp.broadcast_to(sh_ref[...], (bb, tt, RWKV_IN_W)).reshape(R, RWKV_IN_W)
        prev = jnp.where(_imod(row, tt) == 0, carry, rolled)
        sh_ref[...] = pw.reshape(bb, tt, RWKV_IN_W)[:, tt - 1:tt, :]
    xm = pw + mu_ref[...] * (prev - pw)
    r = xm[:, 0:RWKV_W]
    kb = xm[:, RWKV_W:2 * RWKV_W]
    o1 = 3 * RWKV_W
    wl = xm[:, o1:o1 + DECAY_LORA]
    al = xm[:, o1 + DECAY_LORA:o1 + DECAY_LORA + AAA_LORA]
    gl = xm[:, o1 + DECAY_LORA + AAA_LORA:RWKV_IN_W]
    logw = -math.exp(-0.5) * jax.nn.sigmoid(w0_ref[...] + _nn(jnp.tanh(wl), wd_ref[...]))
    a = jax.nn.sigmoid(a0_ref[...] + _nn(al, wa_ref[...]))
    l1 = logw.astype(BF16)
    rem = logw - l1.astype(F32)
    l2 = rem.astype(BF16)
    l3 = (rem - l2.astype(F32)).astype(BF16)
    tri = tri_ref[...]
    cum = jnp.concatenate(
        [sum(jnp.dot(tri, part[k * C:(k + 1) * C], preferred_element_type=F32) for part in (l1, l2, l3))
         for k in range(nblk)], axis=0)
    cum_end = jnp.broadcast_to(cum.reshape(R // sub, sub, RWKV_W)[:, sub - 1:sub, :],
                               (R // sub, sub, RWKV_W)).reshape(R, RWKV_W)
    kk = kb * kk_ref[...]
    kk = kk * jnp.minimum(lax.rsqrt(seg_sum(kk * kk)), 1e12)
    km = kb * (1.0 + (a - 1.0) * ka_ref[...])
    bv = kk * a
    p_end = jnp.exp(cum_end)
    e_neg = jnp.exp(-cum)
    e_end = p_end * e_neg
    at_s[...] = -kk * jnp.exp(cum - logw)
    rt_s[...] = r * jnp.exp(cum)
    bt_s[...] = bv * e_neg
    kt_s[...] = km * e_neg
    bh_s[...] = bv * e_end
    kh_s[...] = km * e_end
    v_s[...] = xm[:, 2 * RWKV_W:3 * RWKV_W]
    g_s[...] = _nn(jax.nn.sigmoid(gl), wg_ref[...])
    bonus_s[...] = seg_sum(r * km * rk_ref[...])
    pend_s[...] = p_end

    row_id = lax.broadcasted_iota(jnp.int32, (C, C), 0)
    col_id = lax.broadcasted_iota(jnp.int32, (C, C), 1)
    strict = row_id > col_id
    incl = row_id >= col_id
    if nseq > 1:
        same = _idiv(row_id, sub) == _idiv(col_id, sub)
        strict = jnp.logical_and(strict, same)
        incl = jnp.logical_and(incl, same)
    eye = jnp.where(row_id == col_id, 1.0, 0.0).astype(F32)

    def triangular(r0s):
        rows = [pl.ds(r0, C) for r0 in r0s for _ in heads]
        sl = [s for _ in r0s for s in hsl]
        at = [at_s[rw, :][:, s] for rw, s in zip(rows, sl)]
        rt = [rt_s[rw, :][:, s] for rw, s in zip(rows, sl)]
        v = [v_s[rw, :][:, s] for rw, s in zip(rows, sl)]
        m1 = [_nt(jnp.concatenate([a_, r_], axis=0), jnp.concatenate([bt_s[rw, :][:, s], kt_s[rw, :][:, s]], axis=0))
              for a_, r_, rw, s in zip(at, rt, rows, sl)]
        low = [jnp.where(strict, m[:C, :C], 0.0) for m in m1]
        rab = [jnp.where(incl, m[C:, :C], 0.0) for m in m1]
        akk = [jnp.concatenate([jnp.where(strict, m[:C, C:], 0.0), jnp.where(incl, m[C:, C:], 0.0)], axis=0)
               for m in m1]
        akv = [_nn(k_, v_) for k_, v_ in zip(akk, v)]
        tinv = [eye + l for l in low]
        if n_sq:
            x = [_nn(l, l) for l in low]
        for j in range(n_sq):
            if j < n_sq - 1:
                xt = [_nn(jnp.concatenate([xi, ti], axis=0), xi) for xi, ti in zip(x, tinv)]
                x = [t[:C] for t in xt]
                tinv = [ti + t[C:] for ti, t in zip(tinv, xt)]
            else:
                tinv = [ti + _nn(ti, xi) for ti, xi in zip(tinv, x)]
        ua = [_nn(t, a_) for t, a_ in zip(tinv, at)]
        uv = [_nn(t, k_[:C]) for t, k_ in zip(tinv, akv)]
        oa = [r_ + _nn(rb, u_) for r_, rb, u_ in zip(rt, rab, ua)]
        ov = [k_[C:] + _nn(rb, u_) for k_, rb, u_ in zip(akv, rab, uv)]
        nh = RWKV_HEADS
        return [dict(ua=ua[k * nh:(k + 1) * nh], uv=uv[k * nh:(k + 1) * nh], oa=oa[k * nh:(k + 1) * nh],
                     ov=ov[k * nh:(k + 1) * nh], v=v[k * nh:(k + 1) * nh]) for k in range(len(r0s))]

    def advance(r0, q):
        rw = pl.ds(r0, C)
        bh, kh, pend = bh_s[rw, :], kh_s[rw, :], pend_s[rw, :]
        items = [(n, h) for n in range(nseq) for h in heads]
        rs = [slice(n * sub, (n + 1) * sub) for n, _ in items]
        uo = [_nt(jnp.concatenate([q["ua"][h][s], q["oa"][h][s]], axis=0), s_ref[n, h])
              for (n, h), s in zip(items, rs)]
        u = [m[:sub] + q["uv"][h][s] for m, (n, h), s in zip(uo, items, rs)]
        upd = [_tn(jnp.concatenate([u_, q["v"][h][s]], axis=0),
                   jnp.concatenate([bh[s, hsl[h]], kh[s, hsl[h]]], axis=0)) for u_, (n, h), s in zip(u, items, rs)]
        for up, (n, h), s in zip(upd, items, rs):
            s_ref[n, h] = s_ref[n, h] * pend[s, hsl[h]][0:1] + up
        o = [m[sub:] + q["ov"][h][s] for m, (n, h), s in zip(uo, items, rs)]
        o = jnp.concatenate([jnp.concatenate([o[n * RWKV_HEADS + h] for n in range(nseq)], axis=0) for h in heads],
                            axis=1)
        d = o - seg_sum(o) * (1.0 / RWKV_HD)
        var = seg_sum(d * d) * (1.0 / RWKV_HD)
        o = d * lax.rsqrt(var + RWKV_GN_EPS) * lng_ref[...] + lnb_ref[...]
        o = (o + bonus_s[rw, :] * v_s[rw, :]) * g_s[rw, :]
        if bb == 1:
            o_ref[0, rw, :] = o
        else:
            o_ref[...] = o.reshape(bb, tt, RWKV_W)

    def body(j, carry):
        r0s = [j * group * C + k * C for k in range(group)]
        r0s = [r if isinstance(r, int) else pl.multiple_of(r, C) for r in r0s]
        for r0, q in zip(r0s, triangular(r0s)):
            advance(r0, q)
        return carry

    if nblk == group:
        body(0, 0)
    else:
        lax.fori_loop(0, nblk // group, body, 0)


def _rwkv7(zw, sh0, s0, p, bb, tt, group):
    nb, t, _ = zw.shape
    sub = min(t, WKV_BLOCK)
    step = jnp.arange(WKV_BLOCK)
    tri = ((step[:, None] >= step[None, :]) & (step[:, None] // sub == step[None, :] // sub)).astype(BF16)
    lane_head = jnp.arange(RWKV_W // 2) // RWKV_HD
    seg = (lane_head[:, None] == lane_head[None, :]).astype(BF16)
    rows = bb * tt
    st_spec = pl.BlockSpec((bb, RWKV_HEADS, RWKV_HD, RWKV_HD), lambda b, j: (b, 0, 0, 0))
    sh_spec = pl.BlockSpec((bb, 1, RWKV_IN_W), lambda b, j: (b, 0, 0))
    vec = _full((1, RWKV_W))
    vmem = _vmem(_nbytes((rows, RWKV_IN_W + RWKV_W)) + 2 * _nbytes((bb, RWKV_HEADS, RWKV_HD, RWKV_HD))
                 + 2 * _nbytes((bb, 8, RWKV_IN_W)),
                 9 * _nbytes((8, RWKV_IN_W)) + _nbytes((DECAY_LORA + AAA_LORA + GATE_LORA + WKV_BLOCK, RWKV_W), BF16)
                 + _nbytes(seg.shape, BF16) + WKV_SCRATCH * _nbytes((rows, RWKV_W)),
                 4 * _nbytes((rows, RWKV_IN_W)) + 8 * _nbytes((rows, RWKV_W))
                 + group * 32 * _nbytes((WKV_BLOCK, RWKV_IN_W)))
    return pl.pallas_call(
        functools.partial(_wkv_kernel, bb=bb, tt=tt, sub=sub, group=group),
        grid=(nb // bb, t // tt),
        in_specs=[pl.BlockSpec((bb, tt, RWKV_IN_W), lambda b, j: (b, j, 0)), sh_spec, st_spec,
                  _full((1, RWKV_IN_W)), vec, vec, vec, vec, vec, vec, vec,
                  _full((DECAY_LORA, RWKV_W)), _full((AAA_LORA, RWKV_W)), _full((GATE_LORA, RWKV_W)),
                  _full(tri.shape), _full(seg.shape)],
        out_specs=[pl.BlockSpec((bb, tt, RWKV_W), lambda b, j: (b, j, 0)), st_spec, sh_spec],
        out_shape=[jax.ShapeDtypeStruct((nb, t, RWKV_W), F32), jax.ShapeDtypeStruct(s0.shape, F32),
                   jax.ShapeDtypeStruct(sh0.shape, F32)],
        scratch_shapes=[pltpu.VMEM((rows, RWKV_W), F32)] * WKV_SCRATCH,
        compiler_params=_cparams(("parallel", "arbitrary"), vmem),
        name="rwkv7",
    )(zw, sh0, s0, p["mu"], p["w0"], p["a0"], p["k_k"], p["k_a"], p["r_k"], p["lnx_g"], p["lnx_b"],
      p["w_decay_up"], p["w_a_up"], p["w_g_up"], tri, seg)


def _merge_rows(x, zg, oa, ob, wa_ref, wb_ref, wo_ref):
    ga = jax.nn.sigmoid(zg[:, 0:D_MODEL])
    gb = jax.nn.sigmoid(zg[:, D_MODEL:GATE_W])
    merged = ga * _nn(oa, wa_ref[...]) + gb * _nn(ob, wb_ref[...])
    return x + _nn(merged, wo_ref[...])


def _merge_kernel(x_ref, zg_ref, oa_ref, ob_ref, wa_ref, wb_ref, wo_ref, o_ref):
    o_ref[...] = _merge_rows(x_ref[...], zg_ref[...], oa_ref[...], ob_ref[...], wa_ref, wb_ref, wo_ref)


def _merge(x, zg, o_ret, o_rwkv, wa, wb, wo):
    n = x.shape[0]
    tm = min(ROW_TILE, n)
    row = lambda w: pl.BlockSpec((tm, w), lambda i: (i, 0))
    vmem = _vmem(_nbytes((tm, 2 * D_MODEL + GATE_W + RET_W + RWKV_W)), _nbytes((2 * D_MODEL, D_MODEL), BF16),
                 6 * _nbytes((tm, D_MODEL)))
    return pl.pallas_call(
        _merge_kernel,
        grid=(n // tm,),
        in_specs=[row(D_MODEL), row(GATE_W), row(RET_W), row(RWKV_W),
                  _full((RET_W, D_MODEL)), _full((RWKV_W, D_MODEL)), _full((D_MODEL, D_MODEL))],
        out_specs=row(D_MODEL),
        out_shape=jax.ShapeDtypeStruct((n, D_MODEL), F32),
        compiler_params=_cparams(("parallel",), vmem),
        name="merge",
    )(x, zg, o_ret, o_rwkv, wa, wb, wo)


def _softmax(sc):
    e = jnp.exp(sc - jnp.max(sc, axis=-1, keepdims=True))
    return e / jnp.sum(e, axis=-1, keepdims=True)


def _attend_heads(q_scr, mk_ref, mv_ref, ox_scr, b, r0, tq):
    for h in range(X_HEADS):
        sl = slice(h * X_HD, (h + 1) * X_HD)
        att = _softmax(_nt(q_scr[pl.ds(r0, tq), sl], mk_ref[b, :, sl]) * (X_HD ** -0.5))
        ox_scr[pl.ds(r0, tq), sl] = _nn(att, mv_ref[b, :, sl])


def _attend_tiles(q_scr, mk_ref, mv_ref, ox_scr, seqs, tq):
    halves = X_HD // LANES
    n = X_HEADS * tq
    slot = lax.broadcasted_iota(jnp.int32, (n, MEM_ROWS), 1) % ROWS_PER_TOKEN
    head = lax.broadcasted_iota(jnp.int32, (n, MEM_ROWS), 0) // tq
    own = slot == head
    other = slot == head + X_HEADS
    qs = []
    for b in seqs:
        q = q_scr[b * tq:(b + 1) * tq, :]
        qs.append(jnp.concatenate([q[:, h * X_HD + c * LANES:h * X_HD + (c + 1) * LANES]
                                   for c in range(halves) for h in range(X_HEADS)], axis=0))
    z = [_nt(q, mk_ref[b]) for q, b in zip(qs, seqs)]
    part = [jnp.where(own, zi[:n], 0.0) + jnp.where(other, zi[n:], 0.0) for zi in z]
    sc = [p + pltpu.roll(p, MEM_ROWS - X_HEADS, axis=1) for p in part]
    att = [_softmax(jnp.where(own, s * (X_HD ** -0.5), -jnp.inf)) for s in sc]
    att2 = [jnp.concatenate([a, pltpu.roll(a, X_HEADS, axis=1)], axis=0) for a in att]
    o = [_nn(a, mv_ref[b]) for a, b in zip(att2, seqs)]
    for oi, b in zip(o, seqs):
        ox_scr[b * tq:(b + 1) * tq, :] = jnp.concatenate(
            [oi[(c * X_HEADS + h) * tq:(c * X_HEADS + h + 1) * tq] for h in range(X_HEADS) for c in range(halves)],
            axis=1)


def _cross_kernel(x_ref, mk_ref, mv_ref, g_ref, wq_ref, wo_ref, o_ref, q_scr, ox_scr, *, bb, tq):
    rows = bb * tq
    x = x_ref[...].reshape(rows, D_MODEL)
    q_scr[...] = _nn(_rms(x, g_ref[...]), wq_ref[...])
    _attend_tiles(q_scr, mk_ref, mv_ref, ox_scr, range(bb), tq)
    o_ref[...] = (x + _nn(ox_scr[...], wo_ref[...])).reshape(bb, tq, D_MODEL)


def _cross(x, mem_k, mem_v, g, wq, wo, bb, tq):
    nb, t, _ = x.shape
    x_spec = pl.BlockSpec((bb, tq, D_MODEL), lambda b, j: (b, j, 0))
    m_spec = pl.BlockSpec((bb, MEM_ROWS, LANES), lambda b, j: (b, 0, 0))
    vmem = _vmem(2 * _nbytes((bb, tq, D_MODEL)) + 2 * _nbytes((bb, MEM_ROWS, LANES)),
                 2 * _nbytes((D_MODEL, D_MODEL), BF16) + 2 * _nbytes((bb * tq, D_MODEL)),
                 6 * _nbytes((bb * tq, D_MODEL)) + bb * 8 * _nbytes((2 * X_HEADS * tq, MEM_ROWS)))
    return pl.pallas_call(
        functools.partial(_cross_kernel, bb=bb, tq=tq),
        grid=(nb // bb, t // tq),
        in_specs=[x_spec, m_spec, m_spec, _full((1, D_MODEL)), _full((D_MODEL, D_MODEL)), _full((D_MODEL, D_MODEL))],
        out_specs=x_spec,
        out_shape=jax.ShapeDtypeStruct(x.shape, F32),
        scratch_shapes=[pltpu.VMEM((bb * tq, D_MODEL), F32), pltpu.VMEM((bb * tq, D_MODEL), F32)],
        compiler_params=_cparams(("parallel", "arbitrary"), vmem),
        name="cross_attn",
    )(x, mem_k, mem_v, g, wq, wo)


MLP_FF_CHUNK = 1024


def _mlp_rows(x, g_ref, wu_ref, wd_ref, gf_ref, final_norm):
    h = _rms(x, g_ref[...]).astype(BF16)
    acc = x
    for c in range(0, D_FF, MLP_FF_CHUNK):
        u = jnp.maximum(jnp.dot(h, wu_ref[:, c:c + MLP_FF_CHUNK], preferred_element_type=F32), 0.0)
        acc = acc + _nn(u * u, wd_ref[c:c + MLP_FF_CHUNK, :])
    return _rms(acc, gf_ref[...]) if final_norm else acc


def _mlp_kernel(x_ref, g_ref, wu_ref, wd_ref, gf_ref, o_ref, *, final_norm):
    o_ref[...] = _mlp_rows(x_ref[...], g_ref, wu_ref, wd_ref, gf_ref, final_norm)


def _mlp(x, g, w_up, w_down, g_final, final_norm):
    n = x.shape[0]
    tm = min(ROW_TILE, n)
    row = pl.BlockSpec((tm, D_MODEL), lambda i: (i, 0))
    vmem = _vmem(2 * _nbytes((tm, D_MODEL)), 2 * _nbytes((D_MODEL, D_FF), BF16),
                 4 * _nbytes((tm, MLP_FF_CHUNK)) + 4 * _nbytes((tm, D_MODEL)))
    return pl.pallas_call(
        functools.partial(_mlp_kernel, final_norm=final_norm),
        grid=(n // tm,),
        in_specs=[row, _full((1, D_MODEL)), _full((D_MODEL, D_FF)), _full((D_FF, D_MODEL)), _full((1, D_MODEL))],
        out_specs=row,
        out_shape=jax.ShapeDtypeStruct((n, D_MODEL), F32),
        compiler_params=_cparams(("parallel",), vmem),
        name="mlp",
    )(x, g, w_up, w_down, g_final)


POST_TILE = 512


def _post_kernel(x_ref, zg_ref, oa_ref, ob_ref, mk_ref, mv_ref, wa_ref, wb_ref, wo_ref, gc_ref, wq_ref, wco_ref,
                 gm_ref, wu_ref, wd_ref, gf_ref, o_ref, q_scr, ox_scr, *, tq, final_norm):
    x1 = _merge_rows(x_ref[0], zg_ref[0], oa_ref[0], ob_ref[0], wa_ref, wb_ref, wo_ref)
    q_scr[...] = _nn(_rms(x1, gc_ref[...]), wq_ref[...])
    _attend_heads(q_scr, mk_ref, mv_ref, ox_scr, 0, 0, tq)
    x2 = x1 + _nn(ox_scr[...], wco_ref[...])
    o_ref[0] = _mlp_rows(x2, gm_ref, wu_ref, wd_ref, gf_ref, final_norm)


def _post(x, zg, o_ret, o_rwkv, mem_k, mem_v, w, g_final, final_norm):
    nb, t, _ = x.shape
    tq = min(POST_TILE, t)
    row = lambda width: pl.BlockSpec((1, tq, width), lambda b, j: (b, j, 0))
    mem = pl.BlockSpec((1, MEM_LEN, D_MODEL), lambda b, j: (b, 0, 0))
    sq = _full((D_MODEL, D_MODEL))
    vec = _full((1, D_MODEL))
    vmem = _vmem(_nbytes((tq, 2 * D_MODEL + GATE_W + RET_W + RWKV_W)) + 2 * _nbytes((MEM_LEN, D_MODEL), BF16),
                 _nbytes((4 * D_MODEL + 2 * D_FF, D_MODEL), BF16) + 2 * _nbytes((tq, D_MODEL)),
                 8 * _nbytes((tq, D_MODEL)) + 4 * _nbytes((tq, MLP_FF_CHUNK)))
    return pl.pallas_call(
        functools.partial(_post_kernel, tq=tq, final_norm=final_norm),
        grid=(nb, t // tq),
        in_specs=[row(D_MODEL), row(GATE_W), row(RET_W), row(RWKV_W), mem, mem,
                  _full((RET_W, D_MODEL)), _full((RWKV_W, D_MODEL)), sq, vec, sq, sq,
                  vec, _full((D_MODEL, D_FF)), _full((D_FF, D_MODEL)), vec],
        out_specs=row(D_MODEL),
        out_shape=jax.ShapeDtypeStruct(x.shape, F32),
        scratch_shapes=[pltpu.VMEM((tq, D_MODEL), F32), pltpu.VMEM((tq, D_MODEL), F32)],
        compiler_params=_cparams(("parallel", "arbitrary"), vmem),
        name="post",
    )(x, zg, o_ret, o_rwkv, mem_k, mem_v, w["w_branch_a"], w["w_branch_b"], w["w_out"], w["g_cross"], w["w_cq"],
      w["w_co"], w["g_mlp"], w["w_up"], w["w_down"], g_final)


class _Tiling(NamedTuple):
    seq_block: int
    time_block: int
    cross_block: int
    ret_group: int
    wkv_group: int


def _tiling(nb, t):
    if t > RET_CHUNK:
        tt = min(ROW_TILE, t)
        return _Tiling(seq_block=1, time_block=tt, cross_block=1,
                       ret_group=math.gcd(4, tt // RET_CHUNK), wkv_group=math.gcd(8, tt // WKV_BLOCK))
    seqs = WKV_BLOCK // t
    return _Tiling(seq_block=seqs, time_block=t, cross_block=math.gcd(4, nb), ret_group=seqs, wkv_group=1)


def _layer(x, pos, s_ret, s_wkv, s_shift, mem_k, mem_v, w, g_final, final_norm):
    nb, t, _ = x.shape
    n = nb * t
    tl = _tiling(nb, t)
    xf = x.reshape(n, D_MODEL)
    zg, zr, zw = _in_proj(xf, w["g_mix"], w["w_in"])
    o_ret, s_ret_new = _retention(zr.reshape(nb, t, 4 * RET_W), s_ret, pos, tl.seq_block, tl.time_block, tl.ret_group)
    o_wkv, s_wkv_new, shift_new = _rwkv7(zw.reshape(nb, t, RWKV_IN_W), s_shift.reshape(nb, 1, RWKV_IN_W), s_wkv,
                                         w, tl.seq_block, tl.time_block, tl.wkv_group)
    if mem_k.shape[1:] == (MEM_LEN, D_MODEL):
        y = _post(x, zg.reshape(nb, t, GATE_W), o_ret, o_wkv, mem_k, mem_v, w, g_final, final_norm)
    else:
        x1 = _merge(xf, zg, o_ret.reshape(n, RET_W), o_wkv.reshape(n, RWKV_W),
                    w["w_branch_a"], w["w_branch_b"], w["w_out"])
        x2 = _cross(x1.reshape(nb, t, D_MODEL), mem_k, mem_v, w["g_cross"], w["w_cq"], w["w_co"],
                    tl.cross_block, tl.time_block)
        y = _mlp(x2.reshape(n, D_MODEL), w["g_mlp"], w["w_up"], w["w_down"], g_final, final_norm).reshape(x.shape)
    return y, s_ret_new, s_wkv_new, shift_new.reshape(nb, RWKV_IN_W)


def _layer_weights(l, g_mix, w_in, w_branch_a, w_branch_b, w_out, mu_shift, w0, w_decay_up, a0, w_a_up, w_g_up,
                   k_k, k_a, r_k, lnx_g, lnx_b, g_cross, w_cq, w_co, g_mlp, w_up, w_down):
    vec = lambda v: v[l].reshape(1, -1).astype(F32)
    mat = lambda m: m[l].astype(BF16)
    return dict(g_mix=vec(g_mix), w_in=mat(w_in), w_branch_a=mat(w_branch_a), w_branch_b=mat(w_branch_b),
                w_out=mat(w_out), mu=vec(mu_shift), w0=vec(w0), w_decay_up=mat(w_decay_up), a0=vec(a0),
                w_a_up=mat(w_a_up), w_g_up=mat(w_g_up), k_k=vec(k_k), k_a=vec(k_a), r_k=vec(r_k), lnx_g=vec(lnx_g),
                lnx_b=vec(lnx_b), g_cross=vec(g_cross), w_cq=mat(w_cq), w_co=mat(w_co), g_mlp=vec(g_mlp),
                w_up=mat(w_up), w_down=mat(w_down))


def kernel(x_prompt, x_sample, mem_prompt, state_ret, state_wkv, state_shift, cache_mem_k, cache_mem_v, g_mix, w_in, w_branch_a, w_branch_b, w_out, mu_shift, w0, w_decay_up, a0, w_a_up, w_g_up, k_k, k_a, r_k, lnx_g, lnx_b, g_cross, g_mem, w_cq, w_ck, w_cv, w_co, g_mlp, w_up, w_down, g_final):
    depth = w_in.shape[0]
    bp, tp, _ = x_prompt.shape
    bs, ts, _ = x_sample.shape
    pos_p = jnp.arange(tp, dtype=jnp.int32)
    pos_s = PAST_LEN + jnp.arange(ts, dtype=jnp.int32)
    gf = g_final.reshape(1, D_MODEL)

    xp, xs = x_prompt, x_sample
    outs_p, outs_s = [], []
    for l in range(depth):
        w = _layer_weights(l, g_mix, w_in, w_branch_a, w_branch_b, w_out, mu_shift, w0, w_decay_up, a0, w_a_up,
                           w_g_up, k_k, k_a, r_k, lnx_g, lnx_b, g_cross, w_cq, w_co, g_mlp, w_up, w_down)
        last = l == depth - 1
        k_tiles, v_tiles, k_att, v_att = _mem_kv(mem_prompt.reshape(bp * MEM_LEN, D_MODEL),
                                                 g_mem[l].reshape(1, D_MODEL), w_ck[l].astype(BF16), w_cv[l].astype(BF16))
        xp, sr, sw, ss = _layer(
            xp, pos_p,
            jnp.zeros((bp, RET_HEADS, RET_HD, RET_HD), F32), jnp.zeros((bp, RWKV_HEADS, RWKV_HD, RWKV_HD), F32),
            jnp.zeros((bp, RWKV_IN_W), F32), k_att.reshape(bp, MEM_LEN, D_MODEL), v_att.reshape(bp, MEM_LEN, D_MODEL),
            w, gf, last)
        outs_p.append((sr, sw, ss, _from_tile_order(k_tiles.reshape(bp, MEM_ROWS, LANES)),
                       _from_tile_order(v_tiles.reshape(bp, MEM_ROWS, LANES))))
        xs, sr2, sw2, ss2 = _layer(
            xs, pos_s, state_ret[l], state_wkv[l], state_shift[l],
            _tile_order(cache_mem_k[l]), _tile_order(cache_mem_v[l]), w, gf, last)
        outs_s.append((sr2, sw2, ss2))

    stack = lambda items, i: jnp.stack([it[i] for it in items])
    return (xp, xs, stack(outs_p, 0), stack(outs_p, 1), stack(outs_p, 2), stack(outs_p, 3), stack(outs_p, 4),
            stack(outs_s, 0), stack(outs_s, 1), stack(outs_s, 2))
```

```python
import functools
import math
from typing import NamedTuple

import jax
import jax.numpy as jnp
from jax import lax
from jax.experimental import pallas as pl
from jax.experimental.pallas import tpu as pltpu

F32 = jnp.float32
BF16 = jnp.bfloat16

D_MODEL = 1024
PAST_LEN = 16384
RET_HEADS = 4
RET_HD = 128
RET_W = RET_HEADS * RET_HD
RET_CHUNK = 128
RET_GN_EPS = 1e-5
ROPE_BASE = 10000.0
RWKV_HEADS = 8
RWKV_HD = 64
RWKV_W = RWKV_HEADS * RWKV_HD
DECAY_LORA = 64
AAA_LORA = 64
GATE_LORA = 128
RWKV_GN_EPS = 64e-5
RWKV_IN_W = 3 * RWKV_W + DECAY_LORA + AAA_LORA + GATE_LORA
GATE_W = 2 * D_MODEL
O_RET = 2 * D_MODEL
O_RWKV = O_RET + 4 * RET_W
IN_W = O_RWKV + RWKV_IN_W
MEM_LEN = 256
X_HEADS = 4
X_HD = D_MODEL // X_HEADS
D_FF = 4 * D_MODEL
RMS_EPS = 1e-6

V7X_VMEM_BYTES = 64 * 1024 * 1024
VMEM_CAP_BYTES = V7X_VMEM_BYTES - 8 * 1024 * 1024
ROW_TILE = 512


def _cparams(sem, vmem_bytes):
    return pltpu.CompilerParams(dimension_semantics=sem, vmem_limit_bytes=int(min(vmem_bytes, VMEM_CAP_BYTES)))


def _nbytes(shape, dtype=F32):
    return math.prod(shape) * jnp.dtype(dtype).itemsize


def _nn(a, b):
    return jnp.dot(a.astype(BF16), b.astype(BF16), preferred_element_type=F32)


def _nt(a, b):
    return lax.dot_general(a.astype(BF16), b.astype(BF16), (((1,), (1,)), ((), ())), preferred_element_type=F32)


def _tn(a, b):
    return lax.dot_general(a.astype(BF16), b.astype(BF16), (((0,), (0,)), ((), ())), preferred_element_type=F32)


def _rms(x, g):
    return x * lax.rsqrt(jnp.mean(x * x, axis=-1, keepdims=True) + RMS_EPS) * g


def _head_norms(blocks, eps):
    rows, width = blocks[0].shape
    ones = jnp.ones((width, width), BF16)
    o = jnp.concatenate(blocks, axis=0)
    d = o - _nn(o, ones) * (1.0 / width)
    out = d * lax.rsqrt(_nn(d * d, ones) * (1.0 / width) + eps)
    return [out[i * rows:(i + 1) * rows] for i in range(len(blocks))]


def _full(shape):
    zeros = (0,) * len(shape)
    return pl.BlockSpec(shape, lambda *_: zeros, pipeline_mode=pl.Buffered(1))


def _vmem(pipelined, resident, temps):
    return 2 * pipelined + resident + temps


def _in_proj_kernel(x_ref, g_ref, w_ref, zg_ref, zr_ref, zw_ref):
    h = _rms(x_ref[...], g_ref[...]).astype(BF16)
    zg_ref[...] = jnp.dot(h, w_ref[:, 0:O_RET], preferred_element_type=F32)
    zr_ref[...] = jnp.dot(h, w_ref[:, O_RET:O_RWKV], preferred_element_type=F32)
    zw_ref[...] = jnp.dot(h, w_ref[:, O_RWKV:IN_W], preferred_element_type=F32)


def _in_proj(x, g, w_in):
    n = x.shape[0]
    tm = min(ROW_TILE, n)
    vmem = _vmem(_nbytes((tm, D_MODEL + IN_W)), _nbytes((D_MODEL, IN_W), BF16), _nbytes((tm, 4 * RET_W + D_MODEL)))
    return pl.pallas_call(
        _in_proj_kernel,
        grid=(n // tm,),
        in_specs=[pl.BlockSpec((tm, D_MODEL), lambda i: (i, 0)), _full((1, D_MODEL)), _full((D_MODEL, IN_W))],
        out_specs=[pl.BlockSpec((tm, GATE_W), lambda i: (i, 0)),
                   pl.BlockSpec((tm, 4 * RET_W), lambda i: (i, 0)),
                   pl.BlockSpec((tm, RWKV_IN_W), lambda i: (i, 0))],
        out_shape=[jax.ShapeDtypeStruct((n, GATE_W), F32), jax.ShapeDtypeStruct((n, 4 * RET_W), F32),
                   jax.ShapeDtypeStruct((n, RWKV_IN_W), F32)],
        compiler_params=_cparams(("parallel",), vmem),
        name="in_proj",
    )(x, g, w_in)


LANES = 128
MEM_ROWS = MEM_LEN * X_HEADS * (X_HD // LANES)
ROWS_PER_TOKEN = MEM_ROWS // MEM_LEN


def _tile_order(mem):
    nb = mem.shape[0]
    return (mem.reshape(nb, MEM_LEN, X_HEADS, X_HD // LANES, LANES).transpose(0, 1, 3, 2, 4)
            .reshape(nb, MEM_ROWS, LANES))


def _from_tile_order(raw):
    nb = raw.shape[0]
    return (raw.reshape(nb, MEM_LEN, X_HD // LANES, X_HEADS, LANES).transpose(0, 1, 3, 2, 4)
            .reshape(nb, MEM_LEN, X_HEADS, X_HD))


def _mem_kv_kernel(x_ref, g_ref, wk_ref, wv_ref, kt_ref, vt_ref, kb_ref, vb_ref, *, tm):
    h = _rms(x_ref[...], g_ref[...]).astype(BF16)
    for w_ref, t_ref, b_ref in ((wk_ref, kt_ref, kb_ref), (wv_ref, vt_ref, vb_ref)):
        y = jnp.dot(h, w_ref[...], preferred_element_type=F32)
        b_ref[...] = y.astype(BF16)
        for hd in range(X_HEADS):
            for c in range(X_HD // LANES):
                col = hd * X_HD + c * LANES
                t_ref[pl.ds(c * X_HEADS + hd, tm, stride=ROWS_PER_TOKEN), :] = y[:, col:col + LANES]


def _mem_kv(mem, g, wk, wv):
    n = mem.shape[0]
    tm = min(ROW_TILE, n)
    vmem = _vmem(4 * _nbytes((tm, D_MODEL)), 2 * _nbytes((D_MODEL, D_MODEL), BF16), 4 * _nbytes((tm, D_MODEL)))
    row = pl.BlockSpec((tm, D_MODEL), lambda i: (i, 0))
    tile = pl.BlockSpec((tm * ROWS_PER_TOKEN, LANES), lambda i: (i, 0))
    return pl.pallas_call(
        functools.partial(_mem_kv_kernel, tm=tm),
        grid=(n // tm,),
        in_specs=[row, _full((1, D_MODEL)), _full((D_MODEL, D_MODEL)), _full((D_MODEL, D_MODEL))],
        out_specs=[tile, tile, row, row],
        out_shape=[jax.ShapeDtypeStruct((n * ROWS_PER_TOKEN, LANES), F32)] * 2
        + [jax.ShapeDtypeStruct((n, D_MODEL), BF16)] * 2,
        compiler_params=_cparams(("parallel",), vmem),
        name="mem_kv",
    )(mem, g, wk, wv)


def _ret_body(zr_ref, cos_ref, sin_ref, dm_ref, qd_ref, kd_ref, cd_ref, o_ref, s_ref, *, bb, tt, chunk, group):
    nch = tt // chunk
    assert bb == 1 or nch == 1, "a block holds either one sequence or one chunk per sequence"
    assert (bb * nch) % group == 0
    heads = range(RET_HEADS)

    def body(j, carry):
        if nch == 1:
            items = [(j * group + n, 0) for n in range(group)]
        else:
            starts = [(j * group + n) * chunk for n in range(group)]
            items = [(0, r if isinstance(r, int) else pl.multiple_of(r, chunk)) for r in starts]
        q, k, v = [], [], []
        for b, r0 in items:
            rows = pl.ds(r0, chunk)
            cos = cos_ref[rows, :]
            sin = sin_ref[rows, :]
            for h in heads:
                qh = zr_ref[b, rows, h * RET_HD:(h + 1) * RET_HD]
                kh = zr_ref[b, rows, RET_W + h * RET_HD:RET_W + (h + 1) * RET_HD]
                q.append(qh * cos + pltpu.roll(qh, RET_HD // 2, axis=1) * sin)
                k.append((kh * cos + pltpu.roll(kh, RET_HD // 2, axis=1) * sin) * (RET_HD ** -0.5))
                v.append(zr_ref[b, rows, 2 * RET_W + h * RET_HD:2 * RET_W + (h + 1) * RET_HD])
        hd = [h for _ in items for h in heads]
        sc = [_nt(qi, ki) * dm_ref[h] for qi, ki, h in zip(q, k, hd)]
        kv = [_tn(ki * kd_ref[h], vi) for ki, vi, h in zip(k, v, hd)]
        inner = [_nn(si, vi) for si, vi in zip(sc, v)]
        states = []
        for n, (b, _) in enumerate(items):
            for h in heads:
                s = s_ref[b, h] if (nch == 1 or n == 0) else states[-RET_HEADS] * cd_ref[h] + kv[(n - 1) * RET_HEADS + h]
                states.append(s)
        last = len(items) - 1
        for n, (b, _) in enumerate(items):
            if nch == 1 or n == last:
                for h in heads:
                    i = n * RET_HEADS + h
                    s_ref[b, h] = states[i] * cd_ref[h] + kv[i]
        cross = [_nn(qi, si) * qd_ref[h] for qi, si, h in zip(q, states, hd)]
        normed = _head_norms([a + c for a, c in zip(inner, cross)], RET_GN_EPS)
        for n, (b, r0) in enumerate(items):
            rows = pl.ds(r0, chunk)
            for h in heads:
                g = zr_ref[b, rows, 3 * RET_W + h * RET_HD:3 * RET_W + (h + 1) * RET_HD]
                o_ref[b, rows, h * RET_HD:(h + 1) * RET_HD] = normed[n * RET_HEADS + h] * (g * jax.nn.sigmoid(g))
        return carry

    if bb * nch == group:
        body(0, 0)
    else:
        lax.fori_loop(0, bb * nch // group, body, 0)


def _ret_consts(chunk):
    lg = jnp.log1p(-jnp.exp2(-5.0 - jnp.arange(RET_HEADS, dtype=F32)))
    idx = jnp.arange(chunk, dtype=F32)
    diff = idx[:, None] - idx[None, :]
    dmask = jnp.where(diff[None] >= 0, jnp.exp(jnp.maximum(diff, 0.0)[None] * lg[:, None, None]), 0.0)
    q_dec = jnp.exp((idx + 1.0)[None, :] * lg[:, None])
    k_dec = jnp.exp((chunk - 1.0 - idx)[None, :] * lg[:, None])
    c_dec = jnp.exp(chunk * lg)
    bc = lambda t: jnp.broadcast_to(t[:, :, None], (RET_HEADS, t.shape[1], RET_HD))
    return dmask, bc(q_dec), bc(k_dec), bc(c_dec[:, None])


def _rope_tables(pos):
    half = RET_HD // 2
    inv = 1.0 / (ROPE_BASE ** (jnp.arange(half, dtype=F32) / half))
    ang = pos.astype(F32)[:, None] * inv[None, :]
    cos, sin = jnp.cos(ang), jnp.sin(ang)
    return jnp.concatenate([cos, cos], axis=1), jnp.concatenate([-sin, sin], axis=1)


WKV_BLOCK = 64
WKV_SCRATCH = 10


def _pow2(n):
    return n & (n - 1) == 0


def _imod(x, n):
    return jnp.bitwise_and(x, n - 1) if _pow2(n) else lax.rem(x, n)


def _idiv(x, n):
    return jnp.right_shift(x, n.bit_length() - 1) if _pow2(n) else lax.div(x, n)


def _wkv_body(zw_ref, mu_ref, w0_ref, a0_ref, kk_ref, ka_ref, rk_ref, lng_ref, lnb_ref,
              wd_ref, wa_ref, wg_ref, tri_ref, seg_ref, o_ref, s_ref, sh_ref,
              at_s, rt_s, bt_s, kt_s, bh_s, kh_s, v_s, g_s, bonus_s, pend_s, *, bb, tt, sub, group):
    C = WKV_BLOCK
    R = bb * tt
    nblk = R // C
    nseq = C // sub
    assert (bb == 1 and sub == C) or (nblk == 1 and sub == tt), "one sequence per tile, or whole sequences in one block"
    assert nblk % group == 0
    n_sq = max(int(math.log2(sub)) - 1, 0)
    heads = range(RWKV_HEADS)
    hsl = [slice(h * RWKV_HD, (h + 1) * RWKV_HD) for h in heads]
    half = RWKV_W // 2

    def seg_sum(t):
        seg = seg_ref[...]
        return jnp.concatenate([_nn(t[:, :half], seg), _nn(t[:, half:], seg)], axis=1)

    pw = zw_ref[...].reshape(R, RWKV_IN_W)
    row = lax.broadcasted_iota(jnp.int32, (R, RWKV_IN_W), 0)
    rolled = pltpu.roll(pw, 1, axis=0)
    if bb == 1:
        prev = jnp.where(row == 0, sh_ref[0], rolled)
        sh_ref[0] = pw[R - 1:R, :]
    else:
        carry = jnp.broadcast_to(sh_ref[...], (bb, tt, RWKV_IN_W)).reshape(R, RWKV_IN_W)
        prev = jnp.where(_imod(row, tt) == 0, carry, rolled)
        sh_ref[...] = pw.reshape(bb, tt, RWKV_IN_W)[:, tt - 1:tt, :]
    xm = pw + mu_ref[...] * (prev - pw)
    r = xm[:, 0:RWKV_W]
    kb = xm[:, RWKV_W:2 * RWKV_W]
    o1 = 3 * RWKV_W
    wl = xm[:, o1:o1 + DECAY_LORA]
    al = xm[:, o1 + DECAY_LORA:o1 + DECAY_LORA + AAA_LORA]
    gl = xm[:, o1 + DECAY_LORA + AAA_LORA:RWKV_IN_W]
    logw = -math.exp(-0.5) * jax.nn.sigmoid(w0_ref[...] + _nn(jnp.tanh(wl), wd_ref[...]))
    a = jax.nn.sigmoid(a0_ref[...] + _nn(al, wa_ref[...]))
    l1 = logw.astype(BF16)
    rem = logw - l1.astype(F32)
    l2 = rem.astype(BF16)
    l3 = (rem - l2.astype(F32)).astype(BF16)
    tri = tri_ref[...]
    cum = jnp.concatenate(
        [sum(jnp.dot(tri, part[k * C:(k + 1) * C], preferred_element_type=F32) for part in (l1, l2, l3))
         for k in range(nblk)], axis=0)
    cum_end = jnp.broadcast_to(cum.reshape(R // sub, sub, RWKV_W)[:, sub - 1:sub, :],
                               (R // sub, sub, RWKV_W)).reshape(R, RWKV_W)
    kk = kb * kk_ref[...]
    kk = kk * jnp.minimum(lax.rsqrt(seg_sum(kk * kk)), 1e12)
    km = kb * (1.0 + (a - 1.0) * ka_ref[...])
    bv = kk * a
    p_end = jnp.exp(cum_end)
    e_neg = jnp.exp(-cum)
    e_end = p_end * e_neg
    at_s[...] = -kk * jnp.exp(cum - logw)
    rt_s[...] = r * jnp.exp(cum)
    bt_s[...] = bv * e_neg
    kt_s[...] = km * e_neg
    bh_s[...] = bv * e_end
    kh_s[...] = km * e_end
    v_s[...] = xm[:, 2 * RWKV_W:3 * RWKV_W]
    g_s[...] = _nn(jax.nn.sigmoid(gl), wg_ref[...])
    bonus_s[...] = seg_sum(r * km * rk_ref[...])
    pend_s[...] = p_end

    row_id = lax.broadcasted_iota(jnp.int32, (C, C), 0)
    col_id = lax.broadcasted_iota(jnp.int32, (C, C), 1)
    strict = row_id > col_id
    incl = row_id >= col_id
    if nseq > 1:
        same = _idiv(row_id, sub) == _idiv(col_id, sub)
        strict = jnp.logical_and(strict, same)
        incl = jnp.logical_and(incl, same)
    eye = jnp.where(row_id == col_id, 1.0, 0.0).astype(F32)

    def triangular(r0s):
        rows = [pl.ds(r0, C) for r0 in r0s for _ in heads]
        sl = [s for _ in r0s for s in hsl]
        at = [at_s[rw, :][:, s] for rw, s in zip(rows, sl)]
        rt = [rt_s[rw, :][:, s] for rw, s in zip(rows, sl)]
        v = [v_s[rw, :][:, s] for rw, s in zip(rows, sl)]
        m1 = [_nt(jnp.concatenate([a_, r_], axis=0), jnp.concatenate([bt_s[rw, :][:, s], kt_s[rw, :][:, s]], axis=0))
              for a_, r_, rw, s in zip(at, rt, rows, sl)]
        low = [jnp.where(strict, m[:C, :C], 0.0) for m in m1]
        rab = [jnp.where(incl, m[C:, :C], 0.0) for m in m1]
        akk = [jnp.concatenate([jnp.where(strict, m[:C, C:], 0.0), jnp.where(incl, m[C:, C:], 0.0)], axis=0)
               for m in m1]
        akv = [_nn(k_, v_) for k_, v_ in zip(akk, v)]
        tinv = [eye + l for l in low]
        if n_sq:
            x = [_nn(l, l) for l in low]
        for j in range(n_sq):
            if j < n_sq - 1:
                xt = [_nn(jnp.concatenate([xi, ti], axis=0), xi) for xi, ti in zip(x, tinv)]
                x = [t[:C] for t in xt]
                tinv = [ti + t[C:] for ti, t in zip(tinv, xt)]
            else:
                tinv = [ti + _nn(ti, xi) for ti, xi in zip(tinv, x)]
        ua = [_nn(t, a_) for t, a_ in zip(tinv, at)]
        uv = [_nn(t, k_[:C]) for t, k_ in zip(tinv, akv)]
        oa = [r_ + _nn(rb, u_) for r_, rb, u_ in zip(rt, rab, ua)]
        ov = [k_[C:] + _nn(rb, u_) for k_, rb, u_ in zip(akv, rab, uv)]
        nh = RWKV_HEADS
        return [dict(ua=ua[k * nh:(k + 1) * nh], uv=uv[k * nh:(k + 1) * nh], oa=oa[k * nh:(k + 1) * nh],
                     ov=ov[k * nh:(k + 1) * nh], v=v[k * nh:(k + 1) * nh]) for k in range(len(r0s))]

    def advance(r0, q):
        rw = pl.ds(r0, C)
        bh, kh, pend = bh_s[rw, :], kh_s[rw, :], pend_s[rw, :]
        items = [(n, h) for n in range(nseq) for h in heads]
        rs = [slice(n * sub, (n + 1) * sub) for n, _ in items]
        uo = [_nt(jnp.concatenate([q["ua"][h][s], q["oa"][h][s]], axis=0), s_ref[n, h])
              for (n, h), s in zip(items, rs)]
        u = [m[:sub] + q["uv"][h][s] for m, (n, h), s in zip(uo, items, rs)]
        upd = [_tn(jnp.concatenate([u_, q["v"][h][s]], axis=0),
                   jnp.concatenate([bh[s, hsl[h]], kh[s, hsl[h]]], axis=0)) for u_, (n, h), s in zip(u, items, rs)]
        for up, (n, h), s in zip(upd, items, rs):
            s_ref[n, h] = s_ref[n, h] * pend[s, hsl[h]][0:1] + up
        o = [m[sub:] + q["ov"][h][s] for m, (n, h), s in zip(uo, items, rs)]
        o = jnp.concatenate([jnp.concatenate([o[n * RWKV_HEADS + h] for n in range(nseq)], axis=0) for h in heads],
                            axis=1)
        d = o - seg_sum(o) * (1.0 / RWKV_HD)
        var = seg_sum(d * d) * (1.0 / RWKV_HD)
        o = d * lax.rsqrt(var + RWKV_GN_EPS) * lng_ref[...] + lnb_ref[...]
        o = (o + bonus_s[rw, :] * v_s[rw, :]) * g_s[rw, :]
        if bb == 1:
            o_ref[0, rw, :] = o
        else:
            o_ref[...] = o.reshape(bb, tt, RWKV_W)

    def body(j, carry):
        r0s = [j * group * C + k * C for k in range(group)]
        r0s = [r if isinstance(r, int) else pl.multiple_of(r, C) for r in r0s]
        for r0, q in zip(r0s, triangular(r0s)):
            advance(r0, q)
        return carry

    if nblk == group:
        body(0, 0)
    else:
        lax.fori_loop(0, nblk // group, body, 0)


RET_INPUTS = 7
WKV_INPUTS = 14


def _mixers_kernel(*refs, ret, wkv):
    it = iter(refs)
    take = lambda n: [next(it) for _ in range(n)]
    ret_s0, = take(1)
    ret_in = take(RET_INPUTS)
    wkv_sh0, wkv_s0 = take(2)
    wkv_in = take(WKV_INPUTS)
    ret_o, ret_s, wkv_o, wkv_s, wkv_sh = take(5)
    scratch = take(WKV_SCRATCH)

    @pl.when(pl.program_id(1) == 0)
    def _():
        ret_s[...] = ret_s0[...]
        wkv_s[...] = wkv_s0[...]
        wkv_sh[...] = wkv_sh0[...]

    _ret_body(*ret_in, ret_o, ret_s, **ret)
    _wkv_body(*wkv_in, wkv_o, wkv_s, wkv_sh, *scratch, **wkv)


def _mixers(zr, zw, ret_s0, wkv_sh0, wkv_s0, pos, p, tl):
    nb, t, _ = zr.shape
    bb, tt = tl.seq_block, tl.time_block
    rows = bb * tt
    chunk = math.gcd(t, RET_CHUNK)
    cos, sin = _rope_tables(pos)
    dmask, q_dec, k_dec, c_dec = _ret_consts(chunk)
    sub = min(t, WKV_BLOCK)
    step = jnp.arange(WKV_BLOCK)
    tri = ((step[:, None] >= step[None, :]) & (step[:, None] // sub == step[None, :] // sub)).astype(BF16)
    lane_head = jnp.arange(RWKV_W // 2) // RWKV_HD
    seg = (lane_head[:, None] == lane_head[None, :]).astype(BF16)

    tile = lambda width: pl.BlockSpec((bb, tt, width), lambda b, j: (b, j, 0))
    ret_st = pl.BlockSpec((bb, RET_HEADS, RET_HD, RET_HD), lambda b, j: (b, 0, 0, 0))
    wkv_st = pl.BlockSpec((bb, RWKV_HEADS, RWKV_HD, RWKV_HD), lambda b, j: (b, 0, 0, 0))
    sh_spec = pl.BlockSpec((bb, 1, RWKV_IN_W), lambda b, j: (b, 0, 0))
    tab_spec = pl.BlockSpec((tt, RET_HD), lambda b, j: (j, 0))
    vec = _full((1, RWKV_W))
    vmem = _vmem(_nbytes((rows, 5 * RET_W + RWKV_IN_W + RWKV_W)) + 2 * _nbytes((bb, RET_HEADS, RET_HD, RET_HD))
                 + 2 * _nbytes((bb, RWKV_HEADS, RWKV_HD, RWKV_HD)) + 2 * _nbytes((bb, 8, RWKV_IN_W))
                 + 2 * _nbytes((tt, RET_HD)),
                 _nbytes(dmask.shape) + 3 * _nbytes(q_dec.shape) + 9 * _nbytes((8, RWKV_IN_W))
                 + _nbytes((DECAY_LORA + AAA_LORA + GATE_LORA + WKV_BLOCK, RWKV_W), BF16) + _nbytes(seg.shape, BF16)
                 + WKV_SCRATCH * _nbytes((rows, RWKV_W)),
                 4 * _nbytes((rows, RWKV_IN_W)) + 8 * _nbytes((rows, RWKV_W))
                 + tl.wkv_group * 32 * _nbytes((WKV_BLOCK, RWKV_IN_W)) + tl.ret_group * 8 * _nbytes((chunk, 4 * RET_W)))
    return pl.pallas_call(
        functools.partial(_mixers_kernel,
                          ret=dict(bb=bb, tt=tt, chunk=chunk, group=tl.ret_group),
                          wkv=dict(bb=bb, tt=tt, sub=sub, group=tl.wkv_group)),
        grid=(nb // bb, t // tt),
        in_specs=[ret_st, tile(4 * RET_W), tab_spec, tab_spec,
                  _full(dmask.shape), _full(q_dec.shape), _full(k_dec.shape), _full(c_dec.shape),
                  sh_spec, wkv_st, tile(RWKV_IN_W),
                  _full((1, RWKV_IN_W)), vec, vec, vec, vec, vec, vec, vec,
                  _full((DECAY_LORA, RWKV_W)), _full((AAA_LORA, RWKV_W)), _full((GATE_LORA, RWKV_W)),
                  _full(tri.shape), _full(seg.shape)],
        out_specs=[tile(RET_W), ret_st, tile(RWKV_W), wkv_st, sh_spec],
        out_shape=[jax.ShapeDtypeStruct((nb, t, RET_W), F32), jax.ShapeDtypeStruct(ret_s0.shape, F32),
                   jax.ShapeDtypeStruct((nb, t, RWKV_W), F32), jax.ShapeDtypeStruct(wkv_s0.shape, F32),
                   jax.ShapeDtypeStruct(wkv_sh0.shape, F32)],
        scratch_shapes=[pltpu.VMEM((rows, RWKV_W), F32)] * WKV_SCRATCH,
        compiler_params=_cparams(("parallel", "arbitrary"), vmem),
        name="mixers",
    )(ret_s0, zr, cos, sin, dmask, q_dec, k_dec, c_dec,
      wkv_sh0, wkv_s0, zw, p["mu"], p["w0"], p["a0"], p["k_k"], p["k_a"], p["r_k"], p["lnx_g"], p["lnx_b"],
      p["w_decay_up"], p["w_a_up"], p["w_g_up"], tri, seg)


def _merge_rows(x, zg, oa, ob, wa_ref, wb_ref, wo_ref):
    ga = jax.nn.sigmoid(zg[:, 0:D_MODEL])
    gb = jax.nn.sigmoid(zg[:, D_MODEL:GATE_W])
    merged = ga * _nn(oa, wa_ref[...]) + gb * _nn(ob, wb_ref[...])
    return x + _nn(merged, wo_ref[...])


def _merge_kernel(x_ref, zg_ref, oa_ref, ob_ref, wa_ref, wb_ref, wo_ref, o_ref):
    o_ref[...] = _merge_rows(x_ref[...], zg_ref[...], oa_ref[...], ob_ref[...], wa_ref, wb_ref, wo_ref)


def _merge(x, zg, o_ret, o_rwkv, wa, wb, wo):
    n = x.shape[0]
    tm = min(ROW_TILE, n)
    row = lambda w: pl.BlockSpec((tm, w), lambda i: (i, 0))
    vmem = _vmem(_nbytes((tm, 2 * D_MODEL + GATE_W + RET_W + RWKV_W)), _nbytes((2 * D_MODEL, D_MODEL), BF16),
                 6 * _nbytes((tm, D_MODEL)))
    return pl.pallas_call(
        _merge_kernel,
        grid=(n // tm,),
        in_specs=[row(D_MODEL), row(GATE_W), row(RET_W), row(RWKV_W),
                  _full((RET_W, D_MODEL)), _full((RWKV_W, D_MODEL)), _full((D_MODEL, D_MODEL))],
        out_specs=row(D_MODEL),
        out_shape=jax.ShapeDtypeStruct((n, D_MODEL), F32),
        compiler_params=_cparams(("parallel",), vmem),
        name="merge",
    )(x, zg, o_ret, o_rwkv, wa, wb, wo)


def _softmax(sc):
    e = jnp.exp(sc - jnp.max(sc, axis=-1, keepdims=True))
    return e / jnp.sum(e, axis=-1, keepdims=True)


def _attend_heads(q_scr, mk_ref, mv_ref, ox_scr, b, r0, tq):
    for h in range(X_HEADS):
        sl = slice(h * X_HD, (h + 1) * X_HD)
        att = _softmax(_nt(q_scr[pl.ds(r0, tq), sl], mk_ref[b, :, sl]) * (X_HD ** -0.5))
        ox_scr[pl.ds(r0, tq), sl] = _nn(att, mv_ref[b, :, sl])


def _attend_tiles(q_scr, mk_ref, mv_ref, ox_scr, seqs, tq):
    halves = X_HD // LANES
    n = X_HEADS * tq
    slot = lax.broadcasted_iota(jnp.int32, (n, MEM_ROWS), 1) % ROWS_PER_TOKEN
    head = lax.broadcasted_iota(jnp.int32, (n, MEM_ROWS), 0) // tq
    own = slot == head
    other = slot == head + X_HEADS
    qs = []
    for b in seqs:
        q = q_scr[b * tq:(b + 1) * tq, :]
        qs.append(jnp.concatenate([q[:, h * X_HD + c * LANES:h * X_HD + (c + 1) * LANES]
                                   for c in range(halves) for h in range(X_HEADS)], axis=0))
    z = [_nt(q, mk_ref[b]) for q, b in zip(qs, seqs)]
    part = [jnp.where(own, zi[:n], 0.0) + jnp.where(other, zi[n:], 0.0) for zi in z]
    sc = [p + pltpu.roll(p, MEM_ROWS - X_HEADS, axis=1) for p in part]
    att = [_softmax(jnp.where(own, s * (X_HD ** -0.5), -jnp.inf)) for s in sc]
    att2 = [jnp.concatenate([a, pltpu.roll(a, X_HEADS, axis=1)], axis=0) for a in att]
    o = [_nn(a, mv_ref[b]) for a, b in zip(att2, seqs)]
    for oi, b in zip(o, seqs):
        ox_scr[b * tq:(b + 1) * tq, :] = jnp.concatenate(
            [oi[(c * X_HEADS + h) * tq:(c * X_HEADS + h + 1) * tq] for h in range(X_HEADS) for c in range(halves)],
            axis=1)


def _cross_kernel(x_ref, mk_ref, mv_ref, g_ref, wq_ref, wo_ref, o_ref, q_scr, ox_scr, *, bb, tq):
    rows = bb * tq
    x = x_ref[...].reshape(rows, D_MODEL)
    q_scr[...] = _nn(_rms(x, g_ref[...]), wq_ref[...])
    _attend_tiles(q_scr, mk_ref, mv_ref, ox_scr, range(bb), tq)
    o_ref[...] = (x + _nn(ox_scr[...], wo_ref[...])).reshape(bb, tq, D_MODEL)


def _cross(x, mem_k, mem_v, g, wq, wo, bb, tq):
    nb, t, _ = x.shape
    x_spec = pl.BlockSpec((bb, tq, D_MODEL), lambda b, j: (b, j, 0))
    m_spec = pl.BlockSpec((bb, MEM_ROWS, LANES), lambda b, j: (b, 0, 0))
    vmem = _vmem(2 * _nbytes((bb, tq, D_MODEL)) + 2 * _nbytes((bb, MEM_ROWS, LANES)),
                 2 * _nbytes((D_MODEL, D_MODEL), BF16) + 2 * _nbytes((bb * tq, D_MODEL)),
                 6 * _nbytes((bb * tq, D_MODEL)) + bb * 8 * _nbytes((2 * X_HEADS * tq, MEM_ROWS)))
    return pl.pallas_call(
        functools.partial(_cross_kernel, bb=bb, tq=tq),
        grid=(nb // bb, t // tq),
        in_specs=[x_spec, m_spec, m_spec, _full((1, D_MODEL)), _full((D_MODEL, D_MODEL)), _full((D_MODEL, D_MODEL))],
        out_specs=x_spec,
        out_shape=jax.ShapeDtypeStruct(x.shape, F32),
        scratch_shapes=[pltpu.VMEM((bb * tq, D_MODEL), F32), pltpu.VMEM((bb * tq, D_MODEL), F32)],
        compiler_params=_cparams(("parallel", "arbitrary"), vmem),
        name="cross_attn",
    )(x, mem_k, mem_v, g, wq, wo)


MLP_FF_CHUNK = 1024


def _mlp_rows(x, g_ref, wu_ref, wd_ref, gf_ref, final_norm):
    h = _rms(x, g_ref[...]).astype(BF16)
    acc = x
    for c in range(0, D_FF, MLP_FF_CHUNK):
        u = jnp.maximum(jnp.dot(h, wu_ref[:, c:c + MLP_FF_CHUNK], preferred_element_type=F32), 0.0)
        acc = acc + _nn(u * u, wd_ref[c:c + MLP_FF_CHUNK, :])
    return _rms(acc, gf_ref[...]) if final_norm else acc


def _mlp_kernel(x_ref, g_ref, wu_ref, wd_ref, gf_ref, o_ref, *, final_norm):
    o_ref[...] = _mlp_rows(x_ref[...], g_ref, wu_ref, wd_ref, gf_ref, final_norm)


def _mlp(x, g, w_up, w_down, g_final, final_norm):
    n = x.shape[0]
    tm = min(ROW_TILE, n)
    row = pl.BlockSpec((tm, D_MODEL), lambda i: (i, 0))
    vmem = _vmem(2 * _nbytes((tm, D_MODEL)), 2 * _nbytes((D_MODEL, D_FF), BF16),
                 4 * _nbytes((tm, MLP_FF_CHUNK)) + 4 * _nbytes((tm, D_MODEL)))
    return pl.pallas_call(
        functools.partial(_mlp_kernel, final_norm=final_norm),
        grid=(n // tm,),
        in_specs=[row, _full((1, D_MODEL)), _full((D_MODEL, D_FF)), _full((D_FF, D_MODEL)), _full((1, D_MODEL))],
        out_specs=row,
        out_shape=jax.ShapeDtypeStruct((n, D_MODEL), F32),
        compiler_params=_cparams(("parallel",), vmem),
        name="mlp",
    )(x, g, w_up, w_down, g_final)


POST_TILE = 512


def _post_kernel(x_ref, zg_ref, oa_ref, ob_ref, mk_ref, mv_ref, wa_ref, wb_ref, wo_ref, gc_ref, wq_ref, wco_ref,
                 gm_ref, wu_ref, wd_ref, gf_ref, o_ref, q_scr, ox_scr, *, tq, final_norm):
    x1 = _merge_rows(x_ref[0], zg_ref[0], oa_ref[0], ob_ref[0], wa_ref, wb_ref, wo_ref)
    q_scr[...] = _nn(_rms(x1, gc_ref[...]), wq_ref[...])
    _attend_heads(q_scr, mk_ref, mv_ref, ox_scr, 0, 0, tq)
    x2 = x1 + _nn(ox_scr[...], wco_ref[...])
    o_ref[0] = _mlp_rows(x2, gm_ref, wu_ref, wd_ref, gf_ref, final_norm)


def _post(x, zg, o_ret, o_rwkv, mem_k, mem_v, w, g_final, final_norm):
    nb, t, _ = x.shape
    tq = min(POST_TILE, t)
    row = lambda width: pl.BlockSpec((1, tq, width), lambda b, j: (b, j, 0))
    mem = pl.BlockSpec((1, MEM_LEN, D_MODEL), lambda b, j: (b, 0, 0))
    sq = _full((D_MODEL, D_MODEL))
    vec = _full((1, D_MODEL))
    vmem = _vmem(_nbytes((tq, 2 * D_MODEL + GATE_W + RET_W + RWKV_W)) + 2 * _nbytes((MEM_LEN, D_MODEL), BF16),
                 _nbytes((4 * D_MODEL + 2 * D_FF, D_MODEL), BF16) + 2 * _nbytes((tq, D_MODEL)),
                 8 * _nbytes((tq, D_MODEL)) + 4 * _nbytes((tq, MLP_FF_CHUNK)))
    return pl.pallas_call(
        functools.partial(_post_kernel, tq=tq, final_norm=final_norm),
        grid=(nb, t // tq),
        in_specs=[row(D_MODEL), row(GATE_W), row(RET_W), row(RWKV_W), mem, mem,
                  _full((RET_W, D_MODEL)), _full((RWKV_W, D_MODEL)), sq, vec, sq, sq,
                  vec, _full((D_MODEL, D_FF)), _full((D_FF, D_MODEL)), vec],
        out_specs=row(D_MODEL),
        out_shape=jax.ShapeDtypeStruct(x.shape, F32),
        scratch_shapes=[pltpu.VMEM((tq, D_MODEL), F32), pltpu.VMEM((tq, D_MODEL), F32)],
        compiler_params=_cparams(("parallel", "arbitrary"), vmem),
        name="post",
    )(x, zg, o_ret, o_rwkv, mem_k, mem_v, w["w_branch_a"], w["w_branch_b"], w["w_out"], w["g_cross"], w["w_cq"],
      w["w_co"], w["g_mlp"], w["w_up"], w["w_down"], g_final)


class _Tiling(NamedTuple):
    seq_block: int
    time_block: int
    cross_block: int
    ret_group: int
    wkv_group: int


def _tiling(nb, t):
    if t > RET_CHUNK:
        tt = min(ROW_TILE, t)
        return _Tiling(seq_block=1, time_block=tt, cross_block=1,
                       ret_group=math.gcd(4, tt // RET_CHUNK), wkv_group=math.gcd(8, tt // WKV_BLOCK))
    seqs = WKV_BLOCK // t
    return _Tiling(seq_block=seqs, time_block=t, cross_block=math.gcd(4, nb), ret_group=seqs, wkv_group=1)


def _layer(x, pos, s_ret, s_wkv, s_shift, mem_k, mem_v, w, g_final, final_norm):
    nb, t, _ = x.shape
    n = nb * t
    tl = _tiling(nb, t)
    xf = x.reshape(n, D_MODEL)
    zg, zr, zw = _in_proj(xf, w["g_mix"], w["w_in"])
    o_ret, s_ret_new, o_wkv, s_wkv_new, shift_new = _mixers(
        zr.reshape(nb, t, 4 * RET_W), zw.reshape(nb, t, RWKV_IN_W), s_ret, s_shift.reshape(nb, 1, RWKV_IN_W), s_wkv,
        pos, w, tl)
    if mem_k.shape[1:] == (MEM_LEN, D_MODEL):
        y = _post(x, zg.reshape(nb, t, GATE_W), o_ret, o_wkv, mem_k, mem_v, w, g_final, final_norm)
    else:
        x1 = _merge(xf, zg, o_ret.reshape(n, RET_W), o_wkv.reshape(n, RWKV_W),
                    w["w_branch_a"], w["w_branch_b"], w["w_out"])
        x2 = _cross(x1.reshape(nb, t, D_MODEL), mem_k, mem_v, w["g_cross"], w["w_cq"], w["w_co"],
                    tl.cross_block, tl.time_block)
        y = _mlp(x2.reshape(n, D_MODEL), w["g_mlp"], w["w_up"], w["w_down"], g_final, final_norm).reshape(x.shape)
    return y, s_ret_new, s_wkv_new, shift_new.reshape(nb, RWKV_IN_W)


def _layer_weights(l, g_mix, w_in, w_branch_a, w_branch_b, w_out, mu_shift, w0, w_decay_up, a0, w_a_up, w_g_up,
                   k_k, k_a, r_k, lnx_g, lnx_b, g_cross, w_cq, w_co, g_mlp, w_up, w_down):
    vec = lambda v: v[l].reshape(1, -1).astype(F32)
    mat = lambda m: m[l].astype(BF16)
    return dict(g_mix=vec(g_mix), w_in=mat(w_in), w_branch_a=mat(w_branch_a), w_branch_b=mat(w_branch_b),
                w_out=mat(w_out), mu=vec(mu_shift), w0=vec(w0), w_decay_up=mat(w_decay_up), a0=vec(a0),
                w_a_up=mat(w_a_up), w_g_up=mat(w_g_up), k_k=vec(k_k), k_a=vec(k_a), r_k=vec(r_k), lnx_g=vec(lnx_g),
                lnx_b=vec(lnx_b), g_cross=vec(g_cross), w_cq=mat(w_cq), w_co=mat(w_co), g_mlp=vec(g_mlp),
                w_up=mat(w_up), w_down=mat(w_down))


def kernel(x_prompt, x_sample, mem_prompt, state_ret, state_wkv, state_shift, cache_mem_k, cache_mem_v, g_mix, w_in, w_branch_a, w_branch_b, w_out, mu_shift, w0, w_decay_up, a0, w_a_up, w_g_up, k_k, k_a, r_k, lnx_g, lnx_b, g_cross, g_mem, w_cq, w_ck, w_cv, w_co, g_mlp, w_up, w_down, g_final):
    depth = w_in.shape[0]
    bp, tp, _ = x_prompt.shape
    bs, ts, _ = x_sample.shape
    pos_p = jnp.arange(tp, dtype=jnp.int32)
    pos_s = PAST_LEN + jnp.arange(ts, dtype=jnp.int32)
    gf = g_final.reshape(1, D_MODEL)

    xp, xs = x_prompt, x_sample
    outs_p, outs_s = [], []
    for l in range(depth):
        w = _layer_weights(l, g_mix, w_in, w_branch_a, w_branch_b, w_out, mu_shift, w0, w_decay_up, a0, w_a_up,
                           w_g_up, k_k, k_a, r_k, lnx_g, lnx_b, g_cross, w_cq, w_co, g_mlp, w_up, w_down)
        last = l == depth - 1
        k_tiles, v_tiles, k_att, v_att = _mem_kv(mem_prompt.reshape(bp * MEM_LEN, D_MODEL),
                                                 g_mem[l].reshape(1, D_MODEL), w_ck[l].astype(BF16), w_cv[l].astype(BF16))
        xp, sr, sw, ss = _layer(
            xp, pos_p,
            jnp.zeros((bp, RET_HEADS, RET_HD, RET_HD), F32), jnp.zeros((bp, RWKV_HEADS, RWKV_HD, RWKV_HD), F32),
            jnp.zeros((bp, RWKV_IN_W), F32), k_att.reshape(bp, MEM_LEN, D_MODEL), v_att.reshape(bp, MEM_LEN, D_MODEL),
            w, gf, last)
        outs_p.append((sr, sw, ss, _from_tile_order(k_tiles.reshape(bp, MEM_ROWS, LANES)),
                       _from_tile_order(v_tiles.reshape(bp, MEM_ROWS, LANES))))
        xs, sr2, sw2, ss2 = _layer(
            xs, pos_s, state_ret[l], state_wkv[l], state_shift[l],
            _tile_order(cache_mem_k[l]), _tile_order(cache_mem_v[l]), w, gf, last)
        outs_s.append((sr2, sw2, ss2))

    stack = lambda items, i: jnp.stack([it[i] for it in items])
    return (xp, xs, stack(outs_p, 0), stack(outs_p, 1), stack(outs_p, 2), stack(outs_p, 3), stack(outs_p, 4),
            stack(outs_s, 0), stack(outs_s, 1), stack(outs_s, 2))
```

```python
import functools
import math
from typing import NamedTuple

import jax
import jax.numpy as jnp
from jax import lax
from jax.experimental import pallas as pl
from jax.experimental.pallas import tpu as pltpu

F32 = jnp.float32
BF16 = jnp.bfloat16

D_MODEL = 1024
PAST_LEN = 16384
RET_HEADS = 4
RET_HD = 128
RET_W = RET_HEADS * RET_HD
RET_CHUNK = 128
RET_GN_EPS = 1e-5
ROPE_BASE = 10000.0
RWKV_HEADS = 8
RWKV_HD = 64
RWKV_W = RWKV_HEADS * RWKV_HD
DECAY_LORA = 64
AAA_LORA = 64
GATE_LORA = 128
RWKV_GN_EPS = 64e-5
RWKV_IN_W = 3 * RWKV_W + DECAY_LORA + AAA_LORA + GATE_LORA
GATE_W = 2 * D_MODEL
O_RET = 2 * D_MODEL
O_RWKV = O_RET + 4 * RET_W
IN_W = O_RWKV + RWKV_IN_W
MEM_LEN = 256
X_HEADS = 4
X_HD = D_MODEL // X_HEADS
D_FF = 4 * D_MODEL
RMS_EPS = 1e-6

V7X_VMEM_BYTES = 64 * 1024 * 1024
VMEM_CAP_BYTES = V7X_VMEM_BYTES - 8 * 1024 * 1024
ROW_TILE = 512


def _cparams(sem, vmem_bytes):
    return pltpu.CompilerParams(dimension_semantics=sem, vmem_limit_bytes=int(min(vmem_bytes, VMEM_CAP_BYTES)))


def _nbytes(shape, dtype=F32):
    return math.prod(shape) * jnp.dtype(dtype).itemsize


def _nn(a, b):
    return jnp.dot(a.astype(BF16), b.astype(BF16), preferred_element_type=F32)


def _nt(a, b):
    return lax.dot_general(a.astype(BF16), b.astype(BF16), (((1,), (1,)), ((), ())), preferred_element_type=F32)


def _tn(a, b):
    return lax.dot_general(a.astype(BF16), b.astype(BF16), (((0,), (0,)), ((), ())), preferred_element_type=F32)


def _rms(x, g):
    return x * lax.rsqrt(jnp.mean(x * x, axis=-1, keepdims=True) + RMS_EPS) * g


def _head_norms(blocks, eps):
    rows, width = blocks[0].shape
    ones = jnp.ones((width, width), BF16)
    o = jnp.concatenate(blocks, axis=0)
    d = o - _nn(o, ones) * (1.0 / width)
    out = d * lax.rsqrt(_nn(d * d, ones) * (1.0 / width) + eps)
    return [out[i * rows:(i + 1) * rows] for i in range(len(blocks))]


def _full(shape):
    zeros = (0,) * len(shape)
    return pl.BlockSpec(shape, lambda *_: zeros, pipeline_mode=pl.Buffered(1))


def _vmem(pipelined, resident, temps):
    return 2 * pipelined + resident + temps


def _in_proj_kernel(x_ref, g_ref, w_ref, zg_ref, zr_ref, zw_ref):
    h = _rms(x_ref[...], g_ref[...]).astype(BF16)
    zg_ref[...] = jnp.dot(h, w_ref[:, 0:O_RET], preferred_element_type=F32)
    zr_ref[...] = jnp.dot(h, w_ref[:, O_RET:O_RWKV], preferred_element_type=F32)
    zw_ref[...] = jnp.dot(h, w_ref[:, O_RWKV:IN_W], preferred_element_type=F32)


def _in_proj(x, g, w_in):
    n = x.shape[0]
    tm = min(ROW_TILE, n)
    vmem = _vmem(_nbytes((tm, D_MODEL + IN_W)), _nbytes((D_MODEL, IN_W), BF16), _nbytes((tm, 4 * RET_W + D_MODEL)))
    return pl.pallas_call(
        _in_proj_kernel,
        grid=(n // tm,),
        in_specs=[pl.BlockSpec((tm, D_MODEL), lambda i: (i, 0)), _full((1, D_MODEL)), _full((D_MODEL, IN_W))],
        out_specs=[pl.BlockSpec((tm, GATE_W), lambda i: (i, 0)),
                   pl.BlockSpec((tm, 4 * RET_W), lambda i: (i, 0)),
                   pl.BlockSpec((tm, RWKV_IN_W), lambda i: (i, 0))],
        out_shape=[jax.ShapeDtypeStruct((n, GATE_W), F32), jax.ShapeDtypeStruct((n, 4 * RET_W), F32),
                   jax.ShapeDtypeStruct((n, RWKV_IN_W), F32)],
        compiler_params=_cparams(("parallel",), vmem),
        name="in_proj",
    )(x, g, w_in)


LANES = 128
MEM_ROWS = MEM_LEN * X_HEADS * (X_HD // LANES)
ROWS_PER_TOKEN = MEM_ROWS // MEM_LEN


def _tile_order(mem):
    nb = mem.shape[0]
    return (mem.reshape(nb, MEM_LEN, X_HEADS, X_HD // LANES, LANES).transpose(0, 1, 3, 2, 4)
            .reshape(nb, MEM_ROWS, LANES))


def _from_tile_order(raw):
    nb = raw.shape[0]
    return (raw.reshape(nb, MEM_LEN, X_HD // LANES, X_HEADS, LANES).transpose(0, 1, 3, 2, 4)
            .reshape(nb, MEM_LEN, X_HEADS, X_HD))


def _mem_kv_kernel(x_ref, g_ref, wk_ref, wv_ref, kt_ref, vt_ref, kb_ref, vb_ref, *, tm):
    h = _rms(x_ref[...], g_ref[...]).astype(BF16)
    for w_ref, t_ref, b_ref in ((wk_ref, kt_ref, kb_ref), (wv_ref, vt_ref, vb_ref)):
        y = jnp.dot(h, w_ref[...], preferred_element_type=F32)
        b_ref[...] = y.astype(BF16)
        for hd in range(X_HEADS):
            for c in range(X_HD // LANES):
                col = hd * X_HD + c * LANES
                t_ref[pl.ds(c * X_HEADS + hd, tm, stride=ROWS_PER_TOKEN), :] = y[:, col:col + LANES]


def _mem_kv(mem, g, wk, wv):
    n = mem.shape[0]
    tm = min(ROW_TILE, n)
    vmem = _vmem(4 * _nbytes((tm, D_MODEL)), 2 * _nbytes((D_MODEL, D_MODEL), BF16), 4 * _nbytes((tm, D_MODEL)))
    row = pl.BlockSpec((tm, D_MODEL), lambda i: (i, 0))
    tile = pl.BlockSpec((tm * ROWS_PER_TOKEN, LANES), lambda i: (i, 0))
    return pl.pallas_call(
        functools.partial(_mem_kv_kernel, tm=tm),
        grid=(n // tm,),
        in_specs=[row, _full((1, D_MODEL)), _full((D_MODEL, D_MODEL)), _full((D_MODEL, D_MODEL))],
        out_specs=[tile, tile, row, row],
        out_shape=[jax.ShapeDtypeStruct((n * ROWS_PER_TOKEN, LANES), F32)] * 2
        + [jax.ShapeDtypeStruct((n, D_MODEL), BF16)] * 2,
        compiler_params=_cparams(("parallel",), vmem),
        name="mem_kv",
    )(mem, g, wk, wv)


def _ret_body(zr_ref, cos_ref, sin_ref, dm_ref, qd_ref, kd_ref, cd_ref, o_ref, s_ref, *, bb, tt, chunk, group):
    nch = tt // chunk
    assert bb == 1 or nch == 1, "a block holds either one sequence or one chunk per sequence"
    assert (bb * nch) % group == 0
    heads = range(RET_HEADS)

    def body(j, carry):
        if nch == 1:
            items = [(j * group + n, 0) for n in range(group)]
        else:
            starts = [(j * group + n) * chunk for n in range(group)]
            items = [(0, r if isinstance(r, int) else pl.multiple_of(r, chunk)) for r in starts]
        q, k, v = [], [], []
        for b, r0 in items:
            rows = pl.ds(r0, chunk)
            cos = cos_ref[rows, :]
            sin = sin_ref[rows, :]
            for h in heads:
                qh = zr_ref[b, rows, h * RET_HD:(h + 1) * RET_HD]
                kh = zr_ref[b, rows, RET_W + h * RET_HD:RET_W + (h + 1) * RET_HD]
                q.append(qh * cos + pltpu.roll(qh, RET_HD // 2, axis=1) * sin)
                k.append((kh * cos + pltpu.roll(kh, RET_HD // 2, axis=1) * sin) * (RET_HD ** -0.5))
                v.append(zr_ref[b, rows, 2 * RET_W + h * RET_HD:2 * RET_W + (h + 1) * RET_HD])
        hd = [h for _ in items for h in heads]
        sc = [_nt(qi, ki) * dm_ref[h] for qi, ki, h in zip(q, k, hd)]
        kv = [_tn(ki * kd_ref[h], vi) for ki, vi, h in zip(k, v, hd)]
        inner = [_nn(si, vi) for si, vi in zip(sc, v)]
        states = []
        for n, (b, _) in enumerate(items):
            for h in heads:
                s = s_ref[b, h] if (nch == 1 or n == 0) else states[-RET_HEADS] * cd_ref[h] + kv[(n - 1) * RET_HEADS + h]
                states.append(s)
        last = len(items) - 1
        for n, (b, _) in enumerate(items):
            if nch == 1 or n == last:
                for h in heads:
                    i = n * RET_HEADS + h
                    s_ref[b, h] = states[i] * cd_ref[h] + kv[i]
        cross = [_nn(qi, si) * qd_ref[h] for qi, si, h in zip(q, states, hd)]
        normed = _head_norms([a + c for a, c in zip(inner, cross)], RET_GN_EPS)
        for n, (b, r0) in enumerate(items):
            rows = pl.ds(r0, chunk)
            for h in heads:
                g = zr_ref[b, rows, 3 * RET_W + h * RET_HD:3 * RET_W + (h + 1) * RET_HD]
                o_ref[b, rows, h * RET_HD:(h + 1) * RET_HD] = normed[n * RET_HEADS + h] * (g * jax.nn.sigmoid(g))
        return carry

    if bb * nch == group:
        body(0, 0)
    else:
        lax.fori_loop(0, bb * nch // group, body, 0)


def _ret_consts(chunk):
    lg = jnp.log1p(-jnp.exp2(-5.0 - jnp.arange(RET_HEADS, dtype=F32)))
    idx = jnp.arange(chunk, dtype=F32)
    diff = idx[:, None] - idx[None, :]
    dmask = jnp.where(diff[None] >= 0, jnp.exp(jnp.maximum(diff, 0.0)[None] * lg[:, None, None]), 0.0)
    q_dec = jnp.exp((idx + 1.0)[None, :] * lg[:, None])
    k_dec = jnp.exp((chunk - 1.0 - idx)[None, :] * lg[:, None])
    c_dec = jnp.exp(chunk * lg)
    bc = lambda t: jnp.broadcast_to(t[:, :, None], (RET_HEADS, t.shape[1], RET_HD))
    return dmask, bc(q_dec), bc(k_dec), bc(c_dec[:, None])


def _rope_tables(pos):
    half = RET_HD // 2
    inv = 1.0 / (ROPE_BASE ** (jnp.arange(half, dtype=F32) / half))
    ang = pos.astype(F32)[:, None] * inv[None, :]
    cos, sin = jnp.cos(ang), jnp.sin(ang)
    return jnp.concatenate([cos, cos], axis=1), jnp.concatenate([-sin, sin], axis=1)


WKV_BLOCK = 64
WKV_SCRATCH = 10


def _pow2(n):
    return n & (n - 1) == 0


def _imod(x, n):
    return jnp.bitwise_and(x, n - 1) if _pow2(n) else lax.rem(x, n)


def _idiv(x, n):
    return jnp.right_shift(x, n.bit_length() - 1) if _pow2(n) else lax.div(x, n)


def _wkv_body(zw_ref, mu_ref, w0_ref, a0_ref, kk_ref, ka_ref, rk_ref, lng_ref, lnb_ref,
              wd_ref, wa_ref, wg_ref, tri_ref, seg_ref, o_ref, s_ref, sh_ref,
              at_s, rt_s, bt_s, kt_s, bh_s, kh_s, v_s, g_s, bonus_s, pend_s, *, bb, tt, sub, group):
    C = WKV_BLOCK
    R = bb * tt
    nblk = R // C
    nseq = C // sub
    assert (bb == 1 and sub == C) or (nblk == 1 and sub == tt), "one sequence per tile, or whole sequences in one block"
    assert nblk % group == 0
    n_sq = max(int(math.log2(sub)) - 1, 0)
    heads = range(RWKV_HEADS)
    hsl = [slice(h * RWKV_HD, (h + 1) * RWKV_HD) for h in heads]
    half = RWKV_W // 2

    def seg_sum(t):
        seg = seg_ref[...]
        return jnp.concatenate([_nn(t[:, :half], seg), _nn(t[:, half:], seg)], axis=1)

    pw = zw_ref[...].reshape(R, RWKV_IN_W)
    row = lax.broadcasted_iota(jnp.int32, (R, RWKV_IN_W), 0)
    rolled = pltpu.roll(pw, 1, axis=0)
    if bb == 1:
        prev = jnp.where(row == 0, sh_ref[0], rolled)
        sh_ref[0] = pw[R - 1:R, :]
    else:
        carry = jnp.broadcast_to(sh_ref[...], (bb, tt, RWKV_IN_W)).reshape(R, RWKV_IN_W)
        prev = jnp.where(_imod(row, tt) == 0, carry, rolled)
        sh_ref[...] = pw.reshape(bb, tt, RWKV_IN_W)[:, tt - 1:tt, :]
    xm = pw + mu_ref[...] * (prev - pw)
    r = xm[:, 0:RWKV_W]
    kb = xm[:, RWKV_W:2 * RWKV_W]
    o1 = 3 * RWKV_W
    wl = xm[:, o1:o1 + DECAY_LORA]
    al = xm[:, o1 + DECAY_LORA:o1 + DECAY_LORA + AAA_LORA]
    gl = xm[:, o1 + DECAY_LORA + AAA_LORA:RWKV_IN_W]
    logw = -math.exp(-0.5) * jax.nn.sigmoid(w0_ref[...] + _nn(jnp.tanh(wl), wd_ref[...]))
    a = jax.nn.sigmoid(a0_ref[...] + _nn(al, wa_ref[...]))
    l1 = logw.astype(BF16)
    rem = logw - l1.astype(F32)
    l2 = rem.astype(BF16)
    l3 = (rem - l2.astype(F32)).astype(BF16)
    tri = tri_ref[...]
    cum = jnp.concatenate(
        [sum(jnp.dot(tri, part[k * C:(k + 1) * C], preferred_element_type=F32) for part in (l1, l2, l3))
         for k in range(nblk)], axis=0)
    cum_end = jnp.broadcast_to(cum.reshape(R // sub, sub, RWKV_W)[:, sub - 1:sub, :],
                               (R // sub, sub, RWKV_W)).reshape(R, RWKV_W)
    kk = kb * kk_ref[...]
    kk = kk * jnp.minimum(lax.rsqrt(seg_sum(kk * kk)), 1e12)
    km = kb * (1.0 + (a - 1.0) * ka_ref[...])
    bv = kk * a
    p_end = jnp.exp(cum_end)
    e_neg = jnp.exp(-cum)
    e_end = p_end * e_neg
    at_s[...] = -kk * jnp.exp(cum - logw)
    rt_s[...] = r * jnp.exp(cum)
    bt_s[...] = bv * e_neg
    kt_s[...] = km * e_neg
    bh_s[...] = bv * e_end
    kh_s[...] = km * e_end
    v_s[...] = xm[:, 2 * RWKV_W:3 * RWKV_W]
    g_s[...] = _nn(jax.nn.sigmoid(gl), wg_ref[...])
    bonus_s[...] = seg_sum(r * km * rk_ref[...])
    pend_s[...] = p_end

    row_id = lax.broadcasted_iota(jnp.int32, (C, C), 0)
    col_id = lax.broadcasted_iota(jnp.int32, (C, C), 1)
    strict = row_id > col_id
    incl = row_id >= col_id
    if nseq > 1:
        same = _idiv(row_id, sub) == _idiv(col_id, sub)
        strict = jnp.logical_and(strict, same)
        incl = jnp.logical_and(incl, same)
    eye = jnp.where(row_id == col_id, 1.0, 0.0).astype(F32)

    def triangular(r0s):
        rows = [pl.ds(r0, C) for r0 in r0s for _ in heads]
        sl = [s for _ in r0s for s in hsl]
        at = [at_s[rw, :][:, s] for rw, s in zip(rows, sl)]
        rt = [rt_s[rw, :][:, s] for rw, s in zip(rows, sl)]
        v = [v_s[rw, :][:, s] for rw, s in zip(rows, sl)]
        m1 = [_nt(jnp.concatenate([a_, r_], axis=0), jnp.concatenate([bt_s[rw, :][:, s], kt_s[rw, :][:, s]], axis=0))
              for a_, r_, rw, s in zip(at, rt, rows, sl)]
        low = [jnp.where(strict, m[:C, :C], 0.0) for m in m1]
        rab = [jnp.where(incl, m[C:, :C], 0.0) for m in m1]
        akk = [jnp.concatenate([jnp.where(strict, m[:C, C:], 0.0), jnp.where(incl, m[C:, C:], 0.0)], axis=0)
               for m in m1]
        akv = [_nn(k_, v_) for k_, v_ in zip(akk, v)]
        tinv = [eye + l for l in low]
        if n_sq:
            x = [_nn(l, l) for l in low]
        for j in range(n_sq):
            if j < n_sq - 1:
                xt = [_nn(jnp.concatenate([xi, ti], axis=0), xi) for xi, ti in zip(x, tinv)]
                x = [t[:C] for t in xt]
                tinv = [ti + t[C:] for ti, t in zip(tinv, xt)]
            else:
                tinv = [ti + _nn(ti, xi) for ti, xi in zip(tinv, x)]
        ua = [_nn(t, a_) for t, a_ in zip(tinv, at)]
        uv = [_nn(t, k_[:C]) for t, k_ in zip(tinv, akv)]
        oa = [r_ + _nn(rb, u_) for r_, rb, u_ in zip(rt, rab, ua)]
        ov = [k_[C:] + _nn(rb, u_) for k_, rb, u_ in zip(akv, rab, uv)]
        nh = RWKV_HEADS
        return [dict(ua=ua[k * nh:(k + 1) * nh], uv=uv[k * nh:(k + 1) * nh], oa=oa[k * nh:(k + 1) * nh],
                     ov=ov[k * nh:(k + 1) * nh], v=v[k * nh:(k + 1) * nh]) for k in range(len(r0s))]

    def advance(r0, q):
        rw = pl.ds(r0, C)
        bh, kh, pend = bh_s[rw, :], kh_s[rw, :], pend_s[rw, :]
        items = [(n, h) for n in range(nseq) for h in heads]
        rs = [slice(n * sub, (n + 1) * sub) for n, _ in items]
        uo = [_nt(jnp.concatenate([q["ua"][h][s], q["oa"][h][s]], axis=0), s_ref[n, h])
              for (n, h), s in zip(items, rs)]
        u = [m[:sub] + q["uv"][h][s] for m, (n, h), s in zip(uo, items, rs)]
        upd = [_tn(jnp.concatenate([u_, q["v"][h][s]], axis=0),
                   jnp.concatenate([bh[s, hsl[h]], kh[s, hsl[h]]], axis=0)) for u_, (n, h), s in zip(u, items, rs)]
        for up, (n, h), s in zip(upd, items, rs):
            s_ref[n, h] = s_ref[n, h] * pend[s, hsl[h]][0:1] + up
        o = [m[sub:] + q["ov"][h][s] for m, (n, h), s in zip(uo, items, rs)]
        o = jnp.concatenate([jnp.concatenate([o[n * RWKV_HEADS + h] for n in range(nseq)], axis=0) for h in heads],
                            axis=1)
        d = o - seg_sum(o) * (1.0 / RWKV_HD)
        var = seg_sum(d * d) * (1.0 / RWKV_HD)
        o = d * lax.rsqrt(var + RWKV_GN_EPS) * lng_ref[...] + lnb_ref[...]
        o = (o + bonus_s[rw, :] * v_s[rw, :]) * g_s[rw, :]
        if bb == 1:
            o_ref[0, rw, :] = o
        else:
            o_ref[...] = o.reshape(bb, tt, RWKV_W)

    def body(j, carry):
        r0s = [j * group * C + k * C for k in range(group)]
        r0s = [r if isinstance(r, int) else pl.multiple_of(r, C) for r in r0s]
        for r0, q in zip(r0s, triangular(r0s)):
            advance(r0, q)
        return carry

    if nblk == group:
        body(0, 0)
    else:
        lax.fori_loop(0, nblk // group, body, 0)


RET_INPUTS = 7
WKV_INPUTS = 14


def _mixers_kernel(*refs, ret, wkv):
    it = iter(refs)
    take = lambda n: [next(it) for _ in range(n)]
    ret_s0, = take(1)
    ret_in = take(RET_INPUTS)
    wkv_sh0, wkv_s0 = take(2)
    wkv_in = take(WKV_INPUTS)
    ret_o, ret_s, wkv_o, wkv_s, wkv_sh = take(5)
    scratch = take(WKV_SCRATCH)

    @pl.when(pl.program_id(1) == 0)
    def _():
        ret_s[...] = ret_s0[...]
        wkv_s[...] = wkv_s0[...]
        wkv_sh[...] = wkv_sh0[...]

    _ret_body(*ret_in, ret_o, ret_s, **ret)
    _wkv_body(*wkv_in, wkv_o, wkv_s, wkv_sh, *scratch, **wkv)


def _mixers(zr, zw, ret_s0, wkv_sh0, wkv_s0, pos, p, tl):
    nb, t, _ = zr.shape
    bb, tt = tl.seq_block, tl.time_block
    rows = bb * tt
    chunk = math.gcd(t, RET_CHUNK)
    cos, sin = _rope_tables(pos)
    dmask, q_dec, k_dec, c_dec = _ret_consts(chunk)
    sub = min(t, WKV_BLOCK)
    step = jnp.arange(WKV_BLOCK)
    tri = ((step[:, None] >= step[None, :]) & (step[:, None] // sub == step[None, :] // sub)).astype(BF16)
    lane_head = jnp.arange(RWKV_W // 2) // RWKV_HD
    seg = (lane_head[:, None] == lane_head[None, :]).astype(BF16)

    tile = lambda width: pl.BlockSpec((bb, tt, width), lambda b, j: (b, j, 0))
    ret_st = pl.BlockSpec((bb, RET_HEADS, RET_HD, RET_HD), lambda b, j: (b, 0, 0, 0))
    wkv_st = pl.BlockSpec((bb, RWKV_HEADS, RWKV_HD, RWKV_HD), lambda b, j: (b, 0, 0, 0))
    sh_spec = pl.BlockSpec((bb, 1, RWKV_IN_W), lambda b, j: (b, 0, 0))
    tab_spec = pl.BlockSpec((tt, RET_HD), lambda b, j: (j, 0))
    vec = _full((1, RWKV_W))
    vmem = _vmem(_nbytes((rows, 5 * RET_W + RWKV_IN_W + RWKV_W)) + 2 * _nbytes((bb, RET_HEADS, RET_HD, RET_HD))
                 + 2 * _nbytes((bb, RWKV_HEADS, RWKV_HD, RWKV_HD)) + 2 * _nbytes((bb, 8, RWKV_IN_W))
                 + 2 * _nbytes((tt, RET_HD)),
                 _nbytes(dmask.shape) + 3 * _nbytes(q_dec.shape) + 9 * _nbytes((8, RWKV_IN_W))
                 + _nbytes((DECAY_LORA + AAA_LORA + GATE_LORA + WKV_BLOCK, RWKV_W), BF16) + _nbytes(seg.shape, BF16)
                 + WKV_SCRATCH * _nbytes((rows, RWKV_W)),
                 4 * _nbytes((rows, RWKV_IN_W)) + 8 * _nbytes((rows, RWKV_W))
                 + tl.wkv_group * 32 * _nbytes((WKV_BLOCK, RWKV_IN_W)) + tl.ret_group * 8 * _nbytes((chunk, 4 * RET_W)))
    return pl.pallas_call(
        functools.partial(_mixers_kernel,
                          ret=dict(bb=bb, tt=tt, chunk=chunk, group=tl.ret_group),
                          wkv=dict(bb=bb, tt=tt, sub=sub, group=tl.wkv_group)),
        grid=(nb // bb, t // tt),
        in_specs=[ret_st, tile(4 * RET_W), tab_spec, tab_spec,
                  _full(dmask.shape), _full(q_dec.shape), _full(k_dec.shape), _full(c_dec.shape),
                  sh_spec, wkv_st, tile(RWKV_IN_W),
                  _full((1, RWKV_IN_W)), vec, vec, vec, vec, vec, vec, vec,
                  _full((DECAY_LORA, RWKV_W)), _full((AAA_LORA, RWKV_W)), _full((GATE_LORA, RWKV_W)),
                  _full(tri.shape), _full(seg.shape)],
        out_specs=[tile(RET_W), ret_st, tile(RWKV_W), wkv_st, sh_spec],
        out_shape=[jax.ShapeDtypeStruct((nb, t, RET_W), F32), jax.ShapeDtypeStruct(ret_s0.shape, F32),
                   jax.ShapeDtypeStruct((nb, t, RWKV_W), F32), jax.ShapeDtypeStruct(wkv_s0.shape, F32),
                   jax.ShapeDtypeStruct(wkv_sh0.shape, F32)],
        scratch_shapes=[pltpu.VMEM((rows, RWKV_W), F32)] * WKV_SCRATCH,
        compiler_params=_cparams(("parallel", "arbitrary"), vmem),
        name="mixers",
    )(ret_s0, zr, cos, sin, dmask, q_dec, k_dec, c_dec,
      wkv_sh0, wkv_s0, zw, p["mu"], p["w0"], p["a0"], p["k_k"], p["k_a"], p["r_k"], p["lnx_g"], p["lnx_b"],
      p["w_decay_up"], p["w_a_up"], p["w_g_up"], tri, seg)


def _merge_rows(x, zg, oa, ob, wa_ref, wb_ref, wo_ref):
    ga = jax.nn.sigmoid(zg[:, 0:D_MODEL])
    gb = jax.nn.sigmoid(zg[:, D_MODEL:GATE_W])
    merged = ga * _nn(oa, wa_ref[...]) + gb * _nn(ob, wb_ref[...])
    return x + _nn(merged, wo_ref[...])


def _merge_kernel(x_ref, zg_ref, oa_ref, ob_ref, wa_ref, wb_ref, wo_ref, o_ref):
    o_ref[...] = _merge_rows(x_ref[...], zg_ref[...], oa_ref[...], ob_ref[...], wa_ref, wb_ref, wo_ref)


def _merge(x, zg, o_ret, o_rwkv, wa, wb, wo):
    n = x.shape[0]
    tm = min(ROW_TILE, n)
    row = lambda w: pl.BlockSpec((tm, w), lambda i: (i, 0))
    vmem = _vmem(_nbytes((tm, 2 * D_MODEL + GATE_W + RET_W + RWKV_W)), _nbytes((2 * D_MODEL, D_MODEL), BF16),
                 6 * _nbytes((tm, D_MODEL)))
    return pl.pallas_call(
        _merge_kernel,
        grid=(n // tm,),
        in_specs=[row(D_MODEL), row(GATE_W), row(RET_W), row(RWKV_W),
                  _full((RET_W, D_MODEL)), _full((RWKV_W, D_MODEL)), _full((D_MODEL, D_MODEL))],
        out_specs=row(D_MODEL),
        out_shape=jax.ShapeDtypeStruct((n, D_MODEL), F32),
        compiler_params=_cparams(("parallel",), vmem),
        name="merge",
    )(x, zg, o_ret, o_rwkv, wa, wb, wo)


def _softmax(sc):
    e = jnp.exp(sc - jnp.max(sc, axis=-1, keepdims=True))
    return e / jnp.sum(e, axis=-1, keepdims=True)


def _attend_heads(q_scr, mk_ref, mv_ref, ox_scr, b, r0, tq):
    for h in range(X_HEADS):
        sl = slice(h * X_HD, (h + 1) * X_HD)
        att = _softmax(_nt(q_scr[pl.ds(r0, tq), sl], mk_ref[b, :, sl]) * (X_HD ** -0.5))
        ox_scr[pl.ds(r0, tq), sl] = _nn(att, mv_ref[b, :, sl])


def _attend_tiles(q_scr, mk_ref, mv_ref, ox_scr, seqs, tq):
    halves = X_HD // LANES
    n = X_HEADS * tq
    slot = lax.broadcasted_iota(jnp.int32, (n, MEM_ROWS), 1) % ROWS_PER_TOKEN
    head = lax.broadcasted_iota(jnp.int32, (n, MEM_ROWS), 0) // tq
    own = slot == head
    other = slot == head + X_HEADS
    qs = []
    for b in seqs:
        q = q_scr[b * tq:(b + 1) * tq, :]
        qs.append(jnp.concatenate([q[:, h * X_HD + c * LANES:h * X_HD + (c + 1) * LANES]
                                   for c in range(halves) for h in range(X_HEADS)], axis=0))
    z = [jnp.concatenate([_nt(q, k_ref[b]) for k_ref in mk_ref], axis=1) for q, b in zip(qs, seqs)]
    part = [jnp.where(own, zi[:n], 0.0) + jnp.where(other, zi[n:], 0.0) for zi in z]
    sc = [p + pltpu.roll(p, MEM_ROWS - X_HEADS, axis=1) for p in part]
    att = [_softmax(jnp.where(own, s * (X_HD ** -0.5), -jnp.inf)) for s in sc]
    att2 = [jnp.concatenate([a, pltpu.roll(a, X_HEADS, axis=1)], axis=0) for a in att]
    band = MEM_ROWS // len(mv_ref)
    o = [sum(_nn(a[:, i * band:(i + 1) * band], v_ref[b]) for i, v_ref in enumerate(mv_ref))
         for a, b in zip(att2, seqs)]
    for oi, b in zip(o, seqs):
        ox_scr[b * tq:(b + 1) * tq, :] = jnp.concatenate(
            [oi[(c * X_HEADS + h) * tq:(c * X_HEADS + h + 1) * tq] for h in range(X_HEADS) for c in range(halves)],
            axis=1)


MEM_STREAMS = 2


def _cross_kernel(x_ref, *refs, bb, tq):
    mk_refs, mv_refs = refs[:MEM_STREAMS], refs[MEM_STREAMS:2 * MEM_STREAMS]
    g_ref, wq_ref, wo_ref, o_ref, q_scr, ox_scr = refs[2 * MEM_STREAMS:]
    rows = bb * tq
    x = x_ref[...].reshape(rows, D_MODEL)
    q_scr[...] = _nn(_rms(x, g_ref[...]), wq_ref[...])
    _attend_tiles(q_scr, mk_refs, mv_refs, ox_scr, range(bb), tq)
    o_ref[...] = (x + _nn(ox_scr[...], wo_ref[...])).reshape(bb, tq, D_MODEL)


def _cross(x, mem_k, mem_v, g, wq, wo, bb, tq):
    nb, t, _ = x.shape
    band = MEM_ROWS // MEM_STREAMS
    x_spec = pl.BlockSpec((bb, tq, D_MODEL), lambda b, j: (b, j, 0))
    m_specs = [pl.BlockSpec((bb, band, LANES), functools.partial(lambda b, j, i: (b, i, 0), i=i))
               for i in range(MEM_STREAMS)]
    vmem = _vmem(2 * _nbytes((bb, tq, D_MODEL)) + 2 * _nbytes((bb, MEM_ROWS, LANES)),
                 2 * _nbytes((D_MODEL, D_MODEL), BF16) + 2 * _nbytes((bb * tq, D_MODEL)),
                 6 * _nbytes((bb * tq, D_MODEL)) + bb * 8 * _nbytes((2 * X_HEADS * tq, MEM_ROWS)))
    return pl.pallas_call(
        functools.partial(_cross_kernel, bb=bb, tq=tq),
        grid=(nb // bb, t // tq),
        in_specs=[x_spec, *m_specs, *m_specs, _full((1, D_MODEL)), _full((D_MODEL, D_MODEL)), _full((D_MODEL, D_MODEL))],
        out_specs=x_spec,
        out_shape=jax.ShapeDtypeStruct(x.shape, F32),
        scratch_shapes=[pltpu.VMEM((bb * tq, D_MODEL), F32), pltpu.VMEM((bb * tq, D_MODEL), F32)],
        compiler_params=_cparams(("parallel", "arbitrary"), vmem),
        name="cross_attn",
    )(x, *([mem_k] * MEM_STREAMS), *([mem_v] * MEM_STREAMS), g, wq, wo)


MLP_FF_CHUNK = 1024


def _mlp_rows(x, g_ref, wu_ref, wd_ref, gf_ref, final_norm):
    h = _rms(x, g_ref[...]).astype(BF16)
    acc = x
    for c in range(0, D_FF, MLP_FF_CHUNK):
        u = jnp.maximum(jnp.dot(h, wu_ref[:, c:c + MLP_FF_CHUNK], preferred_element_type=F32), 0.0)
        acc = acc + _nn(u * u, wd_ref[c:c + MLP_FF_CHUNK, :])
    return _rms(acc, gf_ref[...]) if final_norm else acc


def _mlp_kernel(x_ref, g_ref, wu_ref, wd_ref, gf_ref, o_ref, *, final_norm):
    o_ref[...] = _mlp_rows(x_ref[...], g_ref, wu_ref, wd_ref, gf_ref, final_norm)


def _mlp(x, g, w_up, w_down, g_final, final_norm):
    n = x.shape[0]
    tm = min(ROW_TILE, n)
    row = pl.BlockSpec((tm, D_MODEL), lambda i: (i, 0))
    vmem = _vmem(2 * _nbytes((tm, D_MODEL)), 2 * _nbytes((D_MODEL, D_FF), BF16),
                 4 * _nbytes((tm, MLP_FF_CHUNK)) + 4 * _nbytes((tm, D_MODEL)))
    return pl.pallas_call(
        functools.partial(_mlp_kernel, final_norm=final_norm),
        grid=(n // tm,),
        in_specs=[row, _full((1, D_MODEL)), _full((D_MODEL, D_FF)), _full((D_FF, D_MODEL)), _full((1, D_MODEL))],
        out_specs=row,
        out_shape=jax.ShapeDtypeStruct((n, D_MODEL), F32),
        compiler_params=_cparams(("parallel",), vmem),
        name="mlp",
    )(x, g, w_up, w_down, g_final)


POST_TILE = 512


def _post_kernel(x_ref, zg_ref, oa_ref, ob_ref, mk_ref, mv_ref, wa_ref, wb_ref, wo_ref, gc_ref, wq_ref, wco_ref,
                 gm_ref, wu_ref, wd_ref, gf_ref, o_ref, q_scr, ox_scr, *, tq, final_norm):
    x1 = _merge_rows(x_ref[0], zg_ref[0], oa_ref[0], ob_ref[0], wa_ref, wb_ref, wo_ref)
    q_scr[...] = _nn(_rms(x1, gc_ref[...]), wq_ref[...])
    _attend_heads(q_scr, mk_ref, mv_ref, ox_scr, 0, 0, tq)
    x2 = x1 + _nn(ox_scr[...], wco_ref[...])
    o_ref[0] = _mlp_rows(x2, gm_ref, wu_ref, wd_ref, gf_ref, final_norm)


def _post(x, zg, o_ret, o_rwkv, mem_k, mem_v, w, g_final, final_norm):
    nb, t, _ = x.shape
    tq = min(POST_TILE, t)
    row = lambda width: pl.BlockSpec((1, tq, width), lambda b, j: (b, j, 0))
    mem = pl.BlockSpec((1, MEM_LEN, D_MODEL), lambda b, j: (b, 0, 0))
    sq = _full((D_MODEL, D_MODEL))
    vec = _full((1, D_MODEL))
    vmem = _vmem(_nbytes((tq, 2 * D_MODEL + GATE_W + RET_W + RWKV_W)) + 2 * _nbytes((MEM_LEN, D_MODEL), BF16),
                 _nbytes((4 * D_MODEL + 2 * D_FF, D_MODEL), BF16) + 2 * _nbytes((tq, D_MODEL)),
                 8 * _nbytes((tq, D_MODEL)) + 4 * _nbytes((tq, MLP_FF_CHUNK)))
    return pl.pallas_call(
        functools.partial(_post_kernel, tq=tq, final_norm=final_norm),
        grid=(nb, t // tq),
        in_specs=[row(D_MODEL), row(GATE_W), row(RET_W), row(RWKV_W), mem, mem,
                  _full((RET_W, D_MODEL)), _full((RWKV_W, D_MODEL)), sq, vec, sq, sq,
                  vec, _full((D_MODEL, D_FF)), _full((D_FF, D_MODEL)), vec],
        out_specs=row(D_MODEL),
        out_shape=jax.ShapeDtypeStruct(x.shape, F32),
        scratch_shapes=[pltpu.VMEM((tq, D_MODEL), F32), pltpu.VMEM((tq, D_MODEL), F32)],
        compiler_params=_cparams(("parallel", "arbitrary"), vmem),
        name="post",
    )(x, zg, o_ret, o_rwkv, mem_k, mem_v, w["w_branch_a"], w["w_branch_b"], w["w_out"], w["g_cross"], w["w_cq"],
      w["w_co"], w["g_mlp"], w["w_up"], w["w_down"], g_final)


class _Tiling(NamedTuple):
    seq_block: int
    time_block: int
    cross_block: int
    ret_group: int
    wkv_group: int


def _tiling(nb, t):
    if t > RET_CHUNK:
        tt = min(ROW_TILE, t)
        return _Tiling(seq_block=1, time_block=tt, cross_block=1,
                       ret_group=math.gcd(4, tt // RET_CHUNK), wkv_group=math.gcd(8, tt // WKV_BLOCK))
    seqs = WKV_BLOCK // t
    return _Tiling(seq_block=seqs, time_block=t, cross_block=math.gcd(4, nb), ret_group=seqs, wkv_group=1)


def _layer(x, pos, s_ret, s_wkv, s_shift, mem_k, mem_v, w, g_final, final_norm):
    nb, t, _ = x.shape
    n = nb * t
    tl = _tiling(nb, t)
    xf = x.reshape(n, D_MODEL)
    zg, zr, zw = _in_proj(xf, w["g_mix"], w["w_in"])
    o_ret, s_ret_new, o_wkv, s_wkv_new, shift_new = _mixers(
        zr.reshape(nb, t, 4 * RET_W), zw.reshape(nb, t, RWKV_IN_W), s_ret, s_shift.reshape(nb, 1, RWKV_IN_W), s_wkv,
        pos, w, tl)
    if mem_k.shape[1:] == (MEM_LEN, D_MODEL):
        y = _post(x, zg.reshape(nb, t, GATE_W), o_ret, o_wkv, mem_k, mem_v, w, g_final, final_norm)
    else:
        x1 = _merge(xf, zg, o_ret.reshape(n, RET_W), o_wkv.reshape(n, RWKV_W),
                    w["w_branch_a"], w["w_branch_b"], w["w_out"])
        x2 = _cross(x1.reshape(nb, t, D_MODEL), mem_k, mem_v, w["g_cross"], w["w_cq"], w["w_co"],
                    tl.cross_block, tl.time_block)
        y = _mlp(x2.reshape(n, D_MODEL), w["g_mlp"], w["w_up"], w["w_down"], g_final, final_norm).reshape(x.shape)
    return y, s_ret_new, s_wkv_new, shift_new.reshape(nb, RWKV_IN_W)


def _layer_weights(l, g_mix, w_in, w_branch_a, w_branch_b, w_out, mu_shift, w0, w_decay_up, a0, w_a_up, w_g_up,
                   k_k, k_a, r_k, lnx_g, lnx_b, g_cross, w_cq, w_co, g_mlp, w_up, w_down):
    vec = lambda v: v[l].reshape(1, -1).astype(F32)
    mat = lambda m: m[l].astype(BF16)
    return dict(g_mix=vec(g_mix), w_in=mat(w_in), w_branch_a=mat(w_branch_a), w_branch_b=mat(w_branch_b),
                w_out=mat(w_out), mu=vec(mu_shift), w0=vec(w0), w_decay_up=mat(w_decay_up), a0=vec(a0),
                w_a_up=mat(w_a_up), w_g_up=mat(w_g_up), k_k=vec(k_k), k_a=vec(k_a), r_k=vec(r_k), lnx_g=vec(lnx_g),
                lnx_b=vec(lnx_b), g_cross=vec(g_cross), w_cq=mat(w_cq), w_co=mat(w_co), g_mlp=vec(g_mlp),
                w_up=mat(w_up), w_down=mat(w_down))


def kernel(x_prompt, x_sample, mem_prompt, state_ret, state_wkv, state_shift, cache_mem_k, cache_mem_v, g_mix, w_in, w_branch_a, w_branch_b, w_out, mu_shift, w0, w_decay_up, a0, w_a_up, w_g_up, k_k, k_a, r_k, lnx_g, lnx_b, g_cross, g_mem, w_cq, w_ck, w_cv, w_co, g_mlp, w_up, w_down, g_final):
    depth = w_in.shape[0]
    bp, tp, _ = x_prompt.shape
    bs, ts, _ = x_sample.shape
    pos_p = jnp.arange(tp, dtype=jnp.int32)
    pos_s = PAST_LEN + jnp.arange(ts, dtype=jnp.int32)
    gf = g_final.reshape(1, D_MODEL)

    xp, xs = x_prompt, x_sample
    outs_p, outs_s = [], []
    for l in range(depth):
        w = _layer_weights(l, g_mix, w_in, w_branch_a, w_branch_b, w_out, mu_shift, w0, w_decay_up, a0, w_a_up,
                           w_g_up, k_k, k_a, r_k, lnx_g, lnx_b, g_cross, w_cq, w_co, g_mlp, w_up, w_down)
        last = l == depth - 1
        k_tiles, v_tiles, k_att, v_att = _mem_kv(mem_prompt.reshape(bp * MEM_LEN, D_MODEL),
                                                 g_mem[l].reshape(1, D_MODEL), w_ck[l].astype(BF16), w_cv[l].astype(BF16))
        xp, sr, sw, ss = _layer(
            xp, pos_p,
            jnp.zeros((bp, RET_HEADS, RET_HD, RET_HD), F32), jnp.zeros((bp, RWKV_HEADS, RWKV_HD, RWKV_HD), F32),
            jnp.zeros((bp, RWKV_IN_W), F32), k_att.reshape(bp, MEM_LEN, D_MODEL), v_att.reshape(bp, MEM_LEN, D_MODEL),
            w, gf, last)
        outs_p.append((sr, sw, ss, _from_tile_order(k_tiles.reshape(bp, MEM_ROWS, LANES)),
                       _from_tile_order(v_tiles.reshape(bp, MEM_ROWS, LANES))))
        xs, sr2, sw2, ss2 = _layer(
            xs, pos_s, state_ret[l], state_wkv[l], state_shift[l],
            _tile_order(cache_mem_k[l]), _tile_order(cache_mem_v[l]), w, gf, last)
        outs_s.append((sr2, sw2, ss2))

    stack = lambda items, i: jnp.stack([it[i] for it in items])
    return (xp, xs, stack(outs_p, 0), stack(outs_p, 1), stack(outs_p, 2), stack(outs_p, 3), stack(outs_p, 4),
            stack(outs_s, 0), stack(outs_s, 1), stack(outs_s, 2))
```

```python
import functools
import math
from typing import NamedTuple

import jax
import jax.numpy as jnp
from jax import lax
from jax.experimental import pallas as pl
from jax.experimental.pallas import tpu as pltpu

F32 = jnp.float32
BF16 = jnp.bfloat16

D_MODEL = 1024
PAST_LEN = 16384
RET_HEADS = 4
RET_HD = 128
RET_W = RET_HEADS * RET_HD
RET_CHUNK = 128
RET_GN_EPS = 1e-5
ROPE_BASE = 10000.0
RWKV_HEADS = 8
RWKV_HD = 64
RWKV_W = RWKV_HEADS * RWKV_HD
DECAY_LORA = 64
AAA_LORA = 64
GATE_LORA = 128
RWKV_GN_EPS = 64e-5
RWKV_IN_W = 3 * RWKV_W + DECAY_LORA + AAA_LORA + GATE_LORA
GATE_W = 2 * D_MODEL
O_RET = 2 * D_MODEL
O_RWKV = O_RET + 4 * RET_W
IN_W = O_RWKV + RWKV_IN_W
MEM_LEN = 256
X_HEADS = 4
X_HD = D_MODEL // X_HEADS
D_FF = 4 * D_MODEL
RMS_EPS = 1e-6

V7X_VMEM_BYTES = 64 * 1024 * 1024
VMEM_CAP_BYTES = V7X_VMEM_BYTES - 8 * 1024 * 1024
ROW_TILE = 512


def _cparams(sem, vmem_bytes):
    return pltpu.CompilerParams(dimension_semantics=sem, vmem_limit_bytes=int(min(vmem_bytes, VMEM_CAP_BYTES)))


def _nbytes(shape, dtype=F32):
    return math.prod(shape) * jnp.dtype(dtype).itemsize


def _nn(a, b):
    return jnp.dot(a.astype(BF16), b.astype(BF16), preferred_element_type=F32)


def _nt(a, b):
    return lax.dot_general(a.astype(BF16), b.astype(BF16), (((1,), (1,)), ((), ())), preferred_element_type=F32)


def _tn(a, b):
    return lax.dot_general(a.astype(BF16), b.astype(BF16), (((0,), (0,)), ((), ())), preferred_element_type=F32)


def _rms(x, g):
    return x * lax.rsqrt(jnp.mean(x * x, axis=-1, keepdims=True) + RMS_EPS) * g


def _head_norms(blocks, eps):
    rows, width = blocks[0].shape
    ones = jnp.ones((width, width), BF16)
    o = jnp.concatenate(blocks, axis=0)
    d = o - _nn(o, ones) * (1.0 / width)
    out = d * lax.rsqrt(_nn(d * d, ones) * (1.0 / width) + eps)
    return [out[i * rows:(i + 1) * rows] for i in range(len(blocks))]


def _full(shape):
    zeros = (0,) * len(shape)
    return pl.BlockSpec(shape, lambda *_: zeros, pipeline_mode=pl.Buffered(1))


def _vmem(pipelined, resident, temps):
    return 2 * pipelined + resident + temps


def _in_proj_kernel(x_ref, g_ref, w_ref, zg_ref, zr_ref, zw_ref):
    h = _rms(x_ref[...], g_ref[...]).astype(BF16)
    zg_ref[...] = jnp.dot(h, w_ref[:, 0:O_RET], preferred_element_type=F32)
    zr_ref[...] = jnp.dot(h, w_ref[:, O_RET:O_RWKV], preferred_element_type=F32)
    zw_ref[...] = jnp.dot(h, w_ref[:, O_RWKV:IN_W], preferred_element_type=F32)


def _in_proj(x, g, w_in):
    n = x.shape[0]
    tm = min(ROW_TILE, n)
    vmem = _vmem(_nbytes((tm, D_MODEL + IN_W)), _nbytes((D_MODEL, IN_W), BF16), _nbytes((tm, 4 * RET_W + D_MODEL)))
    return pl.pallas_call(
        _in_proj_kernel,
        grid=(n // tm,),
        in_specs=[pl.BlockSpec((tm, D_MODEL), lambda i: (i, 0)), _full((1, D_MODEL)), _full((D_MODEL, IN_W))],
        out_specs=[pl.BlockSpec((tm, GATE_W), lambda i: (i, 0)),
                   pl.BlockSpec((tm, 4 * RET_W), lambda i: (i, 0)),
                   pl.BlockSpec((tm, RWKV_IN_W), lambda i: (i, 0))],
        out_shape=[jax.ShapeDtypeStruct((n, GATE_W), F32), jax.ShapeDtypeStruct((n, 4 * RET_W), F32),
                   jax.ShapeDtypeStruct((n, RWKV_IN_W), F32)],
        compiler_params=_cparams(("parallel",), vmem),
        name="in_proj",
    )(x, g, w_in)


LANES = 128
MEM_ROWS = MEM_LEN * X_HEADS * (X_HD // LANES)
ROWS_PER_TOKEN = MEM_ROWS // MEM_LEN


def _tile_order(mem):
    nb = mem.shape[0]
    return (mem.reshape(nb, MEM_LEN, X_HEADS, X_HD // LANES, LANES).transpose(0, 1, 3, 2, 4)
            .reshape(nb, MEM_ROWS, LANES))


def _from_tile_order(raw):
    nb = raw.shape[0]
    return (raw.reshape(nb, MEM_LEN, X_HD // LANES, X_HEADS, LANES).transpose(0, 1, 3, 2, 4)
            .reshape(nb, MEM_LEN, X_HEADS, X_HD))


def _mem_kv_kernel(x_ref, g_ref, wk_ref, wv_ref, kt_ref, vt_ref, kb_ref, vb_ref, *, tm):
    h = _rms(x_ref[...], g_ref[...]).astype(BF16)
    for w_ref, t_ref, b_ref in ((wk_ref, kt_ref, kb_ref), (wv_ref, vt_ref, vb_ref)):
        y = jnp.dot(h, w_ref[...], preferred_element_type=F32)
        b_ref[...] = y.astype(BF16)
        for hd in range(X_HEADS):
            for c in range(X_HD // LANES):
                col = hd * X_HD + c * LANES
                t_ref[pl.ds(c * X_HEADS + hd, tm, stride=ROWS_PER_TOKEN), :] = y[:, col:col + LANES]


def _mem_kv(mem, g, wk, wv):
    n = mem.shape[0]
    tm = min(ROW_TILE, n)
    vmem = _vmem(4 * _nbytes((tm, D_MODEL)), 2 * _nbytes((D_MODEL, D_MODEL), BF16), 4 * _nbytes((tm, D_MODEL)))
    row = pl.BlockSpec((tm, D_MODEL), lambda i: (i, 0))
    tile = pl.BlockSpec((tm * ROWS_PER_TOKEN, LANES), lambda i: (i, 0))
    return pl.pallas_call(
        functools.partial(_mem_kv_kernel, tm=tm),
        grid=(n // tm,),
        in_specs=[row, _full((1, D_MODEL)), _full((D_MODEL, D_MODEL)), _full((D_MODEL, D_MODEL))],
        out_specs=[tile, tile, row, row],
        out_shape=[jax.ShapeDtypeStruct((n * ROWS_PER_TOKEN, LANES), F32)] * 2
        + [jax.ShapeDtypeStruct((n, D_MODEL), BF16)] * 2,
        compiler_params=_cparams(("parallel",), vmem),
        name="mem_kv",
    )(mem, g, wk, wv)


def _ret_body(zr_ref, cos_ref, sin_ref, dm_ref, qd_ref, kd_ref, cd_ref, o_ref, s_ref, *, bb, tt, chunk, group):
    nch = tt // chunk
    assert bb == 1 or nch == 1, "a block holds either one sequence or one chunk per sequence"
    assert (bb * nch) % group == 0
    heads = range(RET_HEADS)

    def body(j, carry):
        if nch == 1:
            items = [(j * group + n, 0) for n in range(group)]
        else:
            starts = [(j * group + n) * chunk for n in range(group)]
            items = [(0, r if isinstance(r, int) else pl.multiple_of(r, chunk)) for r in starts]
        q, k, v = [], [], []
        for b, r0 in items:
            rows = pl.ds(r0, chunk)
            cos = cos_ref[rows, :]
            sin = sin_ref[rows, :]
            for h in heads:
                qh = zr_ref[b, rows, h * RET_HD:(h + 1) * RET_HD]
                kh = zr_ref[b, rows, RET_W + h * RET_HD:RET_W + (h + 1) * RET_HD]
                q.append(qh * cos + pltpu.roll(qh, RET_HD // 2, axis=1) * sin)
                k.append((kh * cos + pltpu.roll(kh, RET_HD // 2, axis=1) * sin) * (RET_HD ** -0.5))
                v.append(zr_ref[b, rows, 2 * RET_W + h * RET_HD:2 * RET_W + (h + 1) * RET_HD])
        hd = [h for _ in items for h in heads]
        sc = [_nt(qi, ki) * dm_ref[h] for qi, ki, h in zip(q, k, hd)]
        kv = [_tn(ki * kd_ref[h], vi) for ki, vi, h in zip(k, v, hd)]
        inner = [_nn(si, vi) for si, vi in zip(sc, v)]
        states = []
        for n, (b, _) in enumerate(items):
            for h in heads:
                s = s_ref[b, h] if (nch == 1 or n == 0) else states[-RET_HEADS] * cd_ref[h] + kv[(n - 1) * RET_HEADS + h]
                states.append(s)
        last = len(items) - 1
        for n, (b, _) in enumerate(items):
            if nch == 1 or n == last:
                for h in heads:
                    i = n * RET_HEADS + h
                    s_ref[b, h] = states[i] * cd_ref[h] + kv[i]
        cross = [_nn(qi, si) * qd_ref[h] for qi, si, h in zip(q, states, hd)]
        normed = _head_norms([a + c for a, c in zip(inner, cross)], RET_GN_EPS)
        for n, (b, r0) in enumerate(items):
            rows = pl.ds(r0, chunk)
            for h in heads:
                g = zr_ref[b, rows, 3 * RET_W + h * RET_HD:3 * RET_W + (h + 1) * RET_HD]
                o_ref[b, rows, h * RET_HD:(h + 1) * RET_HD] = normed[n * RET_HEADS + h] * (g * jax.nn.sigmoid(g))
        return carry

    if bb * nch == group:
        body(0, 0)
    else:
        lax.fori_loop(0, bb * nch // group, body, 0)


def _ret_consts(chunk):
    lg = jnp.log1p(-jnp.exp2(-5.0 - jnp.arange(RET_HEADS, dtype=F32)))
    idx = jnp.arange(chunk, dtype=F32)
    diff = idx[:, None] - idx[None, :]
    dmask = jnp.where(diff[None] >= 0, jnp.exp(jnp.maximum(diff, 0.0)[None] * lg[:, None, None]), 0.0)
    q_dec = jnp.exp((idx + 1.0)[None, :] * lg[:, None])
    k_dec = jnp.exp((chunk - 1.0 - idx)[None, :] * lg[:, None])
    c_dec = jnp.exp(chunk * lg)
    bc = lambda t: jnp.broadcast_to(t[:, :, None], (RET_HEADS, t.shape[1], RET_HD))
    return dmask, bc(q_dec), bc(k_dec), bc(c_dec[:, None])


def _rope_tables(pos):
    half = RET_HD // 2
    inv = 1.0 / (ROPE_BASE ** (jnp.arange(half, dtype=F32) / half))
    ang = pos.astype(F32)[:, None] * inv[None, :]
    cos, sin = jnp.cos(ang), jnp.sin(ang)
    return jnp.concatenate([cos, cos], axis=1), jnp.concatenate([-sin, sin], axis=1)


WKV_BLOCK = 64
WKV_SCRATCH = 10


def _pack_heads(s):
    nb = s.shape[0]
    return (s.reshape(nb, RWKV_HEADS // 2, 2, RWKV_HD, RWKV_HD).transpose(0, 1, 3, 2, 4)
            .reshape(nb, RWKV_HEADS // 2, RWKV_HD, 2 * RWKV_HD))


def _unpack_heads(p):
    nb = p.shape[0]
    return (p.reshape(nb, RWKV_HEADS // 2, RWKV_HD, 2, RWKV_HD).transpose(0, 1, 3, 2, 4)
            .reshape(nb, RWKV_HEADS, RWKV_HD, RWKV_HD))


def _head_state(s_ref, n, h):
    if s_ref.shape[1] == RWKV_HEADS:
        return n, h
    return n, h // 2, slice(None), slice((h % 2) * RWKV_HD, (h % 2 + 1) * RWKV_HD)


def _pow2(n):
    return n & (n - 1) == 0


def _imod(x, n):
    return jnp.bitwise_and(x, n - 1) if _pow2(n) else lax.rem(x, n)


def _idiv(x, n):
    return jnp.right_shift(x, n.bit_length() - 1) if _pow2(n) else lax.div(x, n)


def _wkv_body(zw_ref, mu_ref, w0_ref, a0_ref, kk_ref, ka_ref, rk_ref, lng_ref, lnb_ref,
              wd_ref, wa_ref, wg_ref, tri_ref, seg_ref, o_ref, s_ref, sh_ref,
              at_s, rt_s, bt_s, kt_s, bh_s, kh_s, v_s, g_s, bonus_s, pend_s, *, bb, tt, sub, group):
    C = WKV_BLOCK
    R = bb * tt
    nblk = R // C
    nseq = C // sub
    assert (bb == 1 and sub == C) or (nblk == 1 and sub == tt), "one sequence per tile, or whole sequences in one block"
    assert nblk % group == 0
    n_sq = max(int(math.log2(sub)) - 1, 0)
    heads = range(RWKV_HEADS)
    hsl = [slice(h * RWKV_HD, (h + 1) * RWKV_HD) for h in heads]
    half = RWKV_W // 2

    def seg_sum(t):
        seg = seg_ref[...]
        return jnp.concatenate([_nn(t[:, :half], seg), _nn(t[:, half:], seg)], axis=1)

    pw = zw_ref[...].reshape(R, RWKV_IN_W)
    row = lax.broadcasted_iota(jnp.int32, (R, RWKV_IN_W), 0)
    rolled = pltpu.roll(pw, 1, axis=0)
    if bb == 1:
        prev = jnp.where(row == 0, sh_ref[0], rolled)
        sh_ref[0] = pw[R - 1:R, :]
    else:
        carry = jnp.broadcast_to(sh_ref[...], (bb, tt, RWKV_IN_W)).reshape(R, RWKV_IN_W)
        prev = jnp.where(_imod(row, tt) == 0, carry, rolled)
        sh_ref[...] = pw.reshape(bb, tt, RWKV_IN_W)[:, tt - 1:tt, :]
    xm = pw + mu_ref[...] * (prev - pw)
    r = xm[:, 0:RWKV_W]
    kb = xm[:, RWKV_W:2 * RWKV_W]
    o1 = 3 * RWKV_W
    wl = xm[:, o1:o1 + DECAY_LORA]
    al = xm[:, o1 + DECAY_LORA:o1 + DECAY_LORA + AAA_LORA]
    gl = xm[:, o1 + DECAY_LORA + AAA_LORA:RWKV_IN_W]
    logw = -math.exp(-0.5) * jax.nn.sigmoid(w0_ref[...] + _nn(jnp.tanh(wl), wd_ref[...]))
    a = jax.nn.sigmoid(a0_ref[...] + _nn(al, wa_ref[...]))
    l1 = logw.astype(BF16)
    rem = logw - l1.astype(F32)
    l2 = rem.astype(BF16)
    l3 = (rem - l2.astype(F32)).astype(BF16)
    tri = tri_ref[...]
    cum = jnp.concatenate(
        [sum(jnp.dot(tri, part[k * C:(k + 1) * C], preferred_element_type=F32) for part in (l1, l2, l3))
         for k in range(nblk)], axis=0)
    cum_end = jnp.broadcast_to(cum.reshape(R // sub, sub, RWKV_W)[:, sub - 1:sub, :],
                               (R // sub, sub, RWKV_W)).reshape(R, RWKV_W)
    kk = kb * kk_ref[...]
    kk = kk * jnp.minimum(lax.rsqrt(seg_sum(kk * kk)), 1e12)
    km = kb * (1.0 + (a - 1.0) * ka_ref[...])
    bv = kk * a
    p_end = jnp.exp(cum_end)
    e_neg = jnp.exp(-cum)
    e_end = p_end * e_neg
    at_s[...] = -kk * jnp.exp(cum - logw)
    rt_s[...] = r * jnp.exp(cum)
    bt_s[...] = bv * e_neg
    kt_s[...] = km * e_neg
    bh_s[...] = bv * e_end
    kh_s[...] = km * e_end
    v_s[...] = xm[:, 2 * RWKV_W:3 * RWKV_W]
    g_s[...] = _nn(jax.nn.sigmoid(gl), wg_ref[...])
    bonus_s[...] = seg_sum(r * km * rk_ref[...])
    pend_s[...] = p_end

    row_id = lax.broadcasted_iota(jnp.int32, (C, C), 0)
    col_id = lax.broadcasted_iota(jnp.int32, (C, C), 1)
    strict = row_id > col_id
    incl = row_id >= col_id
    if nseq > 1:
        same = _idiv(row_id, sub) == _idiv(col_id, sub)
        strict = jnp.logical_and(strict, same)
        incl = jnp.logical_and(incl, same)
    eye = jnp.where(row_id == col_id, 1.0, 0.0).astype(F32)

    def triangular(r0s):
        rows = [pl.ds(r0, C) for r0 in r0s for _ in heads]
        sl = [s for _ in r0s for s in hsl]
        at = [at_s[rw, :][:, s] for rw, s in zip(rows, sl)]
        rt = [rt_s[rw, :][:, s] for rw, s in zip(rows, sl)]
        v = [v_s[rw, :][:, s] for rw, s in zip(rows, sl)]
        m1 = [_nt(jnp.concatenate([a_, r_], axis=0), jnp.concatenate([bt_s[rw, :][:, s], kt_s[rw, :][:, s]], axis=0))
              for a_, r_, rw, s in zip(at, rt, rows, sl)]
        low = [jnp.where(strict, m[:C, :C], 0.0) for m in m1]
        rab = [jnp.where(incl, m[C:, :C], 0.0) for m in m1]
        akk = [jnp.concatenate([jnp.where(strict, m[:C, C:], 0.0), jnp.where(incl, m[C:, C:], 0.0)], axis=0)
               for m in m1]
        akv = [_nn(k_, v_) for k_, v_ in zip(akk, v)]
        tinv = [eye + l for l in low]
        if n_sq:
            x = [_nn(l, l) for l in low]
        for j in range(n_sq):
            if j < n_sq - 1:
                xt = [_nn(jnp.concatenate([xi, ti], axis=0), xi) for xi, ti in zip(x, tinv)]
                x = [t[:C] for t in xt]
                tinv = [ti + t[C:] for ti, t in zip(tinv, xt)]
            else:
                tinv = [ti + _nn(ti, xi) for ti, xi in zip(tinv, x)]
        ua = [_nn(t, a_) for t, a_ in zip(tinv, at)]
        uv = [_nn(t, k_[:C]) for t, k_ in zip(tinv, akv)]
        oa = [r_ + _nn(rb, u_) for r_, rb, u_ in zip(rt, rab, ua)]
        ov = [k_[C:] + _nn(rb, u_) for k_, rb, u_ in zip(akv, rab, uv)]
        nh = RWKV_HEADS
        return [dict(ua=ua[k * nh:(k + 1) * nh], uv=uv[k * nh:(k + 1) * nh], oa=oa[k * nh:(k + 1) * nh],
                     ov=ov[k * nh:(k + 1) * nh], v=v[k * nh:(k + 1) * nh]) for k in range(len(r0s))]

    def advance(r0, q):
        rw = pl.ds(r0, C)
        bh, kh, pend = bh_s[rw, :], kh_s[rw, :], pend_s[rw, :]
        items = [(n, h) for n in range(nseq) for h in heads]
        rs = [slice(n * sub, (n + 1) * sub) for n, _ in items]
        cur = [s_ref[_head_state(s_ref, n, h)] for n, h in items]
        uo = [_nt(jnp.concatenate([q["ua"][h][s], q["oa"][h][s]], axis=0), c)
              for (n, h), s, c in zip(items, rs, cur)]
        u = [m[:sub] + q["uv"][h][s] for m, (n, h), s in zip(uo, items, rs)]
        upd = [_tn(jnp.concatenate([u_, q["v"][h][s]], axis=0),
                   jnp.concatenate([bh[s, hsl[h]], kh[s, hsl[h]]], axis=0)) for u_, (n, h), s in zip(u, items, rs)]
        for up, (n, h), s, c in zip(upd, items, rs, cur):
            s_ref[_head_state(s_ref, n, h)] = c * pend[s, hsl[h]][0:1] + up
        o = [m[sub:] + q["ov"][h][s] for m, (n, h), s in zip(uo, items, rs)]
        o = jnp.concatenate([jnp.concatenate([o[n * RWKV_HEADS + h] for n in range(nseq)], axis=0) for h in heads],
                            axis=1)
        d = o - seg_sum(o) * (1.0 / RWKV_HD)
        var = seg_sum(d * d) * (1.0 / RWKV_HD)
        o = d * lax.rsqrt(var + RWKV_GN_EPS) * lng_ref[...] + lnb_ref[...]
        o = (o + bonus_s[rw, :] * v_s[rw, :]) * g_s[rw, :]
        if bb == 1:
            o_ref[0, rw, :] = o
        else:
            o_ref[...] = o.reshape(bb, tt, RWKV_W)

    def body(j, carry):
        r0s = [j * group * C + k * C for k in range(group)]
        r0s = [r if isinstance(r, int) else pl.multiple_of(r, C) for r in r0s]
        for r0, q in zip(r0s, triangular(r0s)):
            advance(r0, q)
        return carry

    if nblk == group:
        body(0, 0)
    else:
        lax.fori_loop(0, nblk // group, body, 0)


RET_INPUTS = 7
WKV_INPUTS = 14


def _mixers_kernel(*refs, ret, wkv):
    it = iter(refs)
    take = lambda n: [next(it) for _ in range(n)]
    ret_s0, = take(1)
    ret_in = take(RET_INPUTS)
    wkv_sh0, wkv_s0 = take(2)
    wkv_in = take(WKV_INPUTS)
    ret_o, ret_s, wkv_o, wkv_s, wkv_sh = take(5)
    scratch = take(WKV_SCRATCH)

    @pl.when(pl.program_id(1) == 0)
    def _():
        ret_s[...] = ret_s0[...]
        wkv_s[...] = wkv_s0[...]
        wkv_sh[...] = wkv_sh0[...]

    _ret_body(*ret_in, ret_o, ret_s, **ret)
    _wkv_body(*wkv_in, wkv_o, wkv_s, wkv_sh, *scratch, **wkv)


def _mixers(zr, zw, ret_s0, wkv_sh0, wkv_s0, pos, p, tl):
    nb, t, _ = zr.shape
    bb, tt = tl.seq_block, tl.time_block
    rows = bb * tt
    chunk = math.gcd(t, RET_CHUNK)
    cos, sin = _rope_tables(pos)
    dmask, q_dec, k_dec, c_dec = _ret_consts(chunk)
    sub = min(t, WKV_BLOCK)
    step = jnp.arange(WKV_BLOCK)
    tri = ((step[:, None] >= step[None, :]) & (step[:, None] // sub == step[None, :] // sub)).astype(BF16)
    lane_head = jnp.arange(RWKV_W // 2) // RWKV_HD
    seg = (lane_head[:, None] == lane_head[None, :]).astype(BF16)

    tile = lambda width: pl.BlockSpec((bb, tt, width), lambda b, j: (b, j, 0))
    ret_st = pl.BlockSpec((bb, RET_HEADS, RET_HD, RET_HD), lambda b, j: (b, 0, 0, 0))
    wkv_st = pl.BlockSpec((bb,) + wkv_s0.shape[1:], lambda b, j: (b, 0, 0, 0))
    sh_spec = pl.BlockSpec((bb, 1, RWKV_IN_W), lambda b, j: (b, 0, 0))
    tab_spec = pl.BlockSpec((tt, RET_HD), lambda b, j: (j, 0))
    vec = _full((1, RWKV_W))
    vmem = _vmem(_nbytes((rows, 5 * RET_W + RWKV_IN_W + RWKV_W)) + 2 * _nbytes((bb, RET_HEADS, RET_HD, RET_HD))
                 + 2 * _nbytes((bb, RWKV_HEADS, RWKV_HD, RWKV_HD)) + 2 * _nbytes((bb, 8, RWKV_IN_W))
                 + 2 * _nbytes((tt, RET_HD)),
                 _nbytes(dmask.shape) + 3 * _nbytes(q_dec.shape) + 9 * _nbytes((8, RWKV_IN_W))
                 + _nbytes((DECAY_LORA + AAA_LORA + GATE_LORA + WKV_BLOCK, RWKV_W), BF16) + _nbytes(seg.shape, BF16)
                 + WKV_SCRATCH * _nbytes((rows, RWKV_W)),
                 4 * _nbytes((rows, RWKV_IN_W)) + 8 * _nbytes((rows, RWKV_W))
                 + tl.wkv_group * 32 * _nbytes((WKV_BLOCK, RWKV_IN_W)) + tl.ret_group * 8 * _nbytes((chunk, 4 * RET_W)))
    return pl.pallas_call(
        functools.partial(_mixers_kernel,
                          ret=dict(bb=bb, tt=tt, chunk=chunk, group=tl.ret_group),
                          wkv=dict(bb=bb, tt=tt, sub=sub, group=tl.wkv_group)),
        grid=(nb // bb, t // tt),
        in_specs=[ret_st, tile(4 * RET_W), tab_spec, tab_spec,
                  _full(dmask.shape), _full(q_dec.shape), _full(k_dec.shape), _full(c_dec.shape),
                  sh_spec, wkv_st, tile(RWKV_IN_W),
                  _full((1, RWKV_IN_W)), vec, vec, vec, vec, vec, vec, vec,
                  _full((DECAY_LORA, RWKV_W)), _full((AAA_LORA, RWKV_W)), _full((GATE_LORA, RWKV_W)),
                  _full(tri.shape), _full(seg.shape)],
        out_specs=[tile(RET_W), ret_st, tile(RWKV_W), wkv_st, sh_spec],
        out_shape=[jax.ShapeDtypeStruct((nb, t, RET_W), F32), jax.ShapeDtypeStruct(ret_s0.shape, F32),
                   jax.ShapeDtypeStruct((nb, t, RWKV_W), F32), jax.ShapeDtypeStruct(wkv_s0.shape, F32),
                   jax.ShapeDtypeStruct(wkv_sh0.shape, F32)],
        scratch_shapes=[pltpu.VMEM((rows, RWKV_W), F32)] * WKV_SCRATCH,
        compiler_params=_cparams(("parallel", "arbitrary"), vmem),
        name="mixers",
    )(ret_s0, zr, cos, sin, dmask, q_dec, k_dec, c_dec,
      wkv_sh0, wkv_s0, zw, p["mu"], p["w0"], p["a0"], p["k_k"], p["k_a"], p["r_k"], p["lnx_g"], p["lnx_b"],
      p["w_decay_up"], p["w_a_up"], p["w_g_up"], tri, seg)


def _merge_rows(x, zg, oa, ob, wa_ref, wb_ref, wo_ref):
    ga = jax.nn.sigmoid(zg[:, 0:D_MODEL])
    gb = jax.nn.sigmoid(zg[:, D_MODEL:GATE_W])
    merged = ga * _nn(oa, wa_ref[...]) + gb * _nn(ob, wb_ref[...])
    return x + _nn(merged, wo_ref[...])


def _merge_kernel(x_ref, zg_ref, oa_ref, ob_ref, wa_ref, wb_ref, wo_ref, o_ref):
    o_ref[...] = _merge_rows(x_ref[...], zg_ref[...], oa_ref[...], ob_ref[...], wa_ref, wb_ref, wo_ref)


def _merge(x, zg, o_ret, o_rwkv, wa, wb, wo):
    n = x.shape[0]
    tm = min(ROW_TILE, n)
    row = lambda w: pl.BlockSpec((tm, w), lambda i: (i, 0))
    vmem = _vmem(_nbytes((tm, 2 * D_MODEL + GATE_W + RET_W + RWKV_W)), _nbytes((2 * D_MODEL, D_MODEL), BF16),
                 6 * _nbytes((tm, D_MODEL)))
    return pl.pallas_call(
        _merge_kernel,
        grid=(n // tm,),
        in_specs=[row(D_MODEL), row(GATE_W), row(RET_W), row(RWKV_W),
                  _full((RET_W, D_MODEL)), _full((RWKV_W, D_MODEL)), _full((D_MODEL, D_MODEL))],
        out_specs=row(D_MODEL),
        out_shape=jax.ShapeDtypeStruct((n, D_MODEL), F32),
        compiler_params=_cparams(("parallel",), vmem),
        name="merge",
    )(x, zg, o_ret, o_rwkv, wa, wb, wo)


def _softmax(sc):
    e = jnp.exp(sc - jnp.max(sc, axis=-1, keepdims=True))
    return e / jnp.sum(e, axis=-1, keepdims=True)


def _attend_heads(q_scr, mk_ref, mv_ref, ox_scr, b, r0, tq):
    for h in range(X_HEADS):
        sl = slice(h * X_HD, (h + 1) * X_HD)
        att = _softmax(_nt(q_scr[pl.ds(r0, tq), sl], mk_ref[b, :, sl]) * (X_HD ** -0.5))
        ox_scr[pl.ds(r0, tq), sl] = _nn(att, mv_ref[b, :, sl])


def _attend_tiles(q_scr, mk_ref, mv_ref, ox_scr, seqs, tq):
    halves = X_HD // LANES
    n = X_HEADS * tq
    slot = lax.broadcasted_iota(jnp.int32, (n, MEM_ROWS), 1) % ROWS_PER_TOKEN
    head = lax.broadcasted_iota(jnp.int32, (n, MEM_ROWS), 0) // tq
    own = slot == head
    other = slot == head + X_HEADS
    qs = []
    for b in seqs:
        q = q_scr[b * tq:(b + 1) * tq, :]
        qs.append(jnp.concatenate([q[:, h * X_HD + c * LANES:h * X_HD + (c + 1) * LANES]
                                   for c in range(halves) for h in range(X_HEADS)], axis=0))
    z = [_nt(q, mk_ref[b]) for q, b in zip(qs, seqs)]
    part = [jnp.where(own, zi[:n], 0.0) + jnp.where(other, zi[n:], 0.0) for zi in z]
    sc = [p + pltpu.roll(p, MEM_ROWS - X_HEADS, axis=1) for p in part]
    att = [_softmax(jnp.where(own, s * (X_HD ** -0.5), -jnp.inf)) for s in sc]
    att2 = [jnp.concatenate([a, pltpu.roll(a, X_HEADS, axis=1)], axis=0) for a in att]
    o = [_nn(a, mv_ref[b]) for a, b in zip(att2, seqs)]
    for oi, b in zip(o, seqs):
        ox_scr[b * tq:(b + 1) * tq, :] = jnp.concatenate(
            [oi[(c * X_HEADS + h) * tq:(c * X_HEADS + h + 1) * tq] for h in range(X_HEADS) for c in range(halves)],
            axis=1)


def _cross_kernel(x_ref, mk_ref, mv_ref, g_ref, wq_ref, wo_ref, o_ref, q_scr, ox_scr, *, bb, tq):
    rows = bb * tq
    x = x_ref[...].reshape(rows, D_MODEL)
    q_scr[...] = _nn(_rms(x, g_ref[...]), wq_ref[...])
    _attend_tiles(q_scr, mk_ref, mv_ref, ox_scr, range(bb), tq)
    o_ref[...] = (x + _nn(ox_scr[...], wo_ref[...])).reshape(bb, tq, D_MODEL)


def _cross(x, mem_k, mem_v, g, wq, wo, bb, tq):
    nb, t, _ = x.shape
    x_spec = pl.BlockSpec((bb, tq, D_MODEL), lambda b, j: (b, j, 0))
    m_spec = pl.BlockSpec((bb, MEM_ROWS, LANES), lambda b, j: (b, 0, 0))
    vmem = _vmem(2 * _nbytes((bb, tq, D_MODEL)) + 2 * _nbytes((bb, MEM_ROWS, LANES)),
                 2 * _nbytes((D_MODEL, D_MODEL), BF16) + 2 * _nbytes((bb * tq, D_MODEL)),
                 6 * _nbytes((bb * tq, D_MODEL)) + bb * 8 * _nbytes((2 * X_HEADS * tq, MEM_ROWS)))
    return pl.pallas_call(
        functools.partial(_cross_kernel, bb=bb, tq=tq),
        grid=(nb // bb, t // tq),
        in_specs=[x_spec, m_spec, m_spec, _full((1, D_MODEL)), _full((D_MODEL, D_MODEL)), _full((D_MODEL, D_MODEL))],
        out_specs=x_spec,
        out_shape=jax.ShapeDtypeStruct(x.shape, F32),
        scratch_shapes=[pltpu.VMEM((bb * tq, D_MODEL), F32), pltpu.VMEM((bb * tq, D_MODEL), F32)],
        compiler_params=_cparams(("parallel", "arbitrary"), vmem),
        name="cross_attn",
    )(x, mem_k, mem_v, g, wq, wo)


MLP_FF_CHUNK = 1024


def _mlp_rows(x, g_ref, wu_ref, wd_ref, gf_ref, final_norm):
    h = _rms(x, g_ref[...]).astype(BF16)
    acc = x
    for c in range(0, D_FF, MLP_FF_CHUNK):
        u = jnp.maximum(jnp.dot(h, wu_ref[:, c:c + MLP_FF_CHUNK], preferred_element_type=F32), 0.0)
        acc = acc + _nn(u * u, wd_ref[c:c + MLP_FF_CHUNK, :])
    return _rms(acc, gf_ref[...]) if final_norm else acc


def _mlp_kernel(x_ref, g_ref, wu_ref, wd_ref, gf_ref, o_ref, *, final_norm):
    o_ref[...] = _mlp_rows(x_ref[...], g_ref, wu_ref, wd_ref, gf_ref, final_norm)


def _mlp(x, g, w_up, w_down, g_final, final_norm):
    n = x.shape[0]
    tm = min(ROW_TILE, n)
    row = pl.BlockSpec((tm, D_MODEL), lambda i: (i, 0))
    vmem = _vmem(2 * _nbytes((tm, D_MODEL)), 2 * _nbytes((D_MODEL, D_FF), BF16),
                 4 * _nbytes((tm, MLP_FF_CHUNK)) + 4 * _nbytes((tm, D_MODEL)))
    return pl.pallas_call(
        functools.partial(_mlp_kernel, final_norm=final_norm),
        grid=(n // tm,),
        in_specs=[row, _full((1, D_MODEL)), _full((D_MODEL, D_FF)), _full((D_FF, D_MODEL)), _full((1, D_MODEL))],
        out_specs=row,
        out_shape=jax.ShapeDtypeStruct((n, D_MODEL), F32),
        compiler_params=_cparams(("parallel",), vmem),
        name="mlp",
    )(x, g, w_up, w_down, g_final)


POST_TILE = 512


def _post_kernel(x_ref, zg_ref, oa_ref, ob_ref, mk_ref, mv_ref, wa_ref, wb_ref, wo_ref, gc_ref, wq_ref, wco_ref,
                 gm_ref, wu_ref, wd_ref, gf_ref, o_ref, q_scr, ox_scr, *, tq, final_norm):
    x1 = _merge_rows(x_ref[0], zg_ref[0], oa_ref[0], ob_ref[0], wa_ref, wb_ref, wo_ref)
    q_scr[...] = _nn(_rms(x1, gc_ref[...]), wq_ref[...])
    _attend_heads(q_scr, mk_ref, mv_ref, ox_scr, 0, 0, tq)
    x2 = x1 + _nn(ox_scr[...], wco_ref[...])
    o_ref[0] = _mlp_rows(x2, gm_ref, wu_ref, wd_ref, gf_ref, final_norm)


def _post(x, zg, o_ret, o_rwkv, mem_k, mem_v, w, g_final, final_norm):
    nb, t, _ = x.shape
    tq = min(POST_TILE, t)
    row = lambda width: pl.BlockSpec((1, tq, width), lambda b, j: (b, j, 0))
    mem = pl.BlockSpec((1, MEM_LEN, D_MODEL), lambda b, j: (b, 0, 0))
    sq = _full((D_MODEL, D_MODEL))
    vec = _full((1, D_MODEL))
    vmem = _vmem(_nbytes((tq, 2 * D_MODEL + GATE_W + RET_W + RWKV_W)) + 2 * _nbytes((MEM_LEN, D_MODEL), BF16),
                 _nbytes((4 * D_MODEL + 2 * D_FF, D_MODEL), BF16) + 2 * _nbytes((tq, D_MODEL)),
                 8 * _nbytes((tq, D_MODEL)) + 4 * _nbytes((tq, MLP_FF_CHUNK)))
    return pl.pallas_call(
        functools.partial(_post_kernel, tq=tq, final_norm=final_norm),
        grid=(nb, t // tq),
        in_specs=[row(D_MODEL), row(GATE_W), row(RET_W), row(RWKV_W), mem, mem,
                  _full((RET_W, D_MODEL)), _full((RWKV_W, D_MODEL)), sq, vec, sq, sq,
                  vec, _full((D_MODEL, D_FF)), _full((D_FF, D_MODEL)), vec],
        out_specs=row(D_MODEL),
        out_shape=jax.ShapeDtypeStruct(x.shape, F32),
        scratch_shapes=[pltpu.VMEM((tq, D_MODEL), F32), pltpu.VMEM((tq, D_MODEL), F32)],
        compiler_params=_cparams(("parallel", "arbitrary"), vmem),
        name="post",
    )(x, zg, o_ret, o_rwkv, mem_k, mem_v, w["w_branch_a"], w["w_branch_b"], w["w_out"], w["g_cross"], w["w_cq"],
      w["w_co"], w["g_mlp"], w["w_up"], w["w_down"], g_final)


class _Tiling(NamedTuple):
    seq_block: int
    time_block: int
    cross_block: int
    ret_group: int
    wkv_group: int
    pack_state: bool


def _tiling(nb, t):
    if t > RET_CHUNK:
        tt = min(ROW_TILE, t)
        return _Tiling(seq_block=1, time_block=tt, cross_block=1, pack_state=False,
                       ret_group=math.gcd(4, tt // RET_CHUNK), wkv_group=math.gcd(8, tt // WKV_BLOCK))
    seqs = WKV_BLOCK // t
    return _Tiling(seq_block=seqs, time_block=t, cross_block=math.gcd(4, nb), ret_group=seqs, wkv_group=1,
                   pack_state=True)


def _layer(x, pos, s_ret, s_wkv, s_shift, mem_k, mem_v, w, g_final, final_norm):
    nb, t, _ = x.shape
    n = nb * t
    tl = _tiling(nb, t)
    xf = x.reshape(n, D_MODEL)
    zg, zr, zw = _in_proj(xf, w["g_mix"], w["w_in"])
    o_ret, s_ret_new, o_wkv, s_wkv_new, shift_new = _mixers(
        zr.reshape(nb, t, 4 * RET_W), zw.reshape(nb, t, RWKV_IN_W), s_ret, s_shift.reshape(nb, 1, RWKV_IN_W),
        _pack_heads(s_wkv) if tl.pack_state else s_wkv, pos, w, tl)
    if tl.pack_state:
        s_wkv_new = _unpack_heads(s_wkv_new)
    if mem_k.shape[1:] == (MEM_LEN, D_MODEL):
        y = _post(x, zg.reshape(nb, t, GATE_W), o_ret, o_wkv, mem_k, mem_v, w, g_final, final_norm)
    else:
        x1 = _merge(xf, zg, o_ret.reshape(n, RET_W), o_wkv.reshape(n, RWKV_W),
                    w["w_branch_a"], w["w_branch_b"], w["w_out"])
        x2 = _cross(x1.reshape(nb, t, D_MODEL), mem_k, mem_v, w["g_cross"], w["w_cq"], w["w_co"],
                    tl.cross_block, tl.time_block)
        y = _mlp(x2.reshape(n, D_MODEL), w["g_mlp"], w["w_up"], w["w_down"], g_final, final_norm).reshape(x.shape)
    return y, s_ret_new, s_wkv_new, shift_new.reshape(nb, RWKV_IN_W)


def _layer_weights(l, g_mix, w_in, w_branch_a, w_branch_b, w_out, mu_shift, w0, w_decay_up, a0, w_a_up, w_g_up,
                   k_k, k_a, r_k, lnx_g, lnx_b, g_cross, w_cq, w_co, g_mlp, w_up, w_down):
    vec = lambda v: v[l].reshape(1, -1).astype(F32)
    mat = lambda m: m[l].astype(BF16)
    return dict(g_mix=vec(g_mix), w_in=mat(w_in), w_branch_a=mat(w_branch_a), w_branch_b=mat(w_branch_b),
                w_out=mat(w_out), mu=vec(mu_shift), w0=vec(w0), w_decay_up=mat(w_decay_up), a0=vec(a0),
                w_a_up=mat(w_a_up), w_g_up=mat(w_g_up), k_k=vec(k_k), k_a=vec(k_a), r_k=vec(r_k), lnx_g=vec(lnx_g),
                lnx_b=vec(lnx_b), g_cross=vec(g_cross), w_cq=mat(w_cq), w_co=mat(w_co), g_mlp=vec(g_mlp),
                w_up=mat(w_up), w_down=mat(w_down))


def kernel(x_prompt, x_sample, mem_prompt, state_ret, state_wkv, state_shift, cache_mem_k, cache_mem_v, g_mix, w_in, w_branch_a, w_branch_b, w_out, mu_shift, w0, w_decay_up, a0, w_a_up, w_g_up, k_k, k_a, r_k, lnx_g, lnx_b, g_cross, g_mem, w_cq, w_ck, w_cv, w_co, g_mlp, w_up, w_down, g_final):
    depth = w_in.shape[0]
    bp, tp, _ = x_prompt.shape
    bs, ts, _ = x_sample.shape
    pos_p = jnp.arange(tp, dtype=jnp.int32)
    pos_s = PAST_LEN + jnp.arange(ts, dtype=jnp.int32)
    gf = g_final.reshape(1, D_MODEL)

    xp, xs = x_prompt, x_sample
    outs_p, outs_s = [], []
    for l in range(depth):
        w = _layer_weights(l, g_mix, w_in, w_branch_a, w_branch_b, w_out, mu_shift, w0, w_decay_up, a0, w_a_up,
                           w_g_up, k_k, k_a, r_k, lnx_g, lnx_b, g_cross, w_cq, w_co, g_mlp, w_up, w_down)
        last = l == depth - 1
        k_tiles, v_tiles, k_att, v_att = _mem_kv(mem_prompt.reshape(bp * MEM_LEN, D_MODEL),
                                                 g_mem[l].reshape(1, D_MODEL), w_ck[l].astype(BF16), w_cv[l].astype(BF16))
        xp, sr, sw, ss = _layer(
            xp, pos_p,
            jnp.zeros((bp, RET_HEADS, RET_HD, RET_HD), F32), jnp.zeros((bp, RWKV_HEADS, RWKV_HD, RWKV_HD), F32),
            jnp.zeros((bp, RWKV_IN_W), F32), k_att.reshape(bp, MEM_LEN, D_MODEL), v_att.reshape(bp, MEM_LEN, D_MODEL),
            w, gf, last)
        outs_p.append((sr, sw, ss, _from_tile_order(k_tiles.reshape(bp, MEM_ROWS, LANES)),
                       _from_tile_order(v_tiles.reshape(bp, MEM_ROWS, LANES))))
        xs, sr2, sw2, ss2 = _layer(
            xs, pos_s, state_ret[l], state_wkv[l], state_shift[l],
            _tile_order(cache_mem_k[l]), _tile_order(cache_mem_v[l]), w, gf, last)
        outs_s.append((sr2, sw2, ss2))

    stack = lambda items, i: jnp.stack([it[i] for it in items])
    return (xp, xs, stack(outs_p, 0), stack(outs_p, 1), stack(outs_p, 2), stack(outs_p, 3), stack(outs_p, 4),
            stack(outs_s, 0), stack(outs_s, 1), stack(outs_s, 2))
```

```python
import functools
import math
from typing import NamedTuple

import jax
import jax.numpy as jnp
from jax import lax
from jax.experimental import pallas as pl
from jax.experimental.pallas import tpu as pltpu

F32 = jnp.float32
BF16 = jnp.bfloat16

D_MODEL = 1024
PAST_LEN = 16384
RET_HEADS = 4
RET_HD = 128
RET_W = RET_HEADS * RET_HD
RET_CHUNK = 128
RET_GN_EPS = 1e-5
ROPE_BASE = 10000.0
RWKV_HEADS = 8
RWKV_HD = 64
RWKV_W = RWKV_HEADS * RWKV_HD
DECAY_LORA = 64
AAA_LORA = 64
GATE_LORA = 128
RWKV_GN_EPS = 64e-5
RWKV_IN_W = 3 * RWKV_W + DECAY_LORA + AAA_LORA + GATE_LORA
GATE_W = 2 * D_MODEL
O_RET = 2 * D_MODEL
O_RWKV = O_RET + 4 * RET_W
IN_W = O_RWKV + RWKV_IN_W
MEM_LEN = 256
X_HEADS = 4
X_HD = D_MODEL // X_HEADS
D_FF = 4 * D_MODEL
RMS_EPS = 1e-6

V7X_VMEM_BYTES = 64 * 1024 * 1024
VMEM_CAP_BYTES = V7X_VMEM_BYTES - 8 * 1024 * 1024
ROW_TILE = 512


def _cparams(sem, vmem_bytes):
    return pltpu.CompilerParams(dimension_semantics=sem, vmem_limit_bytes=int(min(vmem_bytes, VMEM_CAP_BYTES)))


def _nbytes(shape, dtype=F32):
    return math.prod(shape) * jnp.dtype(dtype).itemsize


def _nn(a, b):
    return jnp.dot(a.astype(BF16), b.astype(BF16), preferred_element_type=F32)


def _nt(a, b):
    return lax.dot_general(a.astype(BF16), b.astype(BF16), (((1,), (1,)), ((), ())), preferred_element_type=F32)


def _tn(a, b):
    return lax.dot_general(a.astype(BF16), b.astype(BF16), (((0,), (0,)), ((), ())), preferred_element_type=F32)


def _rms(x, g):
    return x * lax.rsqrt(jnp.mean(x * x, axis=-1, keepdims=True) + RMS_EPS) * g


def _head_norms(blocks, eps):
    rows, width = blocks[0].shape
    ones = jnp.ones((width, width), BF16)
    o = jnp.concatenate(blocks, axis=0)
    d = o - _nn(o, ones) * (1.0 / width)
    out = d * lax.rsqrt(_nn(d * d, ones) * (1.0 / width) + eps)
    return [out[i * rows:(i + 1) * rows] for i in range(len(blocks))]


def _full(shape):
    zeros = (0,) * len(shape)
    return pl.BlockSpec(shape, lambda *_: zeros, pipeline_mode=pl.Buffered(1))


def _vmem(pipelined, resident, temps):
    return 2 * pipelined + resident + temps


def _in_proj_kernel(x_ref, g_ref, w_ref, zg_ref, zr_ref, zw_ref):
    h = _rms(x_ref[...], g_ref[...]).astype(BF16)
    zg_ref[...] = jnp.dot(h, w_ref[:, 0:O_RET], preferred_element_type=F32)
    zr_ref[...] = jnp.dot(h, w_ref[:, O_RET:O_RWKV], preferred_element_type=F32)
    zw_ref[...] = jnp.dot(h, w_ref[:, O_RWKV:IN_W], preferred_element_type=F32)


def _in_proj(x, g, w_in):
    n = x.shape[0]
    tm = min(ROW_TILE, n)
    vmem = _vmem(_nbytes((tm, D_MODEL + IN_W)), _nbytes((D_MODEL, IN_W), BF16), _nbytes((tm, 4 * RET_W + D_MODEL)))
    return pl.pallas_call(
        _in_proj_kernel,
        grid=(n // tm,),
        in_specs=[pl.BlockSpec((tm, D_MODEL), lambda i: (i, 0)), _full((1, D_MODEL)), _full((D_MODEL, IN_W))],
        out_specs=[pl.BlockSpec((tm, GATE_W), lambda i: (i, 0)),
                   pl.BlockSpec((tm, 4 * RET_W), lambda i: (i, 0)),
                   pl.BlockSpec((tm, RWKV_IN_W), lambda i: (i, 0))],
        out_shape=[jax.ShapeDtypeStruct((n, GATE_W), F32), jax.ShapeDtypeStruct((n, 4 * RET_W), F32),
                   jax.ShapeDtypeStruct((n, RWKV_IN_W), F32)],
        compiler_params=_cparams(("parallel",), vmem),
        name="in_proj",
    )(x, g, w_in)


LANES = 128
MEM_ROWS = MEM_LEN * X_HEADS * (X_HD // LANES)
ROWS_PER_TOKEN = MEM_ROWS // MEM_LEN


def _tile_order(mem):
    nb = mem.shape[0]
    return (mem.reshape(nb, MEM_LEN, X_HEADS, X_HD // LANES, LANES).transpose(0, 1, 3, 2, 4)
            .reshape(nb, MEM_ROWS, LANES))


def _from_tile_order(raw):
    nb = raw.shape[0]
    return (raw.reshape(nb, MEM_LEN, X_HD // LANES, X_HEADS, LANES).transpose(0, 1, 3, 2, 4)
            .reshape(nb, MEM_LEN, X_HEADS, X_HD))


def _mem_kv_kernel(x_ref, g_ref, wk_ref, wv_ref, kt_ref, vt_ref, kb_ref, vb_ref, *, tm):
    h = _rms(x_ref[...], g_ref[...]).astype(BF16)
    for w_ref, t_ref, b_ref in ((wk_ref, kt_ref, kb_ref), (wv_ref, vt_ref, vb_ref)):
        y = jnp.dot(h, w_ref[...], preferred_element_type=F32)
        b_ref[...] = y.astype(BF16)
        for hd in range(X_HEADS):
            for c in range(X_HD // LANES):
                col = hd * X_HD + c * LANES
                t_ref[pl.ds(c * X_HEADS + hd, tm, stride=ROWS_PER_TOKEN), :] = y[:, col:col + LANES]


def _mem_kv(mem, g, wk, wv):
    n = mem.shape[0]
    tm = min(ROW_TILE, n)
    vmem = _vmem(4 * _nbytes((tm, D_MODEL)), 2 * _nbytes((D_MODEL, D_MODEL), BF16), 4 * _nbytes((tm, D_MODEL)))
    row = pl.BlockSpec((tm, D_MODEL), lambda i: (i, 0))
    tile = pl.BlockSpec((tm * ROWS_PER_TOKEN, LANES), lambda i: (i, 0))
    return pl.pallas_call(
        functools.partial(_mem_kv_kernel, tm=tm),
        grid=(n // tm,),
        in_specs=[row, _full((1, D_MODEL)), _full((D_MODEL, D_MODEL)), _full((D_MODEL, D_MODEL))],
        out_specs=[tile, tile, row, row],
        out_shape=[jax.ShapeDtypeStruct((n * ROWS_PER_TOKEN, LANES), F32)] * 2
        + [jax.ShapeDtypeStruct((n, D_MODEL), BF16)] * 2,
        compiler_params=_cparams(("parallel",), vmem),
        name="mem_kv",
    )(mem, g, wk, wv)


def _ret_body(zr_ref, cos_ref, sin_ref, dm_ref, qd_ref, kd_ref, cd_ref, o_ref, s_ref, *, bb, tt, chunk, group):
    nch = tt // chunk
    assert bb == 1 or nch == 1, "a block holds either one sequence or one chunk per sequence"
    assert (bb * nch) % group == 0
    heads = range(RET_HEADS)

    def body(j, carry):
        if nch == 1:
            items = [(j * group + n, 0) for n in range(group)]
        else:
            starts = [(j * group + n) * chunk for n in range(group)]
            items = [(0, r if isinstance(r, int) else pl.multiple_of(r, chunk)) for r in starts]
        q, k, v = [], [], []
        for b, r0 in items:
            rows = pl.ds(r0, chunk)
            cos = cos_ref[rows, :]
            sin = sin_ref[rows, :]
            for h in heads:
                qh = zr_ref[b, rows, h * RET_HD:(h + 1) * RET_HD]
                kh = zr_ref[b, rows, RET_W + h * RET_HD:RET_W + (h + 1) * RET_HD]
                q.append(qh * cos + pltpu.roll(qh, RET_HD // 2, axis=1) * sin)
                k.append((kh * cos + pltpu.roll(kh, RET_HD // 2, axis=1) * sin) * (RET_HD ** -0.5))
                v.append(zr_ref[b, rows, 2 * RET_W + h * RET_HD:2 * RET_W + (h + 1) * RET_HD])
        hd = [h for _ in items for h in heads]
        sc = [_nt(qi, ki) * dm_ref[h] for qi, ki, h in zip(q, k, hd)]
        kv = [_tn(ki * kd_ref[h], vi) for ki, vi, h in zip(k, v, hd)]
        inner = [_nn(si, vi) for si, vi in zip(sc, v)]
        states = []
        for n, (b, _) in enumerate(items):
            for h in heads:
                s = s_ref[b, h] if (nch == 1 or n == 0) else states[-RET_HEADS] * cd_ref[h] + kv[(n - 1) * RET_HEADS + h]
                states.append(s)
        last = len(items) - 1
        for n, (b, _) in enumerate(items):
            if nch == 1 or n == last:
                for h in heads:
                    i = n * RET_HEADS + h
                    s_ref[b, h] = states[i] * cd_ref[h] + kv[i]
        cross = [_nn(qi, si) * qd_ref[h] for qi, si, h in zip(q, states, hd)]
        normed = _head_norms([a + c for a, c in zip(inner, cross)], RET_GN_EPS)
        for n, (b, r0) in enumerate(items):
            rows = pl.ds(r0, chunk)
            for h in heads:
                g = zr_ref[b, rows, 3 * RET_W + h * RET_HD:3 * RET_W + (h + 1) * RET_HD]
                o_ref[b, rows, h * RET_HD:(h + 1) * RET_HD] = normed[n * RET_HEADS + h] * (g * jax.nn.sigmoid(g))
        return carry

    if bb * nch == group:
        body(0, 0)
    else:
        lax.fori_loop(0, bb * nch // group, body, 0)


def _ret_consts(chunk):
    lg = jnp.log1p(-jnp.exp2(-5.0 - jnp.arange(RET_HEADS, dtype=F32)))
    idx = jnp.arange(chunk, dtype=F32)
    diff = idx[:, None] - idx[None, :]
    dmask = jnp.where(diff[None] >= 0, jnp.exp(jnp.maximum(diff, 0.0)[None] * lg[:, None, None]), 0.0)
    q_dec = jnp.exp((idx + 1.0)[None, :] * lg[:, None])
    k_dec = jnp.exp((chunk - 1.0 - idx)[None, :] * lg[:, None])
    c_dec = jnp.exp(chunk * lg)
    bc = lambda t: jnp.broadcast_to(t[:, :, None], (RET_HEADS, t.shape[1], RET_HD))
    return dmask, bc(q_dec), bc(k_dec), bc(c_dec[:, None])


def _rope_tables(pos):
    half = RET_HD // 2
    inv = 1.0 / (ROPE_BASE ** (jnp.arange(half, dtype=F32) / half))
    ang = pos.astype(F32)[:, None] * inv[None, :]
    cos, sin = jnp.cos(ang), jnp.sin(ang)
    return jnp.concatenate([cos, cos], axis=1), jnp.concatenate([-sin, sin], axis=1)


WKV_BLOCK = 64
WKV_SCRATCH = 10


def _pow2(n):
    return n & (n - 1) == 0


def _imod(x, n):
    return jnp.bitwise_and(x, n - 1) if _pow2(n) else lax.rem(x, n)


def _idiv(x, n):
    return jnp.right_shift(x, n.bit_length() - 1) if _pow2(n) else lax.div(x, n)


def _wkv_body(zw_ref, mu_ref, w0_ref, a0_ref, kk_ref, ka_ref, rk_ref, lng_ref, lnb_ref,
              wd_ref, wa_ref, wg_ref, tri_ref, seg_ref, o_ref, s_ref, sh_ref,
              at_s, rt_s, bt_s, kt_s, bh_s, kh_s, v_s, g_s, bonus_s, pend_s, *, bb, tt, sub, group):
    C = WKV_BLOCK
    R = bb * tt
    nblk = R // C
    nseq = C // sub
    assert (bb == 1 and sub == C) or (nblk == 1 and sub == tt), "one sequence per tile, or whole sequences in one block"
    assert nblk % group == 0
    n_sq = max(int(math.log2(sub)) - 1, 0)
    heads = range(RWKV_HEADS)
    hsl = [slice(h * RWKV_HD, (h + 1) * RWKV_HD) for h in heads]
    half = RWKV_W // 2

    def seg_sum(t):
        seg = seg_ref[...]
        return jnp.concatenate([_nn(t[:, :half], seg), _nn(t[:, half:], seg)], axis=1)

    pw = zw_ref[...].reshape(R, RWKV_IN_W)
    row = lax.broadcasted_iota(jnp.int32, (R, RWKV_IN_W), 0)
    rolled = pltpu.roll(pw, 1, axis=0)
    if bb == 1:
        prev = jnp.where(row == 0, sh_ref[0], rolled)
        sh_ref[0] = pw[R - 1:R, :]
    else:
        carry = jnp.broadcast_to(sh_ref[...], (bb, tt, RWKV_IN_W)).reshape(R, RWKV_IN_W)
        prev = jnp.where(_imod(row, tt) == 0, carry, rolled)
        sh_ref[...] = pw.reshape(bb, tt, RWKV_IN_W)[:, tt - 1:tt, :]
    xm = pw + mu_ref[...] * (prev - pw)
    r = xm[:, 0:RWKV_W]
    kb = xm[:, RWKV_W:2 * RWKV_W]
    o1 = 3 * RWKV_W
    wl = xm[:, o1:o1 + DECAY_LORA]
    al = xm[:, o1 + DECAY_LORA:o1 + DECAY_LORA + AAA_LORA]
    gl = xm[:, o1 + DECAY_LORA + AAA_LORA:RWKV_IN_W]
    logw = -math.exp(-0.5) * jax.nn.sigmoid(w0_ref[...] + _nn(jnp.tanh(wl), wd_ref[...]))
    a = jax.nn.sigmoid(a0_ref[...] + _nn(al, wa_ref[...]))
    l1 = logw.astype(BF16)
    rem = logw - l1.astype(F32)
    l2 = rem.astype(BF16)
    l3 = (rem - l2.astype(F32)).astype(BF16)
    tri = tri_ref[...]
    cum = jnp.concatenate(
        [sum(jnp.dot(tri, part[k * C:(k + 1) * C], preferred_element_type=F32) for part in (l1, l2, l3))
         for k in range(nblk)], axis=0)
    cum_end = jnp.broadcast_to(cum.reshape(R // sub, sub, RWKV_W)[:, sub - 1:sub, :],
                               (R // sub, sub, RWKV_W)).reshape(R, RWKV_W)
    kk = kb * kk_ref[...]
    kk = kk * jnp.minimum(lax.rsqrt(seg_sum(kk * kk)), 1e12)
    km = kb * (1.0 + (a - 1.0) * ka_ref[...])
    bv = kk * a
    p_end = jnp.exp(cum_end)
    e_neg = jnp.exp(-cum)
    e_end = p_end * e_neg
    at_s[...] = -kk * jnp.exp(cum - logw)
    rt_s[...] = r * jnp.exp(cum)
    bt_s[...] = bv * e_neg
    kt_s[...] = km * e_neg
    bh_s[...] = bv * e_end
    kh_s[...] = km * e_end
    v_s[...] = xm[:, 2 * RWKV_W:3 * RWKV_W]
    g_s[...] = _nn(jax.nn.sigmoid(gl), wg_ref[...])
    bonus_s[...] = seg_sum(r * km * rk_ref[...])
    pend_s[...] = p_end

    row_id = lax.broadcasted_iota(jnp.int32, (C, C), 0)
    col_id = lax.broadcasted_iota(jnp.int32, (C, C), 1)
    strict = row_id > col_id
    incl = row_id >= col_id
    if nseq > 1:
        same = _idiv(row_id, sub) == _idiv(col_id, sub)
        strict = jnp.logical_and(strict, same)
        incl = jnp.logical_and(incl, same)
    eye = jnp.where(row_id == col_id, 1.0, 0.0).astype(F32)

    def triangular(r0s):
        rows = [pl.ds(r0, C) for r0 in r0s for _ in heads]
        sl = [s for _ in r0s for s in hsl]
        at = [at_s[rw, :][:, s] for rw, s in zip(rows, sl)]
        rt = [rt_s[rw, :][:, s] for rw, s in zip(rows, sl)]
        v = [v_s[rw, :][:, s] for rw, s in zip(rows, sl)]
        bk_t = [jnp.concatenate([bt_s[pl.ds(r0, C), :], kt_s[pl.ds(r0, C), :]], axis=0).T for r0 in r0s]
        m1 = [_nn(jnp.concatenate([a_, r_], axis=0), bk[s, :])
              for a_, r_, bk, s in zip(at, rt, [t for t in bk_t for _ in heads], sl)]
        low = [jnp.where(strict, m[:C, :C], 0.0) for m in m1]
        rab = [jnp.where(incl, m[C:, :C], 0.0) for m in m1]
        akk = [jnp.concatenate([jnp.where(strict, m[:C, C:], 0.0), jnp.where(incl, m[C:, C:], 0.0)], axis=0)
               for m in m1]
        akv = [_nn(k_, v_) for k_, v_ in zip(akk, v)]
        tinv = [eye + l for l in low]
        if n_sq:
            x = [_nn(l, l) for l in low]
        for j in range(n_sq):
            if j < n_sq - 1:
                xt = [_nn(jnp.concatenate([xi, ti], axis=0), xi) for xi, ti in zip(x, tinv)]
                x = [t[:C] for t in xt]
                tinv = [ti + t[C:] for ti, t in zip(tinv, xt)]
            else:
                tinv = [ti + _nn(ti, xi) for ti, xi in zip(tinv, x)]
        ua = [_nn(t, a_) for t, a_ in zip(tinv, at)]
        uv = [_nn(t, k_[:C]) for t, k_ in zip(tinv, akv)]
        oa = [r_ + _nn(rb, u_) for r_, rb, u_ in zip(rt, rab, ua)]
        ov = [k_[C:] + _nn(rb, u_) for k_, rb, u_ in zip(akv, rab, uv)]
        nh = RWKV_HEADS
        return [dict(ua=ua[k * nh:(k + 1) * nh], uv=uv[k * nh:(k + 1) * nh], oa=oa[k * nh:(k + 1) * nh],
                     ov=ov[k * nh:(k + 1) * nh], v=v[k * nh:(k + 1) * nh]) for k in range(len(r0s))]

    def advance(r0, q):
        rw = pl.ds(r0, C)
        bh, kh, pend = bh_s[rw, :], kh_s[rw, :], pend_s[rw, :]
        items = [(n, h) for n in range(nseq) for h in heads]
        rs = [slice(n * sub, (n + 1) * sub) for n, _ in items]
        uo = [_nt(jnp.concatenate([q["ua"][h][s], q["oa"][h][s]], axis=0), s_ref[n, h])
              for (n, h), s in zip(items, rs)]
        u = [m[:sub] + q["uv"][h][s] for m, (n, h), s in zip(uo, items, rs)]
        upd = [_tn(jnp.concatenate([u_, q["v"][h][s]], axis=0),
                   jnp.concatenate([bh[s, hsl[h]], kh[s, hsl[h]]], axis=0)) for u_, (n, h), s in zip(u, items, rs)]
        for up, (n, h), s in zip(upd, items, rs):
            s_ref[n, h] = s_ref[n, h] * pend[s, hsl[h]][0:1] + up
        o = [m[sub:] + q["ov"][h][s] for m, (n, h), s in zip(uo, items, rs)]
        o = jnp.concatenate([jnp.concatenate([o[n * RWKV_HEADS + h] for n in range(nseq)], axis=0) for h in heads],
                            axis=1)
        d = o - seg_sum(o) * (1.0 / RWKV_HD)
        var = seg_sum(d * d) * (1.0 / RWKV_HD)
        o = d * lax.rsqrt(var + RWKV_GN_EPS) * lng_ref[...] + lnb_ref[...]
        o = (o + bonus_s[rw, :] * v_s[rw, :]) * g_s[rw, :]
        if bb == 1:
            o_ref[0, rw, :] = o
        else:
            o_ref[...] = o.reshape(bb, tt, RWKV_W)

    def body(j, carry):
        r0s = [j * group * C + k * C for k in range(group)]
        r0s = [r if isinstance(r, int) else pl.multiple_of(r, C) for r in r0s]
        for r0, q in zip(r0s, triangular(r0s)):
            advance(r0, q)
        return carry

    if nblk == group:
        body(0, 0)
    else:
        lax.fori_loop(0, nblk // group, body, 0)


RET_INPUTS = 7
WKV_INPUTS = 14


def _mixers_kernel(*refs, ret, wkv):
    it = iter(refs)
    take = lambda n: [next(it) for _ in range(n)]
    ret_s0, = take(1)
    ret_in = take(RET_INPUTS)
    wkv_sh0, wkv_s0 = take(2)
    wkv_in = take(WKV_INPUTS)
    ret_o, ret_s, wkv_o, wkv_s, wkv_sh = take(5)
    scratch = take(WKV_SCRATCH)

    @pl.when(pl.program_id(1) == 0)
    def _():
        ret_s[...] = ret_s0[...]
        wkv_s[...] = wkv_s0[...]
        wkv_sh[...] = wkv_sh0[...]

    _ret_body(*ret_in, ret_o, ret_s, **ret)
    _wkv_body(*wkv_in, wkv_o, wkv_s, wkv_sh, *scratch, **wkv)


def _mixers(zr, zw, ret_s0, wkv_sh0, wkv_s0, pos, p, tl):
    nb, t, _ = zr.shape
    bb, tt = tl.seq_block, tl.time_block
    rows = bb * tt
    chunk = math.gcd(t, RET_CHUNK)
    cos, sin = _rope_tables(pos)
    dmask, q_dec, k_dec, c_dec = _ret_consts(chunk)
    sub = min(t, WKV_BLOCK)
    step = jnp.arange(WKV_BLOCK)
    tri = ((step[:, None] >= step[None, :]) & (step[:, None] // sub == step[None, :] // sub)).astype(BF16)
    lane_head = jnp.arange(RWKV_W // 2) // RWKV_HD
    seg = (lane_head[:, None] == lane_head[None, :]).astype(BF16)

    tile = lambda width: pl.BlockSpec((bb, tt, width), lambda b, j: (b, j, 0))
    ret_st = pl.BlockSpec((bb, RET_HEADS, RET_HD, RET_HD), lambda b, j: (b, 0, 0, 0))
    wkv_st = pl.BlockSpec((bb, RWKV_HEADS, RWKV_HD, RWKV_HD), lambda b, j: (b, 0, 0, 0))
    sh_spec = pl.BlockSpec((bb, 1, RWKV_IN_W), lambda b, j: (b, 0, 0))
    tab_spec = pl.BlockSpec((tt, RET_HD), lambda b, j: (j, 0))
    vec = _full((1, RWKV_W))
    vmem = _vmem(_nbytes((rows, 5 * RET_W + RWKV_IN_W + RWKV_W)) + 2 * _nbytes((bb, RET_HEADS, RET_HD, RET_HD))
                 + 2 * _nbytes((bb, RWKV_HEADS, RWKV_HD, RWKV_HD)) + 2 * _nbytes((bb, 8, RWKV_IN_W))
                 + 2 * _nbytes((tt, RET_HD)),
                 _nbytes(dmask.shape) + 3 * _nbytes(q_dec.shape) + 9 * _nbytes((8, RWKV_IN_W))
                 + _nbytes((DECAY_LORA + AAA_LORA + GATE_LORA + WKV_BLOCK, RWKV_W), BF16) + _nbytes(seg.shape, BF16)
                 + WKV_SCRATCH * _nbytes((rows, RWKV_W)),
                 4 * _nbytes((rows, RWKV_IN_W)) + 8 * _nbytes((rows, RWKV_W))
                 + tl.wkv_group * 32 * _nbytes((WKV_BLOCK, RWKV_IN_W)) + tl.ret_group * 8 * _nbytes((chunk, 4 * RET_W)))
    return pl.pallas_call(
        functools.partial(_mixers_kernel,
                          ret=dict(bb=bb, tt=tt, chunk=chunk, group=tl.ret_group),
                          wkv=dict(bb=bb, tt=tt, sub=sub, group=tl.wkv_group)),
        grid=(nb // bb, t // tt),
        in_specs=[ret_st, tile(4 * RET_W), tab_spec, tab_spec,
                  _full(dmask.shape), _full(q_dec.shape), _full(k_dec.shape), _full(c_dec.shape),
                  sh_spec, wkv_st, tile(RWKV_IN_W),
                  _full((1, RWKV_IN_W)), vec, vec, vec, vec, vec, vec, vec,
                  _full((DECAY_LORA, RWKV_W)), _full((AAA_LORA, RWKV_W)), _full((GATE_LORA, RWKV_W)),
                  _full(tri.shape), _full(seg.shape)],
        out_specs=[tile(RET_W), ret_st, tile(RWKV_W), wkv_st, sh_spec],
        out_shape=[jax.ShapeDtypeStruct((nb, t, RET_W), F32), jax.ShapeDtypeStruct(ret_s0.shape, F32),
                   jax.ShapeDtypeStruct((nb, t, RWKV_W), F32), jax.ShapeDtypeStruct(wkv_s0.shape, F32),
                   jax.ShapeDtypeStruct(wkv_sh0.shape, F32)],
        scratch_shapes=[pltpu.VMEM((rows, RWKV_W), F32)] * WKV_SCRATCH,
        compiler_params=_cparams(("parallel", "arbitrary"), vmem),
        name="mixers",
    )(ret_s0, zr, cos, sin, dmask, q_dec, k_dec, c_dec,
      wkv_sh0, wkv_s0, zw, p["mu"], p["w0"], p["a0"], p["k_k"], p["k_a"], p["r_k"], p["lnx_g"], p["lnx_b"],
      p["w_decay_up"], p["w_a_up"], p["w_g_up"], tri, seg)


def _merge_rows(x, zg, oa, ob, wa_ref, wb_ref, wo_ref):
    ga = jax.nn.sigmoid(zg[:, 0:D_MODEL])
    gb = jax.nn.sigmoid(zg[:, D_MODEL:GATE_W])
    merged = ga * _nn(oa, wa_ref[...]) + gb * _nn(ob, wb_ref[...])
    return x + _nn(merged, wo_ref[...])


def _merge_kernel(x_ref, zg_ref, oa_ref, ob_ref, wa_ref, wb_ref, wo_ref, o_ref):
    o_ref[...] = _merge_rows(x_ref[...], zg_ref[...], oa_ref[...], ob_ref[...], wa_ref, wb_ref, wo_ref)


def _merge(x, zg, o_ret, o_rwkv, wa, wb, wo):
    n = x.shape[0]
    tm = min(ROW_TILE, n)
    row = lambda w: pl.BlockSpec((tm, w), lambda i: (i, 0))
    vmem = _vmem(_nbytes((tm, 2 * D_MODEL + GATE_W + RET_W + RWKV_W)), _nbytes((2 * D_MODEL, D_MODEL), BF16),
                 6 * _nbytes((tm, D_MODEL)))
    return pl.pallas_call(
        _merge_kernel,
        grid=(n // tm,),
        in_specs=[row(D_MODEL), row(GATE_W), row(RET_W), row(RWKV_W),
                  _full((RET_W, D_MODEL)), _full((RWKV_W, D_MODEL)), _full((D_MODEL, D_MODEL))],
        out_specs=row(D_MODEL),
        out_shape=jax.ShapeDtypeStruct((n, D_MODEL), F32),
        compiler_params=_cparams(("parallel",), vmem),
        name="merge",
    )(x, zg, o_ret, o_rwkv, wa, wb, wo)


def _softmax(sc):
    e = jnp.exp(sc - jnp.max(sc, axis=-1, keepdims=True))
    return e / jnp.sum(e, axis=-1, keepdims=True)


def _attend_heads(q_scr, mk_ref, mv_ref, ox_scr, b, r0, tq):
    for h in range(X_HEADS):
        sl = slice(h * X_HD, (h + 1) * X_HD)
        att = _softmax(_nt(q_scr[pl.ds(r0, tq), sl], mk_ref[b, :, sl]) * (X_HD ** -0.5))
        ox_scr[pl.ds(r0, tq), sl] = _nn(att, mv_ref[b, :, sl])


def _attend_tiles(q_scr, mk_ref, mv_ref, ox_scr, seqs, tq):
    halves = X_HD // LANES
    n = X_HEADS * tq
    slot = lax.broadcasted_iota(jnp.int32, (n, MEM_ROWS), 1) % ROWS_PER_TOKEN
    head = lax.broadcasted_iota(jnp.int32, (n, MEM_ROWS), 0) // tq
    own = slot == head
    other = slot == head + X_HEADS
    qs = []
    for b in seqs:
        q = q_scr[b * tq:(b + 1) * tq, :]
        qs.append(jnp.concatenate([q[:, h * X_HD + c * LANES:h * X_HD + (c + 1) * LANES]
                                   for c in range(halves) for h in range(X_HEADS)], axis=0))
    z = [_nt(q, mk_ref[b]) for q, b in zip(qs, seqs)]
    part = [jnp.where(own, zi[:n], 0.0) + jnp.where(other, zi[n:], 0.0) for zi in z]
    sc = [p + pltpu.roll(p, MEM_ROWS - X_HEADS, axis=1) for p in part]
    att = [_softmax(jnp.where(own, s * (X_HD ** -0.5), -jnp.inf)) for s in sc]
    att2 = [jnp.concatenate([a, pltpu.roll(a, X_HEADS, axis=1)], axis=0) for a in att]
    o = [_nn(a, mv_ref[b]) for a, b in zip(att2, seqs)]
    for oi, b in zip(o, seqs):
        ox_scr[b * tq:(b + 1) * tq, :] = jnp.concatenate(
            [oi[(c * X_HEADS + h) * tq:(c * X_HEADS + h + 1) * tq] for h in range(X_HEADS) for c in range(halves)],
            axis=1)


def _cross_kernel(x_ref, mk_ref, mv_ref, g_ref, wq_ref, wo_ref, o_ref, q_scr, ox_scr, *, bb, tq):
    rows = bb * tq
    x = x_ref[...].reshape(rows, D_MODEL)
    q_scr[...] = _nn(_rms(x, g_ref[...]), wq_ref[...])
    _attend_tiles(q_scr, mk_ref, mv_ref, ox_scr, range(bb), tq)
    o_ref[...] = (x + _nn(ox_scr[...], wo_ref[...])).reshape(bb, tq, D_MODEL)


def _cross(x, mem_k, mem_v, g, wq, wo, bb, tq):
    nb, t, _ = x.shape
    x_spec = pl.BlockSpec((bb, tq, D_MODEL), lambda b, j: (b, j, 0))
    m_spec = pl.BlockSpec((bb, MEM_ROWS, LANES), lambda b, j: (b, 0, 0))
    vmem = _vmem(2 * _nbytes((bb, tq, D_MODEL)) + 2 * _nbytes((bb, MEM_ROWS, LANES)),
                 2 * _nbytes((D_MODEL, D_MODEL), BF16) + 2 * _nbytes((bb * tq, D_MODEL)),
                 6 * _nbytes((bb * tq, D_MODEL)) + bb * 8 * _nbytes((2 * X_HEADS * tq, MEM_ROWS)))
    return pl.pallas_call(
        functools.partial(_cross_kernel, bb=bb, tq=tq),
        grid=(nb // bb, t // tq),
        in_specs=[x_spec, m_spec, m_spec, _full((1, D_MODEL)), _full((D_MODEL, D_MODEL)), _full((D_MODEL, D_MODEL))],
        out_specs=x_spec,
        out_shape=jax.ShapeDtypeStruct(x.shape, F32),
        scratch_shapes=[pltpu.VMEM((bb * tq, D_MODEL), F32), pltpu.VMEM((bb * tq, D_MODEL), F32)],
        compiler_params=_cparams(("parallel", "arbitrary"), vmem),
        name="cross_attn",
    )(x, mem_k, mem_v, g, wq, wo)


MLP_FF_CHUNK = 1024


def _mlp_rows(x, g_ref, wu_ref, wd_ref, gf_ref, final_norm):
    h = _rms(x, g_ref[...]).astype(BF16)
    acc = x
    for c in range(0, D_FF, MLP_FF_CHUNK):
        u = jnp.maximum(jnp.dot(h, wu_ref[:, c:c + MLP_FF_CHUNK], preferred_element_type=F32), 0.0)
        acc = acc + _nn(u * u, wd_ref[c:c + MLP_FF_CHUNK, :])
    return _rms(acc, gf_ref[...]) if final_norm else acc


def _mlp_kernel(x_ref, g_ref, wu_ref, wd_ref, gf_ref, o_ref, *, final_norm):
    o_ref[...] = _mlp_rows(x_ref[...], g_ref, wu_ref, wd_ref, gf_ref, final_norm)


def _mlp(x, g, w_up, w_down, g_final, final_norm):
    n = x.shape[0]
    tm = min(ROW_TILE, n)
    row = pl.BlockSpec((tm, D_MODEL), lambda i: (i, 0))
    vmem = _vmem(2 * _nbytes((tm, D_MODEL)), 2 * _nbytes((D_MODEL, D_FF), BF16),
                 4 * _nbytes((tm, MLP_FF_CHUNK)) + 4 * _nbytes((tm, D_MODEL)))
    return pl.pallas_call(
        functools.partial(_mlp_kernel, final_norm=final_norm),
        grid=(n // tm,),
        in_specs=[row, _full((1, D_MODEL)), _full((D_MODEL, D_FF)), _full((D_FF, D_MODEL)), _full((1, D_MODEL))],
        out_specs=row,
        out_shape=jax.ShapeDtypeStruct((n, D_MODEL), F32),
        compiler_params=_cparams(("parallel",), vmem),
        name="mlp",
    )(x, g, w_up, w_down, g_final)


POST_TILE = 512


def _post_kernel(x_ref, zg_ref, oa_ref, ob_ref, mk_ref, mv_ref, wa_ref, wb_ref, wo_ref, gc_ref, wq_ref, wco_ref,
                 gm_ref, wu_ref, wd_ref, gf_ref, o_ref, q_scr, ox_scr, *, tq, final_norm):
    x1 = _merge_rows(x_ref[0], zg_ref[0], oa_ref[0], ob_ref[0], wa_ref, wb_ref, wo_ref)
    q_scr[...] = _nn(_rms(x1, gc_ref[...]), wq_ref[...])
    _attend_heads(q_scr, mk_ref, mv_ref, ox_scr, 0, 0, tq)
    x2 = x1 + _nn(ox_scr[...], wco_ref[...])
    o_ref[0] = _mlp_rows(x2, gm_ref, wu_ref, wd_ref, gf_ref, final_norm)


def _post(x, zg, o_ret, o_rwkv, mem_k, mem_v, w, g_final, final_norm):
    nb, t, _ = x.shape
    tq = min(POST_TILE, t)
    row = lambda width: pl.BlockSpec((1, tq, width), lambda b, j: (b, j, 0))
    mem = pl.BlockSpec((1, MEM_LEN, D_MODEL), lambda b, j: (b, 0, 0))
    sq = _full((D_MODEL, D_MODEL))
    vec = _full((1, D_MODEL))
    vmem = _vmem(_nbytes((tq, 2 * D_MODEL + GATE_W + RET_W + RWKV_W)) + 2 * _nbytes((MEM_LEN, D_MODEL), BF16),
                 _nbytes((4 * D_MODEL + 2 * D_FF, D_MODEL), BF16) + 2 * _nbytes((tq, D_MODEL)),
                 8 * _nbytes((tq, D_MODEL)) + 4 * _nbytes((tq, MLP_FF_CHUNK)))
    return pl.pallas_call(
        functools.partial(_post_kernel, tq=tq, final_norm=final_norm),
        grid=(nb, t // tq),
        in_specs=[row(D_MODEL), row(GATE_W), row(RET_W), row(RWKV_W), mem, mem,
                  _full((RET_W, D_MODEL)), _full((RWKV_W, D_MODEL)), sq, vec, sq, sq,
                  vec, _full((D_MODEL, D_FF)), _full((D_FF, D_MODEL)), vec],
        out_specs=row(D_MODEL),
        out_shape=jax.ShapeDtypeStruct(x.shape, F32),
        scratch_shapes=[pltpu.VMEM((tq, D_MODEL), F32), pltpu.VMEM((tq, D_MODEL), F32)],
        compiler_params=_cparams(("parallel", "arbitrary"), vmem),
        name="post",
    )(x, zg, o_ret, o_rwkv, mem_k, mem_v, w["w_branch_a"], w["w_branch_b"], w["w_out"], w["g_cross"], w["w_cq"],
      w["w_co"], w["g_mlp"], w["w_up"], w["w_down"], g_final)


class _Tiling(NamedTuple):
    seq_block: int
    time_block: int
    cross_block: int
    ret_group: int
    wkv_group: int


def _tiling(nb, t):
    if t > RET_CHUNK:
        tt = min(ROW_TILE, t)
        return _Tiling(seq_block=1, time_block=tt, cross_block=1,
                       ret_group=math.gcd(4, tt // RET_CHUNK), wkv_group=math.gcd(8, tt // WKV_BLOCK))
    seqs = WKV_BLOCK // t
    return _Tiling(seq_block=seqs, time_block=t, cross_block=math.gcd(4, nb), ret_group=seqs, wkv_group=1)


def _layer(x, pos, s_ret, s_wkv, s_shift, mem_k, mem_v, w, g_final, final_norm):
    nb, t, _ = x.shape
    n = nb * t
    tl = _tiling(nb, t)
    xf = x.reshape(n, D_MODEL)
    zg, zr, zw = _in_proj(xf, w["g_mix"], w["w_in"])
    o_ret, s_ret_new, o_wkv, s_wkv_new, shift_new = _mixers(
        zr.reshape(nb, t, 4 * RET_W), zw.reshape(nb, t, RWKV_IN_W), s_ret, s_shift.reshape(nb, 1, RWKV_IN_W), s_wkv,
        pos, w, tl)
    if mem_k.shape[1:] == (MEM_LEN, D_MODEL):
        y = _post(x, zg.reshape(nb, t, GATE_W), o_ret, o_wkv, mem_k, mem_v, w, g_final, final_norm)
    else:
        x1 = _merge(xf, zg, o_ret.reshape(n, RET_W), o_wkv.reshape(n, RWKV_W),
                    w["w_branch_a"], w["w_branch_b"], w["w_out"])
        x2 = _cross(x1.reshape(nb, t, D_MODEL), mem_k, mem_v, w["g_cross"], w["w_cq"], w["w_co"],
                    tl.cross_block, tl.time_block)
        y = _mlp(x2.reshape(n, D_MODEL), w["g_mlp"], w["w_up"], w["w_down"], g_final, final_norm).reshape(x.shape)
    return y, s_ret_new, s_wkv_new, shift_new.reshape(nb, RWKV_IN_W)


def _layer_weights(l, g_mix, w_in, w_branch_a, w_branch_b, w_out, mu_shift, w0, w_decay_up, a0, w_a_up, w_g_up,
                   k_k, k_a, r_k, lnx_g, lnx_b, g_cross, w_cq, w_co, g_mlp, w_up, w_down):
    vec = lambda v: v[l].reshape(1, -1).astype(F32)
    mat = lambda m: m[l].astype(BF16)
    return dict(g_mix=vec(g_mix), w_in=mat(w_in), w_branch_a=mat(w_branch_a), w_branch_b=mat(w_branch_b),
                w_out=mat(w_out), mu=vec(mu_shift), w0=vec(w0), w_decay_up=mat(w_decay_up), a0=vec(a0),
                w_a_up=mat(w_a_up), w_g_up=mat(w_g_up), k_k=vec(k_k), k_a=vec(k_a), r_k=vec(r_k), lnx_g=vec(lnx_g),
                lnx_b=vec(lnx_b), g_cross=vec(g_cross), w_cq=mat(w_cq), w_co=mat(w_co), g_mlp=vec(g_mlp),
                w_up=mat(w_up), w_down=mat(w_down))


def kernel(x_prompt, x_sample, mem_prompt, state_ret, state_wkv, state_shift, cache_mem_k, cache_mem_v, g_mix, w_in, w_branch_a, w_branch_b, w_out, mu_shift, w0, w_decay_up, a0, w_a_up, w_g_up, k_k, k_a, r_k, lnx_g, lnx_b, g_cross, g_mem, w_cq, w_ck, w_cv, w_co, g_mlp, w_up, w_down, g_final):
    depth = w_in.shape[0]
    bp, tp, _ = x_prompt.shape
    bs, ts, _ = x_sample.shape
    pos_p = jnp.arange(tp, dtype=jnp.int32)
    pos_s = PAST_LEN + jnp.arange(ts, dtype=jnp.int32)
    gf = g_final.reshape(1, D_MODEL)

    xp, xs = x_prompt, x_sample
    outs_p, outs_s = [], []
    for l in range(depth):
        w = _layer_weights(l, g_mix, w_in, w_branch_a, w_branch_b, w_out, mu_shift, w0, w_decay_up, a0, w_a_up,
                           w_g_up, k_k, k_a, r_k, lnx_g, lnx_b, g_cross, w_cq, w_co, g_mlp, w_up, w_down)
        last = l == depth - 1
        k_tiles, v_tiles, k_att, v_att = _mem_kv(mem_prompt.reshape(bp * MEM_LEN, D_MODEL),
                                                 g_mem[l].reshape(1, D_MODEL), w_ck[l].astype(BF16), w_cv[l].astype(BF16))
        xp, sr, sw, ss = _layer(
            xp, pos_p,
            jnp.zeros((bp, RET_HEADS, RET_HD, RET_HD), F32), jnp.zeros((bp, RWKV_HEADS, RWKV_HD, RWKV_HD), F32),
            jnp.zeros((bp, RWKV_IN_W), F32), k_att.reshape(bp, MEM_LEN, D_MODEL), v_att.reshape(bp, MEM_LEN, D_MODEL),
            w, gf, last)
        outs_p.append((sr, sw, ss, _from_tile_order(k_tiles.reshape(bp, MEM_ROWS, LANES)),
                       _from_tile_order(v_tiles.reshape(bp, MEM_ROWS, LANES))))
        xs, sr2, sw2, ss2 = _layer(
            xs, pos_s, state_ret[l], state_wkv[l], state_shift[l],
            _tile_order(cache_mem_k[l]), _tile_order(cache_mem_v[l]), w, gf, last)
        outs_s.append((sr2, sw2, ss2))

    stack = lambda items, i: jnp.stack([it[i] for it in items])
    return (xp, xs, stack(outs_p, 0), stack(outs_p, 1), stack(outs_p, 2), stack(outs_p, 3), stack(outs_p, 4),
            stack(outs_s, 0), stack(outs_s, 1), stack(outs_s, 2))
```

```python
import functools
import math
from typing import NamedTuple

import jax
import jax.numpy as jnp
from jax import lax
from jax.experimental import pallas as pl
from jax.experimental.pallas import tpu as pltpu

F32 = jnp.float32
BF16 = jnp.bfloat16

D_MODEL = 1024
PAST_LEN = 16384
RET_HEADS = 4
RET_HD = 128
RET_W = RET_HEADS * RET_HD
RET_CHUNK = 128
RET_GN_EPS = 1e-5
ROPE_BASE = 10000.0
RWKV_HEADS = 8
RWKV_HD = 64
RWKV_W = RWKV_HEADS * RWKV_HD
DECAY_LORA = 64
AAA_LORA = 64
GATE_LORA = 128
RWKV_GN_EPS = 64e-5
RWKV_IN_W = 3 * RWKV_W + DECAY_LORA + AAA_LORA + GATE_LORA
GATE_W = 2 * D_MODEL
O_RET = 2 * D_MODEL
O_RWKV = O_RET + 4 * RET_W
IN_W = O_RWKV + RWKV_IN_W
MEM_LEN = 256
X_HEADS = 4
X_HD = D_MODEL // X_HEADS
D_FF = 4 * D_MODEL
RMS_EPS = 1e-6

SUBLANES = 8
V7X_VMEM_BYTES = 64 * 1024 * 1024
VMEM_CAP_BYTES = V7X_VMEM_BYTES - 8 * 1024 * 1024
ROW_TILE = 512


def _cparams(sem, vmem_bytes):
    return pltpu.CompilerParams(dimension_semantics=sem, vmem_limit_bytes=int(min(vmem_bytes, VMEM_CAP_BYTES)))


def _nbytes(shape, dtype=F32):
    return math.prod(shape) * jnp.dtype(dtype).itemsize


def _nn(a, b):
    return jnp.dot(a.astype(BF16), b.astype(BF16), preferred_element_type=F32)


def _nt(a, b):
    return lax.dot_general(a.astype(BF16), b.astype(BF16), (((1,), (1,)), ((), ())), preferred_element_type=F32)


def _tn(a, b):
    return lax.dot_general(a.astype(BF16), b.astype(BF16), (((0,), (0,)), ((), ())), preferred_element_type=F32)


def _rms(x, g):
    return x * lax.rsqrt(jnp.mean(x * x, axis=-1, keepdims=True) + RMS_EPS) * g


def _head_norms(blocks, eps):
    rows, width = blocks[0].shape
    ones = jnp.ones((width, width), BF16)
    o = jnp.concatenate(blocks, axis=0)
    d = o - _nn(o, ones) * (1.0 / width)
    out = d * lax.rsqrt(_nn(d * d, ones) * (1.0 / width) + eps)
    return [out[i * rows:(i + 1) * rows] for i in range(len(blocks))]


def _full(shape):
    zeros = (0,) * len(shape)
    return pl.BlockSpec(shape, lambda *_: zeros, pipeline_mode=pl.Buffered(1))


def _vmem(pipelined, resident, temps):
    return 2 * pipelined + resident + temps


def _in_proj_kernel(x_ref, g_ref, w_ref, zg_ref, zr_ref, zw_ref):
    h = _rms(x_ref[...], g_ref[...]).astype(BF16)
    zg_ref[...] = jnp.dot(h, w_ref[:, 0:O_RET], preferred_element_type=F32)
    zr_ref[...] = jnp.dot(h, w_ref[:, O_RET:O_RWKV], preferred_element_type=F32)
    zw_ref[...] = jnp.dot(h, w_ref[:, O_RWKV:IN_W], preferred_element_type=F32)


def _in_proj(x, g, w_in):
    n = x.shape[0]
    tm = min(ROW_TILE, n)
    vmem = _vmem(_nbytes((tm, D_MODEL + IN_W)), _nbytes((D_MODEL, IN_W), BF16), _nbytes((tm, 4 * RET_W + D_MODEL)))
    return pl.pallas_call(
        _in_proj_kernel,
        grid=(n // tm,),
        in_specs=[pl.BlockSpec((tm, D_MODEL), lambda i: (i, 0)), _full((1, D_MODEL)), _full((D_MODEL, IN_W))],
        out_specs=[pl.BlockSpec((tm, GATE_W), lambda i: (i, 0)),
                   pl.BlockSpec((tm, 4 * RET_W), lambda i: (i, 0)),
                   pl.BlockSpec((tm, RWKV_IN_W), lambda i: (i, 0))],
        out_shape=[jax.ShapeDtypeStruct((n, GATE_W), F32), jax.ShapeDtypeStruct((n, 4 * RET_W), F32),
                   jax.ShapeDtypeStruct((n, RWKV_IN_W), F32)],
        compiler_params=_cparams(("parallel",), vmem),
        name="in_proj",
    )(x, g, w_in)


LANES = 128
MEM_ROWS = MEM_LEN * X_HEADS * (X_HD // LANES)
ROWS_PER_TOKEN = MEM_ROWS // MEM_LEN


def _tile_order(mem):
    nb = mem.shape[0]
    return (mem.reshape(nb, MEM_LEN, X_HEADS, X_HD // LANES, LANES).transpose(0, 1, 3, 2, 4)
            .reshape(nb, MEM_ROWS, LANES))


def _from_tile_order(raw):
    nb = raw.shape[0]
    return (raw.reshape(nb, MEM_LEN, X_HD // LANES, X_HEADS, LANES).transpose(0, 1, 3, 2, 4)
            .reshape(nb, MEM_LEN, X_HEADS, X_HD))


def _mem_kv_kernel(x_ref, g_ref, wk_ref, wv_ref, kt_ref, vt_ref, kb_ref, vb_ref, *, tm):
    h = _rms(x_ref[...], g_ref[...]).astype(BF16)
    for w_ref, t_ref, b_ref in ((wk_ref, kt_ref, kb_ref), (wv_ref, vt_ref, vb_ref)):
        y = jnp.dot(h, w_ref[...], preferred_element_type=F32)
        b_ref[...] = y.astype(BF16)
        for hd in range(X_HEADS):
            for c in range(X_HD // LANES):
                col = hd * X_HD + c * LANES
                t_ref[pl.ds(c * X_HEADS + hd, tm, stride=ROWS_PER_TOKEN), :] = y[:, col:col + LANES]


def _mem_kv(mem, g, wk, wv):
    n = mem.shape[0]
    tm = min(ROW_TILE, n)
    vmem = _vmem(4 * _nbytes((tm, D_MODEL)), 2 * _nbytes((D_MODEL, D_MODEL), BF16), 4 * _nbytes((tm, D_MODEL)))
    row = pl.BlockSpec((tm, D_MODEL), lambda i: (i, 0))
    tile = pl.BlockSpec((tm * ROWS_PER_TOKEN, LANES), lambda i: (i, 0))
    return pl.pallas_call(
        functools.partial(_mem_kv_kernel, tm=tm),
        grid=(n // tm,),
        in_specs=[row, _full((1, D_MODEL)), _full((D_MODEL, D_MODEL)), _full((D_MODEL, D_MODEL))],
        out_specs=[tile, tile, row, row],
        out_shape=[jax.ShapeDtypeStruct((n * ROWS_PER_TOKEN, LANES), F32)] * 2
        + [jax.ShapeDtypeStruct((n, D_MODEL), BF16)] * 2,
        compiler_params=_cparams(("parallel",), vmem),
        name="mem_kv",
    )(mem, g, wk, wv)


def _ret_body(zr_ref, cos_ref, sin_ref, dm_ref, qd_ref, kd_ref, cd_ref, o_ref, s_ref, *, bb, tt, chunk, group):
    nch = tt // chunk
    assert bb == 1 or nch == 1, "a block holds either one sequence or one chunk per sequence"
    assert (bb * nch) % group == 0
    heads = range(RET_HEADS)

    def body(j, carry):
        if nch == 1:
            items = [(j * group + n, 0) for n in range(group)]
        else:
            starts = [(j * group + n) * chunk for n in range(group)]
            items = [(0, r if isinstance(r, int) else pl.multiple_of(r, chunk)) for r in starts]
        q, k, v = [], [], []
        for b, r0 in items:
            rows = pl.ds(r0, chunk)
            cos = cos_ref[rows, :]
            sin = sin_ref[rows, :]
            for h in heads:
                qh = zr_ref[b, rows, h * RET_HD:(h + 1) * RET_HD]
                kh = zr_ref[b, rows, RET_W + h * RET_HD:RET_W + (h + 1) * RET_HD]
                q.append(qh * cos + pltpu.roll(qh, RET_HD // 2, axis=1) * sin)
                k.append((kh * cos + pltpu.roll(kh, RET_HD // 2, axis=1) * sin) * (RET_HD ** -0.5))
                v.append(zr_ref[b, rows, 2 * RET_W + h * RET_HD:2 * RET_W + (h + 1) * RET_HD])
        hd = [h for _ in items for h in heads]
        sc = [_nt(qi, ki) * dm_ref[h] for qi, ki, h in zip(q, k, hd)]
        kv = [_tn(ki * kd_ref[h], vi) for ki, vi, h in zip(k, v, hd)]
        inner = [_nn(si, vi) for si, vi in zip(sc, v)]
        states = []
        for n, (b, _) in enumerate(items):
            for h in heads:
                s = s_ref[b, h] if (nch == 1 or n == 0) else states[-RET_HEADS] * cd_ref[h] + kv[(n - 1) * RET_HEADS + h]
                states.append(s)
        last = len(items) - 1
        for n, (b, _) in enumerate(items):
            if nch == 1 or n == last:
                for h in heads:
                    i = n * RET_HEADS + h
                    s_ref[b, h] = states[i] * cd_ref[h] + kv[i]
        cross = [_nn(qi, si) * qd_ref[h] for qi, si, h in zip(q, states, hd)]
        normed = _head_norms([a + c for a, c in zip(inner, cross)], RET_GN_EPS)
        for n, (b, r0) in enumerate(items):
            rows = pl.ds(r0, chunk)
            for h in heads:
                g = zr_ref[b, rows, 3 * RET_W + h * RET_HD:3 * RET_W + (h + 1) * RET_HD]
                o_ref[b, rows, h * RET_HD:(h + 1) * RET_HD] = normed[n * RET_HEADS + h] * (g * jax.nn.sigmoid(g))
        return carry

    if bb * nch == group:
        body(0, 0)
    else:
        lax.fori_loop(0, bb * nch // group, body, 0)


def _ret_consts(chunk):
    lg = jnp.log1p(-jnp.exp2(-5.0 - jnp.arange(RET_HEADS, dtype=F32)))
    idx = jnp.arange(chunk, dtype=F32)
    diff = idx[:, None] - idx[None, :]
    dmask = jnp.where(diff[None] >= 0, jnp.exp(jnp.maximum(diff, 0.0)[None] * lg[:, None, None]), 0.0)
    q_dec = jnp.exp((idx + 1.0)[None, :] * lg[:, None])
    k_dec = jnp.exp((chunk - 1.0 - idx)[None, :] * lg[:, None])
    c_dec = jnp.exp(chunk * lg)
    bc = lambda t: jnp.broadcast_to(t[:, :, None], (RET_HEADS, t.shape[1], RET_HD))
    return dmask, bc(q_dec), bc(k_dec), bc(c_dec[:, None])


def _rope_tables(pos):
    half = RET_HD // 2
    inv = 1.0 / (ROPE_BASE ** (jnp.arange(half, dtype=F32) / half))
    ang = pos.astype(F32)[:, None] * inv[None, :]
    cos, sin = jnp.cos(ang), jnp.sin(ang)
    return jnp.concatenate([cos, cos], axis=1), jnp.concatenate([-sin, sin], axis=1)


WKV_BLOCK = 64
WKV_SCRATCH = 10


def _pow2(n):
    return n & (n - 1) == 0


def _imod(x, n):
    return jnp.bitwise_and(x, n - 1) if _pow2(n) else lax.rem(x, n)


def _idiv(x, n):
    return jnp.right_shift(x, n.bit_length() - 1) if _pow2(n) else lax.div(x, n)


def _wkv_body(zw_ref, mu_ref, w0_ref, a0_ref, kk_ref, ka_ref, rk_ref, lng_ref, lnb_ref,
              wd_ref, wa_ref, wg_ref, tri_ref, seg_ref, o_ref, s_ref, sh_ref,
              at_s, rt_s, bt_s, kt_s, bh_s, kh_s, v_s, g_s, bonus_s, pend_s, *, bb, tt, sub, group):
    C = WKV_BLOCK
    R = bb * tt
    nblk = R // C
    nseq = C // sub
    assert (bb == 1 and sub == C) or (nblk == 1 and sub == tt), "one sequence per tile, or whole sequences in one block"
    assert nblk % group == 0
    n_sq = max(int(math.log2(sub)) - 1, 0)
    heads = range(RWKV_HEADS)
    hsl = [slice(h * RWKV_HD, (h + 1) * RWKV_HD) for h in heads]
    half = RWKV_W // 2

    def seg_sum(t):
        seg = seg_ref[...]
        return jnp.concatenate([_nn(t[:, :half], seg), _nn(t[:, half:], seg)], axis=1)

    pw = zw_ref[...].reshape(R, RWKV_IN_W)
    row = lax.broadcasted_iota(jnp.int32, (R, RWKV_IN_W), 0)
    rolled = pltpu.roll(pw, 1, axis=0)
    if bb == 1:
        prev = jnp.where(row == 0, sh_ref[0], rolled)
        sh_ref[0] = pw[R - 1:R, :]
    else:
        carry = jnp.broadcast_to(sh_ref[...], (bb, tt, RWKV_IN_W)).reshape(R, RWKV_IN_W)
        prev = jnp.where(_imod(row, tt) == 0, carry, rolled)
        sh_ref[...] = pw.reshape(bb, tt, RWKV_IN_W)[:, tt - 1:tt, :]
    xm = pw + mu_ref[...] * (prev - pw)
    r = xm[:, 0:RWKV_W]
    kb = xm[:, RWKV_W:2 * RWKV_W]
    o1 = 3 * RWKV_W
    wl = xm[:, o1:o1 + DECAY_LORA]
    al = xm[:, o1 + DECAY_LORA:o1 + DECAY_LORA + AAA_LORA]
    gl = xm[:, o1 + DECAY_LORA + AAA_LORA:RWKV_IN_W]
    logw = -math.exp(-0.5) * jax.nn.sigmoid(w0_ref[...] + _nn(jnp.tanh(wl), wd_ref[...]))
    a = jax.nn.sigmoid(a0_ref[...] + _nn(al, wa_ref[...]))
    l1 = logw.astype(BF16)
    rem = logw - l1.astype(F32)
    l2 = rem.astype(BF16)
    l3 = (rem - l2.astype(F32)).astype(BF16)
    tri = tri_ref[...]
    cum = jnp.concatenate(
        [sum(jnp.dot(tri, part[k * C:(k + 1) * C], preferred_element_type=F32) for part in (l1, l2, l3))
         for k in range(nblk)], axis=0)
    cum_end = jnp.broadcast_to(cum.reshape(R // sub, sub, RWKV_W)[:, sub - 1:sub, :],
                               (R // sub, sub, RWKV_W)).reshape(R, RWKV_W)
    kk = kb * kk_ref[...]
    kk = kk * jnp.minimum(lax.rsqrt(seg_sum(kk * kk)), 1e12)
    km = kb * (1.0 + (a - 1.0) * ka_ref[...])
    bv = kk * a
    p_end = jnp.exp(cum_end)
    e_neg = jnp.exp(-cum)
    e_end = p_end * e_neg
    at_s[...] = -kk * jnp.exp(cum - logw)
    rt_s[...] = r * jnp.exp(cum)
    bt_s[...] = bv * e_neg
    kt_s[...] = km * e_neg
    bh_s[...] = bv * e_end
    kh_s[...] = km * e_end
    v_s[...] = xm[:, 2 * RWKV_W:3 * RWKV_W]
    g_s[...] = _nn(jax.nn.sigmoid(gl), wg_ref[...])
    bonus_s[...] = seg_sum(r * km * rk_ref[...])
    pend_s[...] = p_end

    row_id = lax.broadcasted_iota(jnp.int32, (C, C), 0)
    col_id = lax.broadcasted_iota(jnp.int32, (C, C), 1)
    strict = row_id > col_id
    incl = row_id >= col_id
    if nseq > 1:
        same = _idiv(row_id, sub) == _idiv(col_id, sub)
        strict = jnp.logical_and(strict, same)
        incl = jnp.logical_and(incl, same)
    eye = jnp.where(row_id == col_id, 1.0, 0.0).astype(F32)

    def triangular(r0s):
        rows = [pl.ds(r0, C) for r0 in r0s for _ in heads]
        sl = [s for _ in r0s for s in hsl]
        at = [at_s[rw, :][:, s] for rw, s in zip(rows, sl)]
        rt = [rt_s[rw, :][:, s] for rw, s in zip(rows, sl)]
        v = [v_s[rw, :][:, s] for rw, s in zip(rows, sl)]
        bk_t = [jnp.concatenate([bt_s[pl.ds(r0, C), :], kt_s[pl.ds(r0, C), :]], axis=0).T for r0 in r0s]
        m1 = [_nn(jnp.concatenate([a_, r_], axis=0), bk[s, :])
              for a_, r_, bk, s in zip(at, rt, [t for t in bk_t for _ in heads], sl)]
        low = [jnp.where(strict, m[:C, :C], 0.0) for m in m1]
        rab = [jnp.where(incl, m[C:, :C], 0.0) for m in m1]
        akk = [jnp.concatenate([jnp.where(strict, m[:C, C:], 0.0), jnp.where(incl, m[C:, C:], 0.0)], axis=0)
               for m in m1]
        akv = [_nn(k_, v_) for k_, v_ in zip(akk, v)]
        tinv = [eye + l for l in low]
        if n_sq:
            x = [_nn(l, l) for l in low]
        for j in range(n_sq):
            if j < n_sq - 1:
                xt = [_nn(jnp.concatenate([xi, ti], axis=0), xi) for xi, ti in zip(x, tinv)]
                x = [t[:C] for t in xt]
                tinv = [ti + t[C:] for ti, t in zip(tinv, xt)]
            else:
                tinv = [ti + _nn(ti, xi) for ti, xi in zip(tinv, x)]
        ua = [_nn(t, a_) for t, a_ in zip(tinv, at)]
        uv = [_nn(t, k_[:C]) for t, k_ in zip(tinv, akv)]
        oa = [r_ + _nn(rb, u_) for r_, rb, u_ in zip(rt, rab, ua)]
        ov = [k_[C:] + _nn(rb, u_) for k_, rb, u_ in zip(akv, rab, uv)]
        nh = RWKV_HEADS
        return [dict(ua=ua[k * nh:(k + 1) * nh], uv=uv[k * nh:(k + 1) * nh], oa=oa[k * nh:(k + 1) * nh],
                     ov=ov[k * nh:(k + 1) * nh], v=v[k * nh:(k + 1) * nh]) for k in range(len(r0s))]

    def advance(r0, q):
        rw = pl.ds(r0, C)
        bh, kh, pend = bh_s[rw, :], kh_s[rw, :], pend_s[rw, :]
        items = [(n, h) for n in range(nseq) for h in heads]
        rs = [slice(n * sub, (n + 1) * sub) for n, _ in items]
        uo = [_nt(jnp.concatenate([q["ua"][h][s], q["oa"][h][s]], axis=0), s_ref[n, h])
              for (n, h), s in zip(items, rs)]
        u = [m[:sub] + q["uv"][h][s] for m, (n, h), s in zip(uo, items, rs)]
        upd = [_tn(jnp.concatenate([u_, q["v"][h][s]], axis=0),
                   jnp.concatenate([bh[s, hsl[h]], kh[s, hsl[h]]], axis=0)) for u_, (n, h), s in zip(u, items, rs)]
        for up, (n, h), s in zip(upd, items, rs):
            s_ref[n, h] = s_ref[n, h] * pend[s, hsl[h]][0:1] + up
        o = [m[sub:] + q["ov"][h][s] for m, (n, h), s in zip(uo, items, rs)]
        o = jnp.concatenate([jnp.concatenate([o[n * RWKV_HEADS + h] for n in range(nseq)], axis=0) for h in heads],
                            axis=1)
        d = o - seg_sum(o) * (1.0 / RWKV_HD)
        var = seg_sum(d * d) * (1.0 / RWKV_HD)
        o = d * lax.rsqrt(var + RWKV_GN_EPS) * lng_ref[...] + lnb_ref[...]
        o = (o + bonus_s[rw, :] * v_s[rw, :]) * g_s[rw, :]
        if bb == 1:
            o_ref[0, rw, :] = o
        else:
            o_ref[...] = o.reshape(bb, tt, RWKV_W)

    def body(j, carry):
        r0s = [j * group * C + k * C for k in range(group)]
        r0s = [r if isinstance(r, int) else pl.multiple_of(r, C) for r in r0s]
        for r0, q in zip(r0s, triangular(r0s)):
            advance(r0, q)
        return carry

    if nblk == group:
        body(0, 0)
    else:
        lax.fori_loop(0, nblk // group, body, 0)


RET_INPUTS = 7
WKV_INPUTS = 14


def _mixers_kernel(*refs, ret, wkv):
    it = iter(refs)
    take = lambda n: [next(it) for _ in range(n)]
    ret_s0, = take(1)
    ret_in = take(RET_INPUTS)
    wkv_sh0, wkv_s0 = take(2)
    wkv_in = take(WKV_INPUTS)
    ret_o, ret_s, wkv_o, wkv_s, wkv_sh = take(5)
    scratch = take(WKV_SCRATCH)

    @pl.when(pl.program_id(1) == 0)
    def _():
        ret_s[...] = ret_s0[...]
        wkv_s[...] = wkv_s0[...]
        wkv_sh[...] = wkv_sh0[...]

    _ret_body(*ret_in, ret_o, ret_s, **ret)
    _wkv_body(*wkv_in, wkv_o, wkv_s, wkv_sh, *scratch, **wkv)


def _mixers(zr, zw, ret_s0, wkv_sh0, wkv_s0, pos, p, tl):
    nb, t, _ = zr.shape
    bb, tt = tl.seq_block, tl.time_block
    rows = bb * tt
    chunk = math.gcd(t, RET_CHUNK)
    cos, sin = _rope_tables(pos)
    dmask, q_dec, k_dec, c_dec = _ret_consts(chunk)
    sub = min(t, WKV_BLOCK)
    step = jnp.arange(WKV_BLOCK)
    tri = ((step[:, None] >= step[None, :]) & (step[:, None] // sub == step[None, :] // sub)).astype(BF16)
    lane_head = jnp.arange(RWKV_W // 2) // RWKV_HD
    seg = (lane_head[:, None] == lane_head[None, :]).astype(BF16)

    tile = lambda width: pl.BlockSpec((bb, tt, width), lambda b, j: (b, j, 0))
    ret_st = pl.BlockSpec((bb, RET_HEADS, RET_HD, RET_HD), lambda b, j: (b, 0, 0, 0))
    wkv_st = pl.BlockSpec((bb, RWKV_HEADS, RWKV_HD, RWKV_HD), lambda b, j: (b, 0, 0, 0))
    sh_spec = pl.BlockSpec((bb, 1, RWKV_IN_W), lambda b, j: (b, 0, 0))
    tab_spec = pl.BlockSpec((tt, RET_HD), lambda b, j: (j, 0))
    vec = _full((1, RWKV_W))
    vmem = _vmem(_nbytes((rows, 5 * RET_W + RWKV_IN_W + RWKV_W)) + 2 * _nbytes((bb, RET_HEADS, RET_HD, RET_HD))
                 + 2 * _nbytes((bb, RWKV_HEADS, RWKV_HD, RWKV_HD)) + 2 * _nbytes((bb, SUBLANES, RWKV_IN_W))
                 + 2 * _nbytes((tt, RET_HD)),
                 _nbytes(dmask.shape) + 3 * _nbytes(q_dec.shape) + 9 * _nbytes((SUBLANES, RWKV_IN_W))
                 + _nbytes((DECAY_LORA + AAA_LORA + GATE_LORA + WKV_BLOCK, RWKV_W), BF16) + _nbytes(seg.shape, BF16)
                 + WKV_SCRATCH * _nbytes((rows, RWKV_W)),
                 4 * _nbytes((rows, RWKV_IN_W)) + 8 * _nbytes((rows, RWKV_W))
                 + tl.wkv_group * 32 * _nbytes((WKV_BLOCK, RWKV_IN_W)) + tl.ret_group * 8 * _nbytes((chunk, 4 * RET_W)))
    return pl.pallas_call(
        functools.partial(_mixers_kernel,
                          ret=dict(bb=bb, tt=tt, chunk=chunk, group=tl.ret_group),
                          wkv=dict(bb=bb, tt=tt, sub=sub, group=tl.wkv_group)),
        grid=(nb // bb, t // tt),
        in_specs=[ret_st, tile(4 * RET_W), tab_spec, tab_spec,
                  _full(dmask.shape), _full(q_dec.shape), _full(k_dec.shape), _full(c_dec.shape),
                  sh_spec, wkv_st, tile(RWKV_IN_W),
                  _full((1, RWKV_IN_W)), vec, vec, vec, vec, vec, vec, vec,
                  _full((DECAY_LORA, RWKV_W)), _full((AAA_LORA, RWKV_W)), _full((GATE_LORA, RWKV_W)),
                  _full(tri.shape), _full(seg.shape)],
        out_specs=[tile(RET_W), ret_st, tile(RWKV_W), wkv_st, sh_spec],
        out_shape=[jax.ShapeDtypeStruct((nb, t, RET_W), F32), jax.ShapeDtypeStruct(ret_s0.shape, F32),
                   jax.ShapeDtypeStruct((nb, t, RWKV_W), F32), jax.ShapeDtypeStruct(wkv_s0.shape, F32),
                   jax.ShapeDtypeStruct(wkv_sh0.shape, F32)],
        scratch_shapes=[pltpu.VMEM((rows, RWKV_W), F32)] * WKV_SCRATCH,
        compiler_params=_cparams(("parallel", "arbitrary"), vmem),
        name="mixers",
    )(ret_s0, zr, cos, sin, dmask, q_dec, k_dec, c_dec,
      wkv_sh0, wkv_s0, zw, p["mu"], p["w0"], p["a0"], p["k_k"], p["k_a"], p["r_k"], p["lnx_g"], p["lnx_b"],
      p["w_decay_up"], p["w_a_up"], p["w_g_up"], tri, seg)


def _merge_rows(x, zg, oa, ob, wa_ref, wb_ref, wo_ref):
    ga = jax.nn.sigmoid(zg[:, 0:D_MODEL])
    gb = jax.nn.sigmoid(zg[:, D_MODEL:GATE_W])
    merged = ga * _nn(oa, wa_ref[...]) + gb * _nn(ob, wb_ref[...])
    return x + _nn(merged, wo_ref[...])


def _merge_kernel(x_ref, zg_ref, oa_ref, ob_ref, wa_ref, wb_ref, wo_ref, o_ref):
    o_ref[...] = _merge_rows(x_ref[...], zg_ref[...], oa_ref[...], ob_ref[...], wa_ref, wb_ref, wo_ref)


def _merge(x, zg, o_ret, o_rwkv, wa, wb, wo):
    n = x.shape[0]
    tm = min(ROW_TILE, n)
    row = lambda w: pl.BlockSpec((tm, w), lambda i: (i, 0))
    vmem = _vmem(_nbytes((tm, 2 * D_MODEL + GATE_W + RET_W + RWKV_W)), _nbytes((2 * D_MODEL, D_MODEL), BF16),
                 6 * _nbytes((tm, D_MODEL)))
    return pl.pallas_call(
        _merge_kernel,
        grid=(n // tm,),
        in_specs=[row(D_MODEL), row(GATE_W), row(RET_W), row(RWKV_W),
                  _full((RET_W, D_MODEL)), _full((RWKV_W, D_MODEL)), _full((D_MODEL, D_MODEL))],
        out_specs=row(D_MODEL),
        out_shape=jax.ShapeDtypeStruct((n, D_MODEL), F32),
        compiler_params=_cparams(("parallel",), vmem),
        name="merge",
    )(x, zg, o_ret, o_rwkv, wa, wb, wo)


def _softmax(sc):
    e = jnp.exp(sc - jnp.max(sc, axis=-1, keepdims=True))
    return e / jnp.sum(e, axis=-1, keepdims=True)


def _attend_heads(q_scr, mk_ref, mv_ref, ox_scr, b, r0, tq):
    for h in range(X_HEADS):
        sl = slice(h * X_HD, (h + 1) * X_HD)
        att = _softmax(_nt(q_scr[pl.ds(r0, tq), sl], mk_ref[b, :, sl]) * (X_HD ** -0.5))
        ox_scr[pl.ds(r0, tq), sl] = _nn(att, mv_ref[b, :, sl])


def _attend_tiles(q_scr, mk_ref, mv_ref, ox_scr, seqs, tq):
    halves = X_HD // LANES
    n = X_HEADS * tq
    slot = lax.broadcasted_iota(jnp.int32, (n, MEM_ROWS), 1) % ROWS_PER_TOKEN
    head = lax.broadcasted_iota(jnp.int32, (n, MEM_ROWS), 0) // tq
    own = slot == head
    other = slot == head + X_HEADS
    qs = []
    for b in seqs:
        q = q_scr[b * tq:(b + 1) * tq, :]
        qs.append(jnp.concatenate([q[:, h * X_HD + c * LANES:h * X_HD + (c + 1) * LANES]
                                   for c in range(halves) for h in range(X_HEADS)], axis=0))
    z = [_nt(q, mk_ref[b]) for q, b in zip(qs, seqs)]
    part = [jnp.where(own, zi[:n], 0.0) + jnp.where(other, zi[n:], 0.0) for zi in z]
    sc = [p + pltpu.roll(p, MEM_ROWS - X_HEADS, axis=1) for p in part]
    att = [_softmax(jnp.where(own, s * (X_HD ** -0.5), -jnp.inf)) for s in sc]
    att2 = [jnp.concatenate([a, pltpu.roll(a, X_HEADS, axis=1)], axis=0) for a in att]
    o = [_nn(a, mv_ref[b]) for a, b in zip(att2, seqs)]
    for oi, b in zip(o, seqs):
        ox_scr[b * tq:(b + 1) * tq, :] = jnp.concatenate(
            [oi[(c * X_HEADS + h) * tq:(c * X_HEADS + h + 1) * tq] for h in range(X_HEADS) for c in range(halves)],
            axis=1)


def _cross_kernel(x_ref, mk_ref, mv_ref, g_ref, wq_ref, wo_ref, o_ref, q_scr, ox_scr, *, bb, tq):
    rows = bb * tq
    x = x_ref[...].reshape(rows, D_MODEL)
    q_scr[...] = _nn(_rms(x, g_ref[...]), wq_ref[...])
    _attend_tiles(q_scr, mk_ref, mv_ref, ox_scr, range(bb), tq)
    o_ref[...] = (x + _nn(ox_scr[...], wo_ref[...])).reshape(bb, tq, D_MODEL)


def _cross(x, mem_k, mem_v, g, wq, wo, bb, tq):
    nb, t, _ = x.shape
    x_spec = pl.BlockSpec((bb, tq, D_MODEL), lambda b, j: (b, j, 0))
    m_spec = pl.BlockSpec((bb, MEM_ROWS, LANES), lambda b, j: (b, 0, 0))
    vmem = _vmem(2 * _nbytes((bb, tq, D_MODEL)) + 2 * _nbytes((bb, MEM_ROWS, LANES)),
                 2 * _nbytes((D_MODEL, D_MODEL), BF16) + 2 * _nbytes((bb * tq, D_MODEL)),
                 6 * _nbytes((bb * tq, D_MODEL)) + bb * 8 * _nbytes((2 * X_HEADS * tq, MEM_ROWS)))
    return pl.pallas_call(
        functools.partial(_cross_kernel, bb=bb, tq=tq),
        grid=(nb // bb, t // tq),
        in_specs=[x_spec, m_spec, m_spec, _full((1, D_MODEL)), _full((D_MODEL, D_MODEL)), _full((D_MODEL, D_MODEL))],
        out_specs=x_spec,
        out_shape=jax.ShapeDtypeStruct(x.shape, F32),
        scratch_shapes=[pltpu.VMEM((bb * tq, D_MODEL), F32), pltpu.VMEM((bb * tq, D_MODEL), F32)],
        compiler_params=_cparams(("parallel", "arbitrary"), vmem),
        name="cross_attn",
    )(x, mem_k, mem_v, g, wq, wo)


MLP_FF_CHUNK = 1024


def _mlp_rows(x, g_ref, wu_ref, wd_ref, gf_ref, final_norm):
    h = _rms(x, g_ref[...]).astype(BF16)
    acc = x
    for c in range(0, D_FF, MLP_FF_CHUNK):
        u = jnp.maximum(jnp.dot(h, wu_ref[:, c:c + MLP_FF_CHUNK], preferred_element_type=F32), 0.0)
        acc = acc + _nn(u * u, wd_ref[c:c + MLP_FF_CHUNK, :])
    return _rms(acc, gf_ref[...]) if final_norm else acc


def _mlp_kernel(x_ref, g_ref, wu_ref, wd_ref, gf_ref, o_ref, *, final_norm):
    o_ref[...] = _mlp_rows(x_ref[...], g_ref, wu_ref, wd_ref, gf_ref, final_norm)


def _mlp(x, g, w_up, w_down, g_final, final_norm):
    n = x.shape[0]
    tm = min(ROW_TILE, n)
    row = pl.BlockSpec((tm, D_MODEL), lambda i: (i, 0))
    vmem = _vmem(2 * _nbytes((tm, D_MODEL)), 2 * _nbytes((D_MODEL, D_FF), BF16),
                 4 * _nbytes((tm, MLP_FF_CHUNK)) + 4 * _nbytes((tm, D_MODEL)))
    return pl.pallas_call(
        functools.partial(_mlp_kernel, final_norm=final_norm),
        grid=(n // tm,),
        in_specs=[row, _full((1, D_MODEL)), _full((D_MODEL, D_FF)), _full((D_FF, D_MODEL)), _full((1, D_MODEL))],
        out_specs=row,
        out_shape=jax.ShapeDtypeStruct((n, D_MODEL), F32),
        compiler_params=_cparams(("parallel",), vmem),
        name="mlp",
    )(x, g, w_up, w_down, g_final)


POST_TILE = 512


def _post_kernel(x_ref, zg_ref, oa_ref, ob_ref, mk_ref, mv_ref, wa_ref, wb_ref, wo_ref, gc_ref, wq_ref, wco_ref,
                 gm_ref, wu_ref, wd_ref, gf_ref, o_ref, q_scr, ox_scr, *, tq, final_norm):
    x1 = _merge_rows(x_ref[0], zg_ref[0], oa_ref[0], ob_ref[0], wa_ref, wb_ref, wo_ref)
    q_scr[...] = _nn(_rms(x1, gc_ref[...]), wq_ref[...])
    _attend_heads(q_scr, mk_ref, mv_ref, ox_scr, 0, 0, tq)
    x2 = x1 + _nn(ox_scr[...], wco_ref[...])
    o_ref[0] = _mlp_rows(x2, gm_ref, wu_ref, wd_ref, gf_ref, final_norm)


def _post(x, zg, o_ret, o_rwkv, mem_k, mem_v, w, g_final, final_norm):
    nb, t, _ = x.shape
    tq = min(POST_TILE, t)
    row = lambda width: pl.BlockSpec((1, tq, width), lambda b, j: (b, j, 0))
    mem = pl.BlockSpec((1, MEM_LEN, D_MODEL), lambda b, j: (b, 0, 0))
    sq = _full((D_MODEL, D_MODEL))
    vec = _full((1, D_MODEL))
    vmem = _vmem(_nbytes((tq, 2 * D_MODEL + GATE_W + RET_W + RWKV_W)) + 2 * _nbytes((MEM_LEN, D_MODEL), BF16),
                 _nbytes((4 * D_MODEL + 2 * D_FF, D_MODEL), BF16) + 2 * _nbytes((tq, D_MODEL)),
                 8 * _nbytes((tq, D_MODEL)) + 4 * _nbytes((tq, MLP_FF_CHUNK)))
    return pl.pallas_call(
        functools.partial(_post_kernel, tq=tq, final_norm=final_norm),
        grid=(nb, t // tq),
        in_specs=[row(D_MODEL), row(GATE_W), row(RET_W), row(RWKV_W), mem, mem,
                  _full((RET_W, D_MODEL)), _full((RWKV_W, D_MODEL)), sq, vec, sq, sq,
                  vec, _full((D_MODEL, D_FF)), _full((D_FF, D_MODEL)), vec],
        out_specs=row(D_MODEL),
        out_shape=jax.ShapeDtypeStruct(x.shape, F32),
        scratch_shapes=[pltpu.VMEM((tq, D_MODEL), F32), pltpu.VMEM((tq, D_MODEL), F32)],
        compiler_params=_cparams(("parallel", "arbitrary"), vmem),
        name="post",
    )(x, zg, o_ret, o_rwkv, mem_k, mem_v, w["w_branch_a"], w["w_branch_b"], w["w_out"], w["g_cross"], w["w_cq"],
      w["w_co"], w["g_mlp"], w["w_up"], w["w_down"], g_final)


class _Tiling(NamedTuple):
    seq_block: int
    time_block: int
    cross_block: int
    ret_group: int
    wkv_group: int


def _tiling(nb, t):
    if t > RET_CHUNK:
        tt = min(ROW_TILE, t)
        return _Tiling(seq_block=1, time_block=tt, cross_block=1,
                       ret_group=math.gcd(4, tt // RET_CHUNK), wkv_group=math.gcd(8, tt // WKV_BLOCK))
    seqs = WKV_BLOCK // t
    return _Tiling(seq_block=seqs, time_block=t, cross_block=math.gcd(4, nb), ret_group=seqs, wkv_group=1)


def _layer(x, pos, s_ret, s_wkv, s_shift, mem_k, mem_v, w, g_final, final_norm):
    nb, t, _ = x.shape
    n = nb * t
    tl = _tiling(nb, t)
    xf = x.reshape(n, D_MODEL)
    zg, zr, zw = _in_proj(xf, w["g_mix"], w["w_in"])
    o_ret, s_ret_new, o_wkv, s_wkv_new, shift_new = _mixers(
        zr.reshape(nb, t, 4 * RET_W), zw.reshape(nb, t, RWKV_IN_W), s_ret, s_shift.reshape(nb, 1, RWKV_IN_W), s_wkv,
        pos, w, tl)
    if mem_k.shape[1:] == (MEM_LEN, D_MODEL):
        y = _post(x, zg.reshape(nb, t, GATE_W), o_ret, o_wkv, mem_k, mem_v, w, g_final, final_norm)
    else:
        x1 = _merge(xf, zg, o_ret.reshape(n, RET_W), o_wkv.reshape(n, RWKV_W),
                    w["w_branch_a"], w["w_branch_b"], w["w_out"])
        x2 = _cross(x1.reshape(nb, t, D_MODEL), mem_k, mem_v, w["g_cross"], w["w_cq"], w["w_co"],
                    tl.cross_block, tl.time_block)
        y = _mlp(x2.reshape(n, D_MODEL), w["g_mlp"], w["w_up"], w["w_down"], g_final, final_norm).reshape(x.shape)
    return y, s_ret_new, s_wkv_new, shift_new.reshape(nb, RWKV_IN_W)


def _layer_weights(l, g_mix, w_in, w_branch_a, w_branch_b, w_out, mu_shift, w0, w_decay_up, a0, w_a_up, w_g_up,
                   k_k, k_a, r_k, lnx_g, lnx_b, g_cross, w_cq, w_co, g_mlp, w_up, w_down):
    vec = lambda v: v[l].reshape(1, -1).astype(F32)
    mat = lambda m: m[l].astype(BF16)
    return dict(g_mix=vec(g_mix), w_in=mat(w_in), w_branch_a=mat(w_branch_a), w_branch_b=mat(w_branch_b),
                w_out=mat(w_out), mu=vec(mu_shift), w0=vec(w0), w_decay_up=mat(w_decay_up), a0=vec(a0),
                w_a_up=mat(w_a_up), w_g_up=mat(w_g_up), k_k=vec(k_k), k_a=vec(k_a), r_k=vec(r_k), lnx_g=vec(lnx_g),
                lnx_b=vec(lnx_b), g_cross=vec(g_cross), w_cq=mat(w_cq), w_co=mat(w_co), g_mlp=vec(g_mlp),
                w_up=mat(w_up), w_down=mat(w_down))


def kernel(x_prompt, x_sample, mem_prompt, state_ret, state_wkv, state_shift, cache_mem_k, cache_mem_v, g_mix, w_in, w_branch_a, w_branch_b, w_out, mu_shift, w0, w_decay_up, a0, w_a_up, w_g_up, k_k, k_a, r_k, lnx_g, lnx_b, g_cross, g_mem, w_cq, w_ck, w_cv, w_co, g_mlp, w_up, w_down, g_final):
    depth = w_in.shape[0]
    bp, tp, _ = x_prompt.shape
    bs, ts, _ = x_sample.shape
    pos_p = jnp.arange(tp, dtype=jnp.int32)
    pos_s = PAST_LEN + jnp.arange(ts, dtype=jnp.int32)
    gf = g_final.reshape(1, D_MODEL)

    xp, xs = x_prompt, x_sample
    outs_p, outs_s = [], []
    for l in range(depth):
        w = _layer_weights(l, g_mix, w_in, w_branch_a, w_branch_b, w_out, mu_shift, w0, w_decay_up, a0, w_a_up,
                           w_g_up, k_k, k_a, r_k, lnx_g, lnx_b, g_cross, w_cq, w_co, g_mlp, w_up, w_down)
        last = l == depth - 1
        k_tiles, v_tiles, k_att, v_att = _mem_kv(mem_prompt.reshape(bp * MEM_LEN, D_MODEL),
                                                 g_mem[l].reshape(1, D_MODEL), w_ck[l].astype(BF16), w_cv[l].astype(BF16))
        xp, sr, sw, ss = _layer(
            xp, pos_p,
            jnp.zeros((bp, RET_HEADS, RET_HD, RET_HD), F32), jnp.zeros((bp, RWKV_HEADS, RWKV_HD, RWKV_HD), F32),
            jnp.zeros((bp, RWKV_IN_W), F32), k_att.reshape(bp, MEM_LEN, D_MODEL), v_att.reshape(bp, MEM_LEN, D_MODEL),
            w, gf, last)
        outs_p.append((sr, sw, ss, _from_tile_order(k_tiles.reshape(bp, MEM_ROWS, LANES)),
                       _from_tile_order(v_tiles.reshape(bp, MEM_ROWS, LANES))))
        xs, sr2, sw2, ss2 = _layer(
            xs, pos_s, state_ret[l], state_wkv[l], state_shift[l],
            _tile_order(cache_mem_k[l]), _tile_order(cache_mem_v[l]), w, gf, last)
        outs_s.append((sr2, sw2, ss2))

    stack = lambda items, i: jnp.stack([it[i] for it in items])
    return (xp, xs, stack(outs_p, 0), stack(outs_p, 1), stack(outs_p, 2), stack(outs_p, 3), stack(outs_p, 4),
            stack(outs_s, 0), stack(outs_s, 1), stack(outs_s, 2))
```

```python
import functools
import math
from typing import NamedTuple

import jax
import jax.numpy as jnp
from jax import lax
from jax.experimental import pallas as pl
from jax.experimental.pallas import tpu as pltpu

F32 = jnp.float32
BF16 = jnp.bfloat16

D_MODEL = 1024
PAST_LEN = 16384
RET_HEADS = 4
RET_HD = 128
RET_W = RET_HEADS * RET_HD
RET_CHUNK = 128
RET_GN_EPS = 1e-5
ROPE_BASE = 10000.0
RWKV_HEADS = 8
RWKV_HD = 64
RWKV_W = RWKV_HEADS * RWKV_HD
DECAY_LORA = 64
AAA_LORA = 64
GATE_LORA = 128
RWKV_GN_EPS = 64e-5
RWKV_IN_W = 3 * RWKV_W + DECAY_LORA + AAA_LORA + GATE_LORA
GATE_W = 2 * D_MODEL
O_RET = 2 * D_MODEL
O_RWKV = O_RET + 4 * RET_W
IN_W = O_RWKV + RWKV_IN_W
MEM_LEN = 256
X_HEADS = 4
X_HD = D_MODEL // X_HEADS
D_FF = 4 * D_MODEL
RMS_EPS = 1e-6

SUBLANES = 8
V7X_VMEM_BYTES = 64 * 1024 * 1024
VMEM_CAP_BYTES = V7X_VMEM_BYTES - 8 * 1024 * 1024
ROW_TILE = 512


def _cparams(sem, vmem_bytes):
    return pltpu.CompilerParams(dimension_semantics=sem, vmem_limit_bytes=int(min(vmem_bytes, VMEM_CAP_BYTES)))


def _nbytes(shape, dtype=F32):
    return math.prod(shape) * jnp.dtype(dtype).itemsize


def _nn(a, b):
    return jnp.dot(a.astype(BF16), b.astype(BF16), preferred_element_type=F32)


def _nt(a, b):
    return lax.dot_general(a.astype(BF16), b.astype(BF16), (((1,), (1,)), ((), ())), preferred_element_type=F32)


def _tn(a, b):
    return lax.dot_general(a.astype(BF16), b.astype(BF16), (((0,), (0,)), ((), ())), preferred_element_type=F32)


def _rms(x, g):
    return x * lax.rsqrt(jnp.mean(x * x, axis=-1, keepdims=True) + RMS_EPS) * g


def _head_norms(blocks, eps):
    rows, width = blocks[0].shape
    ones = jnp.ones((width, width), BF16)
    o = jnp.concatenate(blocks, axis=0)
    d = o - _nn(o, ones) * (1.0 / width)
    out = d * lax.rsqrt(_nn(d * d, ones) * (1.0 / width) + eps)
    return [out[i * rows:(i + 1) * rows] for i in range(len(blocks))]


def _full(shape):
    zeros = (0,) * len(shape)
    return pl.BlockSpec(shape, lambda *_: zeros, pipeline_mode=pl.Buffered(1))


def _vmem(pipelined, resident, temps):
    return 2 * pipelined + resident + temps


def _in_proj_kernel(x_ref, g_ref, w_ref, zg_ref, zr_ref, zw_ref):
    h = _rms(x_ref[...], g_ref[...]).astype(BF16)
    zg_ref[...] = jnp.dot(h, w_ref[:, 0:O_RET], preferred_element_type=F32)
    zr_ref[...] = jnp.dot(h, w_ref[:, O_RET:O_RWKV], preferred_element_type=F32)
    zw_ref[...] = jnp.dot(h, w_ref[:, O_RWKV:IN_W], preferred_element_type=F32)


def _in_proj(x, g, w_in):
    n = x.shape[0]
    tm = min(ROW_TILE, n)
    vmem = _vmem(_nbytes((tm, D_MODEL + IN_W)), _nbytes((D_MODEL, IN_W), BF16), _nbytes((tm, 4 * RET_W + D_MODEL)))
    return pl.pallas_call(
        _in_proj_kernel,
        grid=(n // tm,),
        in_specs=[pl.BlockSpec((tm, D_MODEL), lambda i: (i, 0)), _full((1, D_MODEL)), _full((D_MODEL, IN_W))],
        out_specs=[pl.BlockSpec((tm, GATE_W), lambda i: (i, 0)),
                   pl.BlockSpec((tm, 4 * RET_W), lambda i: (i, 0)),
                   pl.BlockSpec((tm, RWKV_IN_W), lambda i: (i, 0))],
        out_shape=[jax.ShapeDtypeStruct((n, GATE_W), F32), jax.ShapeDtypeStruct((n, 4 * RET_W), F32),
                   jax.ShapeDtypeStruct((n, RWKV_IN_W), F32)],
        compiler_params=_cparams(("parallel",), vmem),
        name="in_proj",
    )(x, g, w_in)


LANES = 128
MEM_ROWS = MEM_LEN * X_HEADS * (X_HD // LANES)
ROWS_PER_TOKEN = MEM_ROWS // MEM_LEN


def _tile_order(mem):
    nb = mem.shape[0]
    return (mem.reshape(nb, MEM_LEN, X_HEADS, X_HD // LANES, LANES).transpose(0, 1, 3, 2, 4)
            .reshape(nb, MEM_ROWS, LANES))


def _from_tile_order(raw):
    nb = raw.shape[0]
    return (raw.reshape(nb, MEM_LEN, X_HD // LANES, X_HEADS, LANES).transpose(0, 1, 3, 2, 4)
            .reshape(nb, MEM_LEN, X_HEADS, X_HD))


def _mem_kv_kernel(x_ref, g_ref, wk_ref, wv_ref, kt_ref, vt_ref, kb_ref, vb_ref, *, tm):
    h = _rms(x_ref[...], g_ref[...]).astype(BF16)
    for w_ref, t_ref, b_ref in ((wk_ref, kt_ref, kb_ref), (wv_ref, vt_ref, vb_ref)):
        y = jnp.dot(h, w_ref[...], preferred_element_type=F32)
        b_ref[...] = y.astype(BF16)
        for hd in range(X_HEADS):
            for c in range(X_HD // LANES):
                col = hd * X_HD + c * LANES
                t_ref[pl.ds(c * X_HEADS + hd, tm, stride=ROWS_PER_TOKEN), :] = y[:, col:col + LANES]


def _mem_kv(mem, g, wk, wv):
    n = mem.shape[0]
    tm = min(ROW_TILE, n)
    vmem = _vmem(4 * _nbytes((tm, D_MODEL)), 2 * _nbytes((D_MODEL, D_MODEL), BF16), 4 * _nbytes((tm, D_MODEL)))
    row = pl.BlockSpec((tm, D_MODEL), lambda i: (i, 0))
    tile = pl.BlockSpec((tm * ROWS_PER_TOKEN, LANES), lambda i: (i, 0))
    return pl.pallas_call(
        functools.partial(_mem_kv_kernel, tm=tm),
        grid=(n // tm,),
        in_specs=[row, _full((1, D_MODEL)), _full((D_MODEL, D_MODEL)), _full((D_MODEL, D_MODEL))],
        out_specs=[tile, tile, row, row],
        out_shape=[jax.ShapeDtypeStruct((n * ROWS_PER_TOKEN, LANES), F32)] * 2
        + [jax.ShapeDtypeStruct((n, D_MODEL), BF16)] * 2,
        compiler_params=_cparams(("parallel",), vmem),
        name="mem_kv",
    )(mem, g, wk, wv)


def _ret_body(zr_ref, cos_ref, sin_ref, dm_ref, qd_ref, kd_ref, cd_ref, o_ref, s_ref, *, bb, tt, chunk, group):
    nch = tt // chunk
    assert bb == 1 or nch == 1, "a block holds either one sequence or one chunk per sequence"
    assert (bb * nch) % group == 0
    heads = range(RET_HEADS)

    def body(j, carry):
        if nch == 1:
            items = [(j * group + n, 0) for n in range(group)]
        else:
            starts = [(j * group + n) * chunk for n in range(group)]
            items = [(0, r if isinstance(r, int) else pl.multiple_of(r, chunk)) for r in starts]
        q, k, v = [], [], []
        for b, r0 in items:
            rows = pl.ds(r0, chunk)
            cos = cos_ref[rows, :]
            sin = sin_ref[rows, :]
            for h in heads:
                qh = zr_ref[b, rows, h * RET_HD:(h + 1) * RET_HD]
                kh = zr_ref[b, rows, RET_W + h * RET_HD:RET_W + (h + 1) * RET_HD]
                q.append(qh * cos + pltpu.roll(qh, RET_HD // 2, axis=1) * sin)
                k.append((kh * cos + pltpu.roll(kh, RET_HD // 2, axis=1) * sin) * (RET_HD ** -0.5))
                v.append(zr_ref[b, rows, 2 * RET_W + h * RET_HD:2 * RET_W + (h + 1) * RET_HD])
        hd = [h for _ in items for h in heads]
        sc = [_nt(qi, ki) * dm_ref[h] for qi, ki, h in zip(q, k, hd)]
        kv = [_tn(ki * kd_ref[h], vi) for ki, vi, h in zip(k, v, hd)]
        inner = [_nn(si, vi) for si, vi in zip(sc, v)]
        states = []
        for n, (b, _) in enumerate(items):
            for h in heads:
                s = s_ref[b, h] if (nch == 1 or n == 0) else states[-RET_HEADS] * cd_ref[h] + kv[(n - 1) * RET_HEADS + h]
                states.append(s)
        last = len(items) - 1
        for n, (b, _) in enumerate(items):
            if nch == 1 or n == last:
                for h in heads:
                    i = n * RET_HEADS + h
                    s_ref[b, h] = states[i] * cd_ref[h] + kv[i]
        cross = [_nn(qi, si) * qd_ref[h] for qi, si, h in zip(q, states, hd)]
        normed = _head_norms([a + c for a, c in zip(inner, cross)], RET_GN_EPS)
        for n, (b, r0) in enumerate(items):
            rows = pl.ds(r0, chunk)
            for h in heads:
                g = zr_ref[b, rows, 3 * RET_W + h * RET_HD:3 * RET_W + (h + 1) * RET_HD]
                o_ref[b, rows, h * RET_HD:(h + 1) * RET_HD] = normed[n * RET_HEADS + h] * (g * jax.nn.sigmoid(g))
        return carry

    if bb * nch == group:
        body(0, 0)
    else:
        lax.fori_loop(0, bb * nch // group, body, 0)


def _ret_consts(chunk):
    lg = jnp.log1p(-jnp.exp2(-5.0 - jnp.arange(RET_HEADS, dtype=F32)))
    idx = jnp.arange(chunk, dtype=F32)
    diff = idx[:, None] - idx[None, :]
    dmask = jnp.where(diff[None] >= 0, jnp.exp(jnp.maximum(diff, 0.0)[None] * lg[:, None, None]), 0.0)
    q_dec = jnp.exp((idx + 1.0)[None, :] * lg[:, None])
    k_dec = jnp.exp((chunk - 1.0 - idx)[None, :] * lg[:, None])
    c_dec = jnp.exp(chunk * lg)
    bc = lambda t: jnp.broadcast_to(t[:, :, None], (RET_HEADS, t.shape[1], RET_HD))
    return dmask, bc(q_dec), bc(k_dec), bc(c_dec[:, None])


def _rope_tables(pos):
    half = RET_HD // 2
    inv = 1.0 / (ROPE_BASE ** (jnp.arange(half, dtype=F32) / half))
    ang = pos.astype(F32)[:, None] * inv[None, :]
    cos, sin = jnp.cos(ang), jnp.sin(ang)
    return jnp.concatenate([cos, cos], axis=1), jnp.concatenate([-sin, sin], axis=1)


WKV_BLOCK = 64
WKV_SCRATCH = 10


def _pow2(n):
    return n & (n - 1) == 0


def _imod(x, n):
    return jnp.bitwise_and(x, n - 1) if _pow2(n) else lax.rem(x, n)


def _idiv(x, n):
    return jnp.right_shift(x, n.bit_length() - 1) if _pow2(n) else lax.div(x, n)


def _wkv_body(zw_ref, mu_ref, w0_ref, a0_ref, kk_ref, ka_ref, rk_ref, lng_ref, lnb_ref,
              wd_ref, wa_ref, wg_ref, tri_ref, seg_ref, o_ref, s_ref, sh_ref,
              at_s, rt_s, bt_s, kt_s, bh_s, kh_s, v_s, g_s, bonus_s, pend_s, *, bb, tt, sub, group):
    C = WKV_BLOCK
    R = bb * tt
    nblk = R // C
    nseq = C // sub
    assert (bb == 1 and sub == C) or (nblk == 1 and sub == tt), "one sequence per tile, or whole sequences in one block"
    assert nblk % group == 0
    n_sq = max(int(math.log2(sub)) - 1, 0)
    heads = range(RWKV_HEADS)
    hsl = [slice(h * RWKV_HD, (h + 1) * RWKV_HD) for h in heads]
    half = RWKV_W // 2

    def seg_sum(t):
        seg = seg_ref[...]
        return jnp.concatenate([_nn(t[:, :half], seg), _nn(t[:, half:], seg)], axis=1)

    pw = zw_ref[...].reshape(R, RWKV_IN_W)
    row = lax.broadcasted_iota(jnp.int32, (R, RWKV_IN_W), 0)
    rolled = pltpu.roll(pw, 1, axis=0)
    if bb == 1:
        prev = jnp.where(row == 0, sh_ref[0], rolled)
        sh_ref[0] = pw[R - 1:R, :]
    else:
        carry = jnp.broadcast_to(sh_ref[...], (bb, tt, RWKV_IN_W)).reshape(R, RWKV_IN_W)
        prev = jnp.where(_imod(row, tt) == 0, carry, rolled)
        sh_ref[...] = pw.reshape(bb, tt, RWKV_IN_W)[:, tt - 1:tt, :]
    xm = pw + mu_ref[...] * (prev - pw)
    r = xm[:, 0:RWKV_W]
    kb = xm[:, RWKV_W:2 * RWKV_W]
    o1 = 3 * RWKV_W
    wl = xm[:, o1:o1 + DECAY_LORA]
    al = xm[:, o1 + DECAY_LORA:o1 + DECAY_LORA + AAA_LORA]
    gl = xm[:, o1 + DECAY_LORA + AAA_LORA:RWKV_IN_W]
    logw = -math.exp(-0.5) * jax.nn.sigmoid(w0_ref[...] + _nn(jnp.tanh(wl), wd_ref[...]))
    a = jax.nn.sigmoid(a0_ref[...] + _nn(al, wa_ref[...]))
    l1 = logw.astype(BF16)
    rem = logw - l1.astype(F32)
    l2 = rem.astype(BF16)
    l3 = (rem - l2.astype(F32)).astype(BF16)
    tri = tri_ref[...]
    cum = jnp.concatenate(
        [sum(jnp.dot(tri, part[k * C:(k + 1) * C], preferred_element_type=F32) for part in (l1, l2, l3))
         for k in range(nblk)], axis=0)
    cum_end = jnp.broadcast_to(cum.reshape(R // sub, sub, RWKV_W)[:, sub - 1:sub, :],
                               (R // sub, sub, RWKV_W)).reshape(R, RWKV_W)
    kk = kb * kk_ref[...]
    kk = kk * jnp.minimum(lax.rsqrt(seg_sum(kk * kk)), 1e12)
    km = kb * (1.0 + (a - 1.0) * ka_ref[...])
    bv = kk * a
    p_end = jnp.exp(cum_end)
    e_neg = jnp.exp(-cum)
    e_end = p_end * e_neg
    at_s[...] = -kk * jnp.exp(cum - logw)
    rt_s[...] = r * jnp.exp(cum)
    bt_s[...] = bv * e_neg
    kt_s[...] = km * e_neg
    bh_s[...] = bv * e_end
    kh_s[...] = km * e_end
    v_s[...] = xm[:, 2 * RWKV_W:3 * RWKV_W]
    g_s[...] = _nn(jax.nn.sigmoid(gl), wg_ref[...])
    bonus_s[...] = seg_sum(r * km * rk_ref[...])
    pend_s[...] = p_end

    row_id = lax.broadcasted_iota(jnp.int32, (C, C), 0)
    col_id = lax.broadcasted_iota(jnp.int32, (C, C), 1)
    strict = row_id > col_id
    incl = row_id >= col_id
    if nseq > 1:
        same = _idiv(row_id, sub) == _idiv(col_id, sub)
        strict = jnp.logical_and(strict, same)
        incl = jnp.logical_and(incl, same)
    eye = jnp.where(row_id == col_id, 1.0, 0.0).astype(F32)

    def triangular(r0s):
        rows = [pl.ds(r0, C) for r0 in r0s for _ in heads]
        sl = [s for _ in r0s for s in hsl]
        at = [at_s[rw, :][:, s] for rw, s in zip(rows, sl)]
        rt = [rt_s[rw, :][:, s] for rw, s in zip(rows, sl)]
        v = [v_s[rw, :][:, s] for rw, s in zip(rows, sl)]
        bk_t = [jnp.concatenate([bt_s[pl.ds(r0, C), :], kt_s[pl.ds(r0, C), :]], axis=0).T for r0 in r0s]
        m1 = [_nn(jnp.concatenate([a_, r_], axis=0), bk[s, :])
              for a_, r_, bk, s in zip(at, rt, [t for t in bk_t for _ in heads], sl)]
        low = [jnp.where(strict, m[:C, :C], 0.0) for m in m1]
        rab = [jnp.where(incl, m[C:, :C], 0.0) for m in m1]
        akk = [jnp.concatenate([jnp.where(strict, m[:C, C:], 0.0), jnp.where(incl, m[C:, C:], 0.0)], axis=0)
               for m in m1]
        akv = [_nn(k_, v_) for k_, v_ in zip(akk, v)]
        tinv = [eye + l for l in low]
        if n_sq:
            x = [_nn(l, l) for l in low]
        for j in range(n_sq):
            if j < n_sq - 1:
                xt = [_nn(jnp.concatenate([xi, ti], axis=0), xi) for xi, ti in zip(x, tinv)]
                x = [t[:C] for t in xt]
                tinv = [ti + t[C:] for ti, t in zip(tinv, xt)]
            else:
                tinv = [ti + _nn(ti, xi) for ti, xi in zip(tinv, x)]
        ua = [_nn(t, a_) for t, a_ in zip(tinv, at)]
        uv = [_nn(t, k_[:C]) for t, k_ in zip(tinv, akv)]
        oa = [r_ + _nn(rb, u_) for r_, rb, u_ in zip(rt, rab, ua)]
        ov = [k_[C:] + _nn(rb, u_) for k_, rb, u_ in zip(akv, rab, uv)]
        nh = RWKV_HEADS
        return [dict(ua=ua[k * nh:(k + 1) * nh], uv=uv[k * nh:(k + 1) * nh], oa=oa[k * nh:(k + 1) * nh],
                     ov=ov[k * nh:(k + 1) * nh], v=v[k * nh:(k + 1) * nh]) for k in range(len(r0s))]

    def advance(r0, q):
        rw = pl.ds(r0, C)
        bh, kh, pend = bh_s[rw, :], kh_s[rw, :], pend_s[rw, :]
        items = [(n, h) for n in range(nseq) for h in heads]
        rs = [slice(n * sub, (n + 1) * sub) for n, _ in items]
        uo = [_nt(jnp.concatenate([q["ua"][h][s], q["oa"][h][s]], axis=0), s_ref[n, h])
              for (n, h), s in zip(items, rs)]
        u = [m[:sub] + q["uv"][h][s] for m, (n, h), s in zip(uo, items, rs)]
        upd = [_tn(jnp.concatenate([u_, q["v"][h][s]], axis=0),
                   jnp.concatenate([bh[s, hsl[h]], kh[s, hsl[h]]], axis=0)) for u_, (n, h), s in zip(u, items, rs)]
        for up, (n, h), s in zip(upd, items, rs):
            s_ref[n, h] = s_ref[n, h] * pend[s, hsl[h]][0:1] + up
        o = [m[sub:] + q["ov"][h][s] for m, (n, h), s in zip(uo, items, rs)]
        o = jnp.concatenate([jnp.concatenate([o[n * RWKV_HEADS + h] for n in range(nseq)], axis=0) for h in heads],
                            axis=1)
        d = o - seg_sum(o) * (1.0 / RWKV_HD)
        var = seg_sum(d * d) * (1.0 / RWKV_HD)
        o = d * lax.rsqrt(var + RWKV_GN_EPS) * lng_ref[...] + lnb_ref[...]
        o = (o + bonus_s[rw, :] * v_s[rw, :]) * g_s[rw, :]
        if bb == 1:
            o_ref[0, rw, :] = o
        else:
            o_ref[...] = o.reshape(bb, tt, RWKV_W)

    def body(j, carry):
        r0s = [j * group * C + k * C for k in range(group)]
        r0s = [r if isinstance(r, int) else pl.multiple_of(r, C) for r in r0s]
        for r0, q in zip(r0s, triangular(r0s)):
            advance(r0, q)
        return carry

    if nblk == group:
        body(0, 0)
    else:
        lax.fori_loop(0, nblk // group, body, 0)


RET_INPUTS = 7
WKV_INPUTS = 14


def _mixers_kernel(*refs, ret, wkv):
    it = iter(refs)
    take = lambda n: [next(it) for _ in range(n)]
    ret_s0, = take(1)
    ret_in = take(RET_INPUTS)
    wkv_sh0, wkv_s0 = take(2)
    wkv_in = take(WKV_INPUTS)
    ret_o, ret_s, wkv_o, wkv_s, wkv_sh = take(5)
    scratch = take(WKV_SCRATCH)

    @pl.when(pl.program_id(1) == 0)
    def _():
        ret_s[...] = ret_s0[...]
        wkv_s[...] = wkv_s0[...]
        wkv_sh[...] = wkv_sh0[...]

    _ret_body(*ret_in, ret_o, ret_s, **ret)
    _wkv_body(*wkv_in, wkv_o, wkv_s, wkv_sh, *scratch, **wkv)


def _mixers(zr, zw, ret_s0, wkv_sh0, wkv_s0, pos, p, tl):
    nb, t, _ = zr.shape
    bb, tt = tl.seq_block, tl.time_block
    rows = bb * tt
    chunk = math.gcd(t, RET_CHUNK)
    cos, sin = _rope_tables(pos)
    dmask, q_dec, k_dec, c_dec = _ret_consts(chunk)
    sub = min(t, WKV_BLOCK)
    step = jnp.arange(WKV_BLOCK)
    tri = ((step[:, None] >= step[None, :]) & (step[:, None] // sub == step[None, :] // sub)).astype(BF16)
    lane_head = jnp.arange(RWKV_W // 2) // RWKV_HD
    seg = (lane_head[:, None] == lane_head[None, :]).astype(BF16)

    tile = lambda width: pl.BlockSpec((bb, tt, width), lambda b, j: (b, j, 0))
    ret_st = pl.BlockSpec((bb, RET_HEADS, RET_HD, RET_HD), lambda b, j: (b, 0, 0, 0))
    wkv_st = pl.BlockSpec((bb, RWKV_HEADS, RWKV_HD, RWKV_HD), lambda b, j: (b, 0, 0, 0))
    sh_spec = pl.BlockSpec((bb, 1, RWKV_IN_W), lambda b, j: (b, 0, 0))
    tab_spec = pl.BlockSpec((tt, RET_HD), lambda b, j: (j, 0))
    vec = _full((1, RWKV_W))
    vmem = _vmem(_nbytes((rows, 5 * RET_W + RWKV_IN_W + RWKV_W)) + 2 * _nbytes((bb, RET_HEADS, RET_HD, RET_HD))
                 + 2 * _nbytes((bb, RWKV_HEADS, RWKV_HD, RWKV_HD)) + 2 * _nbytes((bb, SUBLANES, RWKV_IN_W))
                 + 2 * _nbytes((tt, RET_HD)),
                 _nbytes(dmask.shape) + 3 * _nbytes(q_dec.shape) + 9 * _nbytes((SUBLANES, RWKV_IN_W))
                 + _nbytes((DECAY_LORA + AAA_LORA + GATE_LORA + WKV_BLOCK, RWKV_W), BF16) + _nbytes(seg.shape, BF16)
                 + WKV_SCRATCH * _nbytes((rows, RWKV_W)),
                 4 * _nbytes((rows, RWKV_IN_W)) + 8 * _nbytes((rows, RWKV_W))
                 + tl.wkv_group * 32 * _nbytes((WKV_BLOCK, RWKV_IN_W)) + tl.ret_group * 8 * _nbytes((chunk, 4 * RET_W)))
    return pl.pallas_call(
        functools.partial(_mixers_kernel,
                          ret=dict(bb=bb, tt=tt, chunk=chunk, group=tl.ret_group),
                          wkv=dict(bb=bb, tt=tt, sub=sub, group=tl.wkv_group)),
        grid=(nb // bb, t // tt),
        in_specs=[ret_st, tile(4 * RET_W), tab_spec, tab_spec,
                  _full(dmask.shape), _full(q_dec.shape), _full(k_dec.shape), _full(c_dec.shape),
                  sh_spec, wkv_st, tile(RWKV_IN_W),
                  _full((1, RWKV_IN_W)), vec, vec, vec, vec, vec, vec, vec,
                  _full((DECAY_LORA, RWKV_W)), _full((AAA_LORA, RWKV_W)), _full((GATE_LORA, RWKV_W)),
                  _full(tri.shape), _full(seg.shape)],
        out_specs=[tile(RET_W), ret_st, tile(RWKV_W), wkv_st, sh_spec],
        out_shape=[jax.ShapeDtypeStruct((nb, t, RET_W), F32), jax.ShapeDtypeStruct(ret_s0.shape, F32),
                   jax.ShapeDtypeStruct((nb, t, RWKV_W), F32), jax.ShapeDtypeStruct(wkv_s0.shape, F32),
                   jax.ShapeDtypeStruct(wkv_sh0.shape, F32)],
        scratch_shapes=[pltpu.VMEM((rows, RWKV_W), F32)] * WKV_SCRATCH,
        compiler_params=_cparams(("parallel", "arbitrary"), vmem),
        name="mixers",
    )(ret_s0, zr, cos, sin, dmask, q_dec, k_dec, c_dec,
      wkv_sh0, wkv_s0, zw, p["mu"], p["w0"], p["a0"], p["k_k"], p["k_a"], p["r_k"], p["lnx_g"], p["lnx_b"],
      p["w_decay_up"], p["w_a_up"], p["w_g_up"], tri, seg)


def _merge_rows(x, zg, oa, ob, wa_ref, wb_ref, wo_ref):
    ga = jax.nn.sigmoid(zg[:, 0:D_MODEL])
    gb = jax.nn.sigmoid(zg[:, D_MODEL:GATE_W])
    merged = ga * _nn(oa, wa_ref[...]) + gb * _nn(ob, wb_ref[...])
    return x + _nn(merged, wo_ref[...])


def _merge_kernel(x_ref, zg_ref, oa_ref, ob_ref, wa_ref, wb_ref, wo_ref, o_ref):
    o_ref[...] = _merge_rows(x_ref[...], zg_ref[...], oa_ref[...], ob_ref[...], wa_ref, wb_ref, wo_ref)


def _merge(x, zg, o_ret, o_rwkv, wa, wb, wo):
    n = x.shape[0]
    tm = min(ROW_TILE, n)
    row = lambda w: pl.BlockSpec((tm, w), lambda i: (i, 0))
    vmem = _vmem(_nbytes((tm, 2 * D_MODEL + GATE_W + RET_W + RWKV_W)), _nbytes((2 * D_MODEL, D_MODEL), BF16),
                 6 * _nbytes((tm, D_MODEL)))
    return pl.pallas_call(
        _merge_kernel,
        grid=(n // tm,),
        in_specs=[row(D_MODEL), row(GATE_W), row(RET_W), row(RWKV_W),
                  _full((RET_W, D_MODEL)), _full((RWKV_W, D_MODEL)), _full((D_MODEL, D_MODEL))],
        out_specs=row(D_MODEL),
        out_shape=jax.ShapeDtypeStruct((n, D_MODEL), F32),
        compiler_params=_cparams(("parallel",), vmem),
        name="merge",
    )(x, zg, o_ret, o_rwkv, wa, wb, wo)


def _softmax(sc):
    e = jnp.exp(sc - jnp.max(sc, axis=-1, keepdims=True))
    return e / jnp.sum(e, axis=-1, keepdims=True)


def _attend_heads(q_scr, mk_ref, mv_ref, ox_scr, b, r0, tq):
    sls = [slice(h * X_HD, (h + 1) * X_HD) for h in range(X_HEADS)]
    sc = [_nt(q_scr[pl.ds(r0, tq), sl], mk_ref[b, :, sl]) for sl in sls]
    att = [_softmax(s * (X_HD ** -0.5)) for s in sc]
    for sl, a in zip(sls, att):
        ox_scr[pl.ds(r0, tq), sl] = _nn(a, mv_ref[b, :, sl])


def _attend_tiles(q_scr, mk_ref, mv_ref, ox_scr, seqs, tq):
    halves = X_HD // LANES
    n = X_HEADS * tq
    slot = lax.broadcasted_iota(jnp.int32, (n, MEM_ROWS), 1) % ROWS_PER_TOKEN
    head = lax.broadcasted_iota(jnp.int32, (n, MEM_ROWS), 0) // tq
    own = slot == head
    other = slot == head + X_HEADS
    qs = []
    for b in seqs:
        q = q_scr[b * tq:(b + 1) * tq, :]
        qs.append(jnp.concatenate([q[:, h * X_HD + c * LANES:h * X_HD + (c + 1) * LANES]
                                   for c in range(halves) for h in range(X_HEADS)], axis=0))
    z = [_nt(q, mk_ref[b]) for q, b in zip(qs, seqs)]
    part = [jnp.where(own, zi[:n], 0.0) + jnp.where(other, zi[n:], 0.0) for zi in z]
    sc = [p + pltpu.roll(p, MEM_ROWS - X_HEADS, axis=1) for p in part]
    att = [_softmax(jnp.where(own, s * (X_HD ** -0.5), -jnp.inf)) for s in sc]
    att2 = [jnp.concatenate([a, pltpu.roll(a, X_HEADS, axis=1)], axis=0) for a in att]
    o = [_nn(a, mv_ref[b]) for a, b in zip(att2, seqs)]
    for oi, b in zip(o, seqs):
        ox_scr[b * tq:(b + 1) * tq, :] = jnp.concatenate(
            [oi[(c * X_HEADS + h) * tq:(c * X_HEADS + h + 1) * tq] for h in range(X_HEADS) for c in range(halves)],
            axis=1)


def _cross_kernel(x_ref, mk_ref, mv_ref, g_ref, wq_ref, wo_ref, o_ref, q_scr, ox_scr, *, bb, tq):
    rows = bb * tq
    x = x_ref[...].reshape(rows, D_MODEL)
    q_scr[...] = _nn(_rms(x, g_ref[...]), wq_ref[...])
    _attend_tiles(q_scr, mk_ref, mv_ref, ox_scr, range(bb), tq)
    o_ref[...] = (x + _nn(ox_scr[...], wo_ref[...])).reshape(bb, tq, D_MODEL)


def _cross(x, mem_k, mem_v, g, wq, wo, bb, tq):
    nb, t, _ = x.shape
    x_spec = pl.BlockSpec((bb, tq, D_MODEL), lambda b, j: (b, j, 0))
    m_spec = pl.BlockSpec((bb, MEM_ROWS, LANES), lambda b, j: (b, 0, 0))
    vmem = _vmem(2 * _nbytes((bb, tq, D_MODEL)) + 2 * _nbytes((bb, MEM_ROWS, LANES)),
                 2 * _nbytes((D_MODEL, D_MODEL), BF16) + 2 * _nbytes((bb * tq, D_MODEL)),
                 6 * _nbytes((bb * tq, D_MODEL)) + bb * 8 * _nbytes((2 * X_HEADS * tq, MEM_ROWS)))
    return pl.pallas_call(
        functools.partial(_cross_kernel, bb=bb, tq=tq),
        grid=(nb // bb, t // tq),
        in_specs=[x_spec, m_spec, m_spec, _full((1, D_MODEL)), _full((D_MODEL, D_MODEL)), _full((D_MODEL, D_MODEL))],
        out_specs=x_spec,
        out_shape=jax.ShapeDtypeStruct(x.shape, F32),
        scratch_shapes=[pltpu.VMEM((bb * tq, D_MODEL), F32), pltpu.VMEM((bb * tq, D_MODEL), F32)],
        compiler_params=_cparams(("parallel", "arbitrary"), vmem),
        name="cross_attn",
    )(x, mem_k, mem_v, g, wq, wo)


MLP_FF_CHUNK = 1024


def _mlp_rows(x, g_ref, wu_ref, wd_ref, gf_ref, final_norm):
    h = _rms(x, g_ref[...]).astype(BF16)
    acc = x
    for c in range(0, D_FF, MLP_FF_CHUNK):
        u = jnp.maximum(jnp.dot(h, wu_ref[:, c:c + MLP_FF_CHUNK], preferred_element_type=F32), 0.0)
        acc = acc + _nn(u * u, wd_ref[c:c + MLP_FF_CHUNK, :])
    return _rms(acc, gf_ref[...]) if final_norm else acc


def _mlp_kernel(x_ref, g_ref, wu_ref, wd_ref, gf_ref, o_ref, *, final_norm):
    o_ref[...] = _mlp_rows(x_ref[...], g_ref, wu_ref, wd_ref, gf_ref, final_norm)


def _mlp(x, g, w_up, w_down, g_final, final_norm):
    n = x.shape[0]
    tm = min(ROW_TILE, n)
    row = pl.BlockSpec((tm, D_MODEL), lambda i: (i, 0))
    vmem = _vmem(2 * _nbytes((tm, D_MODEL)), 2 * _nbytes((D_MODEL, D_FF), BF16),
                 4 * _nbytes((tm, MLP_FF_CHUNK)) + 4 * _nbytes((tm, D_MODEL)))
    return pl.pallas_call(
        functools.partial(_mlp_kernel, final_norm=final_norm),
        grid=(n // tm,),
        in_specs=[row, _full((1, D_MODEL)), _full((D_MODEL, D_FF)), _full((D_FF, D_MODEL)), _full((1, D_MODEL))],
        out_specs=row,
        out_shape=jax.ShapeDtypeStruct((n, D_MODEL), F32),
        compiler_params=_cparams(("parallel",), vmem),
        name="mlp",
    )(x, g, w_up, w_down, g_final)


POST_TILE = 512


def _post_kernel(x_ref, zg_ref, oa_ref, ob_ref, mk_ref, mv_ref, wa_ref, wb_ref, wo_ref, gc_ref, wq_ref, wco_ref,
                 gm_ref, wu_ref, wd_ref, gf_ref, o_ref, q_scr, ox_scr, *, tq, final_norm):
    x1 = _merge_rows(x_ref[0], zg_ref[0], oa_ref[0], ob_ref[0], wa_ref, wb_ref, wo_ref)
    q_scr[...] = _nn(_rms(x1, gc_ref[...]), wq_ref[...])
    _attend_heads(q_scr, mk_ref, mv_ref, ox_scr, 0, 0, tq)
    x2 = x1 + _nn(ox_scr[...], wco_ref[...])
    o_ref[0] = _mlp_rows(x2, gm_ref, wu_ref, wd_ref, gf_ref, final_norm)


def _post(x, zg, o_ret, o_rwkv, mem_k, mem_v, w, g_final, final_norm):
    nb, t, _ = x.shape
    tq = min(POST_TILE, t)
    row = lambda width: pl.BlockSpec((1, tq, width), lambda b, j: (b, j, 0))
    mem = pl.BlockSpec((1, MEM_LEN, D_MODEL), lambda b, j: (b, 0, 0))
    sq = _full((D_MODEL, D_MODEL))
    vec = _full((1, D_MODEL))
    vmem = _vmem(_nbytes((tq, 2 * D_MODEL + GATE_W + RET_W + RWKV_W)) + 2 * _nbytes((MEM_LEN, D_MODEL), BF16),
                 _nbytes((4 * D_MODEL + 2 * D_FF, D_MODEL), BF16) + 2 * _nbytes((tq, D_MODEL)),
                 8 * _nbytes((tq, D_MODEL)) + 4 * _nbytes((tq, MLP_FF_CHUNK)))
    return pl.pallas_call(
        functools.partial(_post_kernel, tq=tq, final_norm=final_norm),
        grid=(nb, t // tq),
        in_specs=[row(D_MODEL), row(GATE_W), row(RET_W), row(RWKV_W), mem, mem,
                  _full((RET_W, D_MODEL)), _full((RWKV_W, D_MODEL)), sq, vec, sq, sq,
                  vec, _full((D_MODEL, D_FF)), _full((D_FF, D_MODEL)), vec],
        out_specs=row(D_MODEL),
        out_shape=jax.ShapeDtypeStruct(x.shape, F32),
        scratch_shapes=[pltpu.VMEM((tq, D_MODEL), F32), pltpu.VMEM((tq, D_MODEL), F32)],
        compiler_params=_cparams(("parallel", "arbitrary"), vmem),
        name="post",
    )(x, zg, o_ret, o_rwkv, mem_k, mem_v, w["w_branch_a"], w["w_branch_b"], w["w_out"], w["g_cross"], w["w_cq"],
      w["w_co"], w["g_mlp"], w["w_up"], w["w_down"], g_final)


class _Tiling(NamedTuple):
    seq_block: int
    time_block: int
    cross_block: int
    ret_group: int
    wkv_group: int


def _tiling(nb, t):
    if t > RET_CHUNK:
        tt = min(ROW_TILE, t)
        return _Tiling(seq_block=1, time_block=tt, cross_block=1,
                       ret_group=math.gcd(4, tt // RET_CHUNK), wkv_group=math.gcd(8, tt // WKV_BLOCK))
    seqs = WKV_BLOCK // t
    return _Tiling(seq_block=seqs, time_block=t, cross_block=math.gcd(4, nb), ret_group=seqs, wkv_group=1)


def _layer(x, pos, s_ret, s_wkv, s_shift, mem_k, mem_v, w, g_final, final_norm):
    nb, t, _ = x.shape
    n = nb * t
    tl = _tiling(nb, t)
    xf = x.reshape(n, D_MODEL)
    zg, zr, zw = _in_proj(xf, w["g_mix"], w["w_in"])
    o_ret, s_ret_new, o_wkv, s_wkv_new, shift_new = _mixers(
        zr.reshape(nb, t, 4 * RET_W), zw.reshape(nb, t, RWKV_IN_W), s_ret, s_shift.reshape(nb, 1, RWKV_IN_W), s_wkv,
        pos, w, tl)
    if mem_k.shape[1:] == (MEM_LEN, D_MODEL):
        y = _post(x, zg.reshape(nb, t, GATE_W), o_ret, o_wkv, mem_k, mem_v, w, g_final, final_norm)
    else:
        x1 = _merge(xf, zg, o_ret.reshape(n, RET_W), o_wkv.reshape(n, RWKV_W),
                    w["w_branch_a"], w["w_branch_b"], w["w_out"])
        x2 = _cross(x1.reshape(nb, t, D_MODEL), mem_k, mem_v, w["g_cross"], w["w_cq"], w["w_co"],
                    tl.cross_block, tl.time_block)
        y = _mlp(x2.reshape(n, D_MODEL), w["g_mlp"], w["w_up"], w["w_down"], g_final, final_norm).reshape(x.shape)
    return y, s_ret_new, s_wkv_new, shift_new.reshape(nb, RWKV_IN_W)


def _layer_weights(l, g_mix, w_in, w_branch_a, w_branch_b, w_out, mu_shift, w0, w_decay_up, a0, w_a_up, w_g_up,
                   k_k, k_a, r_k, lnx_g, lnx_b, g_cross, w_cq, w_co, g_mlp, w_up, w_down):
    vec = lambda v: v[l].reshape(1, -1).astype(F32)
    mat = lambda m: m[l].astype(BF16)
    return dict(g_mix=vec(g_mix), w_in=mat(w_in), w_branch_a=mat(w_branch_a), w_branch_b=mat(w_branch_b),
                w_out=mat(w_out), mu=vec(mu_shift), w0=vec(w0), w_decay_up=mat(w_decay_up), a0=vec(a0),
                w_a_up=mat(w_a_up), w_g_up=mat(w_g_up), k_k=vec(k_k), k_a=vec(k_a), r_k=vec(r_k), lnx_g=vec(lnx_g),
                lnx_b=vec(lnx_b), g_cross=vec(g_cross), w_cq=mat(w_cq), w_co=mat(w_co), g_mlp=vec(g_mlp),
                w_up=mat(w_up), w_down=mat(w_down))


def kernel(x_prompt, x_sample, mem_prompt, state_ret, state_wkv, state_shift, cache_mem_k, cache_mem_v, g_mix, w_in, w_branch_a, w_branch_b, w_out, mu_shift, w0, w_decay_up, a0, w_a_up, w_g_up, k_k, k_a, r_k, lnx_g, lnx_b, g_cross, g_mem, w_cq, w_ck, w_cv, w_co, g_mlp, w_up, w_down, g_final):
    depth = w_in.shape[0]
    bp, tp, _ = x_prompt.shape
    bs, ts, _ = x_sample.shape
    pos_p = jnp.arange(tp, dtype=jnp.int32)
    pos_s = PAST_LEN + jnp.arange(ts, dtype=jnp.int32)
    gf = g_final.reshape(1, D_MODEL)

    xp, xs = x_prompt, x_sample
    outs_p, outs_s = [], []
    for l in range(depth):
        w = _layer_weights(l, g_mix, w_in, w_branch_a, w_branch_b, w_out, mu_shift, w0, w_decay_up, a0, w_a_up,
                           w_g_up, k_k, k_a, r_k, lnx_g, lnx_b, g_cross, w_cq, w_co, g_mlp, w_up, w_down)
        last = l == depth - 1
        k_tiles, v_tiles, k_att, v_att = _mem_kv(mem_prompt.reshape(bp * MEM_LEN, D_MODEL),
                                                 g_mem[l].reshape(1, D_MODEL), w_ck[l].astype(BF16), w_cv[l].astype(BF16))
        xp, sr, sw, ss = _layer(
            xp, pos_p,
            jnp.zeros((bp, RET_HEADS, RET_HD, RET_HD), F32), jnp.zeros((bp, RWKV_HEADS, RWKV_HD, RWKV_HD), F32),
            jnp.zeros((bp, RWKV_IN_W), F32), k_att.reshape(bp, MEM_LEN, D_MODEL), v_att.reshape(bp, MEM_LEN, D_MODEL),
            w, gf, last)
        outs_p.append((sr, sw, ss, _from_tile_order(k_tiles.reshape(bp, MEM_ROWS, LANES)),
                       _from_tile_order(v_tiles.reshape(bp, MEM_ROWS, LANES))))
        xs, sr2, sw2, ss2 = _layer(
            xs, pos_s, state_ret[l], state_wkv[l], state_shift[l],
            _tile_order(cache_mem_k[l]), _tile_order(cache_mem_v[l]), w, gf, last)
        outs_s.append((sr2, sw2, ss2))

    stack = lambda items, i: jnp.stack([it[i] for it in items])
    return (xp, xs, stack(outs_p, 0), stack(outs_p, 1), stack(outs_p, 2), stack(outs_p, 3), stack(outs_p, 4),
            stack(outs_s, 0), stack(outs_s, 1), stack(outs_s, 2))
```

```python
import functools
import math
from typing import NamedTuple

import jax
import jax.numpy as jnp
from jax import lax
from jax.experimental import pallas as pl
from jax.experimental.pallas import tpu as pltpu

F32 = jnp.float32
BF16 = jnp.bfloat16

D_MODEL = 1024
PAST_LEN = 16384
RET_HEADS = 4
RET_HD = 128
RET_W = RET_HEADS * RET_HD
RET_CHUNK = 128
RET_GN_EPS = 1e-5
ROPE_BASE = 10000.0
RWKV_HEADS = 8
RWKV_HD = 64
RWKV_W = RWKV_HEADS * RWKV_HD
DECAY_LORA = 64
AAA_LORA = 64
GATE_LORA = 128
RWKV_GN_EPS = 64e-5
RWKV_IN_W = 3 * RWKV_W + DECAY_LORA + AAA_LORA + GATE_LORA
GATE_W = 2 * D_MODEL
O_RET = 2 * D_MODEL
O_RWKV = O_RET + 4 * RET_W
IN_W = O_RWKV + RWKV_IN_W
MEM_LEN = 256
X_HEADS = 4
X_HD = D_MODEL // X_HEADS
D_FF = 4 * D_MODEL
RMS_EPS = 1e-6

SUBLANES = 8
V7X_VMEM_BYTES = 64 * 1024 * 1024
VMEM_CAP_BYTES = V7X_VMEM_BYTES - 8 * 1024 * 1024
ROW_TILE = 512


def _cparams(sem, vmem_bytes):
    return pltpu.CompilerParams(dimension_semantics=sem, vmem_limit_bytes=int(min(vmem_bytes, VMEM_CAP_BYTES)))


def _nbytes(shape, dtype=F32):
    return math.prod(shape) * jnp.dtype(dtype).itemsize


def _nn(a, b):
    return jnp.dot(a.astype(BF16), b.astype(BF16), preferred_element_type=F32)


def _nt(a, b):
    return lax.dot_general(a.astype(BF16), b.astype(BF16), (((1,), (1,)), ((), ())), preferred_element_type=F32)


def _tn(a, b):
    return lax.dot_general(a.astype(BF16), b.astype(BF16), (((0,), (0,)), ((), ())), preferred_element_type=F32)


def _rms(x, g):
    return x * lax.rsqrt(jnp.mean(x * x, axis=-1, keepdims=True) + RMS_EPS) * g


def _head_norms(blocks, eps):
    rows, width = blocks[0].shape
    ones = jnp.ones((width, width), BF16)
    o = jnp.concatenate(blocks, axis=0)
    d = o - _nn(o, ones) * (1.0 / width)
    out = d * lax.rsqrt(_nn(d * d, ones) * (1.0 / width) + eps)
    return [out[i * rows:(i + 1) * rows] for i in range(len(blocks))]


def _full(shape):
    zeros = (0,) * len(shape)
    return pl.BlockSpec(shape, lambda *_: zeros, pipeline_mode=pl.Buffered(1))


def _vmem(pipelined, resident, temps):
    return 2 * pipelined + resident + temps


def _in_proj_kernel(x_ref, g_ref, w_ref, zg_ref, zr_ref, zw_ref):
    h = _rms(x_ref[...], g_ref[...]).astype(BF16)
    zg_ref[...] = jnp.dot(h, w_ref[:, 0:O_RET], preferred_element_type=F32)
    zr_ref[...] = jnp.dot(h, w_ref[:, O_RET:O_RWKV], preferred_element_type=F32)
    zw_ref[...] = jnp.dot(h, w_ref[:, O_RWKV:IN_W], preferred_element_type=F32)


def _in_proj(x, g, w_in):
    n = x.shape[0]
    tm = min(ROW_TILE, n)
    vmem = _vmem(_nbytes((tm, D_MODEL + IN_W)), _nbytes((D_MODEL, IN_W), BF16), _nbytes((tm, 4 * RET_W + D_MODEL)))
    return pl.pallas_call(
        _in_proj_kernel,
        grid=(n // tm,),
        in_specs=[pl.BlockSpec((tm, D_MODEL), lambda i: (i, 0)), _full((1, D_MODEL)), _full((D_MODEL, IN_W))],
        out_specs=[pl.BlockSpec((tm, GATE_W), lambda i: (i, 0)),
                   pl.BlockSpec((tm, 4 * RET_W), lambda i: (i, 0)),
                   pl.BlockSpec((tm, RWKV_IN_W), lambda i: (i, 0))],
        out_shape=[jax.ShapeDtypeStruct((n, GATE_W), F32), jax.ShapeDtypeStruct((n, 4 * RET_W), F32),
                   jax.ShapeDtypeStruct((n, RWKV_IN_W), F32)],
        compiler_params=_cparams(("parallel",), vmem),
        name="in_proj",
    )(x, g, w_in)


LANES = 128
MEM_ROWS = MEM_LEN * X_HEADS * (X_HD // LANES)
ROWS_PER_TOKEN = MEM_ROWS // MEM_LEN


def _tile_order(mem):
    nb = mem.shape[0]
    return (mem.reshape(nb, MEM_LEN, X_HEADS, X_HD // LANES, LANES).transpose(0, 1, 3, 2, 4)
            .reshape(nb, MEM_ROWS, LANES))


def _from_tile_order(raw):
    nb = raw.shape[0]
    return (raw.reshape(nb, MEM_LEN, X_HD // LANES, X_HEADS, LANES).transpose(0, 1, 3, 2, 4)
            .reshape(nb, MEM_LEN, X_HEADS, X_HD))


def _mem_kv_kernel(x_ref, g_ref, wk_ref, wv_ref, kt_ref, vt_ref, kb_ref, vb_ref, *, tm):
    h = _rms(x_ref[...], g_ref[...]).astype(BF16)
    for w_ref, t_ref, b_ref in ((wk_ref, kt_ref, kb_ref), (wv_ref, vt_ref, vb_ref)):
        y = jnp.dot(h, w_ref[...], preferred_element_type=F32)
        b_ref[...] = y.astype(BF16)
        for hd in range(X_HEADS):
            for c in range(X_HD // LANES):
                col = hd * X_HD + c * LANES
                t_ref[pl.ds(c * X_HEADS + hd, tm, stride=ROWS_PER_TOKEN), :] = y[:, col:col + LANES]


def _mem_kv(mem, g, wk, wv):
    n = mem.shape[0]
    tm = min(ROW_TILE, n)
    vmem = _vmem(4 * _nbytes((tm, D_MODEL)), 2 * _nbytes((D_MODEL, D_MODEL), BF16), 4 * _nbytes((tm, D_MODEL)))
    row = pl.BlockSpec((tm, D_MODEL), lambda i: (i, 0))
    tile = pl.BlockSpec((tm * ROWS_PER_TOKEN, LANES), lambda i: (i, 0))
    return pl.pallas_call(
        functools.partial(_mem_kv_kernel, tm=tm),
        grid=(n // tm,),
        in_specs=[row, _full((1, D_MODEL)), _full((D_MODEL, D_MODEL)), _full((D_MODEL, D_MODEL))],
        out_specs=[tile, tile, row, row],
        out_shape=[jax.ShapeDtypeStruct((n * ROWS_PER_TOKEN, LANES), F32)] * 2
        + [jax.ShapeDtypeStruct((n, D_MODEL), BF16)] * 2,
        compiler_params=_cparams(("parallel",), vmem),
        name="mem_kv",
    )(mem, g, wk, wv)


def _ret_body(zr_ref, cos_ref, sin_ref, dm_ref, qd_ref, kd_ref, cd_ref, o_ref, s_ref, *, bb, tt, chunk, group):
    nch = tt // chunk
    assert bb == 1 or nch == 1, "a block holds either one sequence or one chunk per sequence"
    assert (bb * nch) % group == 0
    heads = range(RET_HEADS)

    def body(j, carry):
        if nch == 1:
            items = [(j * group + n, 0) for n in range(group)]
        else:
            starts = [(j * group + n) * chunk for n in range(group)]
            items = [(0, r if isinstance(r, int) else pl.multiple_of(r, chunk)) for r in starts]
        q, k, v = [], [], []
        for b, r0 in items:
            rows = pl.ds(r0, chunk)
            cos = cos_ref[rows, :]
            sin = sin_ref[rows, :]
            for h in heads:
                qh = zr_ref[b, rows, h * RET_HD:(h + 1) * RET_HD]
                kh = zr_ref[b, rows, RET_W + h * RET_HD:RET_W + (h + 1) * RET_HD]
                q.append(qh * cos + pltpu.roll(qh, RET_HD // 2, axis=1) * sin)
                k.append((kh * cos + pltpu.roll(kh, RET_HD // 2, axis=1) * sin) * (RET_HD ** -0.5))
                v.append(zr_ref[b, rows, 2 * RET_W + h * RET_HD:2 * RET_W + (h + 1) * RET_HD])
        hd = [h for _ in items for h in heads]
        sc = [_nt(qi, ki) * dm_ref[h] for qi, ki, h in zip(q, k, hd)]
        kv = [_tn(ki * kd_ref[h], vi) for ki, vi, h in zip(k, v, hd)]
        inner = [_nn(si, vi) for si, vi in zip(sc, v)]
        states = []
        for n, (b, _) in enumerate(items):
            for h in heads:
                s = s_ref[b, h] if (nch == 1 or n == 0) else states[-RET_HEADS] * cd_ref[h] + kv[(n - 1) * RET_HEADS + h]
                states.append(s)
        last = len(items) - 1
        for n, (b, _) in enumerate(items):
            if nch == 1 or n == last:
                for h in heads:
                    i = n * RET_HEADS + h
                    s_ref[b, h] = states[i] * cd_ref[h] + kv[i]
        cross = [_nn(qi, si) * qd_ref[h] for qi, si, h in zip(q, states, hd)]
        normed = _head_norms([a + c for a, c in zip(inner, cross)], RET_GN_EPS)
        for n, (b, r0) in enumerate(items):
            rows = pl.ds(r0, chunk)
            for h in heads:
                g = zr_ref[b, rows, 3 * RET_W + h * RET_HD:3 * RET_W + (h + 1) * RET_HD]
                o_ref[b, rows, h * RET_HD:(h + 1) * RET_HD] = normed[n * RET_HEADS + h] * (g * jax.nn.sigmoid(g))
        return carry

    if bb * nch == group:
        body(0, 0)
    else:
        lax.fori_loop(0, bb * nch // group, body, 0)


def _ret_consts(chunk):
    lg = jnp.log1p(-jnp.exp2(-5.0 - jnp.arange(RET_HEADS, dtype=F32)))
    idx = jnp.arange(chunk, dtype=F32)
    diff = idx[:, None] - idx[None, :]
    dmask = jnp.where(diff[None] >= 0, jnp.exp(jnp.maximum(diff, 0.0)[None] * lg[:, None, None]), 0.0)
    q_dec = jnp.exp((idx + 1.0)[None, :] * lg[:, None])
    k_dec = jnp.exp((chunk - 1.0 - idx)[None, :] * lg[:, None])
    c_dec = jnp.exp(chunk * lg)
    bc = lambda t: jnp.broadcast_to(t[:, :, None], (RET_HEADS, t.shape[1], RET_HD))
    return dmask, bc(q_dec), bc(k_dec), bc(c_dec[:, None])


def _rope_tables(pos):
    half = RET_HD // 2
    inv = 1.0 / (ROPE_BASE ** (jnp.arange(half, dtype=F32) / half))
    ang = pos.astype(F32)[:, None] * inv[None, :]
    cos, sin = jnp.cos(ang), jnp.sin(ang)
    return jnp.concatenate([cos, cos], axis=1), jnp.concatenate([-sin, sin], axis=1)


WKV_BLOCK = 64
WKV_SCRATCH = 10


def _pow2(n):
    return n & (n - 1) == 0


def _imod(x, n):
    return jnp.bitwise_and(x, n - 1) if _pow2(n) else lax.rem(x, n)


def _idiv(x, n):
    return jnp.right_shift(x, n.bit_length() - 1) if _pow2(n) else lax.div(x, n)


def _wkv_body(zw_ref, mu_ref, w0_ref, a0_ref, kk_ref, ka_ref, rk_ref, lng_ref, lnb_ref,
              wd_ref, wa_ref, wg_ref, tri_ref, seg_ref, o_ref, s_ref, sh_ref,
              at_s, rt_s, bt_s, kt_s, bh_s, kh_s, v_s, g_s, bonus_s, pend_s, *, bb, tt, sub, group):
    C = WKV_BLOCK
    R = bb * tt
    nblk = R // C
    nseq = C // sub
    assert (bb == 1 and sub == C) or (nblk == 1 and sub == tt), "one sequence per tile, or whole sequences in one block"
    assert nblk % group == 0
    n_sq = max(int(math.log2(sub)) - 1, 0)
    heads = range(RWKV_HEADS)
    hsl = [slice(h * RWKV_HD, (h + 1) * RWKV_HD) for h in heads]
    half = RWKV_W // 2

    def seg_sum(t):
        seg = seg_ref[...]
        return jnp.concatenate([_nn(t[:, :half], seg), _nn(t[:, half:], seg)], axis=1)

    pw = zw_ref[...].reshape(R, RWKV_IN_W)
    row = lax.broadcasted_iota(jnp.int32, (R, RWKV_IN_W), 0)
    rolled = pltpu.roll(pw, 1, axis=0)
    if bb == 1:
        prev = jnp.where(row == 0, sh_ref[0], rolled)
        sh_ref[0] = pw[R - 1:R, :]
    else:
        carry = jnp.broadcast_to(sh_ref[...], (bb, tt, RWKV_IN_W)).reshape(R, RWKV_IN_W)
        prev = jnp.where(_imod(row, tt) == 0, carry, rolled)
        sh_ref[...] = pw.reshape(bb, tt, RWKV_IN_W)[:, tt - 1:tt, :]
    xm = pw + mu_ref[...] * (prev - pw)
    r = xm[:, 0:RWKV_W]
    kb = xm[:, RWKV_W:2 * RWKV_W]
    o1 = 3 * RWKV_W
    wl = xm[:, o1:o1 + DECAY_LORA]
    al = xm[:, o1 + DECAY_LORA:o1 + DECAY_LORA + AAA_LORA]
    gl = xm[:, o1 + DECAY_LORA + AAA_LORA:RWKV_IN_W]
    logw = -math.exp(-0.5) * jax.nn.sigmoid(w0_ref[...] + _nn(jnp.tanh(wl), wd_ref[...]))
    a = jax.nn.sigmoid(a0_ref[...] + _nn(al, wa_ref[...]))
    l1 = logw.astype(BF16)
    rem = logw - l1.astype(F32)
    l2 = rem.astype(BF16)
    l3 = (rem - l2.astype(F32)).astype(BF16)
    tri = tri_ref[...]
    cum = jnp.concatenate(
        [sum(jnp.dot(tri, part[k * C:(k + 1) * C], preferred_element_type=F32) for part in (l1, l2, l3))
         for k in range(nblk)], axis=0)
    cum_end = jnp.broadcast_to(cum.reshape(R // sub, sub, RWKV_W)[:, sub - 1:sub, :],
                               (R // sub, sub, RWKV_W)).reshape(R, RWKV_W)
    kk = kb * kk_ref[...]
    kk = kk * jnp.minimum(lax.rsqrt(seg_sum(kk * kk)), 1e12)
    km = kb * (1.0 + (a - 1.0) * ka_ref[...])
    bv = kk * a
    p_end = jnp.exp(cum_end)
    e_neg = jnp.exp(-cum)
    e_end = p_end * e_neg
    at_s[...] = -kk * jnp.exp(cum - logw)
    rt_s[...] = r * jnp.exp(cum)
    bt_s[...] = bv * e_neg
    kt_s[...] = km * e_neg
    bh_s[...] = bv * e_end
    kh_s[...] = km * e_end
    v_s[...] = xm[:, 2 * RWKV_W:3 * RWKV_W]
    g_s[...] = _nn(jax.nn.sigmoid(gl), wg_ref[...])
    bonus_s[...] = seg_sum(r * km * rk_ref[...])
    pend_s[...] = p_end

    row_id = lax.broadcasted_iota(jnp.int32, (C, C), 0)
    col_id = lax.broadcasted_iota(jnp.int32, (C, C), 1)
    strict = row_id > col_id
    incl = row_id >= col_id
    if nseq > 1:
        same = _idiv(row_id, sub) == _idiv(col_id, sub)
        strict = jnp.logical_and(strict, same)
        incl = jnp.logical_and(incl, same)
    eye = jnp.where(row_id == col_id, 1.0, 0.0).astype(F32)

    def triangular(r0s):
        rows = [pl.ds(r0, C) for r0 in r0s for _ in heads]
        sl = [s for _ in r0s for s in hsl]
        at = [at_s[rw, :][:, s] for rw, s in zip(rows, sl)]
        rt = [rt_s[rw, :][:, s] for rw, s in zip(rows, sl)]
        v = [v_s[rw, :][:, s] for rw, s in zip(rows, sl)]
        bk_t = [jnp.concatenate([bt_s[pl.ds(r0, C), :], kt_s[pl.ds(r0, C), :]], axis=0).T for r0 in r0s]
        m1 = [_nn(jnp.concatenate([a_, r_], axis=0), bk[s, :])
              for a_, r_, bk, s in zip(at, rt, [t for t in bk_t for _ in heads], sl)]
        low = [jnp.where(strict, m[:C, :C], 0.0) for m in m1]
        rab = [jnp.where(incl, m[C:, :C], 0.0) for m in m1]
        akk = [jnp.concatenate([jnp.where(strict, m[:C, C:], 0.0), jnp.where(incl, m[C:, C:], 0.0)], axis=0)
               for m in m1]
        akv = [_nn(k_, v_) for k_, v_ in zip(akk, v)]
        tinv = [eye + l for l in low]
        if n_sq:
            x = [_nn(l, l) for l in low]
        for j in range(n_sq):
            if j < n_sq - 1:
                xt = [_nn(jnp.concatenate([xi, ti], axis=0), xi) for xi, ti in zip(x, tinv)]
                x = [t[:C] for t in xt]
                tinv = [ti + t[C:] for ti, t in zip(tinv, xt)]
            else:
                tinv = [ti + _nn(ti, xi) for ti, xi in zip(tinv, x)]
        ua = [_nn(t, a_) for t, a_ in zip(tinv, at)]
        uv = [_nn(t, k_[:C]) for t, k_ in zip(tinv, akv)]
        oa = [r_ + _nn(rb, u_) for r_, rb, u_ in zip(rt, rab, ua)]
        ov = [k_[C:] + _nn(rb, u_) for k_, rb, u_ in zip(akv, rab, uv)]
        nh = RWKV_HEADS
        return [dict(ua=ua[k * nh:(k + 1) * nh], uv=uv[k * nh:(k + 1) * nh], oa=oa[k * nh:(k + 1) * nh],
                     ov=ov[k * nh:(k + 1) * nh], v=v[k * nh:(k + 1) * nh]) for k in range(len(r0s))]

    def advance(r0, q):
        rw = pl.ds(r0, C)
        bh, kh, pend = bh_s[rw, :], kh_s[rw, :], pend_s[rw, :]
        items = [(n, h) for n in range(nseq) for h in heads]
        rs = [slice(n * sub, (n + 1) * sub) for n, _ in items]
        uo = [_nt(jnp.concatenate([q["ua"][h][s], q["oa"][h][s]], axis=0), s_ref[n, h])
              for (n, h), s in zip(items, rs)]
        u = [m[:sub] + q["uv"][h][s] for m, (n, h), s in zip(uo, items, rs)]
        upd = [_tn(jnp.concatenate([u_, q["v"][h][s]], axis=0),
                   jnp.concatenate([bh[s, hsl[h]], kh[s, hsl[h]]], axis=0)) for u_, (n, h), s in zip(u, items, rs)]
        for up, (n, h), s in zip(upd, items, rs):
            s_ref[n, h] = s_ref[n, h] * pend[s, hsl[h]][0:1] + up
        o = [m[sub:] + q["ov"][h][s] for m, (n, h), s in zip(uo, items, rs)]
        o = jnp.concatenate([jnp.concatenate([o[n * RWKV_HEADS + h] for n in range(nseq)], axis=0) for h in heads],
                            axis=1)
        d = o - seg_sum(o) * (1.0 / RWKV_HD)
        var = seg_sum(d * d) * (1.0 / RWKV_HD)
        o = d * lax.rsqrt(var + RWKV_GN_EPS) * lng_ref[...] + lnb_ref[...]
        o = (o + bonus_s[rw, :] * v_s[rw, :]) * g_s[rw, :]
        if bb == 1:
            o_ref[0, rw, :] = o
        else:
            o_ref[...] = o.reshape(bb, tt, RWKV_W)

    def body(j, carry):
        r0s = [j * group * C + k * C for k in range(group)]
        r0s = [r if isinstance(r, int) else pl.multiple_of(r, C) for r in r0s]
        for r0, q in zip(r0s, triangular(r0s)):
            advance(r0, q)
        return carry

    if nblk == group:
        body(0, 0)
    else:
        lax.fori_loop(0, nblk // group, body, 0)


RET_INPUTS = 7
WKV_INPUTS = 14


def _mixers_kernel(*refs, ret, wkv):
    it = iter(refs)
    take = lambda n: [next(it) for _ in range(n)]
    ret_s0, = take(1)
    ret_in = take(RET_INPUTS)
    wkv_sh0, wkv_s0 = take(2)
    wkv_in = take(WKV_INPUTS)
    ret_o, ret_s, wkv_o, wkv_s, wkv_sh = take(5)
    scratch = take(WKV_SCRATCH)

    @pl.when(pl.program_id(1) == 0)
    def _():
        ret_s[...] = ret_s0[...]
        wkv_s[...] = wkv_s0[...]
        wkv_sh[...] = wkv_sh0[...]

    _ret_body(*ret_in, ret_o, ret_s, **ret)
    _wkv_body(*wkv_in, wkv_o, wkv_s, wkv_sh, *scratch, **wkv)


def _mixers(zr, zw, ret_s0, wkv_sh0, wkv_s0, pos, p, tl):
    nb, t, _ = zr.shape
    bb, tt = tl.seq_block, tl.time_block
    rows = bb * tt
    chunk = math.gcd(t, RET_CHUNK)
    cos, sin = _rope_tables(pos)
    dmask, q_dec, k_dec, c_dec = _ret_consts(chunk)
    sub = min(t, WKV_BLOCK)
    step = jnp.arange(WKV_BLOCK)
    tri = ((step[:, None] >= step[None, :]) & (step[:, None] // sub == step[None, :] // sub)).astype(BF16)
    lane_head = jnp.arange(RWKV_W // 2) // RWKV_HD
    seg = (lane_head[:, None] == lane_head[None, :]).astype(BF16)

    tile = lambda width: pl.BlockSpec((bb, tt, width), lambda b, j: (b, j, 0))
    ret_st = pl.BlockSpec((bb, RET_HEADS, RET_HD, RET_HD), lambda b, j: (b, 0, 0, 0))
    wkv_st = pl.BlockSpec((bb, RWKV_HEADS, RWKV_HD, RWKV_HD), lambda b, j: (b, 0, 0, 0))
    sh_spec = pl.BlockSpec((bb, 1, RWKV_IN_W), lambda b, j: (b, 0, 0))
    tab_spec = pl.BlockSpec((tt, RET_HD), lambda b, j: (j, 0))
    vec = _full((1, RWKV_W))
    vmem = _vmem(_nbytes((rows, 5 * RET_W + RWKV_IN_W + RWKV_W)) + 2 * _nbytes((bb, RET_HEADS, RET_HD, RET_HD))
                 + 2 * _nbytes((bb, RWKV_HEADS, RWKV_HD, RWKV_HD)) + 2 * _nbytes((bb, SUBLANES, RWKV_IN_W))
                 + 2 * _nbytes((tt, RET_HD)),
                 _nbytes(dmask.shape) + 3 * _nbytes(q_dec.shape) + 9 * _nbytes((SUBLANES, RWKV_IN_W))
                 + _nbytes((DECAY_LORA + AAA_LORA + GATE_LORA + WKV_BLOCK, RWKV_W), BF16) + _nbytes(seg.shape, BF16)
                 + WKV_SCRATCH * _nbytes((rows, RWKV_W)),
                 4 * _nbytes((rows, RWKV_IN_W)) + 8 * _nbytes((rows, RWKV_W))
                 + tl.wkv_group * 32 * _nbytes((WKV_BLOCK, RWKV_IN_W)) + tl.ret_group * 8 * _nbytes((chunk, 4 * RET_W)))
    return pl.pallas_call(
        functools.partial(_mixers_kernel,
                          ret=dict(bb=bb, tt=tt, chunk=chunk, group=tl.ret_group),
                          wkv=dict(bb=bb, tt=tt, sub=sub, group=tl.wkv_group)),
        grid=(nb // bb, t // tt),
        in_specs=[ret_st, tile(4 * RET_W), tab_spec, tab_spec,
                  _full(dmask.shape), _full(q_dec.shape), _full(k_dec.shape), _full(c_dec.shape),
                  sh_spec, wkv_st, tile(RWKV_IN_W),
                  _full((1, RWKV_IN_W)), vec, vec, vec, vec, vec, vec, vec,
                  _full((DECAY_LORA, RWKV_W)), _full((AAA_LORA, RWKV_W)), _full((GATE_LORA, RWKV_W)),
                  _full(tri.shape), _full(seg.shape)],
        out_specs=[tile(RET_W), ret_st, tile(RWKV_W), wkv_st, sh_spec],
        out_shape=[jax.ShapeDtypeStruct((nb, t, RET_W), F32), jax.ShapeDtypeStruct(ret_s0.shape, F32),
                   jax.ShapeDtypeStruct((nb, t, RWKV_W), F32), jax.ShapeDtypeStruct(wkv_s0.shape, F32),
                   jax.ShapeDtypeStruct(wkv_sh0.shape, F32)],
        scratch_shapes=[pltpu.VMEM((rows, RWKV_W), F32)] * WKV_SCRATCH,
        compiler_params=_cparams(("parallel", "arbitrary"), vmem),
        name="mixers",
    )(ret_s0, zr, cos, sin, dmask, q_dec, k_dec, c_dec,
      wkv_sh0, wkv_s0, zw, p["mu"], p["w0"], p["a0"], p["k_k"], p["k_a"], p["r_k"], p["lnx_g"], p["lnx_b"],
      p["w_decay_up"], p["w_a_up"], p["w_g_up"], tri, seg)


def _merge_rows(x, zg, oa, ob, wa_ref, wb_ref, wo_ref):
    ga = jax.nn.sigmoid(zg[:, 0:D_MODEL])
    gb = jax.nn.sigmoid(zg[:, D_MODEL:GATE_W])
    merged = ga * _nn(oa, wa_ref[...]) + gb * _nn(ob, wb_ref[...])
    return x + _nn(merged, wo_ref[...])


def _merge_kernel(x_ref, zg_ref, oa_ref, ob_ref, wa_ref, wb_ref, wo_ref, o_ref):
    o_ref[...] = _merge_rows(x_ref[...], zg_ref[...], oa_ref[...], ob_ref[...], wa_ref, wb_ref, wo_ref)


def _merge(x, zg, o_ret, o_rwkv, wa, wb, wo):
    n = x.shape[0]
    tm = min(ROW_TILE, n)
    row = lambda w: pl.BlockSpec((tm, w), lambda i: (i, 0))
    vmem = _vmem(_nbytes((tm, 2 * D_MODEL + GATE_W + RET_W + RWKV_W)), _nbytes((2 * D_MODEL, D_MODEL), BF16),
                 6 * _nbytes((tm, D_MODEL)))
    return pl.pallas_call(
        _merge_kernel,
        grid=(n // tm,),
        in_specs=[row(D_MODEL), row(GATE_W), row(RET_W), row(RWKV_W),
                  _full((RET_W, D_MODEL)), _full((RWKV_W, D_MODEL)), _full((D_MODEL, D_MODEL))],
        out_specs=row(D_MODEL),
        out_shape=jax.ShapeDtypeStruct((n, D_MODEL), F32),
        compiler_params=_cparams(("parallel",), vmem),
        name="merge",
    )(x, zg, o_ret, o_rwkv, wa, wb, wo)


def _softmax(sc):
    e = jnp.exp(sc - jnp.max(sc, axis=-1, keepdims=True))
    return e / jnp.sum(e, axis=-1, keepdims=True)


def _attend_heads(q_scr, mk_ref, mv_ref, ox_scr, b, r0, tq):
    sls = [slice(h * X_HD, (h + 1) * X_HD) for h in range(X_HEADS)]
    sc = [_nt(q_scr[pl.ds(r0, tq), sl], mk_ref[b, :, sl]) for sl in sls]
    att = [_softmax(s * (X_HD ** -0.5)) for s in sc]
    for sl, a in zip(sls, att):
        ox_scr[pl.ds(r0, tq), sl] = _nn(a, mv_ref[b, :, sl])


def _attend_tiles(q_scr, mk_ref, mv_ref, ox_scr, seqs, tq):
    halves = X_HD // LANES
    n = X_HEADS * tq
    slot = lax.broadcasted_iota(jnp.int32, (n, MEM_ROWS), 1) % ROWS_PER_TOKEN
    head = lax.broadcasted_iota(jnp.int32, (n, MEM_ROWS), 0) // tq
    own = slot == head
    other = slot == head + X_HEADS
    qs = []
    for b in seqs:
        q = q_scr[b * tq:(b + 1) * tq, :]
        qs.append(jnp.concatenate([q[:, h * X_HD + c * LANES:h * X_HD + (c + 1) * LANES]
                                   for c in range(halves) for h in range(X_HEADS)], axis=0))
    z = [_nt(q, mk_ref[b]) for q, b in zip(qs, seqs)]
    part = [jnp.where(own, zi[:n], 0.0) + jnp.where(other, zi[n:], 0.0) for zi in z]
    sc = [p + pltpu.roll(p, MEM_ROWS - X_HEADS, axis=1) for p in part]
    att = [_softmax(jnp.where(own, s * (X_HD ** -0.5), -jnp.inf)) for s in sc]
    att2 = [jnp.concatenate([a, pltpu.roll(a, X_HEADS, axis=1)], axis=0) for a in att]
    o = [_nn(a, mv_ref[b]) for a, b in zip(att2, seqs)]
    for oi, b in zip(o, seqs):
        ox_scr[b * tq:(b + 1) * tq, :] = jnp.concatenate(
            [oi[(c * X_HEADS + h) * tq:(c * X_HEADS + h + 1) * tq] for h in range(X_HEADS) for c in range(halves)],
            axis=1)


def _cross_kernel(x_ref, mk_ref, mv_ref, g_ref, wq_ref, wo_ref, o_ref, q_scr, ox_scr, *, bb, tq):
    rows = bb * tq
    x = x_ref[...].reshape(rows, D_MODEL)
    q_scr[...] = _nn(_rms(x, g_ref[...]), wq_ref[...])
    _attend_tiles(q_scr, mk_ref, mv_ref, ox_scr, range(bb), tq)
    o_ref[...] = (x + _nn(ox_scr[...], wo_ref[...])).reshape(bb, tq, D_MODEL)


def _cross(x, mem_k, mem_v, g, wq, wo, bb, tq):
    nb, t, _ = x.shape
    x_spec = pl.BlockSpec((bb, tq, D_MODEL), lambda b, j: (b, j, 0))
    m_spec = pl.BlockSpec((bb, MEM_ROWS, LANES), lambda b, j: (b, 0, 0))
    vmem = _vmem(2 * _nbytes((bb, tq, D_MODEL)) + 2 * _nbytes((bb, MEM_ROWS, LANES)),
                 2 * _nbytes((D_MODEL, D_MODEL), BF16) + 2 * _nbytes((bb * tq, D_MODEL)),
                 6 * _nbytes((bb * tq, D_MODEL)) + bb * 8 * _nbytes((2 * X_HEADS * tq, MEM_ROWS)))
    return pl.pallas_call(
        functools.partial(_cross_kernel, bb=bb, tq=tq),
        grid=(nb // bb, t // tq),
        in_specs=[x_spec, m_spec, m_spec, _full((1, D_MODEL)), _full((D_MODEL, D_MODEL)), _full((D_MODEL, D_MODEL))],
        out_specs=x_spec,
        out_shape=jax.ShapeDtypeStruct(x.shape, F32),
        scratch_shapes=[pltpu.VMEM((bb * tq, D_MODEL), F32), pltpu.VMEM((bb * tq, D_MODEL), F32)],
        compiler_params=_cparams(("parallel", "arbitrary"), vmem),
        name="cross_attn",
    )(x, mem_k, mem_v, g, wq, wo)


MLP_FF_CHUNK = 1024


def _mlp_rows(x, g_ref, wu_ref, wd_ref, gf_ref, final_norm):
    h = _rms(x, g_ref[...]).astype(BF16)
    acc = x
    for c in range(0, D_FF, MLP_FF_CHUNK):
        u = jnp.maximum(jnp.dot(h, wu_ref[:, c:c + MLP_FF_CHUNK], preferred_element_type=F32), 0.0)
        acc = acc + _nn(u * u, wd_ref[c:c + MLP_FF_CHUNK, :])
    return _rms(acc, gf_ref[...]) if final_norm else acc


def _mlp_kernel(x_ref, g_ref, wu_ref, wd_ref, gf_ref, o_ref, *, final_norm):
    o_ref[...] = _mlp_rows(x_ref[...], g_ref, wu_ref, wd_ref, gf_ref, final_norm)


def _mlp(x, g, w_up, w_down, g_final, final_norm):
    n = x.shape[0]
    tm = min(ROW_TILE, n)
    row = pl.BlockSpec((tm, D_MODEL), lambda i: (i, 0))
    vmem = _vmem(2 * _nbytes((tm, D_MODEL)), 2 * _nbytes((D_MODEL, D_FF), BF16),
                 4 * _nbytes((tm, MLP_FF_CHUNK)) + 4 * _nbytes((tm, D_MODEL)))
    return pl.pallas_call(
        functools.partial(_mlp_kernel, final_norm=final_norm),
        grid=(n // tm,),
        in_specs=[row, _full((1, D_MODEL)), _full((D_MODEL, D_FF)), _full((D_FF, D_MODEL)), _full((1, D_MODEL))],
        out_specs=row,
        out_shape=jax.ShapeDtypeStruct((n, D_MODEL), F32),
        compiler_params=_cparams(("parallel",), vmem),
        name="mlp",
    )(x, g, w_up, w_down, g_final)


POST_TILE = 512


def _post_kernel(x_ref, zg_ref, oa_ref, ob_ref, mk_ref, mv_ref, wa_ref, wb_ref, wo_ref, gc_ref, wq_ref, wco_ref,
                 gm_ref, wu_ref, wd_ref, gf_ref, o_ref, q_scr, ox_scr, *, tq, final_norm):
    x1 = _merge_rows(x_ref[0], zg_ref[0], oa_ref[0], ob_ref[0], wa_ref, wb_ref, wo_ref)
    q_scr[...] = _nn(_rms(x1, gc_ref[...]), wq_ref[...])
    _attend_heads(q_scr, mk_ref, mv_ref, ox_scr, 0, 0, tq)
    x2 = x1 + _nn(ox_scr[...], wco_ref[...])
    o_ref[0] = _mlp_rows(x2, gm_ref, wu_ref, wd_ref, gf_ref, final_norm)


def _post(x, zg, o_ret, o_rwkv, mem_k, mem_v, w, g_final, final_norm):
    nb, t, _ = x.shape
    tq = min(POST_TILE, t)
    row = lambda width: pl.BlockSpec((1, tq, width), lambda b, j: (b, j, 0))
    mem = pl.BlockSpec((1, MEM_LEN, D_MODEL), lambda b, j: (b, 0, 0))
    sq = _full((D_MODEL, D_MODEL))
    vec = _full((1, D_MODEL))
    vmem = _vmem(_nbytes((tq, 2 * D_MODEL + GATE_W + RET_W + RWKV_W)) + 2 * _nbytes((MEM_LEN, D_MODEL), BF16),
                 _nbytes((4 * D_MODEL + 2 * D_FF, D_MODEL), BF16) + 2 * _nbytes((tq, D_MODEL)),
                 8 * _nbytes((tq, D_MODEL)) + 4 * _nbytes((tq, MLP_FF_CHUNK)))
    return pl.pallas_call(
        functools.partial(_post_kernel, tq=tq, final_norm=final_norm),
        grid=(nb, t // tq),
        in_specs=[row(D_MODEL), row(GATE_W), row(RET_W), row(RWKV_W), mem, mem,
                  _full((RET_W, D_MODEL)), _full((RWKV_W, D_MODEL)), sq, vec, sq, sq,
                  vec, _full((D_MODEL, D_FF)), _full((D_FF, D_MODEL)), vec],
        out_specs=row(D_MODEL),
        out_shape=jax.ShapeDtypeStruct(x.shape, F32),
        scratch_shapes=[pltpu.VMEM((tq, D_MODEL), F32), pltpu.VMEM((tq, D_MODEL), F32)],
        compiler_params=_cparams(("parallel", "arbitrary"), vmem),
        name="post",
    )(x, zg, o_ret, o_rwkv, mem_k, mem_v, w["w_branch_a"], w["w_branch_b"], w["w_out"], w["g_cross"], w["w_cq"],
      w["w_co"], w["g_mlp"], w["w_up"], w["w_down"], g_final)


class _Tiling(NamedTuple):
    seq_block: int
    time_block: int
    cross_block: int
    ret_group: int
    wkv_group: int


def _tiling(nb, t):
    if t > RET_CHUNK:
        tt = min(ROW_TILE, t)
        return _Tiling(seq_block=1, time_block=tt, cross_block=1,
                       ret_group=math.gcd(4, tt // RET_CHUNK), wkv_group=math.gcd(8, tt // WKV_BLOCK))
    seqs = WKV_BLOCK // t
    return _Tiling(seq_block=seqs, time_block=t, cross_block=math.gcd(8, nb), ret_group=seqs, wkv_group=1)


def _layer(x, pos, s_ret, s_wkv, s_shift, mem_k, mem_v, w, g_final, final_norm):
    nb, t, _ = x.shape
    n = nb * t
    tl = _tiling(nb, t)
    xf = x.reshape(n, D_MODEL)
    zg, zr, zw = _in_proj(xf, w["g_mix"], w["w_in"])
    o_ret, s_ret_new, o_wkv, s_wkv_new, shift_new = _mixers(
        zr.reshape(nb, t, 4 * RET_W), zw.reshape(nb, t, RWKV_IN_W), s_ret, s_shift.reshape(nb, 1, RWKV_IN_W), s_wkv,
        pos, w, tl)
    if mem_k.shape[1:] == (MEM_LEN, D_MODEL):
        y = _post(x, zg.reshape(nb, t, GATE_W), o_ret, o_wkv, mem_k, mem_v, w, g_final, final_norm)
    else:
        x1 = _merge(xf, zg, o_ret.reshape(n, RET_W), o_wkv.reshape(n, RWKV_W),
                    w["w_branch_a"], w["w_branch_b"], w["w_out"])
        x2 = _cross(x1.reshape(nb, t, D_MODEL), mem_k, mem_v, w["g_cross"], w["w_cq"], w["w_co"],
                    tl.cross_block, tl.time_block)
        y = _mlp(x2.reshape(n, D_MODEL), w["g_mlp"], w["w_up"], w["w_down"], g_final, final_norm).reshape(x.shape)
    return y, s_ret_new, s_wkv_new, shift_new.reshape(nb, RWKV_IN_W)


def _layer_weights(l, g_mix, w_in, w_branch_a, w_branch_b, w_out, mu_shift, w0, w_decay_up, a0, w_a_up, w_g_up,
                   k_k, k_a, r_k, lnx_g, lnx_b, g_cross, w_cq, w_co, g_mlp, w_up, w_down):
    vec = lambda v: v[l].reshape(1, -1).astype(F32)
    mat = lambda m: m[l].astype(BF16)
    return dict(g_mix=vec(g_mix), w_in=mat(w_in), w_branch_a=mat(w_branch_a), w_branch_b=mat(w_branch_b),
                w_out=mat(w_out), mu=vec(mu_shift), w0=vec(w0), w_decay_up=mat(w_decay_up), a0=vec(a0),
                w_a_up=mat(w_a_up), w_g_up=mat(w_g_up), k_k=vec(k_k), k_a=vec(k_a), r_k=vec(r_k), lnx_g=vec(lnx_g),
                lnx_b=vec(lnx_b), g_cross=vec(g_cross), w_cq=mat(w_cq), w_co=mat(w_co), g_mlp=vec(g_mlp),
                w_up=mat(w_up), w_down=mat(w_down))


def kernel(x_prompt, x_sample, mem_prompt, state_ret, state_wkv, state_shift, cache_mem_k, cache_mem_v, g_mix, w_in, w_branch_a, w_branch_b, w_out, mu_shift, w0, w_decay_up, a0, w_a_up, w_g_up, k_k, k_a, r_k, lnx_g, lnx_b, g_cross, g_mem, w_cq, w_ck, w_cv, w_co, g_mlp, w_up, w_down, g_final):
    depth = w_in.shape[0]
    bp, tp, _ = x_prompt.shape
    bs, ts, _ = x_sample.shape
    pos_p = jnp.arange(tp, dtype=jnp.int32)
    pos_s = PAST_LEN + jnp.arange(ts, dtype=jnp.int32)
    gf = g_final.reshape(1, D_MODEL)

    xp, xs = x_prompt, x_sample
    outs_p, outs_s = [], []
    for l in range(depth):
        w = _layer_weights(l, g_mix, w_in, w_branch_a, w_branch_b, w_out, mu_shift, w0, w_decay_up, a0, w_a_up,
                           w_g_up, k_k, k_a, r_k, lnx_g, lnx_b, g_cross, w_cq, w_co, g_mlp, w_up, w_down)
        last = l == depth - 1
        k_tiles, v_tiles, k_att, v_att = _mem_kv(mem_prompt.reshape(bp * MEM_LEN, D_MODEL),
                                                 g_mem[l].reshape(1, D_MODEL), w_ck[l].astype(BF16), w_cv[l].astype(BF16))
        xp, sr, sw, ss = _layer(
            xp, pos_p,
            jnp.zeros((bp, RET_HEADS, RET_HD, RET_HD), F32), jnp.zeros((bp, RWKV_HEADS, RWKV_HD, RWKV_HD), F32),
            jnp.zeros((bp, RWKV_IN_W), F32), k_att.reshape(bp, MEM_LEN, D_MODEL), v_att.reshape(bp, MEM_LEN, D_MODEL),
            w, gf, last)
        outs_p.append((sr, sw, ss, _from_tile_order(k_tiles.reshape(bp, MEM_ROWS, LANES)),
                       _from_tile_order(v_tiles.reshape(bp, MEM_ROWS, LANES))))
        xs, sr2, sw2, ss2 = _layer(
            xs, pos_s, state_ret[l], state_wkv[l], state_shift[l],
            _tile_order(cache_mem_k[l]), _tile_order(cache_mem_v[l]), w, gf, last)
        outs_s.append((sr2, sw2, ss2))

    stack = lambda items, i: jnp.stack([it[i] for it in items])
    return (xp, xs, stack(outs_p, 0), stack(outs_p, 1), stack(outs_p, 2), stack(outs_p, 3), stack(outs_p, 4),
            stack(outs_s, 0), stack(outs_s, 1), stack(outs_s, 2))
```

```python
import functools
import math
from typing import NamedTuple

import jax
import jax.numpy as jnp
from jax import lax
from jax.experimental import pallas as pl
from jax.experimental.pallas import tpu as pltpu

F32 = jnp.float32
BF16 = jnp.bfloat16

D_MODEL = 1024
PAST_LEN = 16384
RET_HEADS = 4
RET_HD = 128
RET_W = RET_HEADS * RET_HD
RET_CHUNK = 128
RET_GN_EPS = 1e-5
ROPE_BASE = 10000.0
RWKV_HEADS = 8
RWKV_HD = 64
RWKV_W = RWKV_HEADS * RWKV_HD
DECAY_LORA = 64
AAA_LORA = 64
GATE_LORA = 128
RWKV_GN_EPS = 64e-5
RWKV_IN_W = 3 * RWKV_W + DECAY_LORA + AAA_LORA + GATE_LORA
GATE_W = 2 * D_MODEL
O_RET = 2 * D_MODEL
O_RWKV = O_RET + 4 * RET_W
IN_W = O_RWKV + RWKV_IN_W
MEM_LEN = 256
X_HEADS = 4
X_HD = D_MODEL // X_HEADS
D_FF = 4 * D_MODEL
RMS_EPS = 1e-6

SUBLANES = 8
V7X_VMEM_BYTES = 64 * 1024 * 1024
VMEM_CAP_BYTES = V7X_VMEM_BYTES - 8 * 1024 * 1024
ROW_TILE = 512


def _cparams(sem, vmem_bytes):
    return pltpu.CompilerParams(dimension_semantics=sem, vmem_limit_bytes=int(min(vmem_bytes, VMEM_CAP_BYTES)))


def _nbytes(shape, dtype=F32):
    return math.prod(shape) * jnp.dtype(dtype).itemsize


def _nn(a, b):
    return jnp.dot(a.astype(BF16), b.astype(BF16), preferred_element_type=F32)


def _nt(a, b):
    return lax.dot_general(a.astype(BF16), b.astype(BF16), (((1,), (1,)), ((), ())), preferred_element_type=F32)


def _tn(a, b):
    return lax.dot_general(a.astype(BF16), b.astype(BF16), (((0,), (0,)), ((), ())), preferred_element_type=F32)


def _rms(x, g):
    return x * lax.rsqrt(jnp.mean(x * x, axis=-1, keepdims=True) + RMS_EPS) * g


def _head_norms(blocks, eps):
    rows, width = blocks[0].shape
    ones = jnp.ones((width, width), BF16)
    o = jnp.concatenate(blocks, axis=0)
    d = o - _nn(o, ones) * (1.0 / width)
    out = d * lax.rsqrt(_nn(d * d, ones) * (1.0 / width) + eps)
    return [out[i * rows:(i + 1) * rows] for i in range(len(blocks))]


def _full(shape):
    zeros = (0,) * len(shape)
    return pl.BlockSpec(shape, lambda *_: zeros, pipeline_mode=pl.Buffered(1))


def _vmem(pipelined, resident, temps):
    return 2 * pipelined + resident + temps


def _in_proj_kernel(x_ref, g_ref, w_ref, zg_ref, zr_ref, zw_ref):
    h = _rms(x_ref[...], g_ref[...]).astype(BF16)
    zg_ref[...] = jnp.dot(h, w_ref[:, 0:O_RET], preferred_element_type=F32)
    zr_ref[...] = jnp.dot(h, w_ref[:, O_RET:O_RWKV], preferred_element_type=F32)
    zw_ref[...] = jnp.dot(h, w_ref[:, O_RWKV:IN_W], preferred_element_type=F32)


def _in_proj(x, g, w_in):
    n = x.shape[0]
    tm = min(ROW_TILE, n)
    vmem = _vmem(_nbytes((tm, D_MODEL + IN_W)), _nbytes((D_MODEL, IN_W), BF16), _nbytes((tm, 4 * RET_W + D_MODEL)))
    return pl.pallas_call(
        _in_proj_kernel,
        grid=(n // tm,),
        in_specs=[pl.BlockSpec((tm, D_MODEL), lambda i: (i, 0)), _full((1, D_MODEL)), _full((D_MODEL, IN_W))],
        out_specs=[pl.BlockSpec((tm, GATE_W), lambda i: (i, 0)),
                   pl.BlockSpec((tm, 4 * RET_W), lambda i: (i, 0)),
                   pl.BlockSpec((tm, RWKV_IN_W), lambda i: (i, 0))],
        out_shape=[jax.ShapeDtypeStruct((n, GATE_W), F32), jax.ShapeDtypeStruct((n, 4 * RET_W), F32),
                   jax.ShapeDtypeStruct((n, RWKV_IN_W), F32)],
        compiler_params=_cparams(("parallel",), vmem),
        name="in_proj",
    )(x, g, w_in)


LANES = 128
MEM_ROWS = MEM_LEN * X_HEADS * (X_HD // LANES)
ROWS_PER_TOKEN = MEM_ROWS // MEM_LEN


def _tile_order(mem):
    nb = mem.shape[0]
    return (mem.reshape(nb, MEM_LEN, X_HEADS, X_HD // LANES, LANES).transpose(0, 1, 3, 2, 4)
            .reshape(nb, MEM_ROWS, LANES))


def _from_tile_order(raw):
    nb = raw.shape[0]
    return (raw.reshape(nb, MEM_LEN, X_HD // LANES, X_HEADS, LANES).transpose(0, 1, 3, 2, 4)
            .reshape(nb, MEM_LEN, X_HEADS, X_HD))


def _mem_kv_kernel(x_ref, g_ref, wk_ref, wv_ref, kt_ref, vt_ref, kb_ref, vb_ref, *, tm):
    h = _rms(x_ref[...], g_ref[...]).astype(BF16)
    for w_ref, t_ref, b_ref in ((wk_ref, kt_ref, kb_ref), (wv_ref, vt_ref, vb_ref)):
        y = jnp.dot(h, w_ref[...], preferred_element_type=F32)
        b_ref[...] = y.astype(BF16)
        for hd in range(X_HEADS):
            for c in range(X_HD // LANES):
                col = hd * X_HD + c * LANES
                t_ref[pl.ds(c * X_HEADS + hd, tm, stride=ROWS_PER_TOKEN), :] = y[:, col:col + LANES]


def _mem_kv(mem, g, wk, wv):
    n = mem.shape[0]
    tm = min(ROW_TILE, n)
    vmem = _vmem(4 * _nbytes((tm, D_MODEL)), 2 * _nbytes((D_MODEL, D_MODEL), BF16), 4 * _nbytes((tm, D_MODEL)))
    row = pl.BlockSpec((tm, D_MODEL), lambda i: (i, 0))
    tile = pl.BlockSpec((tm * ROWS_PER_TOKEN, LANES), lambda i: (i, 0))
    return pl.pallas_call(
        functools.partial(_mem_kv_kernel, tm=tm),
        grid=(n // tm,),
        in_specs=[row, _full((1, D_MODEL)), _full((D_MODEL, D_MODEL)), _full((D_MODEL, D_MODEL))],
        out_specs=[tile, tile, row, row],
        out_shape=[jax.ShapeDtypeStruct((n * ROWS_PER_TOKEN, LANES), F32)] * 2
        + [jax.ShapeDtypeStruct((n, D_MODEL), BF16)] * 2,
        compiler_params=_cparams(("parallel",), vmem),
        name="mem_kv",
    )(mem, g, wk, wv)


def _ret_body(zr_ref, cos_ref, sin_ref, dm_ref, qd_ref, kd_ref, cd_ref, o_ref, s_ref, *, bb, tt, chunk, group):
    nch = tt // chunk
    assert bb == 1 or nch == 1, "a block holds either one sequence or one chunk per sequence"
    assert (bb * nch) % group == 0
    heads = range(RET_HEADS)

    def body(j, carry):
        if nch == 1:
            items = [(j * group + n, 0) for n in range(group)]
        else:
            starts = [(j * group + n) * chunk for n in range(group)]
            items = [(0, r if isinstance(r, int) else pl.multiple_of(r, chunk)) for r in starts]
        q, k, v = [], [], []
        for b, r0 in items:
            rows = pl.ds(r0, chunk)
            cos = cos_ref[rows, :]
            sin = sin_ref[rows, :]
            for h in heads:
                qh = zr_ref[b, rows, h * RET_HD:(h + 1) * RET_HD]
                kh = zr_ref[b, rows, RET_W + h * RET_HD:RET_W + (h + 1) * RET_HD]
                q.append(qh * cos + pltpu.roll(qh, RET_HD // 2, axis=1) * sin)
                k.append((kh * cos + pltpu.roll(kh, RET_HD // 2, axis=1) * sin) * (RET_HD ** -0.5))
                v.append(zr_ref[b, rows, 2 * RET_W + h * RET_HD:2 * RET_W + (h + 1) * RET_HD])
        hd = [h for _ in items for h in heads]
        sc = [_nt(qi, ki) * dm_ref[h] for qi, ki, h in zip(q, k, hd)]
        kv = [_tn(ki * kd_ref[h], vi) for ki, vi, h in zip(k, v, hd)]
        inner = [_nn(si, vi) for si, vi in zip(sc, v)]
        states = []
        for n, (b, _) in enumerate(items):
            for h in heads:
                s = s_ref[b, h] if (nch == 1 or n == 0) else states[-RET_HEADS] * cd_ref[h] + kv[(n - 1) * RET_HEADS + h]
                states.append(s)
        last = len(items) - 1
        for n, (b, _) in enumerate(items):
            if nch == 1 or n == last:
                for h in heads:
                    i = n * RET_HEADS + h
                    s_ref[b, h] = states[i] * cd_ref[h] + kv[i]
        cross = [_nn(qi, si) * qd_ref[h] for qi, si, h in zip(q, states, hd)]
        normed = _head_norms([a + c for a, c in zip(inner, cross)], RET_GN_EPS)
        for n, (b, r0) in enumerate(items):
            rows = pl.ds(r0, chunk)
            for h in heads:
                g = zr_ref[b, rows, 3 * RET_W + h * RET_HD:3 * RET_W + (h + 1) * RET_HD]
                o_ref[b, rows, h * RET_HD:(h + 1) * RET_HD] = normed[n * RET_HEADS + h] * (g * jax.nn.sigmoid(g))
        return carry

    if bb * nch == group:
        body(0, 0)
    else:
        lax.fori_loop(0, bb * nch // group, body, 0)


def _ret_consts(chunk):
    lg = jnp.log1p(-jnp.exp2(-5.0 - jnp.arange(RET_HEADS, dtype=F32)))
    idx = jnp.arange(chunk, dtype=F32)
    diff = idx[:, None] - idx[None, :]
    dmask = jnp.where(diff[None] >= 0, jnp.exp(jnp.maximum(diff, 0.0)[None] * lg[:, None, None]), 0.0)
    q_dec = jnp.exp((idx + 1.0)[None, :] * lg[:, None])
    k_dec = jnp.exp((chunk - 1.0 - idx)[None, :] * lg[:, None])
    c_dec = jnp.exp(chunk * lg)
    bc = lambda t: jnp.broadcast_to(t[:, :, None], (RET_HEADS, t.shape[1], RET_HD))
    return dmask, bc(q_dec), bc(k_dec), bc(c_dec[:, None])


def _rope_tables(pos):
    half = RET_HD // 2
    inv = 1.0 / (ROPE_BASE ** (jnp.arange(half, dtype=F32) / half))
    ang = pos.astype(F32)[:, None] * inv[None, :]
    cos, sin = jnp.cos(ang), jnp.sin(ang)
    return jnp.concatenate([cos, cos], axis=1), jnp.concatenate([-sin, sin], axis=1)


WKV_BLOCK = 64
SHORT_WKV_BLOCKS = 2
WKV_SCRATCH = 10


def _pow2(n):
    return n & (n - 1) == 0


def _imod(x, n):
    return jnp.bitwise_and(x, n - 1) if _pow2(n) else lax.rem(x, n)


def _idiv(x, n):
    return jnp.right_shift(x, n.bit_length() - 1) if _pow2(n) else lax.div(x, n)


def _wkv_body(zw_ref, mu_ref, w0_ref, a0_ref, kk_ref, ka_ref, rk_ref, lng_ref, lnb_ref,
              wd_ref, wa_ref, wg_ref, tri_ref, seg_ref, o_ref, s_ref, sh_ref,
              at_s, rt_s, bt_s, kt_s, bh_s, kh_s, v_s, g_s, bonus_s, pend_s, *, bb, tt, sub, group):
    C = WKV_BLOCK
    R = bb * tt
    nblk = R // C
    nseq = C // sub
    assert (bb == 1 and sub == C) or (sub == tt and R % C == 0 and group == nblk), \
        "one sequence per tile, or blocks of whole sequences handled in one group"
    assert nblk % group == 0
    n_sq = max(int(math.log2(sub)) - 1, 0)
    heads = range(RWKV_HEADS)
    hsl = [slice(h * RWKV_HD, (h + 1) * RWKV_HD) for h in heads]
    half = RWKV_W // 2

    def seg_sum(t):
        seg = seg_ref[...]
        return jnp.concatenate([_nn(t[:, :half], seg), _nn(t[:, half:], seg)], axis=1)

    pw = zw_ref[...].reshape(R, RWKV_IN_W)
    row = lax.broadcasted_iota(jnp.int32, (R, RWKV_IN_W), 0)
    rolled = pltpu.roll(pw, 1, axis=0)
    if bb == 1:
        prev = jnp.where(row == 0, sh_ref[0], rolled)
        sh_ref[0] = pw[R - 1:R, :]
    else:
        carry = jnp.broadcast_to(sh_ref[...], (bb, tt, RWKV_IN_W)).reshape(R, RWKV_IN_W)
        prev = jnp.where(_imod(row, tt) == 0, carry, rolled)
        sh_ref[...] = pw.reshape(bb, tt, RWKV_IN_W)[:, tt - 1:tt, :]
    xm = pw + mu_ref[...] * (prev - pw)
    r = xm[:, 0:RWKV_W]
    kb = xm[:, RWKV_W:2 * RWKV_W]
    o1 = 3 * RWKV_W
    wl = xm[:, o1:o1 + DECAY_LORA]
    al = xm[:, o1 + DECAY_LORA:o1 + DECAY_LORA + AAA_LORA]
    gl = xm[:, o1 + DECAY_LORA + AAA_LORA:RWKV_IN_W]
    logw = -math.exp(-0.5) * jax.nn.sigmoid(w0_ref[...] + _nn(jnp.tanh(wl), wd_ref[...]))
    a = jax.nn.sigmoid(a0_ref[...] + _nn(al, wa_ref[...]))
    l1 = logw.astype(BF16)
    rem = logw - l1.astype(F32)
    l2 = rem.astype(BF16)
    l3 = (rem - l2.astype(F32)).astype(BF16)
    tri = tri_ref[...]
    cum = jnp.concatenate(
        [sum(jnp.dot(tri, part[k * C:(k + 1) * C], preferred_element_type=F32) for part in (l1, l2, l3))
         for k in range(nblk)], axis=0)
    cum_end = jnp.broadcast_to(cum.reshape(R // sub, sub, RWKV_W)[:, sub - 1:sub, :],
                               (R // sub, sub, RWKV_W)).reshape(R, RWKV_W)
    kk = kb * kk_ref[...]
    kk = kk * jnp.minimum(lax.rsqrt(seg_sum(kk * kk)), 1e12)
    km = kb * (1.0 + (a - 1.0) * ka_ref[...])
    bv = kk * a
    p_end = jnp.exp(cum_end)
    e_neg = jnp.exp(-cum)
    e_end = p_end * e_neg
    at_s[...] = -kk * jnp.exp(cum - logw)
    rt_s[...] = r * jnp.exp(cum)
    bt_s[...] = bv * e_neg
    kt_s[...] = km * e_neg
    bh_s[...] = bv * e_end
    kh_s[...] = km * e_end
    v_s[...] = xm[:, 2 * RWKV_W:3 * RWKV_W]
    g_s[...] = _nn(jax.nn.sigmoid(gl), wg_ref[...])
    bonus_s[...] = seg_sum(r * km * rk_ref[...])
    pend_s[...] = p_end

    row_id = lax.broadcasted_iota(jnp.int32, (C, C), 0)
    col_id = lax.broadcasted_iota(jnp.int32, (C, C), 1)
    strict = row_id > col_id
    incl = row_id >= col_id
    if nseq > 1:
        same = _idiv(row_id, sub) == _idiv(col_id, sub)
        strict = jnp.logical_and(strict, same)
        incl = jnp.logical_and(incl, same)
    eye = jnp.where(row_id == col_id, 1.0, 0.0).astype(F32)

    def triangular(r0s):
        rows = [pl.ds(r0, C) for r0 in r0s for _ in heads]
        sl = [s for _ in r0s for s in hsl]
        at = [at_s[rw, :][:, s] for rw, s in zip(rows, sl)]
        rt = [rt_s[rw, :][:, s] for rw, s in zip(rows, sl)]
        v = [v_s[rw, :][:, s] for rw, s in zip(rows, sl)]
        bk_t = [jnp.concatenate([bt_s[pl.ds(r0, C), :], kt_s[pl.ds(r0, C), :]], axis=0).T for r0 in r0s]
        m1 = [_nn(jnp.concatenate([a_, r_], axis=0), bk[s, :])
              for a_, r_, bk, s in zip(at, rt, [t for t in bk_t for _ in heads], sl)]
        low = [jnp.where(strict, m[:C, :C], 0.0) for m in m1]
        rab = [jnp.where(incl, m[C:, :C], 0.0) for m in m1]
        akk = [jnp.concatenate([jnp.where(strict, m[:C, C:], 0.0), jnp.where(incl, m[C:, C:], 0.0)], axis=0)
               for m in m1]
        akv = [_nn(k_, v_) for k_, v_ in zip(akk, v)]
        tinv = [eye + l for l in low]
        if n_sq:
            x = [_nn(l, l) for l in low]
        for j in range(n_sq):
            if j < n_sq - 1:
                xt = [_nn(jnp.concatenate([xi, ti], axis=0), xi) for xi, ti in zip(x, tinv)]
                x = [t[:C] for t in xt]
                tinv = [ti + t[C:] for ti, t in zip(tinv, xt)]
            else:
                tinv = [ti + _nn(ti, xi) for ti, xi in zip(tinv, x)]
        ua = [_nn(t, a_) for t, a_ in zip(tinv, at)]
        uv = [_nn(t, k_[:C]) for t, k_ in zip(tinv, akv)]
        oa = [r_ + _nn(rb, u_) for r_, rb, u_ in zip(rt, rab, ua)]
        ov = [k_[C:] + _nn(rb, u_) for k_, rb, u_ in zip(akv, rab, uv)]
        nh = RWKV_HEADS
        return [dict(ua=ua[k * nh:(k + 1) * nh], uv=uv[k * nh:(k + 1) * nh], oa=oa[k * nh:(k + 1) * nh],
                     ov=ov[k * nh:(k + 1) * nh], v=v[k * nh:(k + 1) * nh]) for k in range(len(r0s))]

    def advance(r0, q):
        rw = pl.ds(r0, C)
        bh, kh, pend = bh_s[rw, :], kh_s[rw, :], pend_s[rw, :]
        first = 0 if bb == 1 else (r0 // C) * nseq
        items = [(n, h) for n in range(nseq) for h in heads]
        rs = [slice(n * sub, (n + 1) * sub) for n, _ in items]
        uo = [_nt(jnp.concatenate([q["ua"][h][s], q["oa"][h][s]], axis=0), s_ref[first + n, h])
              for (n, h), s in zip(items, rs)]
        u = [m[:sub] + q["uv"][h][s] for m, (n, h), s in zip(uo, items, rs)]
        upd = [_tn(jnp.concatenate([u_, q["v"][h][s]], axis=0),
                   jnp.concatenate([bh[s, hsl[h]], kh[s, hsl[h]]], axis=0)) for u_, (n, h), s in zip(u, items, rs)]
        for up, (n, h), s in zip(upd, items, rs):
            s_ref[first + n, h] = s_ref[first + n, h] * pend[s, hsl[h]][0:1] + up
        o = [m[sub:] + q["ov"][h][s] for m, (n, h), s in zip(uo, items, rs)]
        o = jnp.concatenate([jnp.concatenate([o[n * RWKV_HEADS + h] for n in range(nseq)], axis=0) for h in heads],
                            axis=1)
        d = o - seg_sum(o) * (1.0 / RWKV_HD)
        var = seg_sum(d * d) * (1.0 / RWKV_HD)
        o = d * lax.rsqrt(var + RWKV_GN_EPS) * lng_ref[...] + lnb_ref[...]
        o = (o + bonus_s[rw, :] * v_s[rw, :]) * g_s[rw, :]
        if bb == 1:
            o_ref[0, rw, :] = o
        else:
            o_ref[first:first + nseq] = o.reshape(nseq, tt, RWKV_W)

    def body(j, carry):
        r0s = [j * group * C + k * C for k in range(group)]
        r0s = [r if isinstance(r, int) else pl.multiple_of(r, C) for r in r0s]
        for r0, q in zip(r0s, triangular(r0s)):
            advance(r0, q)
        return carry

    if nblk == group:
        body(0, 0)
    else:
        lax.fori_loop(0, nblk // group, body, 0)


RET_INPUTS = 7
WKV_INPUTS = 14


def _mixers_kernel(*refs, ret, wkv):
    it = iter(refs)
    take = lambda n: [next(it) for _ in range(n)]
    ret_s0, = take(1)
    ret_in = take(RET_INPUTS)
    wkv_sh0, wkv_s0 = take(2)
    wkv_in = take(WKV_INPUTS)
    ret_o, ret_s, wkv_o, wkv_s, wkv_sh = take(5)
    scratch = take(WKV_SCRATCH)

    @pl.when(pl.program_id(1) == 0)
    def _():
        ret_s[...] = ret_s0[...]
        wkv_s[...] = wkv_s0[...]
        wkv_sh[...] = wkv_sh0[...]

    _ret_body(*ret_in, ret_o, ret_s, **ret)
    _wkv_body(*wkv_in, wkv_o, wkv_s, wkv_sh, *scratch, **wkv)


def _mixers(zr, zw, ret_s0, wkv_sh0, wkv_s0, pos, p, tl):
    nb, t, _ = zr.shape
    bb, tt = tl.seq_block, tl.time_block
    rows = bb * tt
    chunk = math.gcd(t, RET_CHUNK)
    cos, sin = _rope_tables(pos)
    dmask, q_dec, k_dec, c_dec = _ret_consts(chunk)
    sub = min(t, WKV_BLOCK)
    step = jnp.arange(WKV_BLOCK)
    tri = ((step[:, None] >= step[None, :]) & (step[:, None] // sub == step[None, :] // sub)).astype(BF16)
    lane_head = jnp.arange(RWKV_W // 2) // RWKV_HD
    seg = (lane_head[:, None] == lane_head[None, :]).astype(BF16)

    tile = lambda width: pl.BlockSpec((bb, tt, width), lambda b, j: (b, j, 0))
    ret_st = pl.BlockSpec((bb, RET_HEADS, RET_HD, RET_HD), lambda b, j: (b, 0, 0, 0))
    wkv_st = pl.BlockSpec((bb, RWKV_HEADS, RWKV_HD, RWKV_HD), lambda b, j: (b, 0, 0, 0))
    sh_spec = pl.BlockSpec((bb, 1, RWKV_IN_W), lambda b, j: (b, 0, 0))
    tab_spec = pl.BlockSpec((tt, RET_HD), lambda b, j: (j, 0))
    vec = _full((1, RWKV_W))
    vmem = _vmem(_nbytes((rows, 5 * RET_W + RWKV_IN_W + RWKV_W)) + 2 * _nbytes((bb, RET_HEADS, RET_HD, RET_HD))
                 + 2 * _nbytes((bb, RWKV_HEADS, RWKV_HD, RWKV_HD)) + 2 * _nbytes((bb, SUBLANES, RWKV_IN_W))
                 + 2 * _nbytes((tt, RET_HD)),
                 _nbytes(dmask.shape) + 3 * _nbytes(q_dec.shape) + 9 * _nbytes((SUBLANES, RWKV_IN_W))
                 + _nbytes((DECAY_LORA + AAA_LORA + GATE_LORA + WKV_BLOCK, RWKV_W), BF16) + _nbytes(seg.shape, BF16)
                 + WKV_SCRATCH * _nbytes((rows, RWKV_W)),
                 4 * _nbytes((rows, RWKV_IN_W)) + 8 * _nbytes((rows, RWKV_W))
                 + tl.wkv_group * 32 * _nbytes((WKV_BLOCK, RWKV_IN_W)) + tl.ret_group * 8 * _nbytes((chunk, 4 * RET_W)))
    return pl.pallas_call(
        functools.partial(_mixers_kernel,
                          ret=dict(bb=bb, tt=tt, chunk=chunk, group=tl.ret_group),
                          wkv=dict(bb=bb, tt=tt, sub=sub, group=tl.wkv_group)),
        grid=(nb // bb, t // tt),
        in_specs=[ret_st, tile(4 * RET_W), tab_spec, tab_spec,
                  _full(dmask.shape), _full(q_dec.shape), _full(k_dec.shape), _full(c_dec.shape),
                  sh_spec, wkv_st, tile(RWKV_IN_W),
                  _full((1, RWKV_IN_W)), vec, vec, vec, vec, vec, vec, vec,
                  _full((DECAY_LORA, RWKV_W)), _full((AAA_LORA, RWKV_W)), _full((GATE_LORA, RWKV_W)),
                  _full(tri.shape), _full(seg.shape)],
        out_specs=[tile(RET_W), ret_st, tile(RWKV_W), wkv_st, sh_spec],
        out_shape=[jax.ShapeDtypeStruct((nb, t, RET_W), F32), jax.ShapeDtypeStruct(ret_s0.shape, F32),
                   jax.ShapeDtypeStruct((nb, t, RWKV_W), F32), jax.ShapeDtypeStruct(wkv_s0.shape, F32),
                   jax.ShapeDtypeStruct(wkv_sh0.shape, F32)],
        scratch_shapes=[pltpu.VMEM((rows, RWKV_W), F32)] * WKV_SCRATCH,
        compiler_params=_cparams(("parallel", "arbitrary"), vmem),
        name="mixers",
    )(ret_s0, zr, cos, sin, dmask, q_dec, k_dec, c_dec,
      wkv_sh0, wkv_s0, zw, p["mu"], p["w0"], p["a0"], p["k_k"], p["k_a"], p["r_k"], p["lnx_g"], p["lnx_b"],
      p["w_decay_up"], p["w_a_up"], p["w_g_up"], tri, seg)


def _merge_rows(x, zg, oa, ob, wa_ref, wb_ref, wo_ref):
    ga = jax.nn.sigmoid(zg[:, 0:D_MODEL])
    gb = jax.nn.sigmoid(zg[:, D_MODEL:GATE_W])
    merged = ga * _nn(oa, wa_ref[...]) + gb * _nn(ob, wb_ref[...])
    return x + _nn(merged, wo_ref[...])


def _merge_kernel(x_ref, zg_ref, oa_ref, ob_ref, wa_ref, wb_ref, wo_ref, o_ref):
    o_ref[...] = _merge_rows(x_ref[...], zg_ref[...], oa_ref[...], ob_ref[...], wa_ref, wb_ref, wo_ref)


def _merge(x, zg, o_ret, o_rwkv, wa, wb, wo):
    n = x.shape[0]
    tm = min(ROW_TILE, n)
    row = lambda w: pl.BlockSpec((tm, w), lambda i: (i, 0))
    vmem = _vmem(_nbytes((tm, 2 * D_MODEL + GATE_W + RET_W + RWKV_W)), _nbytes((2 * D_MODEL, D_MODEL), BF16),
                 6 * _nbytes((tm, D_MODEL)))
    return pl.pallas_call(
        _merge_kernel,
        grid=(n // tm,),
        in_specs=[row(D_MODEL), row(GATE_W), row(RET_W), row(RWKV_W),
                  _full((RET_W, D_MODEL)), _full((RWKV_W, D_MODEL)), _full((D_MODEL, D_MODEL))],
        out_specs=row(D_MODEL),
        out_shape=jax.ShapeDtypeStruct((n, D_MODEL), F32),
        compiler_params=_cparams(("parallel",), vmem),
        name="merge",
    )(x, zg, o_ret, o_rwkv, wa, wb, wo)


def _softmax(sc):
    e = jnp.exp(sc - jnp.max(sc, axis=-1, keepdims=True))
    return e / jnp.sum(e, axis=-1, keepdims=True)


def _attend_heads(q_scr, mk_ref, mv_ref, ox_scr, b, r0, tq):
    sls = [slice(h * X_HD, (h + 1) * X_HD) for h in range(X_HEADS)]
    sc = [_nt(q_scr[pl.ds(r0, tq), sl], mk_ref[b, :, sl]) for sl in sls]
    att = [_softmax(s * (X_HD ** -0.5)) for s in sc]
    for sl, a in zip(sls, att):
        ox_scr[pl.ds(r0, tq), sl] = _nn(a, mv_ref[b, :, sl])


def _attend_tiles(q_scr, mk_ref, mv_ref, ox_scr, seqs, tq):
    halves = X_HD // LANES
    n = X_HEADS * tq
    slot = lax.broadcasted_iota(jnp.int32, (n, MEM_ROWS), 1) % ROWS_PER_TOKEN
    head = lax.broadcasted_iota(jnp.int32, (n, MEM_ROWS), 0) // tq
    own = slot == head
    other = slot == head + X_HEADS
    qs = []
    for b in seqs:
        q = q_scr[b * tq:(b + 1) * tq, :]
        qs.append(jnp.concatenate([q[:, h * X_HD + c * LANES:h * X_HD + (c + 1) * LANES]
                                   for c in range(halves) for h in range(X_HEADS)], axis=0))
    z = [_nt(q, mk_ref[b]) for q, b in zip(qs, seqs)]
    part = [jnp.where(own, zi[:n], 0.0) + jnp.where(other, zi[n:], 0.0) for zi in z]
    sc = [p + pltpu.roll(p, MEM_ROWS - X_HEADS, axis=1) for p in part]
    att = [_softmax(jnp.where(own, s * (X_HD ** -0.5), -jnp.inf)) for s in sc]
    att2 = [jnp.concatenate([a, pltpu.roll(a, X_HEADS, axis=1)], axis=0) for a in att]
    o = [_nn(a, mv_ref[b]) for a, b in zip(att2, seqs)]
    for oi, b in zip(o, seqs):
        ox_scr[b * tq:(b + 1) * tq, :] = jnp.concatenate(
            [oi[(c * X_HEADS + h) * tq:(c * X_HEADS + h + 1) * tq] for h in range(X_HEADS) for c in range(halves)],
            axis=1)


def _cross_kernel(x_ref, mk_ref, mv_ref, g_ref, wq_ref, wo_ref, o_ref, q_scr, ox_scr, *, bb, tq):
    rows = bb * tq
    x = x_ref[...].reshape(rows, D_MODEL)
    q_scr[...] = _nn(_rms(x, g_ref[...]), wq_ref[...])
    _attend_tiles(q_scr, mk_ref, mv_ref, ox_scr, range(bb), tq)
    o_ref[...] = (x + _nn(ox_scr[...], wo_ref[...])).reshape(bb, tq, D_MODEL)


def _cross(x, mem_k, mem_v, g, wq, wo, bb, tq):
    nb, t, _ = x.shape
    x_spec = pl.BlockSpec((bb, tq, D_MODEL), lambda b, j: (b, j, 0))
    m_spec = pl.BlockSpec((bb, MEM_ROWS, LANES), lambda b, j: (b, 0, 0))
    vmem = _vmem(2 * _nbytes((bb, tq, D_MODEL)) + 2 * _nbytes((bb, MEM_ROWS, LANES)),
                 2 * _nbytes((D_MODEL, D_MODEL), BF16) + 2 * _nbytes((bb * tq, D_MODEL)),
                 6 * _nbytes((bb * tq, D_MODEL)) + bb * 8 * _nbytes((2 * X_HEADS * tq, MEM_ROWS)))
    return pl.pallas_call(
        functools.partial(_cross_kernel, bb=bb, tq=tq),
        grid=(nb // bb, t // tq),
        in_specs=[x_spec, m_spec, m_spec, _full((1, D_MODEL)), _full((D_MODEL, D_MODEL)), _full((D_MODEL, D_MODEL))],
        out_specs=x_spec,
        out_shape=jax.ShapeDtypeStruct(x.shape, F32),
        scratch_shapes=[pltpu.VMEM((bb * tq, D_MODEL), F32), pltpu.VMEM((bb * tq, D_MODEL), F32)],
        compiler_params=_cparams(("parallel", "arbitrary"), vmem),
        name="cross_attn",
    )(x, mem_k, mem_v, g, wq, wo)


MLP_FF_CHUNK = 1024


def _mlp_rows(x, g_ref, wu_ref, wd_ref, gf_ref, final_norm):
    h = _rms(x, g_ref[...]).astype(BF16)
    acc = x
    for c in range(0, D_FF, MLP_FF_CHUNK):
        u = jnp.maximum(jnp.dot(h, wu_ref[:, c:c + MLP_FF_CHUNK], preferred_element_type=F32), 0.0)
        acc = acc + _nn(u * u, wd_ref[c:c + MLP_FF_CHUNK, :])
    return _rms(acc, gf_ref[...]) if final_norm else acc


def _mlp_kernel(x_ref, g_ref, wu_ref, wd_ref, gf_ref, o_ref, *, final_norm):
    o_ref[...] = _mlp_rows(x_ref[...], g_ref, wu_ref, wd_ref, gf_ref, final_norm)


def _mlp(x, g, w_up, w_down, g_final, final_norm):
    n = x.shape[0]
    tm = min(ROW_TILE, n)
    row = pl.BlockSpec((tm, D_MODEL), lambda i: (i, 0))
    vmem = _vmem(2 * _nbytes((tm, D_MODEL)), 2 * _nbytes((D_MODEL, D_FF), BF16),
                 4 * _nbytes((tm, MLP_FF_CHUNK)) + 4 * _nbytes((tm, D_MODEL)))
    return pl.pallas_call(
        functools.partial(_mlp_kernel, final_norm=final_norm),
        grid=(n // tm,),
        in_specs=[row, _full((1, D_MODEL)), _full((D_MODEL, D_FF)), _full((D_FF, D_MODEL)), _full((1, D_MODEL))],
        out_specs=row,
        out_shape=jax.ShapeDtypeStruct((n, D_MODEL), F32),
        compiler_params=_cparams(("parallel",), vmem),
        name="mlp",
    )(x, g, w_up, w_down, g_final)


POST_TILE = 512


def _post_kernel(x_ref, zg_ref, oa_ref, ob_ref, mk_ref, mv_ref, wa_ref, wb_ref, wo_ref, gc_ref, wq_ref, wco_ref,
                 gm_ref, wu_ref, wd_ref, gf_ref, o_ref, q_scr, ox_scr, *, tq, final_norm):
    x1 = _merge_rows(x_ref[0], zg_ref[0], oa_ref[0], ob_ref[0], wa_ref, wb_ref, wo_ref)
    q_scr[...] = _nn(_rms(x1, gc_ref[...]), wq_ref[...])
    _attend_heads(q_scr, mk_ref, mv_ref, ox_scr, 0, 0, tq)
    x2 = x1 + _nn(ox_scr[...], wco_ref[...])
    o_ref[0] = _mlp_rows(x2, gm_ref, wu_ref, wd_ref, gf_ref, final_norm)


def _post(x, zg, o_ret, o_rwkv, mem_k, mem_v, w, g_final, final_norm):
    nb, t, _ = x.shape
    tq = min(POST_TILE, t)
    row = lambda width: pl.BlockSpec((1, tq, width), lambda b, j: (b, j, 0))
    mem = pl.BlockSpec((1, MEM_LEN, D_MODEL), lambda b, j: (b, 0, 0))
    sq = _full((D_MODEL, D_MODEL))
    vec = _full((1, D_MODEL))
    vmem = _vmem(_nbytes((tq, 2 * D_MODEL + GATE_W + RET_W + RWKV_W)) + 2 * _nbytes((MEM_LEN, D_MODEL), BF16),
                 _nbytes((4 * D_MODEL + 2 * D_FF, D_MODEL), BF16) + 2 * _nbytes((tq, D_MODEL)),
                 8 * _nbytes((tq, D_MODEL)) + 4 * _nbytes((tq, MLP_FF_CHUNK)))
    return pl.pallas_call(
        functools.partial(_post_kernel, tq=tq, final_norm=final_norm),
        grid=(nb, t // tq),
        in_specs=[row(D_MODEL), row(GATE_W), row(RET_W), row(RWKV_W), mem, mem,
                  _full((RET_W, D_MODEL)), _full((RWKV_W, D_MODEL)), sq, vec, sq, sq,
                  vec, _full((D_MODEL, D_FF)), _full((D_FF, D_MODEL)), vec],
        out_specs=row(D_MODEL),
        out_shape=jax.ShapeDtypeStruct(x.shape, F32),
        scratch_shapes=[pltpu.VMEM((tq, D_MODEL), F32), pltpu.VMEM((tq, D_MODEL), F32)],
        compiler_params=_cparams(("parallel", "arbitrary"), vmem),
        name="post",
    )(x, zg, o_ret, o_rwkv, mem_k, mem_v, w["w_branch_a"], w["w_branch_b"], w["w_out"], w["g_cross"], w["w_cq"],
      w["w_co"], w["g_mlp"], w["w_up"], w["w_down"], g_final)


class _Tiling(NamedTuple):
    seq_block: int
    time_block: int
    cross_block: int
    ret_group: int
    wkv_group: int


def _tiling(nb, t):
    if t > RET_CHUNK:
        tt = min(ROW_TILE, t)
        return _Tiling(seq_block=1, time_block=tt, cross_block=1,
                       ret_group=math.gcd(4, tt // RET_CHUNK), wkv_group=math.gcd(8, tt // WKV_BLOCK))
    fill = WKV_BLOCK // t
    blocks = math.gcd(SHORT_WKV_BLOCKS, nb // fill)
    seqs = fill * blocks
    return _Tiling(seq_block=seqs, time_block=t, cross_block=math.gcd(8, nb), ret_group=seqs, wkv_group=blocks)


def _layer(x, pos, s_ret, s_wkv, s_shift, mem_k, mem_v, w, g_final, final_norm):
    nb, t, _ = x.shape
    n = nb * t
    tl = _tiling(nb, t)
    xf = x.reshape(n, D_MODEL)
    zg, zr, zw = _in_proj(xf, w["g_mix"], w["w_in"])
    o_ret, s_ret_new, o_wkv, s_wkv_new, shift_new = _mixers(
        zr.reshape(nb, t, 4 * RET_W), zw.reshape(nb, t, RWKV_IN_W), s_ret, s_shift.reshape(nb, 1, RWKV_IN_W), s_wkv,
        pos, w, tl)
    if mem_k.shape[1:] == (MEM_LEN, D_MODEL):
        y = _post(x, zg.reshape(nb, t, GATE_W), o_ret, o_wkv, mem_k, mem_v, w, g_final, final_norm)
    else:
        x1 = _merge(xf, zg, o_ret.reshape(n, RET_W), o_wkv.reshape(n, RWKV_W),
                    w["w_branch_a"], w["w_branch_b"], w["w_out"])
        x2 = _cross(x1.reshape(nb, t, D_MODEL), mem_k, mem_v, w["g_cross"], w["w_cq"], w["w_co"],
                    tl.cross_block, tl.time_block)
        y = _mlp(x2.reshape(n, D_MODEL), w["g_mlp"], w["w_up"], w["w_down"], g_final, final_norm).reshape(x.shape)
    return y, s_ret_new, s_wkv_new, shift_new.reshape(nb, RWKV_IN_W)


def _layer_weights(l, g_mix, w_in, w_branch_a, w_branch_b, w_out, mu_shift, w0, w_decay_up, a0, w_a_up, w_g_up,
                   k_k, k_a, r_k, lnx_g, lnx_b, g_cross, w_cq, w_co, g_mlp, w_up, w_down):
    vec = lambda v: v[l].reshape(1, -1).astype(F32)
    mat = lambda m: m[l].astype(BF16)
    return dict(g_mix=vec(g_mix), w_in=mat(w_in), w_branch_a=mat(w_branch_a), w_branch_b=mat(w_branch_b),
                w_out=mat(w_out), mu=vec(mu_shift), w0=vec(w0), w_decay_up=mat(w_decay_up), a0=vec(a0),
                w_a_up=mat(w_a_up), w_g_up=mat(w_g_up), k_k=vec(k_k), k_a=vec(k_a), r_k=vec(r_k), lnx_g=vec(lnx_g),
                lnx_b=vec(lnx_b), g_cross=vec(g_cross), w_cq=mat(w_cq), w_co=mat(w_co), g_mlp=vec(g_mlp),
                w_up=mat(w_up), w_down=mat(w_down))


def kernel(x_prompt, x_sample, mem_prompt, state_ret, state_wkv, state_shift, cache_mem_k, cache_mem_v, g_mix, w_in, w_branch_a, w_branch_b, w_out, mu_shift, w0, w_decay_up, a0, w_a_up, w_g_up, k_k, k_a, r_k, lnx_g, lnx_b, g_cross, g_mem, w_cq, w_ck, w_cv, w_co, g_mlp, w_up, w_down, g_final):
    depth = w_in.shape[0]
    bp, tp, _ = x_prompt.shape
    bs, ts, _ = x_sample.shape
    pos_p = jnp.arange(tp, dtype=jnp.int32)
    pos_s = PAST_LEN + jnp.arange(ts, dtype=jnp.int32)
    gf = g_final.reshape(1, D_MODEL)

    xp, xs = x_prompt, x_sample
    outs_p, outs_s = [], []
    for l in range(depth):
        w = _layer_weights(l, g_mix, w_in, w_branch_a, w_branch_b, w_out, mu_shift, w0, w_decay_up, a0, w_a_up,
                           w_g_up, k_k, k_a, r_k, lnx_g, lnx_b, g_cross, w_cq, w_co, g_mlp, w_up, w_down)
        last = l == depth - 1
        k_tiles, v_tiles, k_att, v_att = _mem_kv(mem_prompt.reshape(bp * MEM_LEN, D_MODEL),
                                                 g_mem[l].reshape(1, D_MODEL), w_ck[l].astype(BF16), w_cv[l].astype(BF16))
        xp, sr, sw, ss = _layer(
            xp, pos_p,
            jnp.zeros((bp, RET_HEADS, RET_HD, RET_HD), F32), jnp.zeros((bp, RWKV_HEADS, RWKV_HD, RWKV_HD), F32),
            jnp.zeros((bp, RWKV_IN_W), F32), k_att.reshape(bp, MEM_LEN, D_MODEL), v_att.reshape(bp, MEM_LEN, D_MODEL),
            w, gf, last)
        outs_p.append((sr, sw, ss, _from_tile_order(k_tiles.reshape(bp, MEM_ROWS, LANES)),
                       _from_tile_order(v_tiles.reshape(bp, MEM_ROWS, LANES))))
        xs, sr2, sw2, ss2 = _layer(
            xs, pos_s, state_ret[l], state_wkv[l], state_shift[l],
            _tile_order(cache_mem_k[l]), _tile_order(cache_mem_v[l]), w, gf, last)
        outs_s.append((sr2, sw2, ss2))

    stack = lambda items, i: jnp.stack([it[i] for it in items])
    return (xp, xs, stack(outs_p, 0), stack(outs_p, 1), stack(outs_p, 2), stack(outs_p, 3), stack(outs_p, 4),
            stack(outs_s, 0), stack(outs_s, 1), stack(outs_s, 2))
```

```python
import functools
import math
from typing import NamedTuple

import jax
import jax.numpy as jnp
from jax import lax
from jax.experimental import pallas as pl
from jax.experimental.pallas import tpu as pltpu

F32 = jnp.float32
BF16 = jnp.bfloat16

D_MODEL = 1024
PAST_LEN = 16384
RET_HEADS = 4
RET_HD = 128
RET_W = RET_HEADS * RET_HD
RET_CHUNK = 128
RET_GN_EPS = 1e-5
ROPE_BASE = 10000.0
RWKV_HEADS = 8
RWKV_HD = 64
RWKV_W = RWKV_HEADS * RWKV_HD
DECAY_LORA = 64
AAA_LORA = 64
GATE_LORA = 128
RWKV_GN_EPS = 64e-5
RWKV_IN_W = 3 * RWKV_W + DECAY_LORA + AAA_LORA + GATE_LORA
GATE_W = 2 * D_MODEL
O_RET = 2 * D_MODEL
O_RWKV = O_RET + 4 * RET_W
IN_W = O_RWKV + RWKV_IN_W
MEM_LEN = 256
X_HEADS = 4
X_HD = D_MODEL // X_HEADS
D_FF = 4 * D_MODEL
RMS_EPS = 1e-6

SUBLANES = 8
V7X_VMEM_BYTES = 64 * 1024 * 1024
VMEM_CAP_BYTES = V7X_VMEM_BYTES - 8 * 1024 * 1024
ROW_TILE = 512


def _cparams(sem, vmem_bytes):
    return pltpu.CompilerParams(dimension_semantics=sem, vmem_limit_bytes=int(min(vmem_bytes, VMEM_CAP_BYTES)))


def _nbytes(shape, dtype=F32):
    return math.prod(shape) * jnp.dtype(dtype).itemsize


def _nn(a, b):
    return jnp.dot(a.astype(BF16), b.astype(BF16), preferred_element_type=F32)


def _nt(a, b):
    return lax.dot_general(a.astype(BF16), b.astype(BF16), (((1,), (1,)), ((), ())), preferred_element_type=F32)


def _tn(a, b):
    return lax.dot_general(a.astype(BF16), b.astype(BF16), (((0,), (0,)), ((), ())), preferred_element_type=F32)


def _rms(x, g):
    return x * lax.rsqrt(jnp.mean(x * x, axis=-1, keepdims=True) + RMS_EPS) * g


def _head_norms(blocks, eps):
    rows, width = blocks[0].shape
    ones = jnp.ones((width, width), BF16)
    o = jnp.concatenate(blocks, axis=0)
    d = o - _nn(o, ones) * (1.0 / width)
    out = d * lax.rsqrt(_nn(d * d, ones) * (1.0 / width) + eps)
    return [out[i * rows:(i + 1) * rows] for i in range(len(blocks))]


def _full(shape):
    zeros = (0,) * len(shape)
    return pl.BlockSpec(shape, lambda *_: zeros, pipeline_mode=pl.Buffered(1))


def _vmem(pipelined, resident, temps):
    return 2 * pipelined + resident + temps


def _in_proj_kernel(x_ref, g_ref, w_ref, zg_ref, zr_ref, zw_ref):
    h = _rms(x_ref[...], g_ref[...]).astype(BF16)
    zg_ref[...] = jnp.dot(h, w_ref[:, 0:O_RET], preferred_element_type=F32)
    zr_ref[...] = jnp.dot(h, w_ref[:, O_RET:O_RWKV], preferred_element_type=F32)
    zw_ref[...] = jnp.dot(h, w_ref[:, O_RWKV:IN_W], preferred_element_type=F32)


def _in_proj(x, g, w_in):
    n = x.shape[0]
    tm = min(ROW_TILE, n)
    vmem = _vmem(_nbytes((tm, D_MODEL + IN_W)), _nbytes((D_MODEL, IN_W), BF16), _nbytes((tm, 4 * RET_W + D_MODEL)))
    return pl.pallas_call(
        _in_proj_kernel,
        grid=(n // tm,),
        in_specs=[pl.BlockSpec((tm, D_MODEL), lambda i: (i, 0)), _full((1, D_MODEL)), _full((D_MODEL, IN_W))],
        out_specs=[pl.BlockSpec((tm, GATE_W), lambda i: (i, 0)),
                   pl.BlockSpec((tm, 4 * RET_W), lambda i: (i, 0)),
                   pl.BlockSpec((tm, RWKV_IN_W), lambda i: (i, 0))],
        out_shape=[jax.ShapeDtypeStruct((n, GATE_W), F32), jax.ShapeDtypeStruct((n, 4 * RET_W), F32),
                   jax.ShapeDtypeStruct((n, RWKV_IN_W), F32)],
        compiler_params=_cparams(("parallel",), vmem),
        name="in_proj",
    )(x, g, w_in)


LANES = 128
MEM_ROWS = MEM_LEN * X_HEADS * (X_HD // LANES)
ROWS_PER_TOKEN = MEM_ROWS // MEM_LEN


def _tile_order(mem):
    nb = mem.shape[0]
    return (mem.reshape(nb, MEM_LEN, X_HEADS, X_HD // LANES, LANES).transpose(0, 1, 3, 2, 4)
            .reshape(nb, MEM_ROWS, LANES))


def _from_tile_order(raw):
    nb = raw.shape[0]
    return (raw.reshape(nb, MEM_LEN, X_HD // LANES, X_HEADS, LANES).transpose(0, 1, 3, 2, 4)
            .reshape(nb, MEM_LEN, X_HEADS, X_HD))


def _mem_kv_kernel(x_ref, g_ref, wk_ref, wv_ref, kt_ref, vt_ref, kb_ref, vb_ref, *, tm):
    h = _rms(x_ref[...], g_ref[...]).astype(BF16)
    for w_ref, t_ref, b_ref in ((wk_ref, kt_ref, kb_ref), (wv_ref, vt_ref, vb_ref)):
        y = jnp.dot(h, w_ref[...], preferred_element_type=F32)
        b_ref[...] = y.astype(BF16)
        for hd in range(X_HEADS):
            for c in range(X_HD // LANES):
                col = hd * X_HD + c * LANES
                t_ref[pl.ds(c * X_HEADS + hd, tm, stride=ROWS_PER_TOKEN), :] = y[:, col:col + LANES]


def _mem_kv(mem, g, wk, wv):
    n = mem.shape[0]
    tm = min(ROW_TILE, n)
    vmem = _vmem(4 * _nbytes((tm, D_MODEL)), 2 * _nbytes((D_MODEL, D_MODEL), BF16), 4 * _nbytes((tm, D_MODEL)))
    row = pl.BlockSpec((tm, D_MODEL), lambda i: (i, 0))
    tile = pl.BlockSpec((tm * ROWS_PER_TOKEN, LANES), lambda i: (i, 0))
    return pl.pallas_call(
        functools.partial(_mem_kv_kernel, tm=tm),
        grid=(n // tm,),
        in_specs=[row, _full((1, D_MODEL)), _full((D_MODEL, D_MODEL)), _full((D_MODEL, D_MODEL))],
        out_specs=[tile, tile, row, row],
        out_shape=[jax.ShapeDtypeStruct((n * ROWS_PER_TOKEN, LANES), F32)] * 2
        + [jax.ShapeDtypeStruct((n, D_MODEL), BF16)] * 2,
        compiler_params=_cparams(("parallel",), vmem),
        name="mem_kv",
    )(mem, g, wk, wv)


def _ret_body(zr_ref, cos_ref, sin_ref, dm_ref, qd_ref, kd_ref, cd_ref, o_ref, s_ref, *, bb, tt, chunk, group):
    nch = tt // chunk
    assert bb == 1 or nch == 1, "a block holds either one sequence or one chunk per sequence"
    assert (bb * nch) % group == 0
    heads = range(RET_HEADS)

    def body(j, carry):
        if nch == 1:
            items = [(j * group + n, 0) for n in range(group)]
        else:
            starts = [(j * group + n) * chunk for n in range(group)]
            items = [(0, r if isinstance(r, int) else pl.multiple_of(r, chunk)) for r in starts]
        q, k, v = [], [], []
        for b, r0 in items:
            rows = pl.ds(r0, chunk)
            cos = cos_ref[rows, :]
            sin = sin_ref[rows, :]
            for h in heads:
                qh = zr_ref[b, rows, h * RET_HD:(h + 1) * RET_HD]
                kh = zr_ref[b, rows, RET_W + h * RET_HD:RET_W + (h + 1) * RET_HD]
                q.append(qh * cos + pltpu.roll(qh, RET_HD // 2, axis=1) * sin)
                k.append((kh * cos + pltpu.roll(kh, RET_HD // 2, axis=1) * sin) * (RET_HD ** -0.5))
                v.append(zr_ref[b, rows, 2 * RET_W + h * RET_HD:2 * RET_W + (h + 1) * RET_HD])
        hd = [h for _ in items for h in heads]
        sc = [_nt(qi, ki) * dm_ref[h] for qi, ki, h in zip(q, k, hd)]
        kv = [_tn(ki * kd_ref[h], vi) for ki, vi, h in zip(k, v, hd)]
        inner = [_nn(si, vi) for si, vi in zip(sc, v)]
        states = []
        for n, (b, _) in enumerate(items):
            for h in heads:
                s = s_ref[b, h] if (nch == 1 or n == 0) else states[-RET_HEADS] * cd_ref[h] + kv[(n - 1) * RET_HEADS + h]
                states.append(s)
        last = len(items) - 1
        for n, (b, _) in enumerate(items):
            if nch == 1 or n == last:
                for h in heads:
                    i = n * RET_HEADS + h
                    s_ref[b, h] = states[i] * cd_ref[h] + kv[i]
        cross = [_nn(qi, si) * qd_ref[h] for qi, si, h in zip(q, states, hd)]
        normed = _head_norms([a + c for a, c in zip(inner, cross)], RET_GN_EPS)
        for n, (b, r0) in enumerate(items):
            rows = pl.ds(r0, chunk)
            for h in heads:
                g = zr_ref[b, rows, 3 * RET_W + h * RET_HD:3 * RET_W + (h + 1) * RET_HD]
                o_ref[b, rows, h * RET_HD:(h + 1) * RET_HD] = normed[n * RET_HEADS + h] * (g * jax.nn.sigmoid(g))
        return carry

    if bb * nch == group:
        body(0, 0)
    else:
        lax.fori_loop(0, bb * nch // group, body, 0)


def _ret_consts(chunk):
    lg = jnp.log1p(-jnp.exp2(-5.0 - jnp.arange(RET_HEADS, dtype=F32)))
    idx = jnp.arange(chunk, dtype=F32)
    diff = idx[:, None] - idx[None, :]
    dmask = jnp.where(diff[None] >= 0, jnp.exp(jnp.maximum(diff, 0.0)[None] * lg[:, None, None]), 0.0)
    q_dec = jnp.exp((idx + 1.0)[None, :] * lg[:, None])
    k_dec = jnp.exp((chunk - 1.0 - idx)[None, :] * lg[:, None])
    c_dec = jnp.exp(chunk * lg)
    bc = lambda t: jnp.broadcast_to(t[:, :, None], (RET_HEADS, t.shape[1], RET_HD))
    return dmask, bc(q_dec), bc(k_dec), bc(c_dec[:, None])


def _rope_tables(pos):
    half = RET_HD // 2
    inv = 1.0 / (ROPE_BASE ** (jnp.arange(half, dtype=F32) / half))
    ang = pos.astype(F32)[:, None] * inv[None, :]
    cos, sin = jnp.cos(ang), jnp.sin(ang)
    return jnp.concatenate([cos, cos], axis=1), jnp.concatenate([-sin, sin], axis=1)


WKV_BLOCK = 64
SHORT_WKV_BLOCKS = 2
WKV_SCRATCH = 10


def _pow2(n):
    return n & (n - 1) == 0


def _imod(x, n):
    return jnp.bitwise_and(x, n - 1) if _pow2(n) else lax.rem(x, n)


def _idiv(x, n):
    return jnp.right_shift(x, n.bit_length() - 1) if _pow2(n) else lax.div(x, n)


def _wkv_body(zw_ref, mu_ref, w0_ref, a0_ref, kk_ref, ka_ref, rk_ref, lng_ref, lnb_ref,
              wd_ref, wa_ref, wg_ref, tri_ref, seg_ref, o_ref, s_ref, sh_ref,
              at_s, rt_s, bt_s, kt_s, bh_s, kh_s, v_s, g_s, bonus_s, pend_s, *, bb, tt, sub, group):
    C = WKV_BLOCK
    R = bb * tt
    nblk = R // C
    nseq = C // sub
    assert (bb == 1 and sub == C) or (sub == tt and R % C == 0 and group == nblk), \
        "one sequence per tile, or blocks of whole sequences handled in one group"
    assert nblk % group == 0
    n_sq = max(int(math.log2(sub)) - 1, 0)
    heads = range(RWKV_HEADS)
    hsl = [slice(h * RWKV_HD, (h + 1) * RWKV_HD) for h in heads]
    half = RWKV_W // 2

    def seg_sum(t):
        seg = seg_ref[...]
        return jnp.concatenate([_nn(t[:, :half], seg), _nn(t[:, half:], seg)], axis=1)

    pw = zw_ref[...].reshape(R, RWKV_IN_W)
    row = lax.broadcasted_iota(jnp.int32, (R, RWKV_IN_W), 0)
    rolled = pltpu.roll(pw, 1, axis=0)
    if bb == 1:
        prev = jnp.where(row == 0, sh_ref[0], rolled)
        sh_ref[0] = pw[R - 1:R, :]
    else:
        carry = jnp.broadcast_to(sh_ref[...], (bb, tt, RWKV_IN_W)).reshape(R, RWKV_IN_W)
        prev = jnp.where(_imod(row, tt) == 0, carry, rolled)
        sh_ref[...] = pw.reshape(bb, tt, RWKV_IN_W)[:, tt - 1:tt, :]
    xm = pw + mu_ref[...] * (prev - pw)
    r = xm[:, 0:RWKV_W]
    kb = xm[:, RWKV_W:2 * RWKV_W]
    o1 = 3 * RWKV_W
    wl = xm[:, o1:o1 + DECAY_LORA]
    al = xm[:, o1 + DECAY_LORA:o1 + DECAY_LORA + AAA_LORA]
    gl = xm[:, o1 + DECAY_LORA + AAA_LORA:RWKV_IN_W]
    logw = -math.exp(-0.5) * jax.nn.sigmoid(w0_ref[...] + _nn(jnp.tanh(wl), wd_ref[...]))
    a = jax.nn.sigmoid(a0_ref[...] + _nn(al, wa_ref[...]))
    l1 = logw.astype(BF16)
    rem = logw - l1.astype(F32)
    l2 = rem.astype(BF16)
    l3 = (rem - l2.astype(F32)).astype(BF16)
    tri = tri_ref[...]
    cum = jnp.concatenate(
        [sum(jnp.dot(tri, part[k * C:(k + 1) * C], preferred_element_type=F32) for part in (l1, l2, l3))
         for k in range(nblk)], axis=0)
    cum_end = jnp.broadcast_to(cum.reshape(R // sub, sub, RWKV_W)[:, sub - 1:sub, :],
                               (R // sub, sub, RWKV_W)).reshape(R, RWKV_W)
    kk = kb * kk_ref[...]
    kk = kk * jnp.minimum(lax.rsqrt(seg_sum(kk * kk)), 1e12)
    km = kb * (1.0 + (a - 1.0) * ka_ref[...])
    bv = kk * a
    p_end = jnp.exp(cum_end)
    e_neg = jnp.exp(-cum)
    e_end = p_end * e_neg
    at_s[...] = -kk * jnp.exp(cum - logw)
    rt_s[...] = r * jnp.exp(cum)
    bt_s[...] = bv * e_neg
    kt_s[...] = km * e_neg
    bh_s[...] = bv * e_end
    kh_s[...] = km * e_end
    v_s[...] = xm[:, 2 * RWKV_W:3 * RWKV_W]
    g_s[...] = _nn(jax.nn.sigmoid(gl), wg_ref[...])
    bonus_s[...] = seg_sum(r * km * rk_ref[...])
    pend_s[...] = p_end

    row_id = lax.broadcasted_iota(jnp.int32, (C, C), 0)
    col_id = lax.broadcasted_iota(jnp.int32, (C, C), 1)
    strict = row_id > col_id
    incl = row_id >= col_id
    if nseq > 1:
        same = _idiv(row_id, sub) == _idiv(col_id, sub)
        strict = jnp.logical_and(strict, same)
        incl = jnp.logical_and(incl, same)
    eye = jnp.where(row_id == col_id, 1.0, 0.0).astype(F32)

    def triangular(r0s):
        rows = [pl.ds(r0, C) for r0 in r0s for _ in heads]
        sl = [s for _ in r0s for s in hsl]
        at = [at_s[rw, :][:, s] for rw, s in zip(rows, sl)]
        rt = [rt_s[rw, :][:, s] for rw, s in zip(rows, sl)]
        v = [v_s[rw, :][:, s] for rw, s in zip(rows, sl)]
        bk_t = [jnp.concatenate([bt_s[pl.ds(r0, C), :], kt_s[pl.ds(r0, C), :]], axis=0).T for r0 in r0s]
        m1 = [_nn(jnp.concatenate([a_, r_], axis=0), bk[s, :])
              for a_, r_, bk, s in zip(at, rt, [t for t in bk_t for _ in heads], sl)]
        low = [jnp.where(strict, m[:C, :C], 0.0) for m in m1]
        rab = [jnp.where(incl, m[C:, :C], 0.0) for m in m1]
        akk = [jnp.concatenate([jnp.where(strict, m[:C, C:], 0.0), jnp.where(incl, m[C:, C:], 0.0)], axis=0)
               for m in m1]
        akv = [_nn(k_, v_) for k_, v_ in zip(akk, v)]
        tinv = [eye + l for l in low]
        if n_sq:
            x = [_nn(l, l) for l in low]
        for j in range(n_sq):
            if j < n_sq - 1:
                xt = [_nn(jnp.concatenate([xi, ti], axis=0), xi) for xi, ti in zip(x, tinv)]
                x = [t[:C] for t in xt]
                tinv = [ti + t[C:] for ti, t in zip(tinv, xt)]
            else:
                tinv = [ti + _nn(ti, xi) for ti, xi in zip(tinv, x)]
        ua = [_nn(t, a_) for t, a_ in zip(tinv, at)]
        uv = [_nn(t, k_[:C]) for t, k_ in zip(tinv, akv)]
        oa = [r_ + _nn(rb, u_) for r_, rb, u_ in zip(rt, rab, ua)]
        ov = [k_[C:] + _nn(rb, u_) for k_, rb, u_ in zip(akv, rab, uv)]
        nh = RWKV_HEADS
        return [dict(ua=ua[k * nh:(k + 1) * nh], uv=uv[k * nh:(k + 1) * nh], oa=oa[k * nh:(k + 1) * nh],
                     ov=ov[k * nh:(k + 1) * nh], v=v[k * nh:(k + 1) * nh]) for k in range(len(r0s))]

    def advance(r0, q):
        rw = pl.ds(r0, C)
        bh, kh, pend = bh_s[rw, :], kh_s[rw, :], pend_s[rw, :]
        first = 0 if bb == 1 else (r0 // C) * nseq
        items = [(n, h) for n in range(nseq) for h in heads]
        rs = [slice(n * sub, (n + 1) * sub) for n, _ in items]
        uo = [_nt(jnp.concatenate([q["ua"][h][s], q["oa"][h][s]], axis=0), s_ref[first + n, h])
              for (n, h), s in zip(items, rs)]
        u = [m[:sub] + q["uv"][h][s] for m, (n, h), s in zip(uo, items, rs)]
        upd = [_tn(jnp.concatenate([u_, q["v"][h][s]], axis=0),
                   jnp.concatenate([bh[s, hsl[h]], kh[s, hsl[h]]], axis=0)) for u_, (n, h), s in zip(u, items, rs)]
        for up, (n, h), s in zip(upd, items, rs):
            s_ref[first + n, h] = s_ref[first + n, h] * pend[s, hsl[h]][0:1] + up
        o = [m[sub:] + q["ov"][h][s] for m, (n, h), s in zip(uo, items, rs)]
        o = jnp.concatenate([jnp.concatenate([o[n * RWKV_HEADS + h] for n in range(nseq)], axis=0) for h in heads],
                            axis=1)
        d = o - seg_sum(o) * (1.0 / RWKV_HD)
        var = seg_sum(d * d) * (1.0 / RWKV_HD)
        o = d * lax.rsqrt(var + RWKV_GN_EPS) * lng_ref[...] + lnb_ref[...]
        o = (o + bonus_s[rw, :] * v_s[rw, :]) * g_s[rw, :]
        if bb == 1:
            o_ref[0, rw, :] = o
        else:
            o_ref[first:first + nseq] = o.reshape(nseq, tt, RWKV_W)

    def body(j, carry):
        r0s = [j * group * C + k * C for k in range(group)]
        r0s = [r if isinstance(r, int) else pl.multiple_of(r, C) for r in r0s]
        for r0, q in zip(r0s, triangular(r0s)):
            advance(r0, q)
        return carry

    if nblk == group:
        body(0, 0)
    else:
        lax.fori_loop(0, nblk // group, body, 0)


RET_INPUTS = 7
WKV_INPUTS = 14


def _mixers_kernel(*refs, ret, wkv):
    it = iter(refs)
    take = lambda n: [next(it) for _ in range(n)]
    ret_s0, = take(1)
    ret_in = take(RET_INPUTS)
    wkv_sh0, wkv_s0 = take(2)
    wkv_in = take(WKV_INPUTS)
    ret_o, ret_s, wkv_o, wkv_s, wkv_sh = take(5)
    scratch = take(WKV_SCRATCH)

    @pl.when(pl.program_id(1) == 0)
    def _():
        ret_s[...] = ret_s0[...]
        wkv_s[...] = wkv_s0[...]
        wkv_sh[...] = wkv_sh0[...]

    _ret_body(*ret_in, ret_o, ret_s, **ret)
    _wkv_body(*wkv_in, wkv_o, wkv_s, wkv_sh, *scratch, **wkv)


def _mixers(zr, zw, ret_s0, wkv_sh0, wkv_s0, pos, p, tl):
    nb, t, _ = zr.shape
    bb, tt = tl.seq_block, tl.time_block
    rows = bb * tt
    chunk = math.gcd(t, RET_CHUNK)
    cos, sin = _rope_tables(pos)
    dmask, q_dec, k_dec, c_dec = _ret_consts(chunk)
    sub = min(t, WKV_BLOCK)
    step = jnp.arange(WKV_BLOCK)
    tri = ((step[:, None] >= step[None, :]) & (step[:, None] // sub == step[None, :] // sub)).astype(BF16)
    lane_head = jnp.arange(RWKV_W // 2) // RWKV_HD
    seg = (lane_head[:, None] == lane_head[None, :]).astype(BF16)

    tile = lambda width: pl.BlockSpec((bb, tt, width), lambda b, j: (b, j, 0))
    ret_st = pl.BlockSpec((bb, RET_HEADS, RET_HD, RET_HD), lambda b, j: (b, 0, 0, 0))
    wkv_st = pl.BlockSpec((bb, RWKV_HEADS, RWKV_HD, RWKV_HD), lambda b, j: (b, 0, 0, 0))
    sh_spec = pl.BlockSpec((bb, 1, RWKV_IN_W), lambda b, j: (b, 0, 0))
    tab_spec = pl.BlockSpec((tt, RET_HD), lambda b, j: (j, 0))
    vec = _full((1, RWKV_W))
    vmem = _vmem(_nbytes((rows, 5 * RET_W + RWKV_IN_W + RWKV_W)) + 2 * _nbytes((bb, RET_HEADS, RET_HD, RET_HD))
                 + 2 * _nbytes((bb, RWKV_HEADS, RWKV_HD, RWKV_HD)) + 2 * _nbytes((bb, SUBLANES, RWKV_IN_W))
                 + 2 * _nbytes((tt, RET_HD)),
                 _nbytes(dmask.shape) + 3 * _nbytes(q_dec.shape) + 9 * _nbytes((SUBLANES, RWKV_IN_W))
                 + _nbytes((DECAY_LORA + AAA_LORA + GATE_LORA + WKV_BLOCK, RWKV_W), BF16) + _nbytes(seg.shape, BF16)
                 + WKV_SCRATCH * _nbytes((rows, RWKV_W)),
                 4 * _nbytes((rows, RWKV_IN_W)) + 8 * _nbytes((rows, RWKV_W))
                 + tl.wkv_group * 32 * _nbytes((WKV_BLOCK, RWKV_IN_W)) + tl.ret_group * 8 * _nbytes((chunk, 4 * RET_W)))
    return pl.pallas_call(
        functools.partial(_mixers_kernel,
                          ret=dict(bb=bb, tt=tt, chunk=chunk, group=tl.ret_group),
                          wkv=dict(bb=bb, tt=tt, sub=sub, group=tl.wkv_group)),
        grid=(nb // bb, t // tt),
        in_specs=[ret_st, tile(4 * RET_W), tab_spec, tab_spec,
                  _full(dmask.shape), _full(q_dec.shape), _full(k_dec.shape), _full(c_dec.shape),
                  sh_spec, wkv_st, tile(RWKV_IN_W),
                  _full((1, RWKV_IN_W)), vec, vec, vec, vec, vec, vec, vec,
                  _full((DECAY_LORA, RWKV_W)), _full((AAA_LORA, RWKV_W)), _full((GATE_LORA, RWKV_W)),
                  _full(tri.shape), _full(seg.shape)],
        out_specs=[tile(RET_W), ret_st, tile(RWKV_W), wkv_st, sh_spec],
        out_shape=[jax.ShapeDtypeStruct((nb, t, RET_W), F32), jax.ShapeDtypeStruct(ret_s0.shape, F32),
                   jax.ShapeDtypeStruct((nb, t, RWKV_W), F32), jax.ShapeDtypeStruct(wkv_s0.shape, F32),
                   jax.ShapeDtypeStruct(wkv_sh0.shape, F32)],
        scratch_shapes=[pltpu.VMEM((rows, RWKV_W), F32)] * WKV_SCRATCH,
        compiler_params=_cparams(("parallel", "arbitrary"), vmem),
        name="mixers",
    )(ret_s0, zr, cos, sin, dmask, q_dec, k_dec, c_dec,
      wkv_sh0, wkv_s0, zw, p["mu"], p["w0"], p["a0"], p["k_k"], p["k_a"], p["r_k"], p["lnx_g"], p["lnx_b"],
      p["w_decay_up"], p["w_a_up"], p["w_g_up"], tri, seg)


def _merge_rows(x, zg, oa, ob, wa_ref, wb_ref, wo_ref):
    ga = jax.nn.sigmoid(zg[:, 0:D_MODEL])
    gb = jax.nn.sigmoid(zg[:, D_MODEL:GATE_W])
    merged = ga * _nn(oa, wa_ref[...]) + gb * _nn(ob, wb_ref[...])
    return x + _nn(merged, wo_ref[...])


def _merge_kernel(x_ref, zg_ref, oa_ref, ob_ref, wa_ref, wb_ref, wo_ref, o_ref):
    o_ref[...] = _merge_rows(x_ref[...], zg_ref[...], oa_ref[...], ob_ref[...], wa_ref, wb_ref, wo_ref)


def _merge(x, zg, o_ret, o_rwkv, wa, wb, wo):
    n = x.shape[0]
    tm = min(ROW_TILE, n)
    row = lambda w: pl.BlockSpec((tm, w), lambda i: (i, 0))
    vmem = _vmem(_nbytes((tm, 2 * D_MODEL + GATE_W + RET_W + RWKV_W)), _nbytes((2 * D_MODEL, D_MODEL), BF16),
                 6 * _nbytes((tm, D_MODEL)))
    return pl.pallas_call(
        _merge_kernel,
        grid=(n // tm,),
        in_specs=[row(D_MODEL), row(GATE_W), row(RET_W), row(RWKV_W),
                  _full((RET_W, D_MODEL)), _full((RWKV_W, D_MODEL)), _full((D_MODEL, D_MODEL))],
        out_specs=row(D_MODEL),
        out_shape=jax.ShapeDtypeStruct((n, D_MODEL), F32),
        compiler_params=_cparams(("parallel",), vmem),
        name="merge",
    )(x, zg, o_ret, o_rwkv, wa, wb, wo)


def _softmax(sc):
    e = jnp.exp(sc - jnp.max(sc, axis=-1, keepdims=True))
    return e / jnp.sum(e, axis=-1, keepdims=True)


def _attend_heads(q_scr, mk_ref, mv_ref, ox_scr, b, r0, tq):
    sls = [slice(h * X_HD, (h + 1) * X_HD) for h in range(X_HEADS)]
    sc = [_nt(q_scr[pl.ds(r0, tq), sl], mk_ref[b, :, sl]) for sl in sls]
    att = [_softmax(s * (X_HD ** -0.5)) for s in sc]
    for sl, a in zip(sls, att):
        ox_scr[pl.ds(r0, tq), sl] = _nn(a, mv_ref[b, :, sl])


def _attend_tiles(q_scr, mk_ref, mv_ref, ox_scr, seqs, tq):
    halves = X_HD // LANES
    n = X_HEADS * tq
    slot = lax.broadcasted_iota(jnp.int32, (n, MEM_ROWS), 1) % ROWS_PER_TOKEN
    head = lax.broadcasted_iota(jnp.int32, (n, MEM_ROWS), 0) // tq
    own = slot == head
    other = slot == head + X_HEADS
    qs = []
    for b in seqs:
        q = q_scr[b * tq:(b + 1) * tq, :]
        qs.append(jnp.concatenate([q[:, h * X_HD + c * LANES:h * X_HD + (c + 1) * LANES]
                                   for c in range(halves) for h in range(X_HEADS)], axis=0))
    z = [_nt(q, mk_ref[b]) for q, b in zip(qs, seqs)]
    part = [jnp.where(own, zi[:n], 0.0) + jnp.where(other, zi[n:], 0.0) for zi in z]
    sc = [p + pltpu.roll(p, MEM_ROWS - X_HEADS, axis=1) for p in part]
    att = [_softmax(jnp.where(own, s * (X_HD ** -0.5), -jnp.inf)) for s in sc]
    att2 = [jnp.concatenate([a, pltpu.roll(a, X_HEADS, axis=1)], axis=0) for a in att]
    o = [_nn(a, mv_ref[b]) for a, b in zip(att2, seqs)]
    for oi, b in zip(o, seqs):
        ox_scr[b * tq:(b + 1) * tq, :] = jnp.concatenate(
            [oi[(c * X_HEADS + h) * tq:(c * X_HEADS + h + 1) * tq] for h in range(X_HEADS) for c in range(halves)],
            axis=1)


def _cross_kernel(x_ref, mk_ref, mv_ref, g_ref, wq_ref, wo_ref, o_ref, q_scr, ox_scr, *, bb, tq):
    rows = bb * tq
    x = x_ref[...].reshape(rows, D_MODEL)
    q_scr[...] = _nn(_rms(x, g_ref[...]), wq_ref[...])
    _attend_tiles(q_scr, mk_ref, mv_ref, ox_scr, range(bb), tq)
    o_ref[...] = (x + _nn(ox_scr[...], wo_ref[...])).reshape(bb, tq, D_MODEL)


def _cross(x, mem_k, mem_v, g, wq, wo, bb, tq):
    nb, t, _ = x.shape
    x_spec = pl.BlockSpec((bb, tq, D_MODEL), lambda b, j: (b, j, 0))
    m_spec = pl.BlockSpec((bb, MEM_ROWS, LANES), lambda b, j: (b, 0, 0))
    vmem = _vmem(2 * _nbytes((bb, tq, D_MODEL)) + 2 * _nbytes((bb, MEM_ROWS, LANES)),
                 2 * _nbytes((D_MODEL, D_MODEL), BF16) + 2 * _nbytes((bb * tq, D_MODEL)),
                 6 * _nbytes((bb * tq, D_MODEL)) + bb * 8 * _nbytes((2 * X_HEADS * tq, MEM_ROWS)))
    return pl.pallas_call(
        functools.partial(_cross_kernel, bb=bb, tq=tq),
        grid=(nb // bb, t // tq),
        in_specs=[x_spec, m_spec, m_spec, _full((1, D_MODEL)), _full((D_MODEL, D_MODEL)), _full((D_MODEL, D_MODEL))],
        out_specs=x_spec,
        out_shape=jax.ShapeDtypeStruct(x.shape, F32),
        scratch_shapes=[pltpu.VMEM((bb * tq, D_MODEL), F32), pltpu.VMEM((bb * tq, D_MODEL), F32)],
        compiler_params=_cparams(("parallel", "arbitrary"), vmem),
        name="cross_attn",
    )(x, mem_k, mem_v, g, wq, wo)


MLP_FF_CHUNK = 1024


def _mlp_rows(x, g_ref, wu_ref, wd_ref, gf_ref, final_norm):
    h = _rms(x, g_ref[...]).astype(BF16)
    acc = x
    for c in range(0, D_FF, MLP_FF_CHUNK):
        u = jnp.maximum(jnp.dot(h, wu_ref[:, c:c + MLP_FF_CHUNK], preferred_element_type=F32), 0.0)
        acc = acc + _nn(u * u, wd_ref[c:c + MLP_FF_CHUNK, :])
    return _rms(acc, gf_ref[...]) if final_norm else acc


MLP_ROW_TILE = 1024


def _mlp_kernel(x_ref, g_ref, wu_ref, wd_ref, gf_ref, o_ref, h_scr, *, final_norm):
    c = pl.program_id(1)

    @pl.when(c == 0)
    def _():
        h_scr[...] = _rms(x_ref[...], g_ref[...]).astype(BF16)
        o_ref[...] = x_ref[...]

    u = jnp.maximum(jnp.dot(h_scr[...], wu_ref[...], preferred_element_type=F32), 0.0)
    o_ref[...] += _nn(u * u, wd_ref[...])

    if final_norm:
        @pl.when(c == pl.num_programs(1) - 1)
        def _():
            o_ref[...] = _rms(o_ref[...], gf_ref[...])


def _mlp(x, g, w_up, w_down, g_final, final_norm):
    n = x.shape[0]
    tm = min(MLP_ROW_TILE, n)
    row = pl.BlockSpec((tm, D_MODEL), lambda i, c: (i, 0))
    vmem = _vmem(4 * _nbytes((tm, D_MODEL)) + 4 * _nbytes((D_MODEL, MLP_FF_CHUNK), BF16), _nbytes((tm, D_MODEL), BF16),
                 4 * _nbytes((tm, MLP_FF_CHUNK)) + 2 * _nbytes((tm, D_MODEL)))
    return pl.pallas_call(
        functools.partial(_mlp_kernel, final_norm=final_norm),
        grid=(n // tm, D_FF // MLP_FF_CHUNK),
        in_specs=[row, _full((1, D_MODEL)),
                  pl.BlockSpec((D_MODEL, MLP_FF_CHUNK), lambda i, c: (0, c)),
                  pl.BlockSpec((MLP_FF_CHUNK, D_MODEL), lambda i, c: (c, 0)),
                  _full((1, D_MODEL))],
        out_specs=row,
        out_shape=jax.ShapeDtypeStruct((n, D_MODEL), F32),
        scratch_shapes=[pltpu.VMEM((tm, D_MODEL), BF16)],
        compiler_params=_cparams(("parallel", "arbitrary"), vmem),
        name="mlp",
    )(x, g, w_up, w_down, g_final)


POST_TILE = 512


def _post_kernel(x_ref, zg_ref, oa_ref, ob_ref, mk_ref, mv_ref, wa_ref, wb_ref, wo_ref, gc_ref, wq_ref, wco_ref,
                 gm_ref, wu_ref, wd_ref, gf_ref, o_ref, q_scr, ox_scr, *, tq, final_norm):
    x1 = _merge_rows(x_ref[0], zg_ref[0], oa_ref[0], ob_ref[0], wa_ref, wb_ref, wo_ref)
    q_scr[...] = _nn(_rms(x1, gc_ref[...]), wq_ref[...])
    _attend_heads(q_scr, mk_ref, mv_ref, ox_scr, 0, 0, tq)
    x2 = x1 + _nn(ox_scr[...], wco_ref[...])
    o_ref[0] = _mlp_rows(x2, gm_ref, wu_ref, wd_ref, gf_ref, final_norm)


def _post(x, zg, o_ret, o_rwkv, mem_k, mem_v, w, g_final, final_norm):
    nb, t, _ = x.shape
    tq = min(POST_TILE, t)
    row = lambda width: pl.BlockSpec((1, tq, width), lambda b, j: (b, j, 0))
    mem = pl.BlockSpec((1, MEM_LEN, D_MODEL), lambda b, j: (b, 0, 0))
    sq = _full((D_MODEL, D_MODEL))
    vec = _full((1, D_MODEL))
    vmem = _vmem(_nbytes((tq, 2 * D_MODEL + GATE_W + RET_W + RWKV_W)) + 2 * _nbytes((MEM_LEN, D_MODEL), BF16),
                 _nbytes((4 * D_MODEL + 2 * D_FF, D_MODEL), BF16) + 2 * _nbytes((tq, D_MODEL)),
                 8 * _nbytes((tq, D_MODEL)) + 4 * _nbytes((tq, MLP_FF_CHUNK)))
    return pl.pallas_call(
        functools.partial(_post_kernel, tq=tq, final_norm=final_norm),
        grid=(nb, t // tq),
        in_specs=[row(D_MODEL), row(GATE_W), row(RET_W), row(RWKV_W), mem, mem,
                  _full((RET_W, D_MODEL)), _full((RWKV_W, D_MODEL)), sq, vec, sq, sq,
                  vec, _full((D_MODEL, D_FF)), _full((D_FF, D_MODEL)), vec],
        out_specs=row(D_MODEL),
        out_shape=jax.ShapeDtypeStruct(x.shape, F32),
        scratch_shapes=[pltpu.VMEM((tq, D_MODEL), F32), pltpu.VMEM((tq, D_MODEL), F32)],
        compiler_params=_cparams(("parallel", "arbitrary"), vmem),
        name="post",
    )(x, zg, o_ret, o_rwkv, mem_k, mem_v, w["w_branch_a"], w["w_branch_b"], w["w_out"], w["g_cross"], w["w_cq"],
      w["w_co"], w["g_mlp"], w["w_up"], w["w_down"], g_final)


class _Tiling(NamedTuple):
    seq_block: int
    time_block: int
    cross_block: int
    ret_group: int
    wkv_group: int


def _tiling(nb, t):
    if t > RET_CHUNK:
        tt = min(ROW_TILE, t)
        return _Tiling(seq_block=1, time_block=tt, cross_block=1,
                       ret_group=math.gcd(4, tt // RET_CHUNK), wkv_group=math.gcd(8, tt // WKV_BLOCK))
    fill = WKV_BLOCK // t
    blocks = math.gcd(SHORT_WKV_BLOCKS, nb // fill)
    seqs = fill * blocks
    return _Tiling(seq_block=seqs, time_block=t, cross_block=math.gcd(8, nb), ret_group=seqs, wkv_group=blocks)


def _layer(x, pos, s_ret, s_wkv, s_shift, mem_k, mem_v, w, g_final, final_norm):
    nb, t, _ = x.shape
    n = nb * t
    tl = _tiling(nb, t)
    xf = x.reshape(n, D_MODEL)
    zg, zr, zw = _in_proj(xf, w["g_mix"], w["w_in"])
    o_ret, s_ret_new, o_wkv, s_wkv_new, shift_new = _mixers(
        zr.reshape(nb, t, 4 * RET_W), zw.reshape(nb, t, RWKV_IN_W), s_ret, s_shift.reshape(nb, 1, RWKV_IN_W), s_wkv,
        pos, w, tl)
    if mem_k.shape[1:] == (MEM_LEN, D_MODEL):
        y = _post(x, zg.reshape(nb, t, GATE_W), o_ret, o_wkv, mem_k, mem_v, w, g_final, final_norm)
    else:
        x1 = _merge(xf, zg, o_ret.reshape(n, RET_W), o_wkv.reshape(n, RWKV_W),
                    w["w_branch_a"], w["w_branch_b"], w["w_out"])
        x2 = _cross(x1.reshape(nb, t, D_MODEL), mem_k, mem_v, w["g_cross"], w["w_cq"], w["w_co"],
                    tl.cross_block, tl.time_block)
        y = _mlp(x2.reshape(n, D_MODEL), w["g_mlp"], w["w_up"], w["w_down"], g_final, final_norm).reshape(x.shape)
    return y, s_ret_new, s_wkv_new, shift_new.reshape(nb, RWKV_IN_W)


def _layer_weights(l, g_mix, w_in, w_branch_a, w_branch_b, w_out, mu_shift, w0, w_decay_up, a0, w_a_up, w_g_up,
                   k_k, k_a, r_k, lnx_g, lnx_b, g_cross, w_cq, w_co, g_mlp, w_up, w_down):
    vec = lambda v: v[l].reshape(1, -1).astype(F32)
    mat = lambda m: m[l].astype(BF16)
    return dict(g_mix=vec(g_mix), w_in=mat(w_in), w_branch_a=mat(w_branch_a), w_branch_b=mat(w_branch_b),
                w_out=mat(w_out), mu=vec(mu_shift), w0=vec(w0), w_decay_up=mat(w_decay_up), a0=vec(a0),
                w_a_up=mat(w_a_up), w_g_up=mat(w_g_up), k_k=vec(k_k), k_a=vec(k_a), r_k=vec(r_k), lnx_g=vec(lnx_g),
                lnx_b=vec(lnx_b), g_cross=vec(g_cross), w_cq=mat(w_cq), w_co=mat(w_co), g_mlp=vec(g_mlp),
                w_up=mat(w_up), w_down=mat(w_down))


def kernel(x_prompt, x_sample, mem_prompt, state_ret, state_wkv, state_shift, cache_mem_k, cache_mem_v, g_mix, w_in, w_branch_a, w_branch_b, w_out, mu_shift, w0, w_decay_up, a0, w_a_up, w_g_up, k_k, k_a, r_k, lnx_g, lnx_b, g_cross, g_mem, w_cq, w_ck, w_cv, w_co, g_mlp, w_up, w_down, g_final):
    depth = w_in.shape[0]
    bp, tp, _ = x_prompt.shape
    bs, ts, _ = x_sample.shape
    pos_p = jnp.arange(tp, dtype=jnp.int32)
    pos_s = PAST_LEN + jnp.arange(ts, dtype=jnp.int32)
    gf = g_final.reshape(1, D_MODEL)

    xp, xs = x_prompt, x_sample
    outs_p, outs_s = [], []
    for l in range(depth):
        w = _layer_weights(l, g_mix, w_in, w_branch_a, w_branch_b, w_out, mu_shift, w0, w_decay_up, a0, w_a_up,
                           w_g_up, k_k, k_a, r_k, lnx_g, lnx_b, g_cross, w_cq, w_co, g_mlp, w_up, w_down)
        last = l == depth - 1
        k_tiles, v_tiles, k_att, v_att = _mem_kv(mem_prompt.reshape(bp * MEM_LEN, D_MODEL),
                                                 g_mem[l].reshape(1, D_MODEL), w_ck[l].astype(BF16), w_cv[l].astype(BF16))
        xp, sr, sw, ss = _layer(
            xp, pos_p,
            jnp.zeros((bp, RET_HEADS, RET_HD, RET_HD), F32), jnp.zeros((bp, RWKV_HEADS, RWKV_HD, RWKV_HD), F32),
            jnp.zeros((bp, RWKV_IN_W), F32), k_att.reshape(bp, MEM_LEN, D_MODEL), v_att.reshape(bp, MEM_LEN, D_MODEL),
            w, gf, last)
        outs_p.append((sr, sw, ss, _from_tile_order(k_tiles.reshape(bp, MEM_ROWS, LANES)),
                       _from_tile_order(v_tiles.reshape(bp, MEM_ROWS, LANES))))
        xs, sr2, sw2, ss2 = _layer(
            xs, pos_s, state_ret[l], state_wkv[l], state_shift[l],
            _tile_order(cache_mem_k[l]), _tile_order(cache_mem_v[l]), w, gf, last)
        outs_s.append((sr2, sw2, ss2))

    stack = lambda items, i: jnp.stack([it[i] for it in items])
    return (xp, xs, stack(outs_p, 0), stack(outs_p, 1), stack(outs_p, 2), stack(outs_p, 3), stack(outs_p, 4),
            stack(outs_s, 0), stack(outs_s, 1), stack(outs_s, 2))
```
